```python
import math
import jax, jax.numpy as jnp
from jax import lax
import numpy as np

D_MODEL = 1024
BATCH = 4
SEQ = 4096
DEPTH = 1

D_MIX = 2 * D_MODEL
SSM_D = D_MIX // 2
SSM_HEAD_DIM = 64
SSM_HEADS = SSM_D // SSM_HEAD_DIM
SSM_GROUPS = 2
SSM_STATE = 128
SSM_CONV = 4
SSM_CHUNK = 128
SSM_CONV_DIM = SSM_D + 2 * SSM_GROUPS * SSM_STATE
GLA_DV = D_MIX - SSM_D
GLA_HEADS = 4
GLA_HEAD_V = GLA_DV // GLA_HEADS
GLA_DK = GLA_DV // 2
GLA_HEAD_K = GLA_DK // GLA_HEADS
GLA_GATE_RANK = 16
GLA_GATE_NORM = 16.0
GLA_CHUNK = 64
N_EXPERTS = 32
TOP_K = 4
D_EXPERT = D_MODEL
SWIGLU_LIMIT = 7.0
SWIGLU_ALPHA = 1.702
MOE_BLOCK = 128
N_MOD = 6
EPS = 1e-6
IN_SIZES = (SSM_D, SSM_CONV_DIM, SSM_HEADS, GLA_DK, GLA_DK, GLA_DV, GLA_GATE_RANK, GLA_DV)
IN_COLS = SSM_D + SSM_CONV_DIM + SSM_HEADS + 2 * GLA_DK + GLA_DV + GLA_GATE_RANK + GLA_DV

kernel_name = 'hybrid_ssd_gla_moe_block'


def rms_norm(x, g):
    xf = x.astype(jnp.float32)
    y = xf * lax.rsqrt(jnp.mean(xf * xf, axis=-1, keepdims=True) + EPS)
    return (y * g.astype(jnp.float32)).astype(x.dtype)


def grouped_rms_norm(y, g, n_groups):
    shp = y.shape
    yg = y.astype(jnp.float32).reshape(shp[:-1] + (n_groups, shp[-1] // n_groups))
    yg = yg * lax.rsqrt(jnp.mean(yg * yg, axis=-1, keepdims=True) + EPS)
    return (yg.reshape(shp) * g.astype(jnp.float32)).astype(y.dtype)


def modulate(h, shift, scale):
    return h * (1.0 + scale[:, None, :]) + shift[:, None, :]


def causal_depthwise_conv(u, w, b):
    k = w.shape[0]
    out = lax.conv_general_dilated(
        u, w[:, None, :], window_strides=(1,), padding=[(k - 1, 0)],
        dimension_numbers=('NWC', 'WIO', 'NWC'), feature_group_count=u.shape[-1])
    return out + b


def chunk_state_scan(states, decays):
    def step(prev, inp):
        st, dec = inp
        return prev * dec + st, prev
    init = jnp.zeros(states.shape[:1] + states.shape[2:], jnp.float32)
    _, prev = lax.scan(step, init, (jnp.moveaxis(states, 1, 0), jnp.moveaxis(decays, 1, 0)))
    return jnp.moveaxis(prev, 0, 1)


def ssd_chunked(xh, dt, a, bm, cm):
    bsz, s, h, p = xh.shape
    g, n = bm.shape[-2:]
    l = SSM_CHUNK
    nc = s // l
    hpg = h // g
    f32 = jnp.float32
    dtc = dt.astype(f32).reshape(bsz, nc, l, h)
    xdt = xh.astype(f32).reshape(bsz, nc, l, h, p) * dtc[..., None]
    bm = bm.astype(f32).reshape(bsz, nc, l, g, n)
    cm = cm.astype(f32).reshape(bsz, nc, l, g, n)
    cs = jnp.cumsum(dtc * a.astype(f32), axis=2)
    cs_h = jnp.moveaxis(cs, 3, 2)
    causal = jnp.tril(jnp.ones((l, l), bool))
    decay = jnp.exp(jnp.where(causal, cs_h[..., :, None] - cs_h[..., None, :], -jnp.inf))
    cb = jnp.repeat(jnp.einsum('bclgn,bcsgn->bcgls', cm, bm), hpg, axis=2)
    y_diag = jnp.einsum('bchls,bcshp->bclhp', cb * decay, xdt)
    bm_h = jnp.repeat(bm, hpg, axis=3)
    cm_h = jnp.repeat(cm, hpg, axis=3)
    decay_to_end = jnp.exp(cs[:, :, -1:, :] - cs)
    states = jnp.einsum('bclhn,bclh,bclhp->bchpn', bm_h, decay_to_end, xdt)
    chunk_decay = jnp.exp(cs[:, :, -1, :])[..., None, None]
    prev = chunk_state_scan(states, chunk_decay)
    y_off = jnp.einsum('bclhn,bchpn,bclh->bclhp', cm_h, prev, jnp.exp(cs))
    return (y_diag + y_off).reshape(bsz, s, h, p)


def gla_chunked(q, k, v, gk):
    bsz, s, h, dk = q.shape
    dv = v.shape[-1]
    l = GLA_CHUNK
    nc = s // l
    f32 = jnp.float32
    q = q.astype(f32).reshape(bsz, nc, l, h, dk) * (dk ** -0.5)
    k = k.astype(f32).reshape(bsz, nc, l, h, dk)
    v = v.astype(f32).reshape(bsz, nc, l, h, dv)
    bcum = jnp.cumsum(gk.astype(f32).reshape(bsz, nc, l, h, dk), axis=2)
    q_t = q * jnp.exp(bcum)
    k_t = k * jnp.exp(-bcum)
    causal = jnp.tril(jnp.ones((l, l), bool))
    att = jnp.where(causal, jnp.einsum('bclhk,bcshk->bchls', q_t, k_t), 0.0)
    o_intra = jnp.einsum('bchls,bcshv->bclhv', att, v)
    b_last = bcum[:, :, -1:]
    states = jnp.einsum('bclhk,bclhv->bchkv', k * jnp.exp(b_last - bcum), v)
    chunk_decay = jnp.exp(b_last[:, :, 0])[..., None]
    prev = chunk_state_scan(states, chunk_decay)
    o_inter = jnp.einsum('bclhk,bchkv->bclhv', q_t, prev)
    return (o_intra + o_inter).reshape(bsz, s, h, dv)


def hybrid_mixer(h, w_in, conv_w, conv_b, dt_bias, a_log, d_skip, ssm_norm_g,
                 gla_wg2, gla_bg, gla_norm_g, w_out):
    bsz, s, _ = h.shape
    proj = h @ w_in
    split_idx = np.cumsum(IN_SIZES)[:-1].tolist()
    z, xbc, dt_raw, q, k, v, g_low, r = jnp.split(proj, split_idx, axis=-1)
    xbc = jax.nn.silu(causal_depthwise_conv(xbc, conv_w, conv_b))
    xs, bm, cm = jnp.split(xbc, [SSM_D, SSM_D + SSM_GROUPS * SSM_STATE], axis=-1)
    dt = jax.nn.softplus(dt_raw.astype(jnp.float32) + dt_bias.astype(jnp.float32))
    a = -jnp.exp(a_log.astype(jnp.float32))
    xh = xs.reshape(bsz, s, SSM_HEADS, SSM_HEAD_DIM)
    y = ssd_chunked(xh, dt, a,
                    bm.reshape(bsz, s, SSM_GROUPS, SSM_STATE),
                    cm.reshape(bsz, s, SSM_GROUPS, SSM_STATE))
    y = y + d_skip.astype(jnp.float32)[:, None] * xh.astype(jnp.float32)
    y = y.reshape(bsz, s, SSM_D).astype(h.dtype) * jax.nn.silu(z)
    y = grouped_rms_norm(y, ssm_norm_g, SSM_GROUPS)
    gk = jax.nn.log_sigmoid((g_low @ gla_wg2 + gla_bg).astype(jnp.float32)) / GLA_GATE_NORM
    o = gla_chunked(q.reshape(bsz, s, GLA_HEADS, GLA_HEAD_K),
                    k.reshape(bsz, s, GLA_HEADS, GLA_HEAD_K),
                    v.reshape(bsz, s, GLA_HEADS, GLA_HEAD_V),
                    gk.reshape(bsz, s, GLA_HEADS, GLA_HEAD_K)).astype(h.dtype)
    o = rms_norm(o, gla_norm_g) * jax.nn.silu(r.reshape(bsz, s, GLA_HEADS, GLA_HEAD_V))
    o = o.reshape(bsz, s, GLA_DV)
    return jnp.concatenate([y, o], axis=-1) @ w_out


def expert_swiglu(xb, wg, bg, wu, bu, wd, bd):
    gate = jnp.minimum(xb @ wg + bg, SWIGLU_LIMIT)
    up = jnp.clip(xb @ wu + bu, -SWIGLU_LIMIT, SWIGLU_LIMIT)
    glu = gate * jax.nn.sigmoid(SWIGLU_ALPHA * gate)
    return ((up + 1.0) * glu) @ wd + bd


def moe_ffn(t, w_router, b_router, w_gate, b_gate, w_up, b_up, w_down, b_down):
    n_tok, d = t.shape
    logits = (t @ w_router + b_router).astype(jnp.float32)
    topv, topi = lax.top_k(logits, TOP_K)
    topw = jax.nn.softmax(topv, axis=-1)
    n_asg = n_tok * TOP_K
    e_flat = topi.reshape(-1)
    tok_flat = jnp.repeat(jnp.arange(n_tok, dtype=jnp.int32), TOP_K)
    w_flat = topw.reshape(-1)
    order = jnp.argsort(e_flat)
    e_sorted = e_flat[order]
    counts = jnp.bincount(e_flat, length=N_EXPERTS)
    padded = (counts + MOE_BLOCK - 1) // MOE_BLOCK * MOE_BLOCK
    start = jnp.cumsum(counts) - counts
    pend = jnp.cumsum(padded)
    pstart = pend - padded
    dest = pstart[e_sorted] + jnp.arange(n_asg, dtype=jnp.int32) - start[e_sorted]
    n_blocks = -(-n_asg // MOE_BLOCK) + N_EXPERTS
    n_rows = n_blocks * MOE_BLOCK
    row_tok = jnp.zeros((n_rows,), jnp.int32).at[dest].set(tok_flat[order])
    row_w = jnp.zeros((n_rows,), jnp.float32).at[dest].set(w_flat[order])
    block_start = jnp.arange(n_blocks, dtype=jnp.int32) * MOE_BLOCK
    block_e = jnp.minimum(jnp.searchsorted(pend, block_start, side='right'), N_EXPERTS - 1)

    def run_block(args):
        toks, e = args
        return expert_swiglu(t[toks], w_gate[e], b_gate[e], w_up[e], b_up[e], w_down[e], b_down[e])

    y_rows = lax.map(run_block, (row_tok.reshape(n_blocks, MOE_BLOCK), block_e))
    y_rows = y_rows.reshape(n_rows, d).astype(jnp.float32) * row_w[:, None]
    return jax.ops.segment_sum(y_rows, row_tok, num_segments=n_tok).astype(t.dtype)


def _normal(key, shape, scale):
    return scale * jax.random.normal(key, shape, jnp.float32)


def setup_inputs(seed: int = 0) -> dict:
    key = jax.random.key(seed)
    ks = jax.random.split(key, 28)
    L, D, F, E = DEPTH, D_MODEL, D_EXPERT, N_EXPERTS
    gate_offset = jnp.zeros((N_MOD, D), jnp.float32).at[jnp.array([2, 5])].set(1.0).reshape(-1)
    dt0 = jnp.exp(jax.random.uniform(ks[8], (L, SSM_HEADS), jnp.float32,
                                     math.log(1e-3), math.log(1e-1)))
    return {
        'x': _normal(ks[0], (BATCH, SEQ, D), 1.0),
        'c': _normal(ks[1], (BATCH, D), 1.0),
        'w_ada': _normal(ks[2], (L, D, N_MOD * D), 0.5 * D ** -0.5),
        'b_ada': _normal(ks[3], (L, N_MOD * D), 0.02) + gate_offset,
        'norm1_g': 1.0 + _normal(ks[4], (L, D), 0.02),
        'w_in': _normal(ks[5], (L, D, IN_COLS), D ** -0.5),
        'conv_w': _normal(ks[6], (L, SSM_CONV, SSM_CONV_DIM), SSM_CONV ** -0.5),
        'conv_b': _normal(ks[7], (L, SSM_CONV_DIM), 0.02),
        'dt_bias': dt0 + jnp.log(-jnp.expm1(-dt0)),
        'a_log': jnp.log(jax.random.uniform(ks[9], (L, SSM_HEADS), jnp.float32, 1.0, 16.0)),
        'd_skip': 1.0 + _normal(ks[10], (L, SSM_HEADS), 0.02),
        'ssm_norm_g': 1.0 + _normal(ks[11], (L, SSM_D), 0.02),
        'gla_wg2': _normal(ks[12], (L, GLA_GATE_RANK, GLA_DK), GLA_GATE_RANK ** -0.5),
        'gla_bg': _normal(ks[13], (L, GLA_DK), 0.02),
        'gla_norm_g': 1.0 + _normal(ks[14], (L, GLA_HEAD_V), 0.02),
        'w_out': _normal(ks[15], (L, D_MIX, D), D_MIX ** -0.5),
        'norm2_g': 1.0 + _normal(ks[16], (L, D), 0.02),
        'w_router': _normal(ks[17], (L, D, E), D ** -0.5),
        'b_router': _normal(ks[18], (L, E), 0.01),
        'w_gate': _normal(ks[19], (L, E, D, F), D ** -0.5),
        'b_gate': _normal(ks[20], (L, E, F), 0.01),
        'w_up': _normal(ks[21], (L, E, D, F), D ** -0.5),
        'b_up': _normal(ks[22], (L, E, F), 0.01),
        'w_down': _normal(ks[23], (L, E, F, D), F ** -0.5),
        'b_down': _normal(ks[24], (L, E, D), 0.01),
        'final_norm_g': 1.0 + _normal(ks[25], (D,), 0.02),
    }


def reference(x, c, w_ada, b_ada, norm1_g, w_in, conv_w, conv_b, dt_bias, a_log, d_skip,
              ssm_norm_g, gla_wg2, gla_bg, gla_norm_g, w_out, norm2_g, w_router, b_router,
              w_gate, b_gate, w_up, b_up, w_down, b_down, final_norm_g):
    bsz, s, d = x.shape
    cond = jax.nn.silu(c)
    for l in range(DEPTH):
        mod = cond @ w_ada[l] + b_ada[l]
        sh1, sc1, gt1, sh2, sc2, gt2 = jnp.split(mod, N_MOD, axis=-1)
        h = modulate(rms_norm(x, norm1_g[l]), sh1, sc1)
        mix = hybrid_mixer(h, w_in[l], conv_w[l], conv_b[l], dt_bias[l], a_log[l], d_skip[l],
                           ssm_norm_g[l], gla_wg2[l], gla_bg[l], gla_norm_g[l], w_out[l])
        x = x + gt1[:, None, :] * mix
        h = modulate(rms_norm(x, norm2_g[l]), sh2, sc2)
        ffn = moe_ffn(h.reshape(bsz * s, d), w_router[l], b_router[l], w_gate[l], b_gate[l],
                      w_up[l], b_up[l], w_down[l], b_down[l]).reshape(bsz, s, d)
        x = x + gt2[:, None, :] * ffn
    return rms_norm(x, final_norm_g)
```

```python
import functools

import jax
import jax.numpy as jnp
from jax import lax
from jax.experimental import pallas as pl
from jax.experimental.pallas import tpu as pltpu

F32 = jnp.float32
BF16 = jnp.bfloat16
I32 = jnp.int32
HIGHEST = lax.Precision.HIGHEST

EPS = 1e-6
SSM_HEAD_DIM = 64
SSM_GROUPS = 2
SSM_STATE = 128
SSM_CONV = 4
SSM_CHUNK = 128
GLA_HEADS = 4
GLA_GATE_RANK = 16
GLA_GATE_NORM = 16.0
GLA_CHUNK = 64
TOP_K = 4
SWIGLU_LIMIT = 7.0
SWIGLU_ALPHA = 1.702

LANES = 128
SUBLANES = 8
VMEM_LIMIT_BYTES = 56 * 1024 * 1024

TOKEN_TILE = 512
GLA_STEP_ROWS = 128
ROUTE_TILE = 512
EXPERT_BLOCK = 256
DISPATCH_TILE = 256
COMBINE_TILE = 256

_NT = (((1,), (1,)), ((), ()))
_TN = (((0,), (0,)), ((), ()))


def _silu(v):
    return v / (1.0 + jnp.exp(-v))


def _softplus(v):
    return jnp.maximum(v, 0.0) + jnp.log1p(jnp.exp(-jnp.abs(v)))


def _log_sigmoid(v):
    return jnp.minimum(v, 0.0) - jnp.log1p(jnp.exp(-jnp.abs(v)))


def _rms(v):
    return v * lax.rsqrt(jnp.mean(v * v, axis=-1, keepdims=True) + EPS)


def _params(*semantics):
    return pltpu.CompilerParams(dimension_semantics=semantics, vmem_limit_bytes=VMEM_LIMIT_BYTES)


def _const_spec(shape):
    nd = len(shape)
    return pl.BlockSpec(shape, lambda *_: (0,) * nd)


def _ada_kernel(c_ref, w_ref, b_ref, o_ref):
    cond = _silu(c_ref[...])
    o_ref[...] = jnp.dot(cond, w_ref[...], precision=HIGHEST, preferred_element_type=F32) + b_ref[...]


def _ada(c_pad, w_ada, b_ada):
    rows, d = c_pad.shape
    n = w_ada.shape[1]
    tn = d
    return pl.pallas_call(
        _ada_kernel,
        out_shape=jax.ShapeDtypeStruct((rows, n), F32),
        grid=(n // tn,),
        in_specs=[pl.BlockSpec((rows, d), lambda j: (0, 0)),
                  pl.BlockSpec((d, tn), lambda j: (0, j)),
                  pl.BlockSpec((1, tn), lambda j: (0, j))],
        out_specs=pl.BlockSpec((rows, tn), lambda j: (0, j)),
        compiler_params=_params("arbitrary"),
        name="ada",
    )(c_pad, w_ada, b_ada)


def _inproj_kernel(x_ref, sc_ref, sh_ref, g_ref, *refs):
    n_w = len(refs) // 2
    w_refs, o_refs = refs[:n_w], refs[n_w:]
    h = (_rms(x_ref[...]) * g_ref[...]) * (1.0 + sc_ref[0]) + sh_ref[0]
    hb = h.astype(BF16)
    for w_ref, o_ref in zip(w_refs, o_refs):
        o_ref[...] = jnp.dot(hb, w_ref[...], preferred_element_type=F32).astype(o_ref.dtype)


def _inproj(x2, sc, sh, g, weights, out_dtypes, seq):
    t, d = x2.shape
    tm = min(TOKEN_TILE, seq)
    per_batch = seq // tm
    mod_spec = pl.BlockSpec((1, 1, d), lambda i: (i // per_batch, 0, 0))
    return pl.pallas_call(
        _inproj_kernel,
        out_shape=[jax.ShapeDtypeStruct((t, w.shape[1]), dt) for w, dt in zip(weights, out_dtypes)],
        grid=(t // tm,),
        in_specs=[pl.BlockSpec((tm, d), lambda i: (i, 0)), mod_spec, mod_spec, _const_spec((1, d))]
                 + [_const_spec(w.shape) for w in weights],
        out_specs=[pl.BlockSpec((tm, w.shape[1]), lambda i: (i, 0)) for w in weights],
        compiler_params=_params("arbitrary"),
        name="inproj",
    )(x2, sc, sh, g, *weights)


def _ssd_kernel(xbc_ref, z_ref, sm_ref, cw_ref, cb_ref, dtb_r_ref, dtb_c_ref, alog_r_ref, alog_c_ref,
                dsk_ref, g_ref, y_ref, ext_ref, st_ref, *, n_heads, d_ssm):
    L = SSM_CHUNK
    P = SSM_HEAD_DIM
    N = SSM_STATE
    G = SSM_GROUPS
    gw = d_ssm // G
    c = pl.program_id(1)

    @pl.when(c == 0)
    def _():
        ext_ref[0:SUBLANES, :] = jnp.zeros((SUBLANES, ext_ref.shape[1]), F32)
        st_ref[...] = jnp.zeros(st_ref.shape, F32)

    ext_ref[SUBLANES:SUBLANES + L, :] = xbc_ref[...].astype(F32)
    acc = cb_ref[...]
    for j in range(SSM_CONV):
        off = SUBLANES - (SSM_CONV - 1) + j
        acc = acc + cw_ref[j:j + 1, :] * ext_ref[off:off + L, :]
    ext_ref[0:SUBLANES, :] = ext_ref[L:L + SUBLANES, :]
    act = _silu(acc)
    xs = act[:, :d_ssm]
    bm = act[:, d_ssm:d_ssm + G * N].astype(BF16)
    cm = act[:, d_ssm + G * N:].astype(BF16)

    sm = sm_ref[...]
    dt_col = _softplus(sm[:, :n_heads] + dtb_r_ref[...])
    dt_row = _softplus(sm.T[:n_heads, :] + dtb_c_ref[...])
    da_col = dt_col * (-jnp.exp(alog_r_ref[...]))
    da_row = dt_row * (-jnp.exp(alog_c_ref[...]))
    ri = lax.broadcasted_iota(I32, (L, L), 0)
    ci = lax.broadcasted_iota(I32, (L, L), 1)
    causal = ri >= ci
    cs_col = jnp.dot(causal.astype(F32), da_col, precision=HIGHEST, preferred_element_type=F32)
    cs_row = jnp.dot(da_row, (ri <= ci).astype(F32), precision=HIGHEST, preferred_element_type=F32)
    cs_last = cs_col[L - 1:L, :]

    hh = lax.broadcasted_iota(I32, (n_heads, d_ssm), 0)
    jj = lax.broadcasted_iota(I32, (n_heads, d_ssm), 1)
    expand = ((jj // P) == hh).astype(F32)
    per_head = jnp.concatenate([dt_col, jnp.exp(cs_col), jnp.exp(cs_last - cs_col)], axis=0)
    per_ch = jnp.dot(per_head, expand, precision=HIGHEST, preferred_element_type=F32)
    dt_e, ecs_e, dte_e = per_ch[0:L], per_ch[L:2 * L], per_ch[2 * L:3 * L]
    chunk_decay = ecs_e[L - 1:L, :]

    xdt = xs * dt_e
    xdt_b = xdt.astype(BF16)
    xdec_b = (xdt * dte_e).astype(BF16)

    lane = lax.broadcasted_iota(I32, (L, LANES), 1)
    first_half = lane < P
    heads_per_group = n_heads // G
    y_parts = []
    y_off_parts = []
    for g in range(G):
        bm_g = bm[:, g * N:(g + 1) * N]
        cm_g = cm[:, g * N:(g + 1) * N]
        cb = lax.dot_general(cm_g, bm_g, _NT, preferred_element_type=F32)
        prev = st_ref[g]
        y_off_parts.append(jnp.dot(cm_g, prev.astype(BF16), preferred_element_type=F32))
        s_new = lax.dot_general(bm_g, xdec_b[:, g * gw:(g + 1) * gw], _TN, preferred_element_type=F32)
        st_ref[g] = prev * chunk_decay[:, g * gw:(g + 1) * gw] + s_new
        for p in range(heads_per_group // 2):
            h0 = g * heads_per_group + 2 * p
            ms = []
            for h in (h0, h0 + 1):
                diff = cs_col[:, h:h + 1] - cs_row[h:h + 1, :]
                ms.append((cb * jnp.exp(jnp.where(causal, diff, -jnp.inf))).astype(BF16))
            lhs = jnp.concatenate(ms, axis=1)
            xp = xdt_b[:, h0 * P:(h0 + 2) * P]
            zero = jnp.zeros_like(xp)
            rhs = jnp.concatenate([jnp.where(first_half, xp, zero), jnp.where(first_half, zero, xp)], axis=0)
            y_parts.append(jnp.dot(lhs, rhs, preferred_element_type=F32))
    y = jnp.concatenate(y_parts, axis=1) + jnp.concatenate(y_off_parts, axis=1) * ecs_e + dsk_ref[...] * xs
    y = y * _silu(z_ref[...].astype(F32))
    y = jnp.concatenate([_rms(y[:, g * gw:(g + 1) * gw]) for g in range(G)], axis=1) * g_ref[...]
    y_ref[...] = y.astype(y_ref.dtype)


def _ssd(xbc, z, small, conv_w, conv_b, dt_bias, a_log, d_skip_e, norm_g, batch, seq):
    t, cd = xbc.shape
    d_ssm = z.shape[1]
    n_heads = dt_bias.shape[0]
    L = SSM_CHUNK
    nc = seq // L
    row = lambda b, c: (b * nc + c, 0)
    kern = functools.partial(_ssd_kernel, n_heads=n_heads, d_ssm=d_ssm)
    return pl.pallas_call(
        kern,
        out_shape=jax.ShapeDtypeStruct((t, d_ssm), BF16),
        grid=(batch, nc),
        in_specs=[pl.BlockSpec((L, cd), row), pl.BlockSpec((L, d_ssm), row), pl.BlockSpec((L, LANES), row),
                  _const_spec(conv_w.shape), _const_spec((1, cd)),
                  _const_spec((1, n_heads)), _const_spec((n_heads, 1)),
                  _const_spec((1, n_heads)), _const_spec((n_heads, 1)),
                  _const_spec((1, d_ssm)), _const_spec((1, d_ssm))],
        out_specs=pl.BlockSpec((L, d_ssm), row),
        scratch_shapes=[pltpu.VMEM((L + 2 * SUBLANES, cd), F32),
                        pltpu.VMEM((SSM_GROUPS, SSM_STATE, d_ssm // SSM_GROUPS), F32)],
        compiler_params=_params("arbitrary", "arbitrary"),
        name="ssd",
    )(xbc, z, small, conv_w, conv_b.reshape(1, cd), dt_bias.reshape(1, n_heads), dt_bias.reshape(n_heads, 1),
      a_log.reshape(1, n_heads), a_log.reshape(n_heads, 1), d_skip_e, norm_g.reshape(1, d_ssm))


def _gla_kernel(q_ref, k_ref, v_ref, r_ref, sm_ref, wg2_ref, bg_ref, gn_ref, o_ref, st_ref, *, gate_col):
    L = GLA_CHUNK
    H = GLA_HEADS
    dk = q_ref.shape[1] // H
    dv = v_ref.shape[1] // H
    c = pl.program_id(1)

    @pl.when(c == 0)
    def _():
        st_ref[...] = jnp.zeros(st_ref.shape, F32)

    ri = lax.broadcasted_iota(I32, (L, L), 0)
    ci = lax.broadcasted_iota(I32, (L, L), 1)
    causal = ri >= ci
    tril = causal.astype(F32)
    last_row = (lax.broadcasted_iota(I32, (L, LANES), 0) == L - 1).astype(F32)

    for s in range(q_ref.shape[0] // L):
        rows = slice(s * L, (s + 1) * L)
        q = q_ref[rows, :].astype(F32) * (dk ** -0.5)
        k = k_ref[rows, :].astype(F32)
        v = v_ref[rows, :]
        r = r_ref[rows, :].astype(F32)
        g_low = sm_ref[rows, gate_col:gate_col + GLA_GATE_RANK]
        pre = jnp.dot(g_low, wg2_ref[...], precision=HIGHEST, preferred_element_type=F32) + bg_ref[...]
        gk = _log_sigmoid(pre) / GLA_GATE_NORM
        bcum = jnp.dot(tril, gk, precision=HIGHEST, preferred_element_type=F32)
        b_last = bcum[L - 1:L, :]
        q_t = (q * jnp.exp(bcum)).astype(BF16)
        k_t = (k * jnp.exp(-bcum)).astype(BF16)
        k_dec = (k * jnp.exp(b_last - bcum)).astype(BF16)
        dcol = jnp.exp(lax.dot_general(bcum, last_row, _TN, precision=HIGHEST, preferred_element_type=F32))
        outs = []
        for h in range(H):
            ks = slice(h * dk, (h + 1) * dk)
            vs = slice(h * dv, (h + 1) * dv)
            att = lax.dot_general(q_t[:, ks], k_t[:, ks], _NT, preferred_element_type=F32)
            att = jnp.where(causal, att, 0.0).astype(BF16)
            prev = st_ref[h]
            o = (jnp.dot(att, v[:, vs], preferred_element_type=F32)
                 + jnp.dot(q_t[:, ks], prev.astype(BF16), preferred_element_type=F32))
            s_new = lax.dot_general(k_dec[:, ks], v[:, vs], _TN, preferred_element_type=F32)
            dec = dcol[ks, :]
            st_ref[h] = prev * jnp.concatenate([dec] * (dv // LANES), axis=1) + s_new
            outs.append(_rms(o) * gn_ref[...] * _silu(r[:, vs]))
        o_ref[rows, :] = jnp.concatenate(outs, axis=1).astype(o_ref.dtype)


def _gla(q, k, v, r, small, wg2, bg, norm_g, batch, seq, gate_col):
    t, dkt = q.shape
    dvt = v.shape[1]
    rows = GLA_STEP_ROWS
    nc = seq // rows
    row = lambda b, c: (b * nc + c, 0)
    kern = functools.partial(_gla_kernel, gate_col=gate_col)
    return pl.pallas_call(
        kern,
        out_shape=jax.ShapeDtypeStruct((t, dvt), BF16),
        grid=(batch, nc),
        in_specs=[pl.BlockSpec((rows, dkt), row), pl.BlockSpec((rows, dkt), row), pl.BlockSpec((rows, dvt), row),
                  pl.BlockSpec((rows, dvt), row), pl.BlockSpec((rows, LANES), row),
                  _const_spec(wg2.shape), _const_spec((1, dkt)), _const_spec((1, dvt // GLA_HEADS))],
        out_specs=pl.BlockSpec((rows, dvt), row),
        scratch_shapes=[pltpu.VMEM((GLA_HEADS, dkt // GLA_HEADS, dvt // GLA_HEADS), F32)],
        compiler_params=_params("arbitrary", "arbitrary"),
        name="gla",
    )(q, k, v, r, small, wg2, bg.reshape(1, dkt), norm_g.reshape(1, dvt // GLA_HEADS))


def _outproj_kernel(y_ref, o_ref, x_ref, gt_ref, sc_ref, sh_ref, g_ref, wy_ref, wo_ref, wr_ref, br_ref,
                    x1_ref, h_ref, ti_ref, tw_ref):
    mix = (jnp.dot(y_ref[...], wy_ref[...], preferred_element_type=F32)
           + jnp.dot(o_ref[...], wo_ref[...], preferred_element_type=F32))
    x1 = x_ref[...] + gt_ref[0] * mix
    x1_ref[...] = x1
    h = (_rms(x1) * g_ref[...]) * (1.0 + sc_ref[0]) + sh_ref[0]
    h_ref[...] = h
    logits = jnp.dot(h, wr_ref[...], precision=HIGHEST, preferred_element_type=F32) + br_ref[...]
    n_e = logits.shape[1]
    lane = lax.broadcasted_iota(I32, logits.shape, 1)
    vals, idxs = [], []
    for _ in range(TOP_K):
        m = jnp.max(logits, axis=1, keepdims=True)
        idx = jnp.min(jnp.where(logits == m, lane, n_e), axis=1, keepdims=True)
        vals.append(m)
        idxs.append(idx)
        logits = jnp.where(lane == idx, -jnp.inf, logits)
    exps = [jnp.exp(v - vals[0]) for v in vals]
    denom = functools.reduce(lambda a, b: a + b, exps)
    ti_ref[...] = jnp.concatenate(idxs, axis=1)
    tw_ref[...] = jnp.concatenate([e / denom for e in exps], axis=1)


def _outproj(y, o, x2, gt, sc, sh, g, wy, wo, w_router, b_router, seq):
    t, d = x2.shape
    n_e = w_router.shape[1]
    tm = min(TOKEN_TILE, seq)
    per_batch = seq // tm
    row = lambda i: (i, 0)
    mod_spec = pl.BlockSpec((1, 1, d), lambda i: (i // per_batch, 0, 0))
    return pl.pallas_call(
        _outproj_kernel,
        out_shape=[jax.ShapeDtypeStruct((t, d), F32), jax.ShapeDtypeStruct((t, d), F32),
                   jax.ShapeDtypeStruct((t, TOP_K), I32), jax.ShapeDtypeStruct((t, TOP_K), F32)],
        grid=(t // tm,),
        in_specs=[pl.BlockSpec((tm, y.shape[1]), row), pl.BlockSpec((tm, o.shape[1]), row),
                  pl.BlockSpec((tm, d), row), mod_spec, mod_spec, mod_spec, _const_spec((1, d)),
                  _const_spec(wy.shape), _const_spec(wo.shape), _const_spec(w_router.shape),
                  _const_spec((1, n_e))],
        out_specs=[pl.BlockSpec((tm, d), row), pl.BlockSpec((tm, d), row),
                   pl.BlockSpec((tm, TOP_K), row), pl.BlockSpec((tm, TOP_K), row)],
        compiler_params=_params("arbitrary"),
        name="outproj",
    )(y, o, x2, gt, sc, sh, g, wy, wo, w_router, b_router.reshape(1, n_e))


def _route_kernel(ti_ref, dest_ref, be_ref, pend_ref, cnt_ref, run_ref, *, n_experts, n_blocks_pad):
    phase = pl.program_id(0)
    i = pl.program_id(1)
    tr = ti_ref.shape[0]
    ti = ti_ref[...]
    lane = lax.broadcasted_iota(I32, (tr, LANES), 1)
    onehots = [ti[:, k:k + 1] == lane for k in range(TOP_K)]
    cnt = functools.reduce(lambda a, b: a + b, [oh.astype(F32) for oh in onehots])
    tile_counts = jnp.sum(cnt, axis=0, keepdims=True)

    @pl.when(jnp.logical_and(phase == 0, i == 0))
    def _():
        cnt_ref[...] = jnp.zeros(cnt_ref.shape, F32)

    @pl.when(phase == 0)
    def _():
        cnt_ref[...] = cnt_ref[...] + tile_counts

    @pl.when(jnp.logical_and(phase == 1, i == 0))
    def _():
        counts = cnt_ref[...]
        padded = jnp.ceil(counts / EXPERT_BLOCK) * EXPERT_BLOCK
        ri = lax.broadcasted_iota(I32, (LANES, LANES), 0)
        ci = lax.broadcasted_iota(I32, (LANES, LANES), 1)
        pend = jnp.dot(padded, (ri <= ci).astype(F32), precision=HIGHEST, preferred_element_type=F32)
        pend_ref[...] = pend
        run_ref[...] = pend - padded
        start = (lax.broadcasted_iota(I32, (n_blocks_pad, LANES), 0) * EXPERT_BLOCK).astype(F32)
        col = lax.broadcasted_iota(I32, (n_blocks_pad, LANES), 1)
        ended = jnp.logical_and(pend[0:1, :] <= start, col < n_experts)
        be = jnp.sum(ended.astype(F32), axis=1, keepdims=True)
        be_ref[...] = jnp.minimum(be, n_experts - 1).astype(I32)

    @pl.when(phase == 1)
    def _():
        ri = lax.broadcasted_iota(I32, (tr, tr), 0)
        ci = lax.broadcasted_iota(I32, (tr, tr), 1)
        before = jnp.dot((ri > ci).astype(BF16), cnt.astype(BF16), preferred_element_type=F32)
        base = run_ref[0:1, :] + before
        dest = [jnp.sum(jnp.where(oh, base, 0.0), axis=1, keepdims=True) for oh in onehots]
        dest_ref[...] = jnp.concatenate(dest, axis=1).astype(I32)
        run_ref[...] = run_ref[...] + tile_counts


def _route(topi, n_experts, n_blocks):
    t = topi.shape[0]
    tr = min(ROUTE_TILE, t)
    n_blocks_pad = -(-n_blocks // SUBLANES) * SUBLANES
    kern = functools.partial(_route_kernel, n_experts=n_experts, n_blocks_pad=n_blocks_pad)
    return pl.pallas_call(
        kern,
        out_shape=[jax.ShapeDtypeStruct((t, TOP_K), I32), jax.ShapeDtypeStruct((n_blocks_pad, 1), I32),
                   jax.ShapeDtypeStruct((SUBLANES, LANES), F32)],
        grid=(2, t // tr),
        in_specs=[pl.BlockSpec((tr, TOP_K), lambda p, i: (i, 0))],
        out_specs=[pl.BlockSpec((tr, TOP_K), lambda p, i: (i * p, 0)),
                   pl.BlockSpec((n_blocks_pad, 1), lambda p, i: (0, 0)),
                   pl.BlockSpec((SUBLANES, LANES), lambda p, i: (0, 0))],
        scratch_shapes=[pltpu.VMEM((SUBLANES, LANES), F32), pltpu.VMEM((SUBLANES, LANES), F32)],
        compiler_params=_params("arbitrary", "arbitrary"),
        name="route",
    )(topi)


def _dispatch_kernel(pend_ref, dest_hbm, h_ref, xs_hbm, idx_ref, zero_ref, idx_sem, row_sem, *, n_experts):
    i = pl.program_id(0)
    n_idx = idx_ref.shape[0]
    tg = n_idx // TOP_K

    @pl.when(i == 0)
    def _():
        zero_ref[...] = jnp.zeros(zero_ref.shape, zero_ref.dtype)
        for e in range(n_experts):
            end = pend_ref[e]
            prev = pend_ref[e - 1] if e > 0 else 0

            @pl.when(end > prev)
            def _():
                start = pl.multiple_of(end - EXPERT_BLOCK, EXPERT_BLOCK)
                cp = pltpu.make_async_copy(zero_ref, xs_hbm.at[pl.ds(start, EXPERT_BLOCK)], row_sem)
                cp.start()
                cp.wait()

        n_blocks = xs_hbm.shape[0] // EXPERT_BLOCK
        total = pend_ref[n_experts - 1]
        for b in range(n_blocks - n_experts, n_blocks):
            @pl.when(b * EXPERT_BLOCK >= total)
            def _():
                cp = pltpu.make_async_copy(zero_ref, xs_hbm.at[pl.ds(b * EXPERT_BLOCK, EXPERT_BLOCK)], row_sem)
                cp.start()
                cp.wait()

    idx_cp = pltpu.make_async_copy(dest_hbm.at[pl.ds(i * n_idx, n_idx)], idx_ref, idx_sem)
    idx_cp.start()
    idx_cp.wait()

    def issue(tl, carry):
        src = h_ref.at[pl.ds(tl, 1)]
        for k in range(TOP_K):
            d = idx_ref[tl * TOP_K + k]
            pltpu.make_async_copy(src, xs_hbm.at[pl.ds(d, 1)], row_sem).start(priority=k % 2)
        return carry

    lax.fori_loop(0, tg, issue, 0)
    for _ in range(TOP_K):
        pltpu.make_async_copy(h_ref, xs_hbm.at[pl.ds(0, tg)], row_sem).wait()


def _dispatch(pend_i, dest_flat, h, n_rows, n_experts):
    t, w = h.shape
    tg = min(DISPATCH_TILE, t)
    kern = functools.partial(_dispatch_kernel, n_experts=n_experts)
    return pl.pallas_call(
        kern,
        out_shape=jax.ShapeDtypeStruct((n_rows, w), h.dtype),
        grid_spec=pltpu.PrefetchScalarGridSpec(
            num_scalar_prefetch=1,
            grid=(t // tg,),
            in_specs=[pl.BlockSpec(memory_space=pl.ANY), pl.BlockSpec((tg, w), lambda i, pend: (i, 0))],
            out_specs=pl.BlockSpec(memory_space=pl.ANY),
            scratch_shapes=[pltpu.SMEM((tg * TOP_K,), I32), pltpu.VMEM((EXPERT_BLOCK, w), h.dtype),
                            pltpu.SemaphoreType.DMA, pltpu.SemaphoreType.DMA]),
        compiler_params=pltpu.CompilerParams(dimension_semantics=("arbitrary",), has_side_effects=True,
                                             vmem_limit_bytes=VMEM_LIMIT_BYTES),
        name="dispatch",
    )(pend_i, dest_flat, h)


def _expert_kernel(be_ref, nu_ref, xs_ref, wg_ref, bg_ref, wu_ref, bu_ref, wd_ref, bd_ref, y_ref,
                   wg_b, wu_b, wd_b):
    i = pl.program_id(0)
    used = i < nu_ref[0]

    @pl.when(jnp.logical_and(used, jnp.logical_or(i == 0, be_ref[i] != be_ref[jnp.maximum(i - 1, 0)])))
    def _():
        wg_b[...] = wg_ref[...].astype(BF16)
        wu_b[...] = wu_ref[...].astype(BF16)
        wd_b[...] = wd_ref[...].astype(BF16)

    @pl.when(used)
    def _():
        x = xs_ref[...].astype(BF16)
        gate = jnp.minimum(jnp.dot(x, wg_b[...], preferred_element_type=F32) + bg_ref[...], SWIGLU_LIMIT)
        up = jnp.clip(jnp.dot(x, wu_b[...], preferred_element_type=F32) + bu_ref[...],
                      -SWIGLU_LIMIT, SWIGLU_LIMIT)
        glu = gate / (1.0 + jnp.exp(-SWIGLU_ALPHA * gate))
        mid = ((up + 1.0) * glu).astype(BF16)
        y = jnp.dot(mid, wd_b[...], preferred_element_type=F32) + bd_ref[...]
        y_ref[...] = y

    @pl.when(jnp.logical_not(used))
    def _():
        y_ref[...] = jnp.zeros(y_ref.shape, y_ref.dtype)


def _experts(block_e, n_used, xs, w_gate, b_gate, w_up, b_up, w_down, b_down):
    n_rows, w = xs.shape
    n_e, d, f = w_gate.shape
    nb = n_rows // EXPERT_BLOCK
    last = lambda i, be, nu: jnp.minimum(i, nu[0] - 1)
    wspec = lambda a, b: pl.BlockSpec((None, a, b), lambda i, be, nu: (be[last(i, be, nu)], 0, 0))
    return pl.pallas_call(
        _expert_kernel,
        out_shape=jax.ShapeDtypeStruct((n_rows, d), F32),
        grid_spec=pltpu.PrefetchScalarGridSpec(
            num_scalar_prefetch=2,
            grid=(nb,),
            in_specs=[pl.BlockSpec((EXPERT_BLOCK, w), lambda i, be, nu: (last(i, be, nu), 0)),
                      wspec(d, f), wspec(1, f), wspec(d, f), wspec(1, f), wspec(f, d), wspec(1, d)],
            out_specs=pl.BlockSpec((EXPERT_BLOCK, d), lambda i, be, nu: (i, 0)),
            scratch_shapes=[pltpu.VMEM((d, f), BF16), pltpu.VMEM((d, f), BF16), pltpu.VMEM((f, d), BF16)]),
        compiler_params=_params("arbitrary"),
        name="experts",
    )(block_e, n_used, xs, w_gate, b_gate.reshape(n_e, 1, f), w_up, b_up.reshape(n_e, 1, f),
      w_down, b_down.reshape(n_e, 1, d))


def _combine_kernel(dest_hbm, ys_hbm, tw_ref, x1_ref, gt_ref, g_ref, o_ref, idx_ref, buf_ref, idx_sem, row_sem):
    i = pl.program_id(0)
    n_idx = idx_ref.shape[0]
    tc = n_idx // TOP_K
    idx_cp = pltpu.make_async_copy(dest_hbm.at[pl.ds(i * n_idx, n_idx)], idx_ref, idx_sem)
    idx_cp.start()
    idx_cp.wait()

    def issue(tl, carry):
        for k in range(TOP_K):
            d = idx_ref[tl * TOP_K + k]
            pltpu.make_async_copy(ys_hbm.at[pl.ds(d, 1)], buf_ref.at[k, pl.ds(tl, 1)], row_sem).start(priority=k % 2)
        return carry

    lax.fori_loop(0, tc, issue, 0)
    for k in range(TOP_K):
        pltpu.make_async_copy(ys_hbm.at[pl.ds(0, tc)], buf_ref.at[k], row_sem).wait()

    tw = tw_ref[...]
    ffn = tw[:, 0:1] * buf_ref[0]
    for k in range(1, TOP_K):
        ffn = ffn + tw[:, k:k + 1] * buf_ref[k]
    x2 = x1_ref[...] + gt_ref[0] * ffn
    o_ref[...] = _rms(x2) * g_ref[...]


def _combine(dest_flat, ys, topw, x1, gt, g, seq):
    t, d = x1.shape
    tc = min(COMBINE_TILE, seq)
    per_batch = seq // tc
    row = lambda i: (i, 0)
    return pl.pallas_call(
        _combine_kernel,
        out_shape=jax.ShapeDtypeStruct((t, d), F32),
        grid=(t // tc,),
        in_specs=[pl.BlockSpec(memory_space=pl.ANY), pl.BlockSpec(memory_space=pl.ANY),
                  pl.BlockSpec((tc, TOP_K), row), pl.BlockSpec((tc, d), row),
                  pl.BlockSpec((1, 1, d), lambda i: (i // per_batch, 0, 0)), _const_spec((1, d))],
        out_specs=pl.BlockSpec((tc, d), row),
        scratch_shapes=[pltpu.SMEM((tc * TOP_K,), I32), pltpu.VMEM((TOP_K, tc, d), F32),
                        pltpu.SemaphoreType.DMA, pltpu.SemaphoreType.DMA],
        compiler_params=_params("arbitrary"),
        name="combine",
    )(dest_flat, ys, topw, x1, gt, g)


def _layer(x2, mod, batch, seq, norm1_g, w_in, conv_w, conv_b, dt_bias, a_log, d_skip, ssm_norm_g,
           gla_wg2, gla_bg, gla_norm_g, w_out, norm2_g, w_router, b_router,
           w_gate, b_gate, w_up, b_up, w_down, b_down):
    t, d = x2.shape
    n_heads = dt_bias.shape[0]
    d_ssm = n_heads * SSM_HEAD_DIM
    cd = conv_w.shape[1]
    dkt = gla_wg2.shape[1]
    dvt = w_out.shape[0] - d_ssm
    n_experts = w_router.shape[1]

    sh1, sc1, gt1, sh2, sc2, gt2 = [m.reshape(batch, 1, d) for m in jnp.split(mod[:batch], 6, axis=1)]

    sizes = (d_ssm, cd, n_heads, dkt, dkt, dvt, GLA_GATE_RANK, dvt)
    offs = [0]
    for s in sizes:
        offs.append(offs[-1] + s)
    col = lambda j: w_in[:, offs[j]:offs[j + 1]]
    pad = jnp.zeros((d, LANES - n_heads - GLA_GATE_RANK), F32)
    w_small = jnp.concatenate([col(2), col(6), pad], axis=1)
    weights = [col(0), col(1), col(3), col(4), col(5), col(7), w_small]
    weights = [w.astype(BF16) for w in weights]
    z, xbc, q, k, v, r, small = _inproj(x2, sc1, sh1, norm1_g.reshape(1, d), weights,
                                        [BF16] * 6 + [F32], seq)

    d_skip_e = jnp.repeat(d_skip, SSM_HEAD_DIM).reshape(1, d_ssm)
    y = _ssd(xbc, z, small, conv_w, conv_b, dt_bias, a_log, d_skip_e, ssm_norm_g, batch, seq)
    o = _gla(q, k, v, r, small, gla_wg2, gla_bg, gla_norm_g, batch, seq, gate_col=n_heads)

    x1, h2, topi, topw = _outproj(y, o, x2, gt1, sc2, sh2, norm2_g.reshape(1, d),
                                        w_out[:d_ssm].astype(BF16), w_out[d_ssm:].astype(BF16),
                                        w_router, b_router, seq)

    n_blocks = (t * TOP_K) // EXPERT_BLOCK + n_experts
    dest, block_e, pend = _route(topi, n_experts, n_blocks)
    pend_i = pend[0, :n_experts].astype(I32)
    n_used = (pend_i[n_experts - 1:] // EXPERT_BLOCK).astype(I32)
    dest_flat = dest.reshape(t * TOP_K)
    xs = _dispatch(pend_i, dest_flat, h2, n_blocks * EXPERT_BLOCK, n_experts)
    ys = _experts(block_e[:n_blocks, 0], n_used, xs, w_gate, b_gate, w_up, b_up, w_down, b_down)
    return dest_flat, ys, topw, x1, gt2


def kernel(x, c, w_ada, b_ada, norm1_g, w_in, conv_w, conv_b, dt_bias, a_log, d_skip, ssm_norm_g, gla_wg2,
           gla_bg, gla_norm_g, w_out, norm2_g, w_router, b_router, w_gate, b_gate, w_up, b_up, w_down, b_down,
           final_norm_g):
    batch, seq, d = x.shape
    assert w_ada.shape[0] == 1, "single-layer trunk"
    assert seq % min(seq, max(TOKEN_TILE, SSM_CHUNK, GLA_STEP_ROWS, COMBINE_TILE, DISPATCH_TILE)) == 0
    assert seq % max(SSM_CHUNK, GLA_STEP_ROWS) == 0
    x2 = x.reshape(batch * seq, d)
    c_pad = jnp.zeros((SUBLANES, d), F32).at[:batch].set(c)
    mod = _ada(c_pad, w_ada[0], b_ada)
    dest_flat, ys, topw, x1, gt2 = _layer(
        x2, mod, batch, seq, norm1_g[0], w_in[0], conv_w[0], conv_b[0], dt_bias[0], a_log[0], d_skip[0],
        ssm_norm_g[0], gla_wg2[0], gla_bg[0], gla_norm_g[0], w_out[0], norm2_g[0], w_router[0], b_router[0],
        w_gate[0], b_gate[0], w_up[0], b_up[0], w_down[0], b_down[0])
    out = _combine(dest_flat, ys, topw, x1, gt2, final_norm_g.reshape(1, d), seq)
    return out.reshape(batch, seq, d)
```

```python
import functools

import jax
import jax.numpy as jnp
from jax import lax
from jax.experimental import pallas as pl
from jax.experimental.pallas import tpu as pltpu

F32 = jnp.float32
BF16 = jnp.bfloat16
I32 = jnp.int32
HIGHEST = lax.Precision.HIGHEST

EPS = 1e-6
SSM_HEAD_DIM = 64
SSM_GROUPS = 2
SSM_STATE = 128
SSM_CONV = 4
SSM_CHUNK = 128
GLA_HEADS = 4
GLA_GATE_RANK = 16
GLA_GATE_NORM = 16.0
GLA_CHUNK = 64
TOP_K = 4
SWIGLU_LIMIT = 7.0
SWIGLU_ALPHA = 1.702

LANES = 128
SUBLANES = 8
VMEM_LIMIT_BYTES = 56 * 1024 * 1024

TOKEN_TILE = 512
GLA_STEP_ROWS = 128
ROUTE_TILE = 512
EXPERT_BLOCK = 256
DISPATCH_TILE = 512
COMBINE_TILE = 256

_NT = (((1,), (1,)), ((), ()))
_TN = (((0,), (0,)), ((), ()))


def _silu(v):
    return v / (1.0 + jnp.exp(-v))


def _softplus(v):
    return jnp.maximum(v, 0.0) + jnp.log1p(jnp.exp(-jnp.abs(v)))


def _log_sigmoid(v):
    return jnp.minimum(v, 0.0) - jnp.log1p(jnp.exp(-jnp.abs(v)))


def _rms(v):
    return v * lax.rsqrt(jnp.mean(v * v, axis=-1, keepdims=True) + EPS)


def _pieces(a, n):
    out = []
    for _ in range(n - 1):
        p = a.astype(BF16)
        out.append(p)
        a = a - p.astype(F32)
    out.append(a.astype(BF16))
    return out


def _dot_pieces(a, b, n, dims=None):
    dims = dims or (((a.ndim - 1,), (0,)), ((), ()))
    return sum(lax.dot_general(p, b, dims, preferred_element_type=F32) for p in _pieces(a, n))


def _token_rows_load(ref, rows):
    return jnp.concatenate([ref[pl.ds(s, rows, stride=SUBLANES), :] for s in range(SUBLANES)], axis=1)


def _token_rows_store(ref, v):
    rows = v.shape[0]
    for s in range(SUBLANES):
        ref[pl.ds(s, rows, stride=SUBLANES), :] = v[:, s * LANES:(s + 1) * LANES]


def _params(*semantics):
    return pltpu.CompilerParams(dimension_semantics=semantics, vmem_limit_bytes=VMEM_LIMIT_BYTES)


def _const_spec(shape):
    nd = len(shape)
    return pl.BlockSpec(shape, lambda *_: (0,) * nd)


def _ada_kernel(c_ref, w_ref, b_ref, o_ref):
    cond = _silu(c_ref[...])
    o_ref[...] = jnp.dot(cond, w_ref[...], precision=HIGHEST, preferred_element_type=F32) + b_ref[...]


def _ada(c_pad, w_ada, b_ada):
    rows, d = c_pad.shape
    n = w_ada.shape[1]
    tn = d
    return pl.pallas_call(
        _ada_kernel,
        out_shape=jax.ShapeDtypeStruct((rows, n), F32),
        grid=(n // tn,),
        in_specs=[pl.BlockSpec((rows, d), lambda j: (0, 0)),
                  pl.BlockSpec((d, tn), lambda j: (0, j)),
                  pl.BlockSpec((1, tn), lambda j: (0, j))],
        out_specs=pl.BlockSpec((rows, tn), lambda j: (0, j)),
        compiler_params=_params("arbitrary"),
        name="ada",
    )(c_pad, w_ada, b_ada)


def _inproj_kernel(x_ref, sc_ref, sh_ref, g_ref, *refs):
    n_w = len(refs) // 2
    w_refs, o_refs = refs[:n_w], refs[n_w:]
    h = (_rms(x_ref[...]) * g_ref[...]) * (1.0 + sc_ref[0]) + sh_ref[0]
    hb = h.astype(BF16)
    for w_ref, o_ref in zip(w_refs, o_refs):
        o_ref[...] = jnp.dot(hb, w_ref[...], preferred_element_type=F32).astype(o_ref.dtype)


def _inproj(x2, sc, sh, g, weights, out_dtypes, seq):
    t, d = x2.shape
    tm = min(TOKEN_TILE, seq)
    per_batch = seq // tm
    mod_spec = pl.BlockSpec((1, 1, d), lambda i: (i // per_batch, 0, 0))
    return pl.pallas_call(
        _inproj_kernel,
        out_shape=[jax.ShapeDtypeStruct((t, w.shape[1]), dt) for w, dt in zip(weights, out_dtypes)],
        grid=(t // tm,),
        in_specs=[pl.BlockSpec((tm, d), lambda i: (i, 0)), mod_spec, mod_spec, _const_spec((1, d))]
                 + [_const_spec(w.shape) for w in weights],
        out_specs=[pl.BlockSpec((tm, w.shape[1]), lambda i: (i, 0)) for w in weights],
        compiler_params=_params("arbitrary"),
        name="inproj",
    )(x2, sc, sh, g, *weights)


def _ssd_kernel(xbc_ref, z_ref, sm_ref, cw_ref, cb_ref, dtb_r_ref, dtb_c_ref, alog_r_ref, alog_c_ref,
                dsk_ref, g_ref, y_ref, ext_ref, st_ref, *, n_heads, d_ssm):
    L = SSM_CHUNK
    P = SSM_HEAD_DIM
    N = SSM_STATE
    G = SSM_GROUPS
    gw = d_ssm // G
    c = pl.program_id(1)

    @pl.when(c == 0)
    def _():
        ext_ref[0:SUBLANES, :] = jnp.zeros((SUBLANES, ext_ref.shape[1]), F32)
        st_ref[...] = jnp.zeros(st_ref.shape, F32)

    ext_ref[SUBLANES:SUBLANES + L, :] = xbc_ref[...].astype(F32)
    acc = cb_ref[...]
    for j in range(SSM_CONV):
        off = SUBLANES - (SSM_CONV - 1) + j
        acc = acc + cw_ref[j:j + 1, :] * ext_ref[off:off + L, :]
    ext_ref[0:SUBLANES, :] = ext_ref[L:L + SUBLANES, :]
    act = _silu(acc)
    xs = act[:, :d_ssm]
    bm = act[:, d_ssm:d_ssm + G * N].astype(BF16)
    cm = act[:, d_ssm + G * N:].astype(BF16)

    sm = sm_ref[...]
    dt_col = _softplus(sm[:, :n_heads] + dtb_r_ref[...])
    dt_row = _softplus(sm.T[:n_heads, :] + dtb_c_ref[...])
    da_col = dt_col * (-jnp.exp(alog_r_ref[...]))
    da_row = dt_row * (-jnp.exp(alog_c_ref[...]))
    ri = lax.broadcasted_iota(I32, (L, L), 0)
    ci = lax.broadcasted_iota(I32, (L, L), 1)
    causal = ri >= ci
    cs_col = sum(jnp.dot(causal.astype(BF16), p, preferred_element_type=F32) for p in _pieces(da_col, 3))
    cs_row = _dot_pieces(da_row, (ri <= ci).astype(BF16), 3)
    cs_last = cs_col[L - 1:L, :]

    hh = lax.broadcasted_iota(I32, (n_heads, d_ssm), 0)
    jj = lax.broadcasted_iota(I32, (n_heads, d_ssm), 1)
    expand = ((jj // P) == hh).astype(BF16)
    per_head = jnp.concatenate([dt_col, jnp.exp(cs_col), jnp.exp(cs_last - cs_col)], axis=0)
    per_ch = _dot_pieces(per_head, expand, 2)
    dt_e, ecs_e, dte_e = per_ch[0:L], per_ch[L:2 * L], per_ch[2 * L:3 * L]
    chunk_decay = ecs_e[L - 1:L, :]

    xdt = xs * dt_e
    xdt_b = xdt.astype(BF16)
    xdec_b = (xdt * dte_e).astype(BF16)

    lane = lax.broadcasted_iota(I32, (L, LANES), 1)
    first_half = lane < P
    heads_per_group = n_heads // G
    y_parts = []
    y_off_parts = []
    for g in range(G):
        bm_g = bm[:, g * N:(g + 1) * N]
        cm_g = cm[:, g * N:(g + 1) * N]
        cb = lax.dot_general(cm_g, bm_g, _NT, preferred_element_type=F32)
        prev = st_ref[g]
        y_off_parts.append(jnp.dot(cm_g, prev.astype(BF16), preferred_element_type=F32))
        s_new = lax.dot_general(bm_g, xdec_b[:, g * gw:(g + 1) * gw], _TN, preferred_element_type=F32)
        st_ref[g] = prev * chunk_decay[:, g * gw:(g + 1) * gw] + s_new
        for p in range(heads_per_group // 2):
            h0 = g * heads_per_group + 2 * p
            ms = []
            for h in (h0, h0 + 1):
                diff = cs_col[:, h:h + 1] - cs_row[h:h + 1, :]
                ms.append((cb * jnp.exp(jnp.where(causal, diff, -jnp.inf))).astype(BF16))
            lhs = jnp.concatenate(ms, axis=1)
            xp = xdt_b[:, h0 * P:(h0 + 2) * P]
            zero = jnp.zeros_like(xp)
            rhs = jnp.concatenate([jnp.where(first_half, xp, zero), jnp.where(first_half, zero, xp)], axis=0)
            y_parts.append(jnp.dot(lhs, rhs, preferred_element_type=F32))
    y = jnp.concatenate(y_parts, axis=1) + jnp.concatenate(y_off_parts, axis=1) * ecs_e + dsk_ref[...] * xs
    y = y * _silu(z_ref[...].astype(F32))
    y = jnp.concatenate([_rms(y[:, g * gw:(g + 1) * gw]) for g in range(G)], axis=1) * g_ref[...]
    y_ref[...] = y.astype(y_ref.dtype)


def _ssd(xbc, z, small, conv_w, conv_b, dt_bias, a_log, d_skip_e, norm_g, batch, seq):
    t, cd = xbc.shape
    d_ssm = z.shape[1]
    n_heads = dt_bias.shape[0]
    L = SSM_CHUNK
    nc = seq // L
    row = lambda b, c: (b * nc + c, 0)
    kern = functools.partial(_ssd_kernel, n_heads=n_heads, d_ssm=d_ssm)
    return pl.pallas_call(
        kern,
        out_shape=jax.ShapeDtypeStruct((t, d_ssm), BF16),
        grid=(batch, nc),
        in_specs=[pl.BlockSpec((L, cd), row), pl.BlockSpec((L, d_ssm), row), pl.BlockSpec((L, LANES), row),
                  _const_spec(conv_w.shape), _const_spec((1, cd)),
                  _const_spec((1, n_heads)), _const_spec((n_heads, 1)),
                  _const_spec((1, n_heads)), _const_spec((n_heads, 1)),
                  _const_spec((1, d_ssm)), _const_spec((1, d_ssm))],
        out_specs=pl.BlockSpec((L, d_ssm), row),
        scratch_shapes=[pltpu.VMEM((L + 2 * SUBLANES, cd), F32),
                        pltpu.VMEM((SSM_GROUPS, SSM_STATE, d_ssm // SSM_GROUPS), F32)],
        compiler_params=_params("arbitrary", "arbitrary"),
        name="ssd",
    )(xbc, z, small, conv_w, conv_b.reshape(1, cd), dt_bias.reshape(1, n_heads), dt_bias.reshape(n_heads, 1),
      a_log.reshape(1, n_heads), a_log.reshape(n_heads, 1), d_skip_e, norm_g.reshape(1, d_ssm))


def _gla_kernel(q_ref, k_ref, v_ref, r_ref, sm_ref, wg2_ref, bg_ref, gn_ref, o_ref, st_ref, *, gate_col):
    L = GLA_CHUNK
    H = GLA_HEADS
    dk = q_ref.shape[1] // H
    dv = v_ref.shape[1] // H
    c = pl.program_id(1)

    @pl.when(c == 0)
    def _():
        st_ref[...] = jnp.zeros(st_ref.shape, F32)

    ri = lax.broadcasted_iota(I32, (L, L), 0)
    ci = lax.broadcasted_iota(I32, (L, L), 1)
    causal = ri >= ci
    tril = causal.astype(BF16)
    last_row = (lax.broadcasted_iota(I32, (L, LANES), 0) == L - 1).astype(BF16)
    wg2_hi, wg2_lo = _pieces(wg2_ref[...], 2)

    for s in range(q_ref.shape[0] // L):
        rows = slice(s * L, (s + 1) * L)
        q = q_ref[rows, :].astype(F32) * (dk ** -0.5)
        k = k_ref[rows, :].astype(F32)
        v = v_ref[rows, :]
        r = r_ref[rows, :].astype(F32)
        g_low = sm_ref[rows, gate_col:gate_col + GLA_GATE_RANK]
        g_hi, g_lo = _pieces(g_low, 2)
        pre = (jnp.dot(g_hi, wg2_hi, preferred_element_type=F32) + jnp.dot(g_lo, wg2_hi, preferred_element_type=F32)
               + jnp.dot(g_hi, wg2_lo, preferred_element_type=F32)) + bg_ref[...]
        gk = _log_sigmoid(pre) / GLA_GATE_NORM
        bcum = sum(jnp.dot(tril, p, preferred_element_type=F32) for p in _pieces(gk, 3))
        b_last = bcum[L - 1:L, :]
        q_t = (q * jnp.exp(bcum)).astype(BF16)
        k_t = (k * jnp.exp(-bcum)).astype(BF16)
        k_dec = (k * jnp.exp(b_last - bcum)).astype(BF16)
        dcol = jnp.exp(_dot_pieces(bcum, last_row, 3, dims=_TN))
        outs = []
        for h in range(H):
            ks = slice(h * dk, (h + 1) * dk)
            vs = slice(h * dv, (h + 1) * dv)
            att = lax.dot_general(q_t[:, ks], k_t[:, ks], _NT, preferred_element_type=F32)
            att = jnp.where(causal, att, 0.0).astype(BF16)
            prev = st_ref[h]
            o = (jnp.dot(att, v[:, vs], preferred_element_type=F32)
                 + jnp.dot(q_t[:, ks], prev.astype(BF16), preferred_element_type=F32))
            s_new = lax.dot_general(k_dec[:, ks], v[:, vs], _TN, preferred_element_type=F32)
            dec = dcol[ks, :]
            st_ref[h] = prev * jnp.concatenate([dec] * (dv // LANES), axis=1) + s_new
            outs.append(_rms(o) * gn_ref[...] * _silu(r[:, vs]))
        o_ref[rows, :] = jnp.concatenate(outs, axis=1).astype(o_ref.dtype)


def _gla(q, k, v, r, small, wg2, bg, norm_g, batch, seq, gate_col):
    t, dkt = q.shape
    dvt = v.shape[1]
    rows = GLA_STEP_ROWS
    nc = seq // rows
    row = lambda b, c: (b * nc + c, 0)
    kern = functools.partial(_gla_kernel, gate_col=gate_col)
    return pl.pallas_call(
        kern,
        out_shape=jax.ShapeDtypeStruct((t, dvt), BF16),
        grid=(batch, nc),
        in_specs=[pl.BlockSpec((rows, dkt), row), pl.BlockSpec((rows, dkt), row), pl.BlockSpec((rows, dvt), row),
                  pl.BlockSpec((rows, dvt), row), pl.BlockSpec((rows, LANES), row),
                  _const_spec(wg2.shape), _const_spec((1, dkt)), _const_spec((1, dvt // GLA_HEADS))],
        out_specs=pl.BlockSpec((rows, dvt), row),
        scratch_shapes=[pltpu.VMEM((GLA_HEADS, dkt // GLA_HEADS, dvt // GLA_HEADS), F32)],
        compiler_params=_params("arbitrary", "arbitrary"),
        name="gla",
    )(q, k, v, r, small, wg2, bg.reshape(1, dkt), norm_g.reshape(1, dvt // GLA_HEADS))


def _outproj_kernel(y_ref, o_ref, x_ref, gt_ref, sc_ref, sh_ref, g_ref, wy_ref, wo_ref, wr_ref, br_ref,
                    x1_ref, h_ref, ti_ref, tw_ref):
    mix = (jnp.dot(y_ref[...], wy_ref[...], preferred_element_type=F32)
           + jnp.dot(o_ref[...], wo_ref[...], preferred_element_type=F32))
    x1 = x_ref[...] + gt_ref[0] * mix
    x1_ref[...] = x1
    h = (_rms(x1) * g_ref[...]) * (1.0 + sc_ref[0]) + sh_ref[0]
    _token_rows_store(h_ref, h)
    n_e = br_ref.shape[1]
    h_hi, h_lo = _pieces(h, 2)
    wr = wr_ref[...]
    hw = jnp.dot(h_hi, wr, preferred_element_type=F32)
    logits = (hw[:, :n_e] + hw[:, n_e:] + jnp.dot(h_lo, wr[:, :n_e], preferred_element_type=F32)) + br_ref[...]
    lane = lax.broadcasted_iota(I32, logits.shape, 1)
    vals, idxs = [], []
    for _ in range(TOP_K):
        m = jnp.max(logits, axis=1, keepdims=True)
        idx = jnp.min(jnp.where(logits == m, lane, n_e), axis=1, keepdims=True)
        vals.append(m)
        idxs.append(idx)
        logits = jnp.where(lane == idx, -jnp.inf, logits)
    exps = [jnp.exp(v - vals[0]) for v in vals]
    denom = functools.reduce(lambda a, b: a + b, exps)
    ti_ref[...] = jnp.concatenate(idxs, axis=1)
    tw_ref[...] = jnp.concatenate([e / denom for e in exps], axis=1)


def _outproj(y, o, x2, gt, sc, sh, g, wy, wo, w_router, b_router, seq):
    t, d = x2.shape
    n_e = w_router.shape[1]
    tm = min(TOKEN_TILE, seq)
    per_batch = seq // tm
    row = lambda i: (i, 0)
    mod_spec = pl.BlockSpec((1, 1, d), lambda i: (i // per_batch, 0, 0))
    wr_hi = w_router.astype(BF16)
    wr_cat = jnp.concatenate([wr_hi, (w_router - wr_hi.astype(F32)).astype(BF16)], axis=1)
    return pl.pallas_call(
        _outproj_kernel,
        out_shape=[jax.ShapeDtypeStruct((t, d), F32), jax.ShapeDtypeStruct((t * SUBLANES, LANES), F32),
                   jax.ShapeDtypeStruct((t, TOP_K), I32), jax.ShapeDtypeStruct((t, TOP_K), F32)],
        grid=(t // tm,),
        in_specs=[pl.BlockSpec((tm, y.shape[1]), row), pl.BlockSpec((tm, o.shape[1]), row),
                  pl.BlockSpec((tm, d), row), mod_spec, mod_spec, mod_spec, _const_spec((1, d)),
                  _const_spec(wy.shape), _const_spec(wo.shape), _const_spec(wr_cat.shape),
                  _const_spec((1, n_e))],
        out_specs=[pl.BlockSpec((tm, d), row), pl.BlockSpec((tm * SUBLANES, LANES), row),
                   pl.BlockSpec((tm, TOP_K), row), pl.BlockSpec((tm, TOP_K), row)],
        compiler_params=_params("arbitrary"),
        name="outproj",
    )(y, o, x2, gt, sc, sh, g, wy, wo, wr_cat, b_router.reshape(1, n_e))


def _route_kernel(ti_ref, dest_ref, be_ref, pend_ref, cnt_ref, run_ref, *, n_experts, n_blocks_pad):
    phase = pl.program_id(0)
    i = pl.program_id(1)
    tr = ti_ref.shape[0]
    ti = ti_ref[...]
    lane = lax.broadcasted_iota(I32, (tr, LANES), 1)
    onehots = [ti[:, k:k + 1] == lane for k in range(TOP_K)]
    cnt = functools.reduce(lambda a, b: a + b, [oh.astype(F32) for oh in onehots])
    tile_counts = jnp.sum(cnt, axis=0, keepdims=True)

    @pl.when(jnp.logical_and(phase == 0, i == 0))
    def _():
        cnt_ref[...] = jnp.zeros(cnt_ref.shape, F32)

    @pl.when(phase == 0)
    def _():
        cnt_ref[...] = cnt_ref[...] + tile_counts

    @pl.when(jnp.logical_and(phase == 1, i == 0))
    def _():
        counts = cnt_ref[...]
        padded = jnp.ceil(counts / EXPERT_BLOCK) * EXPERT_BLOCK
        ri = lax.broadcasted_iota(I32, (LANES, LANES), 0)
        ci = lax.broadcasted_iota(I32, (LANES, LANES), 1)
        pend = jnp.dot(padded, (ri <= ci).astype(F32), precision=HIGHEST, preferred_element_type=F32)
        pend_ref[...] = pend
        run_ref[...] = pend - padded
        start = (lax.broadcasted_iota(I32, (n_blocks_pad, LANES), 0) * EXPERT_BLOCK).astype(F32)
        col = lax.broadcasted_iota(I32, (n_blocks_pad, LANES), 1)
        ended = jnp.logical_and(pend[0:1, :] <= start, col < n_experts)
        be = jnp.sum(ended.astype(F32), axis=1, keepdims=True)
        be_ref[...] = jnp.minimum(be, n_experts - 1).astype(I32)

    @pl.when(phase == 1)
    def _():
        ri = lax.broadcasted_iota(I32, (tr, tr), 0)
        ci = lax.broadcasted_iota(I32, (tr, tr), 1)
        before = jnp.dot((ri > ci).astype(BF16), cnt.astype(BF16), preferred_element_type=F32)
        base = run_ref[0:1, :] + before
        dest = [jnp.sum(jnp.where(oh, base, 0.0), axis=1, keepdims=True) for oh in onehots]
        dest_ref[...] = jnp.concatenate(dest, axis=1).astype(I32)
        run_ref[...] = run_ref[...] + tile_counts


def _route(topi, n_experts, n_blocks):
    t = topi.shape[0]
    tr = min(ROUTE_TILE, t)
    n_blocks_pad = -(-n_blocks // SUBLANES) * SUBLANES
    kern = functools.partial(_route_kernel, n_experts=n_experts, n_blocks_pad=n_blocks_pad)
    return pl.pallas_call(
        kern,
        out_shape=[jax.ShapeDtypeStruct((t, TOP_K), I32), jax.ShapeDtypeStruct((n_blocks_pad, 1), I32),
                   jax.ShapeDtypeStruct((SUBLANES, LANES), F32)],
        grid=(2, t // tr),
        in_specs=[pl.BlockSpec((tr, TOP_K), lambda p, i: (i, 0))],
        out_specs=[pl.BlockSpec((tr, TOP_K), lambda p, i: (i * p, 0)),
                   pl.BlockSpec((n_blocks_pad, 1), lambda p, i: (0, 0)),
                   pl.BlockSpec((SUBLANES, LANES), lambda p, i: (0, 0))],
        scratch_shapes=[pltpu.VMEM((SUBLANES, LANES), F32), pltpu.VMEM((SUBLANES, LANES), F32)],
        compiler_params=_params("arbitrary", "arbitrary"),
        name="route",
    )(topi)


def _dispatch_kernel(pend_ref, dest_hbm, h_ref, xs_hbm, idx_ref, zero_ref, idx_sem, row_sem, *, n_experts):
    i = pl.program_id(0)
    n_idx = idx_ref.shape[0]
    tg = n_idx // TOP_K

    @pl.when(i == 0)
    def _():
        zero_ref[...] = jnp.zeros(zero_ref.shape, zero_ref.dtype)
        for e in range(n_experts):
            end = pend_ref[e]
            prev = pend_ref[e - 1] if e > 0 else 0

            @pl.when(end > prev)
            def _():
                start = pl.multiple_of((end - EXPERT_BLOCK) * SUBLANES, EXPERT_BLOCK * SUBLANES)
                cp = pltpu.make_async_copy(zero_ref, xs_hbm.at[pl.ds(start, EXPERT_BLOCK * SUBLANES)], row_sem)
                cp.start()
                cp.wait()

        n_blocks = xs_hbm.shape[0] // (EXPERT_BLOCK * SUBLANES)
        total = pend_ref[n_experts - 1]
        for b in range(n_blocks - n_experts, n_blocks):
            @pl.when(b * EXPERT_BLOCK >= total)
            def _():
                cp = pltpu.make_async_copy(
                    zero_ref, xs_hbm.at[pl.ds(b * EXPERT_BLOCK * SUBLANES, EXPERT_BLOCK * SUBLANES)], row_sem)
                cp.start()
                cp.wait()

    idx_cp = pltpu.make_async_copy(dest_hbm.at[pl.ds(i * n_idx, n_idx)], idx_ref, idx_sem)
    idx_cp.start()
    idx_cp.wait()

    def issue(tl, carry):
        src = h_ref.at[pl.ds(pl.multiple_of(tl * SUBLANES, SUBLANES), SUBLANES)]
        for k in range(TOP_K):
            d = pl.multiple_of(idx_ref[tl * TOP_K + k] * SUBLANES, SUBLANES)
            pltpu.make_async_copy(src, xs_hbm.at[pl.ds(d, SUBLANES)], row_sem).start(priority=k % 2)
        return carry

    lax.fori_loop(0, tg, issue, 0)
    for _ in range(TOP_K):
        pltpu.make_async_copy(h_ref, xs_hbm.at[pl.ds(0, tg * SUBLANES)], row_sem).wait()


def _dispatch(pend_i, dest_flat, h, n_rows, n_experts):
    t = h.shape[0] // SUBLANES
    tg = min(DISPATCH_TILE, t)
    kern = functools.partial(_dispatch_kernel, n_experts=n_experts)
    return pl.pallas_call(
        kern,
        out_shape=jax.ShapeDtypeStruct((n_rows * SUBLANES, LANES), h.dtype),
        grid_spec=pltpu.PrefetchScalarGridSpec(
            num_scalar_prefetch=1,
            grid=(t // tg,),
            in_specs=[pl.BlockSpec(memory_space=pl.ANY),
                      pl.BlockSpec((tg * SUBLANES, LANES), lambda i, pend: (i, 0))],
            out_specs=pl.BlockSpec(memory_space=pl.ANY),
            scratch_shapes=[pltpu.SMEM((tg * TOP_K,), I32), pltpu.VMEM((EXPERT_BLOCK * SUBLANES, LANES), h.dtype),
                            pltpu.SemaphoreType.DMA, pltpu.SemaphoreType.DMA]),
        compiler_params=pltpu.CompilerParams(dimension_semantics=("arbitrary",), has_side_effects=True,
                                             vmem_limit_bytes=VMEM_LIMIT_BYTES),
        name="dispatch",
    )(pend_i, dest_flat, h)


def _expert_kernel(be_ref, nu_ref, pend_ref, xs_ref, wg_hbm, bg_ref, wu_hbm, bu_ref, wd_hbm, bd_ref, y_ref,
                   wg_f, wu_f, wd_f, wg_b, wu_b, wd_b, slot_ref, sems):
    i = pl.program_id(0)
    used = i < nu_ref[0]
    e = be_ref[i]

    def fetch(expert, slot):
        return [pltpu.make_async_copy(src.at[expert], dst.at[slot], sems.at[slot])
                for src, dst in ((wg_hbm, wg_f), (wu_hbm, wu_f), (wd_hbm, wd_f))]

    @pl.when(i == 0)
    def _():
        slot_ref[0] = 0
        for cp in fetch(e, 0):
            cp.start()

    first_of_expert = jnp.logical_or(i == 0, e != be_ref[jnp.maximum(i - 1, 0)])

    @pl.when(jnp.logical_and(used, first_of_expert))
    def _():
        slot = slot_ref[0]
        for cp in fetch(e, slot):
            cp.wait()
        wg_b[...] = wg_f[slot].astype(BF16)
        wu_b[...] = wu_f[slot].astype(BF16)
        wd_b[...] = wd_f[slot].astype(BF16)
        nxt = lax.div(pend_ref[e], EXPERT_BLOCK)

        @pl.when(nxt < nu_ref[0])
        def _():
            for cp in fetch(be_ref[nxt], 1 - slot):
                cp.start()

        slot_ref[0] = 1 - slot

    @pl.when(used)
    def _():
        x = _token_rows_load(xs_ref, EXPERT_BLOCK).astype(BF16)
        gate = jnp.minimum(jnp.dot(x, wg_b[...], preferred_element_type=F32) + bg_ref[...], SWIGLU_LIMIT)
        up = jnp.clip(jnp.dot(x, wu_b[...], preferred_element_type=F32) + bu_ref[...],
                      -SWIGLU_LIMIT, SWIGLU_LIMIT)
        glu = gate / (1.0 + jnp.exp(-SWIGLU_ALPHA * gate))
        mid = ((up + 1.0) * glu).astype(BF16)
        y = jnp.dot(mid, wd_b[...], preferred_element_type=F32) + bd_ref[...]
        _token_rows_store(y_ref, y)

    @pl.when(jnp.logical_not(used))
    def _():
        y_ref[...] = jnp.zeros(y_ref.shape, y_ref.dtype)


def _experts(block_e, n_used, pend_i, xs, w_gate, b_gate, w_up, b_up, w_down, b_down):
    n_rows = xs.shape[0] // SUBLANES
    n_e, d, f = w_gate.shape
    nb = n_rows // EXPERT_BLOCK
    blk = (EXPERT_BLOCK * SUBLANES, LANES)
    last = lambda i, be, nu, pend: jnp.maximum(jnp.minimum(i, nu[0] - 1), 0)
    bspec = lambda n: pl.BlockSpec((None, 1, n), lambda i, be, nu, pend: (be[last(i, be, nu, pend)], 0, 0))
    hbm = pl.BlockSpec(memory_space=pl.ANY)
    return pl.pallas_call(
        _expert_kernel,
        out_shape=jax.ShapeDtypeStruct((n_rows * SUBLANES, LANES), F32),
        grid_spec=pltpu.PrefetchScalarGridSpec(
            num_scalar_prefetch=3,
            grid=(nb,),
            in_specs=[pl.BlockSpec(blk, lambda i, be, nu, pend: (last(i, be, nu, pend), 0)),
                      hbm, bspec(f), hbm, bspec(f), hbm, bspec(d)],
            out_specs=pl.BlockSpec(blk, lambda i, be, nu, pend: (i, 0)),
            scratch_shapes=[pltpu.VMEM((2, d, f), F32), pltpu.VMEM((2, d, f), F32), pltpu.VMEM((2, f, d), F32),
                            pltpu.VMEM((d, f), BF16), pltpu.VMEM((d, f), BF16), pltpu.VMEM((f, d), BF16),
                            pltpu.SMEM((1,), I32), pltpu.SemaphoreType.DMA((2,))]),
        compiler_params=_params("arbitrary"),
        name="experts",
    )(block_e, n_used, pend_i, xs, w_gate, b_gate.reshape(n_e, 1, f), w_up, b_up.reshape(n_e, 1, f),
      w_down, b_down.reshape(n_e, 1, d))


def _combine_kernel(dest_hbm, ys_hbm, tw_ref, x1_ref, gt_ref, g_ref, o_ref, idx_ref, buf_ref, idx_sem, row_sem):
    i = pl.program_id(0)
    n_idx = idx_ref.shape[0]
    tc = n_idx // TOP_K
    idx_cp = pltpu.make_async_copy(dest_hbm.at[pl.ds(i * n_idx, n_idx)], idx_ref, idx_sem)
    idx_cp.start()
    idx_cp.wait()

    def issue(tl, carry):
        dst_row = pl.multiple_of(tl * SUBLANES, SUBLANES)
        for k in range(TOP_K):
            d = pl.multiple_of(idx_ref[tl * TOP_K + k] * SUBLANES, SUBLANES)
            pltpu.make_async_copy(ys_hbm.at[pl.ds(d, SUBLANES)], buf_ref.at[k, pl.ds(dst_row, SUBLANES)],
                                  row_sem).start(priority=k % 2)
        return carry

    lax.fori_loop(0, tc, issue, 0)
    for k in range(TOP_K):
        pltpu.make_async_copy(ys_hbm.at[pl.ds(0, tc * SUBLANES)], buf_ref.at[k], row_sem).wait()

    tw = tw_ref[...]
    ffn = tw[:, 0:1] * _token_rows_load(buf_ref.at[0], tc)
    for k in range(1, TOP_K):
        ffn = ffn + tw[:, k:k + 1] * _token_rows_load(buf_ref.at[k], tc)
    x2 = x1_ref[...] + gt_ref[0] * ffn
    o_ref[...] = _rms(x2) * g_ref[...]


def _combine(dest_flat, ys, topw, x1, gt, g, seq):
    t, d = x1.shape
    tc = min(COMBINE_TILE, seq)
    per_batch = seq // tc
    row = lambda i: (i, 0)
    return pl.pallas_call(
        _combine_kernel,
        out_shape=jax.ShapeDtypeStruct((t, d), F32),
        grid=(t // tc,),
        in_specs=[pl.BlockSpec(memory_space=pl.ANY), pl.BlockSpec(memory_space=pl.ANY),
                  pl.BlockSpec((tc, TOP_K), row), pl.BlockSpec((tc, d), row),
                  pl.BlockSpec((1, 1, d), lambda i: (i // per_batch, 0, 0)), _const_spec((1, d))],
        out_specs=pl.BlockSpec((tc, d), row),
        scratch_shapes=[pltpu.SMEM((tc * TOP_K,), I32), pltpu.VMEM((TOP_K, tc * SUBLANES, LANES), F32),
                        pltpu.SemaphoreType.DMA, pltpu.SemaphoreType.DMA],
        compiler_params=_params("arbitrary"),
        name="combine",
    )(dest_flat, ys, topw, x1, gt, g)


def _layer(x2, mod, batch, seq, norm1_g, w_in, conv_w, conv_b, dt_bias, a_log, d_skip, ssm_norm_g,
           gla_wg2, gla_bg, gla_norm_g, w_out, norm2_g, w_router, b_router,
           w_gate, b_gate, w_up, b_up, w_down, b_down):
    t, d = x2.shape
    n_heads = dt_bias.shape[0]
    d_ssm = n_heads * SSM_HEAD_DIM
    cd = conv_w.shape[1]
    dkt = gla_wg2.shape[1]
    dvt = w_out.shape[0] - d_ssm
    n_experts = w_router.shape[1]

    sh1, sc1, gt1, sh2, sc2, gt2 = [m.reshape(batch, 1, d) for m in jnp.split(mod[:batch], 6, axis=1)]

    sizes = (d_ssm, cd, n_heads, dkt, dkt, dvt, GLA_GATE_RANK, dvt)
    offs = [0]
    for s in sizes:
        offs.append(offs[-1] + s)
    col = lambda j: w_in[:, offs[j]:offs[j + 1]]
    pad = jnp.zeros((d, LANES - n_heads - GLA_GATE_RANK), F32)
    w_small = jnp.concatenate([col(2), col(6), pad], axis=1)
    weights = [col(0), col(1), col(3), col(4), col(5), col(7), w_small]
    weights = [w.astype(BF16) for w in weights]
    z, xbc, q, k, v, r, small = _inproj(x2, sc1, sh1, norm1_g.reshape(1, d), weights,
                                        [BF16] * 6 + [F32], seq)

    d_skip_e = jnp.repeat(d_skip, SSM_HEAD_DIM).reshape(1, d_ssm)
    y = _ssd(xbc, z, small, conv_w, conv_b, dt_bias, a_log, d_skip_e, ssm_norm_g, batch, seq)
    o = _gla(q, k, v, r, small, gla_wg2, gla_bg, gla_norm_g, batch, seq, gate_col=n_heads)

    x1, h2, topi, topw = _outproj(y, o, x2, gt1, sc2, sh2, norm2_g.reshape(1, d),
                                        w_out[:d_ssm].astype(BF16), w_out[d_ssm:].astype(BF16),
                                        w_router, b_router, seq)

    n_blocks = (t * TOP_K) // EXPERT_BLOCK + n_experts
    dest, block_e, pend = _route(topi, n_experts, n_blocks)
    pend_i = pend[0, :n_experts].astype(I32)
    n_used = (pend_i[n_experts - 1:] // EXPERT_BLOCK).astype(I32)
    dest_flat = dest.reshape(t * TOP_K)
    xs = _dispatch(pend_i, dest_flat, h2, n_blocks * EXPERT_BLOCK, n_experts)
    ys = _experts(block_e[:n_blocks, 0], n_used, pend_i, xs, w_gate, b_gate, w_up, b_up, w_down, b_down)
    return dest_flat, ys, topw, x1, gt2


def kernel(x, c, w_ada, b_ada, norm1_g, w_in, conv_w, conv_b, dt_bias, a_log, d_skip, ssm_norm_g, gla_wg2,
           gla_bg, gla_norm_g, w_out, norm2_g, w_router, b_router, w_gate, b_gate, w_up, b_up, w_down, b_down,
           final_norm_g):
    batch, seq, d = x.shape
    assert w_ada.shape[0] == 1, "single-layer trunk"
    assert d == SUBLANES * LANES, "token rows are moved as one (8, 128) f32 tile each"
    assert seq % min(seq, max(TOKEN_TILE, SSM_CHUNK, GLA_STEP_ROWS, COMBINE_TILE, DISPATCH_TILE)) == 0
    assert seq % max(SSM_CHUNK, GLA_STEP_ROWS) == 0
    x2 = x.reshape(batch * seq, d)
    c_pad = jnp.zeros((SUBLANES, d), F32).at[:batch].set(c)
    mod = _ada(c_pad, w_ada[0], b_ada)
    dest_flat, ys, topw, x1, gt2 = _layer(
        x2, mod, batch, seq, norm1_g[0], w_in[0], conv_w[0], conv_b[0], dt_bias[0], a_log[0], d_skip[0],
        ssm_norm_g[0], gla_wg2[0], gla_bg[0], gla_norm_g[0], w_out[0], norm2_g[0], w_router[0], b_router[0],
        w_gate[0], b_gate[0], w_up[0], b_up[0], w_down[0], b_down[0])
    out = _combine(dest_flat, ys, topw, x1, gt2, final_norm_g.reshape(1, d), seq)
    return out.reshape(batch, seq, d)
```

```python
import functools

import jax
import jax.numpy as jnp
from jax import lax
from jax.experimental import pallas as pl
from jax.experimental.pallas import tpu as pltpu

F32 = jnp.float32
BF16 = jnp.bfloat16
I32 = jnp.int32
HIGHEST = lax.Precision.HIGHEST

EPS = 1e-6
SSM_HEAD_DIM = 64
SSM_GROUPS = 2
SSM_STATE = 128
SSM_CONV = 4
SSM_CHUNK = 128
GLA_HEADS = 4
GLA_GATE_RANK = 16
GLA_GATE_NORM = 16.0
GLA_CHUNK = 64
TOP_K = 4
SWIGLU_LIMIT = 7.0
SWIGLU_ALPHA = 1.702

LANES = 128
SUBLANES = 8
VMEM_LIMIT_BYTES = 56 * 1024 * 1024

TOKEN_TILE = 512
SSD_STEP_ROWS = 256
CONV_CARRY = 16
GLA_STEP_ROWS = 256
ROUTE_TILE = 512
EXPERT_BLOCK = 256
DISPATCH_TILE = 512
COMBINE_TILE = 256

_NT = (((1,), (1,)), ((), ()))
_TN = (((0,), (0,)), ((), ()))


def _sigmoid(v):
    return 0.5 * jnp.tanh(0.5 * v) + 0.5


def _silu(v):
    return v * _sigmoid(v)


def _softplus(v):
    return jnp.maximum(v, 0.0) + jnp.log1p(jnp.exp(-jnp.abs(v)))


def _log_sigmoid(v):
    return jnp.minimum(v, 0.0) - jnp.log(1.0 + jnp.exp(-jnp.abs(v)))


def _rms(v):
    return v * lax.rsqrt(jnp.mean(v * v, axis=-1, keepdims=True) + EPS)


def _pieces(a, n):
    out = []
    for _ in range(n - 1):
        p = a.astype(BF16)
        out.append(p)
        a = a - p.astype(F32)
    out.append(a.astype(BF16))
    return out


def _dot_pieces(a, b, n, dims=None):
    dims = dims or (((a.ndim - 1,), (0,)), ((), ()))
    return sum(lax.dot_general(p, b, dims, preferred_element_type=F32) for p in _pieces(a, n))


def _token_rows_load(ref, rows):
    return jnp.concatenate([ref[pl.ds(s, rows, stride=SUBLANES), :] for s in range(SUBLANES)], axis=1)


def _token_rows_store(ref, v):
    rows = v.shape[0]
    for s in range(SUBLANES):
        ref[pl.ds(s, rows, stride=SUBLANES), :] = v[:, s * LANES:(s + 1) * LANES]


def _params(*semantics):
    return pltpu.CompilerParams(dimension_semantics=semantics, vmem_limit_bytes=VMEM_LIMIT_BYTES)


def _const_spec(shape):
    nd = len(shape)
    return pl.BlockSpec(shape, lambda *_: (0,) * nd)


def _ada_kernel(c_ref, w_ref, b_ref, o_ref):
    cond = _silu(c_ref[...])
    o_ref[...] = jnp.dot(cond, w_ref[...], precision=HIGHEST, preferred_element_type=F32) + b_ref[...]


def _ada(c_pad, w_ada, b_ada):
    rows, d = c_pad.shape
    n = w_ada.shape[1]
    tn = d
    return pl.pallas_call(
        _ada_kernel,
        out_shape=jax.ShapeDtypeStruct((rows, n), F32),
        grid=(n // tn,),
        in_specs=[pl.BlockSpec((rows, d), lambda j: (0, 0)),
                  pl.BlockSpec((d, tn), lambda j: (0, j)),
                  pl.BlockSpec((1, tn), lambda j: (0, j))],
        out_specs=pl.BlockSpec((rows, tn), lambda j: (0, j)),
        compiler_params=_params("arbitrary"),
        name="ada",
    )(c_pad, w_ada, b_ada)


def _inproj_kernel(x_ref, sc_ref, sh_ref, g_ref, *refs):
    n_w = len(refs) // 2
    w_refs, o_refs = refs[:n_w], refs[n_w:]
    h = (_rms(x_ref[...]) * g_ref[...]) * (1.0 + sc_ref[0]) + sh_ref[0]
    hb = h.astype(BF16)
    for w_ref, o_ref in zip(w_refs, o_refs):
        o_ref[...] = jnp.dot(hb, w_ref[...], preferred_element_type=F32).astype(o_ref.dtype)


def _inproj(x2, sc, sh, g, weights, out_dtypes, seq):
    t, d = x2.shape
    tm = min(TOKEN_TILE, seq)
    per_batch = seq // tm
    mod_spec = pl.BlockSpec((1, 1, d), lambda i: (i // per_batch, 0, 0))
    return pl.pallas_call(
        _inproj_kernel,
        out_shape=[jax.ShapeDtypeStruct((t, w.shape[1]), dt) for w, dt in zip(weights, out_dtypes)],
        grid=(t // tm,),
        in_specs=[pl.BlockSpec((tm, d), lambda i: (i, 0)), mod_spec, mod_spec, _const_spec((1, d))]
                 + [_const_spec(w.shape) for w in weights],
        out_specs=[pl.BlockSpec((tm, w.shape[1]), lambda i: (i, 0)) for w in weights],
        compiler_params=_params("arbitrary"),
        name="inproj",
    )(x2, sc, sh, g, *weights)


def _ssd_kernel(tail_ref, xbc_ref, z_ref, sm_ref, cw_ref, cb_ref, dtb_r_ref, dtb_c_ref, alog_r_ref, alog_c_ref,
                dsk_ref, g_ref, y_ref, st_ref, *, n_heads, d_ssm):
    L = SSM_CHUNK
    P = SSM_HEAD_DIM
    N = SSM_STATE
    G = SSM_GROUPS
    gw = d_ssm // G
    R = xbc_ref.shape[0]
    c = pl.program_id(1)

    @pl.when(c == 0)
    def _():
        st_ref[...] = jnp.zeros(st_ref.shape, F32)

    tail = tail_ref[...]
    tail = jnp.where(c == 0, jnp.zeros_like(tail), tail)
    conv_in = jnp.concatenate([tail, xbc_ref[...]], axis=0)

    ri = lax.broadcasted_iota(I32, (L, L), 0)
    ci = lax.broadcasted_iota(I32, (L, L), 1)
    causal = ri >= ci
    causal_b = causal.astype(BF16)
    upper_b = (ri <= ci).astype(BF16)
    wr = lax.broadcasted_iota(I32, ((SSM_CONV - 1) * L, CONV_CARRY + L), 0)
    wc = lax.broadcasted_iota(I32, ((SSM_CONV - 1) * L, CONV_CARRY + L), 1)
    shifts = (wc == (wr % L) + CONV_CARRY - (SSM_CONV - 1) + wr // L).astype(BF16)
    hh = lax.broadcasted_iota(I32, (n_heads, d_ssm), 0)
    jj = lax.broadcasted_iota(I32, (n_heads, d_ssm), 1)
    expand = ((jj // P) == hh).astype(BF16)
    expand2 = jnp.concatenate([expand, expand], axis=0)
    lane = lax.broadcasted_iota(I32, (L, LANES), 1)
    first_half = lane < P
    heads_per_group = n_heads // G
    a_row = -jnp.exp(alog_r_ref[...])
    a_col = -jnp.exp(alog_c_ref[...])

    for s in range(R // L):
        rows = slice(s * L, (s + 1) * L)
        window = conv_in[s * L:s * L + CONV_CARRY + L, :]
        taps = jnp.dot(shifts, window, preferred_element_type=F32)
        acc = cb_ref[...] + cw_ref[SSM_CONV - 1:SSM_CONV, :] * window[CONV_CARRY:, :].astype(F32)
        for j in range(SSM_CONV - 1):
            acc = acc + cw_ref[j:j + 1, :] * taps[j * L:(j + 1) * L, :]
        act = _silu(acc)
        xs = act[:, :d_ssm]
        bm = act[:, d_ssm:d_ssm + G * N].astype(BF16)
        cm = act[:, d_ssm + G * N:].astype(BF16)

        sm = sm_ref[rows, :]
        dt_col = _softplus(sm[:, :n_heads] + dtb_r_ref[...])
        dt_row = _softplus(sm.T[:n_heads, :] + dtb_c_ref[...])
        cs_col = sum(jnp.dot(causal_b, p, preferred_element_type=F32) for p in _pieces(dt_col * a_row, 3))
        cs_row = _dot_pieces(dt_row * a_col, upper_b, 3)
        cs_last = cs_col[L - 1:L, :]

        per_head = jnp.concatenate([dt_col, jnp.exp(cs_col), jnp.exp(cs_last - cs_col)], axis=0)
        per_ch = jnp.dot(jnp.concatenate(_pieces(per_head, 2), axis=1), expand2,
                         preferred_element_type=F32)
        dt_e, ecs_e, dte_e = per_ch[0:L], per_ch[L:2 * L], per_ch[2 * L:3 * L]
        chunk_decay = ecs_e[L - 1:L, :]

        xdt = xs * dt_e
        xdt_b = xdt.astype(BF16)
        xdec_b = (xdt * dte_e).astype(BF16)

        y_parts = []
        y_off_parts = []
        for g in range(G):
            bm_g = bm[:, g * N:(g + 1) * N]
            cm_g = cm[:, g * N:(g + 1) * N]
            cb = lax.dot_general(cm_g, bm_g, _NT, preferred_element_type=F32)
            prev = st_ref[g]
            y_off_parts.append(jnp.dot(cm_g, prev.astype(BF16), preferred_element_type=F32))
            s_new = lax.dot_general(bm_g, xdec_b[:, g * gw:(g + 1) * gw], _TN, preferred_element_type=F32)
            st_ref[g] = prev * chunk_decay[:, g * gw:(g + 1) * gw] + s_new
            for p in range(heads_per_group // 2):
                h0 = g * heads_per_group + 2 * p
                ms = []
                for h in (h0, h0 + 1):
                    diff = cs_col[:, h:h + 1] - cs_row[h:h + 1, :]
                    ms.append((cb * jnp.exp(jnp.where(causal, diff, -jnp.inf))).astype(BF16))
                lhs = jnp.concatenate(ms, axis=1)
                xp = xdt_b[:, h0 * P:(h0 + 2) * P]
                zero = jnp.zeros_like(xp)
                rhs = jnp.concatenate([jnp.where(first_half, xp, zero), jnp.where(first_half, zero, xp)], axis=0)
                y_parts.append(jnp.dot(lhs, rhs, preferred_element_type=F32))
        y = jnp.concatenate(y_parts, axis=1) + jnp.concatenate(y_off_parts, axis=1) * ecs_e + dsk_ref[...] * xs
        y = y * _silu(z_ref[rows, :].astype(F32))
        y = jnp.concatenate([_rms(y[:, g * gw:(g + 1) * gw]) for g in range(G)], axis=1) * g_ref[...]
        y_ref[rows, :] = y.astype(y_ref.dtype)


def _ssd(xbc, z, small, conv_w, conv_b, dt_bias, a_log, d_skip_e, norm_g, batch, seq):
    t, cd = xbc.shape
    d_ssm = z.shape[1]
    n_heads = dt_bias.shape[0]
    L = min(SSD_STEP_ROWS, seq)
    nc = seq // L
    row = lambda b, c: (b * nc + c, 0)
    tail = lambda b, c: (jnp.maximum((b * nc + c) * (L // CONV_CARRY) - 1, 0), 0)
    kern = functools.partial(_ssd_kernel, n_heads=n_heads, d_ssm=d_ssm)
    return pl.pallas_call(
        kern,
        out_shape=jax.ShapeDtypeStruct((t, d_ssm), BF16),
        grid=(batch, nc),
        in_specs=[pl.BlockSpec((CONV_CARRY, cd), tail),
                  pl.BlockSpec((L, cd), row), pl.BlockSpec((L, d_ssm), row), pl.BlockSpec((L, LANES), row),
                  _const_spec(conv_w.shape), _const_spec((1, cd)),
                  _const_spec((1, n_heads)), _const_spec((n_heads, 1)),
                  _const_spec((1, n_heads)), _const_spec((n_heads, 1)),
                  _const_spec((1, d_ssm)), _const_spec((1, d_ssm))],
        out_specs=pl.BlockSpec((L, d_ssm), row),
        scratch_shapes=[pltpu.VMEM((SSM_GROUPS, SSM_STATE, d_ssm // SSM_GROUPS), F32)],
        compiler_params=_params("arbitrary", "arbitrary"),
        name="ssd",
    )(xbc, xbc, z, small, conv_w, conv_b.reshape(1, cd), dt_bias.reshape(1, n_heads), dt_bias.reshape(n_heads, 1),
      a_log.reshape(1, n_heads), a_log.reshape(n_heads, 1), d_skip_e, norm_g.reshape(1, d_ssm))


def _gla_kernel(q_ref, k_ref, v_ref, r_ref, sm_ref, wg2_ref, bg_ref, gn_ref, o_ref, st_ref, *, gate_col):
    L = GLA_CHUNK
    H = GLA_HEADS
    dk = q_ref.shape[1] // H
    dv = v_ref.shape[1] // H
    c = pl.program_id(1)

    @pl.when(c == 0)
    def _():
        st_ref[...] = jnp.zeros(st_ref.shape, F32)

    R = q_ref.shape[0]
    n_chunks = R // L
    ri = lax.broadcasted_iota(I32, (R, R), 0)
    ci = lax.broadcasted_iota(I32, (R, R), 1)
    same_chunk = (ri // L) == (ci // L)
    causal = jnp.logical_and(same_chunk, ri >= ci)
    tril = causal.astype(BF16)
    later = jnp.logical_and(same_chunk, ri < ci).astype(BF16)
    sr = lax.broadcasted_iota(I32, (R, n_chunks * LANES), 0)
    sc = lax.broadcasted_iota(I32, (R, n_chunks * LANES), 1)
    last_rows = (sr == (sc // LANES) * L + (L - 1)).astype(BF16)
    wg2_hi, wg2_lo = _pieces(wg2_ref[...], 2)

    q = q_ref[...].astype(F32) * (dk ** -0.5)
    k = k_ref[...].astype(F32)
    v = v_ref[...]
    r = r_ref[...].astype(F32)
    g_hi, g_lo = _pieces(sm_ref[:, gate_col:gate_col + GLA_GATE_RANK], 2)
    pre = (jnp.dot(g_hi, wg2_hi, preferred_element_type=F32) + jnp.dot(g_lo, wg2_hi, preferred_element_type=F32)
           + jnp.dot(g_hi, wg2_lo, preferred_element_type=F32)) + bg_ref[...]
    gk_p = _pieces(_log_sigmoid(pre) / GLA_GATE_NORM, 3)
    bcum = sum(jnp.dot(tril, p, preferred_element_type=F32) for p in gk_p)
    to_end = sum(jnp.dot(later, p, preferred_element_type=F32) for p in gk_p)
    q_t = (q * jnp.exp(bcum)).astype(BF16)
    k_t = (k * jnp.exp(-bcum)).astype(BF16)
    k_dec = (k * jnp.exp(to_end)).astype(BF16)
    dcol = jnp.exp(_dot_pieces(bcum, last_rows, 3, dims=_TN))
    outs = []
    for h in range(H):
        ks = slice(h * dk, (h + 1) * dk)
        vs = slice(h * dv, (h + 1) * dv)
        att = lax.dot_general(q_t[:, ks], k_t[:, ks], _NT, preferred_element_type=F32)
        att = jnp.where(causal, att, 0.0).astype(BF16)
        o = jnp.dot(att, v[:, vs], preferred_element_type=F32)
        state = st_ref[h]
        inter = []
        for c in range(n_chunks):
            rows = slice(c * L, (c + 1) * L)
            inter.append(jnp.dot(q_t[rows, ks], state.astype(BF16), preferred_element_type=F32))
            s_new = lax.dot_general(k_dec[rows, ks], v[rows, vs], _TN, preferred_element_type=F32)
            dec = dcol[ks, c * LANES:(c + 1) * LANES]
            state = state * jnp.concatenate([dec] * (dv // LANES), axis=1) + s_new
        st_ref[h] = state
        o = o + jnp.concatenate(inter, axis=0)
        outs.append(_rms(o) * gn_ref[...] * _silu(r[:, vs]))
    o_ref[...] = jnp.concatenate(outs, axis=1).astype(o_ref.dtype)


def _gla(q, k, v, r, small, wg2, bg, norm_g, batch, seq, gate_col):
    t, dkt = q.shape
    dvt = v.shape[1]
    rows = min(GLA_STEP_ROWS, seq)
    nc = seq // rows
    row = lambda b, c: (b * nc + c, 0)
    kern = functools.partial(_gla_kernel, gate_col=gate_col)
    return pl.pallas_call(
        kern,
        out_shape=jax.ShapeDtypeStruct((t, dvt), BF16),
        grid=(batch, nc),
        in_specs=[pl.BlockSpec((rows, dkt), row), pl.BlockSpec((rows, dkt), row), pl.BlockSpec((rows, dvt), row),
                  pl.BlockSpec((rows, dvt), row), pl.BlockSpec((rows, LANES), row),
                  _const_spec(wg2.shape), _const_spec((1, dkt)), _const_spec((1, dvt // GLA_HEADS))],
        out_specs=pl.BlockSpec((rows, dvt), row),
        scratch_shapes=[pltpu.VMEM((GLA_HEADS, dkt // GLA_HEADS, dvt // GLA_HEADS), F32)],
        compiler_params=_params("arbitrary", "arbitrary"),
        name="gla",
    )(q, k, v, r, small, wg2, bg.reshape(1, dkt), norm_g.reshape(1, dvt // GLA_HEADS))


def _outproj_kernel(y_ref, o_ref, x_ref, gt_ref, sc_ref, sh_ref, g_ref, wy_ref, wo_ref, wr_ref, br_ref,
                    x1_ref, h_ref, ti_ref, tw_ref):
    mix = (jnp.dot(y_ref[...], wy_ref[...], preferred_element_type=F32)
           + jnp.dot(o_ref[...], wo_ref[...], preferred_element_type=F32))
    x1 = x_ref[...] + gt_ref[0] * mix
    x1_ref[...] = x1
    h = (_rms(x1) * g_ref[...]) * (1.0 + sc_ref[0]) + sh_ref[0]
    _token_rows_store(h_ref, h)
    n_e = br_ref.shape[1]
    h_hi, h_lo = _pieces(h, 2)
    wr = wr_ref[...]
    hw = jnp.dot(h_hi, wr, preferred_element_type=F32)
    logits = (hw[:, :n_e] + hw[:, n_e:] + jnp.dot(h_lo, wr[:, :n_e], preferred_element_type=F32)) + br_ref[...]
    lane = lax.broadcasted_iota(I32, logits.shape, 1)
    vals, idxs = [], []
    for _ in range(TOP_K):
        m = jnp.max(logits, axis=1, keepdims=True)
        idx = jnp.min(jnp.where(logits == m, lane, n_e), axis=1, keepdims=True)
        vals.append(m)
        idxs.append(idx)
        logits = jnp.where(lane == idx, -jnp.inf, logits)
    exps = [jnp.exp(v - vals[0]) for v in vals]
    denom = functools.reduce(lambda a, b: a + b, exps)
    ti_ref[...] = jnp.concatenate(idxs, axis=1)
    tw_ref[...] = jnp.concatenate([e / denom for e in exps], axis=1)


def _outproj(y, o, x2, gt, sc, sh, g, wy, wo, w_router, b_router, seq):
    t, d = x2.shape
    n_e = w_router.shape[1]
    tm = min(TOKEN_TILE, seq)
    per_batch = seq // tm
    row = lambda i: (i, 0)
    mod_spec = pl.BlockSpec((1, 1, d), lambda i: (i // per_batch, 0, 0))
    wr_hi = w_router.astype(BF16)
    wr_cat = jnp.concatenate([wr_hi, (w_router - wr_hi.astype(F32)).astype(BF16)], axis=1)
    return pl.pallas_call(
        _outproj_kernel,
        out_shape=[jax.ShapeDtypeStruct((t, d), F32), jax.ShapeDtypeStruct((t * SUBLANES, LANES), F32),
                   jax.ShapeDtypeStruct((t, TOP_K), I32), jax.ShapeDtypeStruct((t, TOP_K), F32)],
        grid=(t // tm,),
        in_specs=[pl.BlockSpec((tm, y.shape[1]), row), pl.BlockSpec((tm, o.shape[1]), row),
                  pl.BlockSpec((tm, d), row), mod_spec, mod_spec, mod_spec, _const_spec((1, d)),
                  _const_spec(wy.shape), _const_spec(wo.shape), _const_spec(wr_cat.shape),
                  _const_spec((1, n_e))],
        out_specs=[pl.BlockSpec((tm, d), row), pl.BlockSpec((tm * SUBLANES, LANES), row),
                   pl.BlockSpec((tm, TOP_K), row), pl.BlockSpec((tm, TOP_K), row)],
        compiler_params=_params("arbitrary"),
        name="outproj",
    )(y, o, x2, gt, sc, sh, g, wy, wo, wr_cat, b_router.reshape(1, n_e))


def _route_kernel(ti_ref, dest_ref, be_ref, pend_ref, cnt_ref, run_ref, *, n_experts, n_blocks_pad):
    phase = pl.program_id(0)
    i = pl.program_id(1)
    tr = ti_ref.shape[0]
    ti = ti_ref[...]
    lane = lax.broadcasted_iota(I32, (tr, LANES), 1)
    onehots = [ti[:, k:k + 1] == lane for k in range(TOP_K)]
    cnt = functools.reduce(lambda a, b: a + b, [oh.astype(F32) for oh in onehots])
    tile_counts = jnp.sum(cnt, axis=0, keepdims=True)

    @pl.when(jnp.logical_and(phase == 0, i == 0))
    def _():
        cnt_ref[...] = jnp.zeros(cnt_ref.shape, F32)

    @pl.when(phase == 0)
    def _():
        cnt_ref[...] = cnt_ref[...] + tile_counts

    @pl.when(jnp.logical_and(phase == 1, i == 0))
    def _():
        counts = cnt_ref[...]
        padded = jnp.ceil(counts / EXPERT_BLOCK) * EXPERT_BLOCK
        ri = lax.broadcasted_iota(I32, (LANES, LANES), 0)
        ci = lax.broadcasted_iota(I32, (LANES, LANES), 1)
        pend = jnp.dot(padded, (ri <= ci).astype(F32), precision=HIGHEST, preferred_element_type=F32)
        pend_ref[...] = pend
        run_ref[...] = pend - padded
        start = (lax.broadcasted_iota(I32, (n_blocks_pad, LANES), 0) * EXPERT_BLOCK).astype(F32)
        col = lax.broadcasted_iota(I32, (n_blocks_pad, LANES), 1)
        ended = jnp.logical_and(pend[0:1, :] <= start, col < n_experts)
        be = jnp.sum(ended.astype(F32), axis=1, keepdims=True)
        be_ref[...] = jnp.minimum(be, n_experts - 1).astype(I32)

    @pl.when(phase == 1)
    def _():
        ri = lax.broadcasted_iota(I32, (tr, tr), 0)
        ci = lax.broadcasted_iota(I32, (tr, tr), 1)
        before = jnp.dot((ri > ci).astype(BF16), cnt.astype(BF16), preferred_element_type=F32)
        base = run_ref[0:1, :] + before
        dest = [jnp.sum(jnp.where(oh, base, 0.0), axis=1, keepdims=True) for oh in onehots]
        dest_ref[...] = jnp.concatenate(dest, axis=1).astype(I32)
        run_ref[...] = run_ref[...] + tile_counts


def _route(topi, n_experts, n_blocks):
    t = topi.shape[0]
    tr = min(ROUTE_TILE, t)
    n_blocks_pad = -(-n_blocks // SUBLANES) * SUBLANES
    kern = functools.partial(_route_kernel, n_experts=n_experts, n_blocks_pad=n_blocks_pad)
    return pl.pallas_call(
        kern,
        out_shape=[jax.ShapeDtypeStruct((t, TOP_K), I32), jax.ShapeDtypeStruct((n_blocks_pad, 1), I32),
                   jax.ShapeDtypeStruct((SUBLANES, LANES), F32)],
        grid=(2, t // tr),
        in_specs=[pl.BlockSpec((tr, TOP_K), lambda p, i: (i, 0))],
        out_specs=[pl.BlockSpec((tr, TOP_K), lambda p, i: (i * p, 0)),
                   pl.BlockSpec((n_blocks_pad, 1), lambda p, i: (0, 0)),
                   pl.BlockSpec((SUBLANES, LANES), lambda p, i: (0, 0))],
        scratch_shapes=[pltpu.VMEM((SUBLANES, LANES), F32), pltpu.VMEM((SUBLANES, LANES), F32)],
        compiler_params=_params("arbitrary", "arbitrary"),
        name="route",
    )(topi)


def _dispatch_kernel(pend_ref, dest_hbm, h_ref, xs_hbm, idx_ref, zero_ref, idx_sem, row_sem, *, n_experts):
    i = pl.program_id(0)
    n_idx = idx_ref.shape[0]
    tg = n_idx // TOP_K

    @pl.when(i == 0)
    def _():
        zero_ref[...] = jnp.zeros(zero_ref.shape, zero_ref.dtype)
        for e in range(n_experts):
            end = pend_ref[e]
            prev = pend_ref[e - 1] if e > 0 else 0

            @pl.when(end > prev)
            def _():
                start = pl.multiple_of((end - EXPERT_BLOCK) * SUBLANES, EXPERT_BLOCK * SUBLANES)
                cp = pltpu.make_async_copy(zero_ref, xs_hbm.at[pl.ds(start, EXPERT_BLOCK * SUBLANES)], row_sem)
                cp.start()
                cp.wait()

        n_blocks = xs_hbm.shape[0] // (EXPERT_BLOCK * SUBLANES)
        total = pend_ref[n_experts - 1]
        for b in range(n_blocks - n_experts, n_blocks):
            @pl.when(b * EXPERT_BLOCK >= total)
            def _():
                cp = pltpu.make_async_copy(
                    zero_ref, xs_hbm.at[pl.ds(b * EXPERT_BLOCK * SUBLANES, EXPERT_BLOCK * SUBLANES)], row_sem)
                cp.start()
                cp.wait()

    idx_cp = pltpu.make_async_copy(dest_hbm.at[pl.ds(i * n_idx, n_idx)], idx_ref, idx_sem)
    idx_cp.start()
    idx_cp.wait()

    def issue(tl, carry):
        src = h_ref.at[pl.ds(pl.multiple_of(tl * SUBLANES, SUBLANES), SUBLANES)]
        for k in range(TOP_K):
            d = pl.multiple_of(idx_ref[tl * TOP_K + k] * SUBLANES, SUBLANES)
            pltpu.make_async_copy(src, xs_hbm.at[pl.ds(d, SUBLANES)], row_sem).start(priority=k % 2)
        return carry

    lax.fori_loop(0, tg, issue, 0)
    for _ in range(TOP_K):
        pltpu.make_async_copy(h_ref, xs_hbm.at[pl.ds(0, tg * SUBLANES)], row_sem).wait()


def _dispatch(pend_i, dest_flat, h, n_rows, n_experts):
    t = h.shape[0] // SUBLANES
    tg = min(DISPATCH_TILE, t)
    kern = functools.partial(_dispatch_kernel, n_experts=n_experts)
    return pl.pallas_call(
        kern,
        out_shape=jax.ShapeDtypeStruct((n_rows * SUBLANES, LANES), h.dtype),
        grid_spec=pltpu.PrefetchScalarGridSpec(
            num_scalar_prefetch=1,
            grid=(t // tg,),
            in_specs=[pl.BlockSpec(memory_space=pl.ANY),
                      pl.BlockSpec((tg * SUBLANES, LANES), lambda i, pend: (i, 0))],
            out_specs=pl.BlockSpec(memory_space=pl.ANY),
            scratch_shapes=[pltpu.SMEM((tg * TOP_K,), I32), pltpu.VMEM((EXPERT_BLOCK * SUBLANES, LANES), h.dtype),
                            pltpu.SemaphoreType.DMA, pltpu.SemaphoreType.DMA]),
        compiler_params=pltpu.CompilerParams(dimension_semantics=("arbitrary",), has_side_effects=True,
                                             vmem_limit_bytes=VMEM_LIMIT_BYTES),
        name="dispatch",
    )(pend_i, dest_flat, h)


def _expert_kernel(be_ref, nu_ref, pend_ref, xs_ref, wg_hbm, bg_ref, wu_hbm, bu_ref, wd_hbm, bd_ref, y_ref,
                   wg_f, wu_f, wd_f, wg_b, wu_b, wd_b, slot_ref, sems):
    i = pl.program_id(0)
    used = i < nu_ref[0]
    e = be_ref[i]

    def fetch(expert, slot):
        return [pltpu.make_async_copy(src.at[expert], dst.at[slot], sems.at[slot])
                for src, dst in ((wg_hbm, wg_f), (wu_hbm, wu_f), (wd_hbm, wd_f))]

    @pl.when(i == 0)
    def _():
        slot_ref[0] = 0
        for cp in fetch(e, 0):
            cp.start()

    first_of_expert = jnp.logical_or(i == 0, e != be_ref[jnp.maximum(i - 1, 0)])

    @pl.when(jnp.logical_and(used, first_of_expert))
    def _():
        slot = slot_ref[0]
        for cp in fetch(e, slot):
            cp.wait()
        wg_b[...] = wg_f[slot].astype(BF16)
        wu_b[...] = wu_f[slot].astype(BF16)
        wd_b[...] = wd_f[slot].astype(BF16)
        nxt = lax.div(pend_ref[e], EXPERT_BLOCK)

        @pl.when(nxt < nu_ref[0])
        def _():
            for cp in fetch(be_ref[nxt], 1 - slot):
                cp.start()

        slot_ref[0] = 1 - slot

    @pl.when(used)
    def _():
        x = _token_rows_load(xs_ref, EXPERT_BLOCK).astype(BF16)
        gate = jnp.minimum(jnp.dot(x, wg_b[...], preferred_element_type=F32) + bg_ref[...], SWIGLU_LIMIT)
        up = jnp.clip(jnp.dot(x, wu_b[...], preferred_element_type=F32) + bu_ref[...],
                      -SWIGLU_LIMIT, SWIGLU_LIMIT)
        glu = gate * _sigmoid(SWIGLU_ALPHA * gate)
        mid = ((up + 1.0) * glu).astype(BF16)
        y = jnp.dot(mid, wd_b[...], preferred_element_type=F32) + bd_ref[...]
        _token_rows_store(y_ref, y)

    @pl.when(jnp.logical_not(used))
    def _():
        y_ref[...] = jnp.zeros(y_ref.shape, y_ref.dtype)


def _experts(block_e, n_used, pend_i, xs, w_gate, b_gate, w_up, b_up, w_down, b_down):
    n_rows = xs.shape[0] // SUBLANES
    n_e, d, f = w_gate.shape
    nb = n_rows // EXPERT_BLOCK
    blk = (EXPERT_BLOCK * SUBLANES, LANES)
    last = lambda i, be, nu, pend: jnp.maximum(jnp.minimum(i, nu[0] - 1), 0)
    bspec = lambda n: pl.BlockSpec((None, 1, n), lambda i, be, nu, pend: (be[last(i, be, nu, pend)], 0, 0))
    hbm = pl.BlockSpec(memory_space=pl.ANY)
    return pl.pallas_call(
        _expert_kernel,
        out_shape=jax.ShapeDtypeStruct((n_rows * SUBLANES, LANES), F32),
        grid_spec=pltpu.PrefetchScalarGridSpec(
            num_scalar_prefetch=3,
            grid=(nb,),
            in_specs=[pl.BlockSpec(blk, lambda i, be, nu, pend: (last(i, be, nu, pend), 0)),
                      hbm, bspec(f), hbm, bspec(f), hbm, bspec(d)],
            out_specs=pl.BlockSpec(blk, lambda i, be, nu, pend: (i, 0)),
            scratch_shapes=[pltpu.VMEM((2, d, f), F32), pltpu.VMEM((2, d, f), F32), pltpu.VMEM((2, f, d), F32),
                            pltpu.VMEM((d, f), BF16), pltpu.VMEM((d, f), BF16), pltpu.VMEM((f, d), BF16),
                            pltpu.SMEM((1,), I32), pltpu.SemaphoreType.DMA((2,))]),
        compiler_params=_params("arbitrary"),
        name="experts",
    )(block_e, n_used, pend_i, xs, w_gate, b_gate.reshape(n_e, 1, f), w_up, b_up.reshape(n_e, 1, f),
      w_down, b_down.reshape(n_e, 1, d))


def _combine_kernel(dest_hbm, ys_hbm, tw_ref, x1_ref, gt_ref, g_ref, o_ref, idx_ref, buf_ref, idx_sem, row_sem):
    i = pl.program_id(0)
    n = pl.num_programs(0)
    n_idx = idx_ref.shape[1]
    tc = n_idx // TOP_K
    slot = lax.rem(i, 2)

    def idx_copy(tile, s):
        return pltpu.make_async_copy(dest_hbm.at[pl.ds(tile * n_idx, n_idx)], idx_ref.at[s], idx_sem.at[s])

    def issue_rows(s):
        def issue(tl, carry):
            dst_row = pl.multiple_of(tl * SUBLANES, SUBLANES)
            for k in range(TOP_K):
                d = pl.multiple_of(idx_ref[s, tl * TOP_K + k] * SUBLANES, SUBLANES)
                pltpu.make_async_copy(ys_hbm.at[pl.ds(d, SUBLANES)], buf_ref.at[s, k, pl.ds(dst_row, SUBLANES)],
                                      row_sem.at[s]).start(priority=k % 2)
            return carry

        lax.fori_loop(0, tc, issue, 0)

    @pl.when(i == 0)
    def _():
        idx_copy(0, 0).start()
        idx_copy(0, 0).wait()
        issue_rows(0)

        @pl.when(n > 1)
        def _():
            idx_copy(1, 1).start()

    @pl.when(i + 1 < n)
    def _():
        idx_copy(i + 1, 1 - slot).wait()
        issue_rows(1 - slot)

        @pl.when(i + 2 < n)
        def _():
            idx_copy(i + 2, slot).start()

    for k in range(TOP_K):
        pltpu.make_async_copy(ys_hbm.at[pl.ds(0, tc * SUBLANES)], buf_ref.at[slot, k], row_sem.at[slot]).wait()

    tw = tw_ref[...]
    ffn = tw[:, 0:1] * _token_rows_load(buf_ref.at[slot, 0], tc)
    for k in range(1, TOP_K):
        ffn = ffn + tw[:, k:k + 1] * _token_rows_load(buf_ref.at[slot, k], tc)
    x2 = x1_ref[...] + gt_ref[0] * ffn
    o_ref[...] = _rms(x2) * g_ref[...]


def _combine(dest_flat, ys, topw, x1, gt, g, seq):
    t, d = x1.shape
    tc = min(COMBINE_TILE, seq)
    per_batch = seq // tc
    row = lambda i: (i, 0)
    return pl.pallas_call(
        _combine_kernel,
        out_shape=jax.ShapeDtypeStruct((t, d), F32),
        grid=(t // tc,),
        in_specs=[pl.BlockSpec(memory_space=pl.ANY), pl.BlockSpec(memory_space=pl.ANY),
                  pl.BlockSpec((tc, TOP_K), row), pl.BlockSpec((tc, d), row),
                  pl.BlockSpec((1, 1, d), lambda i: (i // per_batch, 0, 0)), _const_spec((1, d))],
        out_specs=pl.BlockSpec((tc, d), row),
        scratch_shapes=[pltpu.SMEM((2, tc * TOP_K), I32), pltpu.VMEM((2, TOP_K, tc * SUBLANES, LANES), F32),
                        pltpu.SemaphoreType.DMA((2,)), pltpu.SemaphoreType.DMA((2,))],
        compiler_params=_params("arbitrary"),
        name="combine",
    )(dest_flat, ys, topw, x1, gt, g)


def _layer(x2, mod, batch, seq, norm1_g, w_in, conv_w, conv_b, dt_bias, a_log, d_skip, ssm_norm_g,
           gla_wg2, gla_bg, gla_norm_g, w_out, norm2_g, w_router, b_router,
           w_gate, b_gate, w_up, b_up, w_down, b_down):
    t, d = x2.shape
    n_heads = dt_bias.shape[0]
    d_ssm = n_heads * SSM_HEAD_DIM
    cd = conv_w.shape[1]
    dkt = gla_wg2.shape[1]
    dvt = w_out.shape[0] - d_ssm
    n_experts = w_router.shape[1]

    sh1, sc1, gt1, sh2, sc2, gt2 = [m.reshape(batch, 1, d) for m in jnp.split(mod[:batch], 6, axis=1)]

    sizes = (d_ssm, cd, n_heads, dkt, dkt, dvt, GLA_GATE_RANK, dvt)
    offs = [0]
    for s in sizes:
        offs.append(offs[-1] + s)
    col = lambda j: w_in[:, offs[j]:offs[j + 1]]
    pad = jnp.zeros((d, LANES - n_heads - GLA_GATE_RANK), F32)
    w_small = jnp.concatenate([col(2), col(6), pad], axis=1)
    weights = [col(0), col(1), col(3), col(4), col(5), col(7), w_small]
    weights = [w.astype(BF16) for w in weights]
    z, xbc, q, k, v, r, small = _inproj(x2, sc1, sh1, norm1_g.reshape(1, d), weights,
                                        [BF16] * 6 + [F32], seq)

    d_skip_e = jnp.repeat(d_skip, SSM_HEAD_DIM).reshape(1, d_ssm)
    y = _ssd(xbc, z, small, conv_w, conv_b, dt_bias, a_log, d_skip_e, ssm_norm_g, batch, seq)
    o = _gla(q, k, v, r, small, gla_wg2, gla_bg, gla_norm_g, batch, seq, gate_col=n_heads)

    x1, h2, topi, topw = _outproj(y, o, x2, gt1, sc2, sh2, norm2_g.reshape(1, d),
                                        w_out[:d_ssm].astype(BF16), w_out[d_ssm:].astype(BF16),
                                        w_router, b_router, seq)

    n_blocks = (t * TOP_K) // EXPERT_BLOCK + n_experts
    dest, block_e, pend = _route(topi, n_experts, n_blocks)
    pend_i = pend[0, :n_experts].astype(I32)
    n_used = (pend_i[n_experts - 1:] // EXPERT_BLOCK).astype(I32)
    dest_flat = dest.reshape(t * TOP_K)
    xs = _dispatch(pend_i, dest_flat, h2, n_blocks * EXPERT_BLOCK, n_experts)
    ys = _experts(block_e[:n_blocks, 0], n_used, pend_i, xs, w_gate, b_gate, w_up, b_up, w_down, b_down)
    return dest_flat, ys, topw, x1, gt2


def kernel(x, c, w_ada, b_ada, norm1_g, w_in, conv_w, conv_b, dt_bias, a_log, d_skip, ssm_norm_g, gla_wg2,
           gla_bg, gla_norm_g, w_out, norm2_g, w_router, b_router, w_gate, b_gate, w_up, b_up, w_down, b_down,
           final_norm_g):
    batch, seq, d = x.shape
    assert w_ada.shape[0] == 1, "single-layer trunk"
    assert d == SUBLANES * LANES, "token rows are moved as one (8, 128) f32 tile each"
    assert seq % min(seq, max(TOKEN_TILE, SSM_CHUNK, GLA_STEP_ROWS, COMBINE_TILE, DISPATCH_TILE)) == 0
    assert seq % max(SSM_CHUNK, GLA_STEP_ROWS) == 0
    x2 = x.reshape(batch * seq, d)
    c_pad = jnp.zeros((SUBLANES, d), F32).at[:batch].set(c)
    mod = _ada(c_pad, w_ada[0], b_ada)
    dest_flat, ys, topw, x1, gt2 = _layer(
        x2, mod, batch, seq, norm1_g[0], w_in[0], conv_w[0], conv_b[0], dt_bias[0], a_log[0], d_skip[0],
        ssm_norm_g[0], gla_wg2[0], gla_bg[0], gla_norm_g[0], w_out[0], norm2_g[0], w_router[0], b_router[0],
        w_gate[0], b_gate[0], w_up[0], b_up[0], w_down[0], b_down[0])
    out = _combine(dest_flat, ys, topw, x1, gt2, final_norm_g.reshape(1, d), seq)
    return out.reshape(batch, seq, d)
```

```python
import functools

import jax
import jax.numpy as jnp
from jax import lax
from jax.experimental import pallas as pl
from jax.experimental.pallas import tpu as pltpu

F32 = jnp.float32
BF16 = jnp.bfloat16
I32 = jnp.int32
HIGHEST = lax.Precision.HIGHEST

EPS = 1e-6
SSM_HEAD_DIM = 64
SSM_GROUPS = 2
SSM_STATE = 128
SSM_CONV = 4
SSM_CHUNK = 128
GLA_HEADS = 4
GLA_GATE_RANK = 16
GLA_GATE_NORM = 16.0
GLA_CHUNK = 64
TOP_K = 4
SWIGLU_LIMIT = 7.0
SWIGLU_ALPHA = 1.702

LANES = 128
SUBLANES = 8
VMEM_LIMIT_BYTES = 56 * 1024 * 1024

TOKEN_TILE = 512
SSD_STEP_ROWS = 256
CONV_CARRY = 16
GLA_STEP_ROWS = 256
ROUTE_TILE = 512
EXPERT_BLOCK = 256
DISPATCH_TILE = 512
COMBINE_TILE = 256

_NT = (((1,), (1,)), ((), ()))
_TN = (((0,), (0,)), ((), ()))


def _sigmoid(v):
    return 0.5 * jnp.tanh(0.5 * v) + 0.5


def _silu(v):
    return v * _sigmoid(v)


def _softplus(v):
    return jnp.maximum(v, 0.0) + jnp.log1p(jnp.exp(-jnp.abs(v)))


def _log_sigmoid(v):
    return jnp.minimum(v, 0.0) - jnp.log(1.0 + jnp.exp(-jnp.abs(v)))


def _rms(v):
    return v * lax.rsqrt(jnp.mean(v * v, axis=-1, keepdims=True) + EPS)


def _pieces(a, n):
    out = []
    for _ in range(n - 1):
        p = a.astype(BF16)
        out.append(p)
        a = a - p.astype(F32)
    out.append(a.astype(BF16))
    return out


def _dot_pieces(a, b, n, dims=None):
    dims = dims or (((a.ndim - 1,), (0,)), ((), ()))
    return sum(lax.dot_general(p, b, dims, preferred_element_type=F32) for p in _pieces(a, n))


def _token_rows_load(ref, rows):
    return jnp.concatenate([ref[pl.ds(s, rows, stride=SUBLANES), :] for s in range(SUBLANES)], axis=1)


def _token_rows_store(ref, v):
    rows = v.shape[0]
    for s in range(SUBLANES):
        ref[pl.ds(s, rows, stride=SUBLANES), :] = v[:, s * LANES:(s + 1) * LANES]


def _pack_token_rows(stage_ref, v):
    _token_rows_store(stage_ref, v)
    return stage_ref[...].astype(BF16)


def _unpack_token_rows(stage_ref, words, rows):
    stage_ref[...] = words.astype(F32)
    return _token_rows_load(stage_ref, rows)


def _params(*semantics):
    return pltpu.CompilerParams(dimension_semantics=semantics, vmem_limit_bytes=VMEM_LIMIT_BYTES)


def _const_spec(shape):
    nd = len(shape)
    return pl.BlockSpec(shape, lambda *_: (0,) * nd)


def _ada_kernel(c_ref, w_ref, b_ref, o_ref):
    cond = _silu(c_ref[...])
    o_ref[...] = jnp.dot(cond, w_ref[...], precision=HIGHEST, preferred_element_type=F32) + b_ref[...]


def _ada(c_pad, w_ada, b_ada):
    rows, d = c_pad.shape
    n = w_ada.shape[1]
    tn = d
    return pl.pallas_call(
        _ada_kernel,
        out_shape=jax.ShapeDtypeStruct((rows, n), F32),
        grid=(n // tn,),
        in_specs=[pl.BlockSpec((rows, d), lambda j: (0, 0)),
                  pl.BlockSpec((d, tn), lambda j: (0, j)),
                  pl.BlockSpec((1, tn), lambda j: (0, j))],
        out_specs=pl.BlockSpec((rows, tn), lambda j: (0, j)),
        compiler_params=_params("arbitrary"),
        name="ada",
    )(c_pad, w_ada, b_ada)


def _inproj_kernel(x_ref, sc_ref, sh_ref, g_ref, *refs):
    n_w = len(refs) // 2
    w_refs, o_refs = refs[:n_w], refs[n_w:]
    h = (_rms(x_ref[...]) * g_ref[...]) * (1.0 + sc_ref[0]) + sh_ref[0]
    hb = h.astype(BF16)
    for w_ref, o_ref in zip(w_refs, o_refs):
        o_ref[...] = jnp.dot(hb, w_ref[...], preferred_element_type=F32).astype(o_ref.dtype)


def _inproj(x2, sc, sh, g, weights, out_dtypes, seq):
    t, d = x2.shape
    tm = min(TOKEN_TILE, seq)
    per_batch = seq // tm
    mod_spec = pl.BlockSpec((1, 1, d), lambda i: (i // per_batch, 0, 0))
    return pl.pallas_call(
        _inproj_kernel,
        out_shape=[jax.ShapeDtypeStruct((t, w.shape[1]), dt) for w, dt in zip(weights, out_dtypes)],
        grid=(t // tm,),
        in_specs=[pl.BlockSpec((tm, d), lambda i: (i, 0)), mod_spec, mod_spec, _const_spec((1, d))]
                 + [_const_spec(w.shape) for w in weights],
        out_specs=[pl.BlockSpec((tm, w.shape[1]), lambda i: (i, 0)) for w in weights],
        compiler_params=_params("arbitrary"),
        name="inproj",
    )(x2, sc, sh, g, *weights)


def _ssd_kernel(tail_ref, xbc_ref, z_ref, sm_ref, cw_ref, cb_ref, dtb_r_ref, dtb_c_ref, alog_r_ref, alog_c_ref,
                dsk_ref, g_ref, y_ref, st_ref, *, n_heads, d_ssm):
    L = SSM_CHUNK
    P = SSM_HEAD_DIM
    N = SSM_STATE
    G = SSM_GROUPS
    gw = d_ssm // G
    R = xbc_ref.shape[0]
    c = pl.program_id(1)

    @pl.when(c == 0)
    def _():
        st_ref[...] = jnp.zeros(st_ref.shape, F32)

    tail = tail_ref[...]
    tail = jnp.where(c == 0, jnp.zeros_like(tail), tail)
    conv_in = jnp.concatenate([tail, xbc_ref[...]], axis=0)

    ri = lax.broadcasted_iota(I32, (L, L), 0)
    ci = lax.broadcasted_iota(I32, (L, L), 1)
    causal = ri >= ci
    causal_b = causal.astype(BF16)
    upper_b = (ri <= ci).astype(BF16)
    wr = lax.broadcasted_iota(I32, ((SSM_CONV - 1) * L, CONV_CARRY + L), 0)
    wc = lax.broadcasted_iota(I32, ((SSM_CONV - 1) * L, CONV_CARRY + L), 1)
    shifts = (wc == (wr % L) + CONV_CARRY - (SSM_CONV - 1) + wr // L).astype(BF16)
    hh = lax.broadcasted_iota(I32, (n_heads, d_ssm), 0)
    jj = lax.broadcasted_iota(I32, (n_heads, d_ssm), 1)
    expand = ((jj // P) == hh).astype(BF16)
    expand2 = jnp.concatenate([expand, expand], axis=0)
    lane = lax.broadcasted_iota(I32, (L, LANES), 1)
    first_half = lane < P
    heads_per_group = n_heads // G
    a_row = -jnp.exp(alog_r_ref[...])
    a_col = -jnp.exp(alog_c_ref[...])

    for s in range(R // L):
        rows = slice(s * L, (s + 1) * L)
        window = conv_in[s * L:s * L + CONV_CARRY + L, :]
        taps = jnp.dot(shifts, window, preferred_element_type=F32)
        acc = cb_ref[...] + cw_ref[SSM_CONV - 1:SSM_CONV, :] * window[CONV_CARRY:, :].astype(F32)
        for j in range(SSM_CONV - 1):
            acc = acc + cw_ref[j:j + 1, :] * taps[j * L:(j + 1) * L, :]
        act = _silu(acc)
        xs = act[:, :d_ssm]
        bm = act[:, d_ssm:d_ssm + G * N].astype(BF16)
        cm = act[:, d_ssm + G * N:].astype(BF16)

        sm = sm_ref[rows, :]
        dt_col = _softplus(sm[:, :n_heads] + dtb_r_ref[...])
        dt_row = _softplus(sm.T[:n_heads, :] + dtb_c_ref[...])
        cs_col = sum(jnp.dot(causal_b, p, preferred_element_type=F32) for p in _pieces(dt_col * a_row, 3))
        cs_row = _dot_pieces(dt_row * a_col, upper_b, 3)
        cs_last = cs_col[L - 1:L, :]

        per_head = jnp.concatenate([dt_col, jnp.exp(cs_col), jnp.exp(cs_last - cs_col)], axis=0)
        per_ch = jnp.dot(jnp.concatenate(_pieces(per_head, 2), axis=1), expand2,
                         preferred_element_type=F32)
        dt_e, ecs_e, dte_e = per_ch[0:L], per_ch[L:2 * L], per_ch[2 * L:3 * L]
        chunk_decay = ecs_e[L - 1:L, :]

        xdt = xs * dt_e
        xdt_b = xdt.astype(BF16)
        xdec_b = (xdt * dte_e).astype(BF16)

        y_parts = []
        y_off_parts = []
        for g in range(G):
            bm_g = bm[:, g * N:(g + 1) * N]
            cm_g = cm[:, g * N:(g + 1) * N]
            cb = lax.dot_general(cm_g, bm_g, _NT, preferred_element_type=F32)
            prev = st_ref[g]
            y_off_parts.append(jnp.dot(cm_g, prev.astype(BF16), preferred_element_type=F32))
            s_new = lax.dot_general(bm_g, xdec_b[:, g * gw:(g + 1) * gw], _TN, preferred_element_type=F32)
            st_ref[g] = prev * chunk_decay[:, g * gw:(g + 1) * gw] + s_new
            for p in range(heads_per_group // 2):
                h0 = g * heads_per_group + 2 * p
                ms = []
                for h in (h0, h0 + 1):
                    diff = cs_col[:, h:h + 1] - cs_row[h:h + 1, :]
                    ms.append((cb * jnp.exp(jnp.where(causal, diff, -jnp.inf))).astype(BF16))
                lhs = jnp.concatenate(ms, axis=1)
                xp = xdt_b[:, h0 * P:(h0 + 2) * P]
                zero = jnp.zeros_like(xp)
                rhs = jnp.concatenate([jnp.where(first_half, xp, zero), jnp.where(first_half, zero, xp)], axis=0)
                y_parts.append(jnp.dot(lhs, rhs, preferred_element_type=F32))
        y = jnp.concatenate(y_parts, axis=1) + jnp.concatenate(y_off_parts, axis=1) * ecs_e + dsk_ref[...] * xs
        y = y * _silu(z_ref[rows, :].astype(F32))
        y = jnp.concatenate([_rms(y[:, g * gw:(g + 1) * gw]) for g in range(G)], axis=1) * g_ref[...]
        y_ref[rows, :] = y.astype(y_ref.dtype)


def _ssd(xbc, z, small, conv_w, conv_b, dt_bias, a_log, d_skip_e, norm_g, batch, seq):
    t, cd = xbc.shape
    d_ssm = z.shape[1]
    n_heads = dt_bias.shape[0]
    L = min(SSD_STEP_ROWS, seq)
    nc = seq // L
    row = lambda b, c: (b * nc + c, 0)
    tail = lambda b, c: (jnp.maximum((b * nc + c) * (L // CONV_CARRY) - 1, 0), 0)
    kern = functools.partial(_ssd_kernel, n_heads=n_heads, d_ssm=d_ssm)
    return pl.pallas_call(
        kern,
        out_shape=jax.ShapeDtypeStruct((t, d_ssm), BF16),
        grid=(batch, nc),
        in_specs=[pl.BlockSpec((CONV_CARRY, cd), tail),
                  pl.BlockSpec((L, cd), row), pl.BlockSpec((L, d_ssm), row), pl.BlockSpec((L, LANES), row),
                  _const_spec(conv_w.shape), _const_spec((1, cd)),
                  _const_spec((1, n_heads)), _const_spec((n_heads, 1)),
                  _const_spec((1, n_heads)), _const_spec((n_heads, 1)),
                  _const_spec((1, d_ssm)), _const_spec((1, d_ssm))],
        out_specs=pl.BlockSpec((L, d_ssm), row),
        scratch_shapes=[pltpu.VMEM((SSM_GROUPS, SSM_STATE, d_ssm // SSM_GROUPS), F32)],
        compiler_params=_params("arbitrary", "arbitrary"),
        name="ssd",
    )(xbc, xbc, z, small, conv_w, conv_b.reshape(1, cd), dt_bias.reshape(1, n_heads), dt_bias.reshape(n_heads, 1),
      a_log.reshape(1, n_heads), a_log.reshape(n_heads, 1), d_skip_e, norm_g.reshape(1, d_ssm))


def _gla_kernel(q_ref, k_ref, v_ref, r_ref, sm_ref, wg2_ref, bg_ref, gn_ref, o_ref, st_ref, *, gate_col):
    L = GLA_CHUNK
    H = GLA_HEADS
    dk = q_ref.shape[1] // H
    dv = v_ref.shape[1] // H
    c = pl.program_id(1)

    @pl.when(c == 0)
    def _():
        st_ref[...] = jnp.zeros(st_ref.shape, F32)

    R = q_ref.shape[0]
    n_chunks = R // L
    ri = lax.broadcasted_iota(I32, (R, R), 0)
    ci = lax.broadcasted_iota(I32, (R, R), 1)
    same_chunk = (ri // L) == (ci // L)
    causal = jnp.logical_and(same_chunk, ri >= ci)
    tril = causal.astype(BF16)
    later = jnp.logical_and(same_chunk, ri < ci).astype(BF16)
    sr = lax.broadcasted_iota(I32, (R, n_chunks * LANES), 0)
    sc = lax.broadcasted_iota(I32, (R, n_chunks * LANES), 1)
    last_rows = (sr == (sc // LANES) * L + (L - 1)).astype(BF16)
    wg2_hi, wg2_lo = _pieces(wg2_ref[...], 2)

    q = q_ref[...].astype(F32) * (dk ** -0.5)
    k = k_ref[...].astype(F32)
    v = v_ref[...]
    r = r_ref[...].astype(F32)
    g_hi, g_lo = _pieces(sm_ref[:, gate_col:gate_col + GLA_GATE_RANK], 2)
    pre = (jnp.dot(g_hi, wg2_hi, preferred_element_type=F32) + jnp.dot(g_lo, wg2_hi, preferred_element_type=F32)
           + jnp.dot(g_hi, wg2_lo, preferred_element_type=F32)) + bg_ref[...]
    gk_p = _pieces(_log_sigmoid(pre) / GLA_GATE_NORM, 3)
    bcum = sum(jnp.dot(tril, p, preferred_element_type=F32) for p in gk_p)
    to_end = sum(jnp.dot(later, p, preferred_element_type=F32) for p in gk_p)
    q_t = (q * jnp.exp(bcum)).astype(BF16)
    k_t = (k * jnp.exp(-bcum)).astype(BF16)
    k_dec = (k * jnp.exp(to_end)).astype(BF16)
    dcol = jnp.exp(_dot_pieces(bcum, last_rows, 3, dims=_TN))
    outs = []
    for h in range(H):
        ks = slice(h * dk, (h + 1) * dk)
        vs = slice(h * dv, (h + 1) * dv)
        att = lax.dot_general(q_t[:, ks], k_t[:, ks], _NT, preferred_element_type=F32)
        att = jnp.where(causal, att, 0.0).astype(BF16)
        o = jnp.dot(att, v[:, vs], preferred_element_type=F32)
        state = st_ref[h]
        inter = []
        for c in range(n_chunks):
            rows = slice(c * L, (c + 1) * L)
            inter.append(jnp.dot(q_t[rows, ks], state.astype(BF16), preferred_element_type=F32))
            s_new = lax.dot_general(k_dec[rows, ks], v[rows, vs], _TN, preferred_element_type=F32)
            dec = dcol[ks, c * LANES:(c + 1) * LANES]
            state = state * jnp.concatenate([dec] * (dv // LANES), axis=1) + s_new
        st_ref[h] = state
        o = o + jnp.concatenate(inter, axis=0)
        outs.append(_rms(o) * gn_ref[...] * _silu(r[:, vs]))
    o_ref[...] = jnp.concatenate(outs, axis=1).astype(o_ref.dtype)


def _gla(q, k, v, r, small, wg2, bg, norm_g, batch, seq, gate_col):
    t, dkt = q.shape
    dvt = v.shape[1]
    rows = min(GLA_STEP_ROWS, seq)
    nc = seq // rows
    row = lambda b, c: (b * nc + c, 0)
    kern = functools.partial(_gla_kernel, gate_col=gate_col)
    return pl.pallas_call(
        kern,
        out_shape=jax.ShapeDtypeStruct((t, dvt), BF16),
        grid=(batch, nc),
        in_specs=[pl.BlockSpec((rows, dkt), row), pl.BlockSpec((rows, dkt), row), pl.BlockSpec((rows, dvt), row),
                  pl.BlockSpec((rows, dvt), row), pl.BlockSpec((rows, LANES), row),
                  _const_spec(wg2.shape), _const_spec((1, dkt)), _const_spec((1, dvt // GLA_HEADS))],
        out_specs=pl.BlockSpec((rows, dvt), row),
        scratch_shapes=[pltpu.VMEM((GLA_HEADS, dkt // GLA_HEADS, dvt // GLA_HEADS), F32)],
        compiler_params=_params("arbitrary", "arbitrary"),
        name="gla",
    )(q, k, v, r, small, wg2, bg.reshape(1, dkt), norm_g.reshape(1, dvt // GLA_HEADS))


def _outproj_kernel(y_ref, o_ref, x_ref, gt_ref, sc_ref, sh_ref, g_ref, wy_ref, wo_ref, wr_ref, br_ref,
                    x1_ref, h_ref, ti_ref, tw_ref, stage_ref):
    mix = (jnp.dot(y_ref[...], wy_ref[...], preferred_element_type=F32)
           + jnp.dot(o_ref[...], wo_ref[...], preferred_element_type=F32))
    x1 = x_ref[...] + gt_ref[0] * mix
    x1_ref[...] = x1
    h = (_rms(x1) * g_ref[...]) * (1.0 + sc_ref[0]) + sh_ref[0]
    h_ref[...] = _pack_token_rows(stage_ref, h)
    n_e = br_ref.shape[1]
    h_hi, h_lo = _pieces(h, 2)
    wr = wr_ref[...]
    hw = jnp.dot(h_hi, wr, preferred_element_type=F32)
    logits = (hw[:, :n_e] + hw[:, n_e:] + jnp.dot(h_lo, wr[:, :n_e], preferred_element_type=F32)) + br_ref[...]
    lane = lax.broadcasted_iota(I32, logits.shape, 1)
    vals, idxs = [], []
    for _ in range(TOP_K):
        m = jnp.max(logits, axis=1, keepdims=True)
        idx = jnp.min(jnp.where(logits == m, lane, n_e), axis=1, keepdims=True)
        vals.append(m)
        idxs.append(idx)
        logits = jnp.where(lane == idx, -jnp.inf, logits)
    exps = [jnp.exp(v - vals[0]) for v in vals]
    denom = functools.reduce(lambda a, b: a + b, exps)
    ti_ref[...] = jnp.concatenate(idxs, axis=1)
    tw_ref[...] = jnp.concatenate([e / denom for e in exps], axis=1)


def _outproj(y, o, x2, gt, sc, sh, g, wy, wo, w_router, b_router, seq):
    t, d = x2.shape
    n_e = w_router.shape[1]
    tm = min(TOKEN_TILE, seq)
    per_batch = seq // tm
    row = lambda i: (i, 0)
    mod_spec = pl.BlockSpec((1, 1, d), lambda i: (i // per_batch, 0, 0))
    wr_hi = w_router.astype(BF16)
    wr_cat = jnp.concatenate([wr_hi, (w_router - wr_hi.astype(F32)).astype(BF16)], axis=1)
    return pl.pallas_call(
        _outproj_kernel,
        out_shape=[jax.ShapeDtypeStruct((t, d), F32), jax.ShapeDtypeStruct((t * SUBLANES, LANES), BF16),
                   jax.ShapeDtypeStruct((t, TOP_K), I32), jax.ShapeDtypeStruct((t, TOP_K), F32)],
        grid=(t // tm,),
        in_specs=[pl.BlockSpec((tm, y.shape[1]), row), pl.BlockSpec((tm, o.shape[1]), row),
                  pl.BlockSpec((tm, d), row), mod_spec, mod_spec, mod_spec, _const_spec((1, d)),
                  _const_spec(wy.shape), _const_spec(wo.shape), _const_spec(wr_cat.shape),
                  _const_spec((1, n_e))],
        out_specs=[pl.BlockSpec((tm, d), row), pl.BlockSpec((tm * SUBLANES, LANES), row),
                   pl.BlockSpec((tm, TOP_K), row), pl.BlockSpec((tm, TOP_K), row)],
        scratch_shapes=[pltpu.VMEM((tm * SUBLANES, LANES), F32)],
        compiler_params=_params("arbitrary"),
        name="outproj",
    )(y, o, x2, gt, sc, sh, g, wy, wo, wr_cat, b_router.reshape(1, n_e))


def _route_kernel(ti_ref, dest_ref, be_ref, pend_ref, cnt_ref, run_ref, *, n_experts, n_blocks_pad):
    phase = pl.program_id(0)
    i = pl.program_id(1)
    tr = ti_ref.shape[0]
    ti = ti_ref[...]
    lane = lax.broadcasted_iota(I32, (tr, LANES), 1)
    onehots = [ti[:, k:k + 1] == lane for k in range(TOP_K)]
    cnt = functools.reduce(lambda a, b: a + b, [oh.astype(F32) for oh in onehots])
    tile_counts = jnp.sum(cnt, axis=0, keepdims=True)

    @pl.when(jnp.logical_and(phase == 0, i == 0))
    def _():
        cnt_ref[...] = jnp.zeros(cnt_ref.shape, F32)

    @pl.when(phase == 0)
    def _():
        cnt_ref[...] = cnt_ref[...] + tile_counts

    @pl.when(jnp.logical_and(phase == 1, i == 0))
    def _():
        counts = cnt_ref[...]
        padded = jnp.ceil(counts / EXPERT_BLOCK) * EXPERT_BLOCK
        ri = lax.broadcasted_iota(I32, (LANES, LANES), 0)
        ci = lax.broadcasted_iota(I32, (LANES, LANES), 1)
        pend = jnp.dot(padded, (ri <= ci).astype(F32), precision=HIGHEST, preferred_element_type=F32)
        pend_ref[...] = pend
        run_ref[...] = pend - padded
        start = (lax.broadcasted_iota(I32, (n_blocks_pad, LANES), 0) * EXPERT_BLOCK).astype(F32)
        col = lax.broadcasted_iota(I32, (n_blocks_pad, LANES), 1)
        ended = jnp.logical_and(pend[0:1, :] <= start, col < n_experts)
        be = jnp.sum(ended.astype(F32), axis=1, keepdims=True)
        be_ref[...] = jnp.minimum(be, n_experts - 1).astype(I32)

    @pl.when(phase == 1)
    def _():
        ri = lax.broadcasted_iota(I32, (tr, tr), 0)
        ci = lax.broadcasted_iota(I32, (tr, tr), 1)
        before = jnp.dot((ri > ci).astype(BF16), cnt.astype(BF16), preferred_element_type=F32)
        base = run_ref[0:1, :] + before
        dest = [jnp.sum(jnp.where(oh, base, 0.0), axis=1, keepdims=True) for oh in onehots]
        dest_ref[...] = jnp.concatenate(dest, axis=1).astype(I32)
        run_ref[...] = run_ref[...] + tile_counts


def _route(topi, n_experts, n_blocks):
    t = topi.shape[0]
    tr = min(ROUTE_TILE, t)
    n_blocks_pad = -(-n_blocks // SUBLANES) * SUBLANES
    kern = functools.partial(_route_kernel, n_experts=n_experts, n_blocks_pad=n_blocks_pad)
    return pl.pallas_call(
        kern,
        out_shape=[jax.ShapeDtypeStruct((t, TOP_K), I32), jax.ShapeDtypeStruct((n_blocks_pad, 1), I32),
                   jax.ShapeDtypeStruct((SUBLANES, LANES), F32)],
        grid=(2, t // tr),
        in_specs=[pl.BlockSpec((tr, TOP_K), lambda p, i: (i, 0))],
        out_specs=[pl.BlockSpec((tr, TOP_K), lambda p, i: (i * p, 0)),
                   pl.BlockSpec((n_blocks_pad, 1), lambda p, i: (0, 0)),
                   pl.BlockSpec((SUBLANES, LANES), lambda p, i: (0, 0))],
        scratch_shapes=[pltpu.VMEM((SUBLANES, LANES), F32), pltpu.VMEM((SUBLANES, LANES), F32)],
        compiler_params=_params("arbitrary", "arbitrary"),
        name="route",
    )(topi)


def _dispatch_kernel(pend_ref, dest_hbm, h_ref, xs_hbm, idx_ref, zero_ref, idx_sem, row_sem, *, n_experts):
    i = pl.program_id(0)
    n_idx = idx_ref.shape[0]
    tg = n_idx // TOP_K

    @pl.when(i == 0)
    def _():
        zero_ref[...] = jnp.zeros(zero_ref.shape, zero_ref.dtype)
        for e in range(n_experts):
            end = pend_ref[e]
            prev = pend_ref[e - 1] if e > 0 else 0

            @pl.when(end > prev)
            def _():
                start = pl.multiple_of((end - EXPERT_BLOCK) * SUBLANES, EXPERT_BLOCK * SUBLANES)
                cp = pltpu.make_async_copy(zero_ref, xs_hbm.at[pl.ds(start, EXPERT_BLOCK * SUBLANES)], row_sem)
                cp.start()
                cp.wait()

        n_blocks = xs_hbm.shape[0] // (EXPERT_BLOCK * SUBLANES)
        total = pend_ref[n_experts - 1]
        for b in range(n_blocks - n_experts, n_blocks):
            @pl.when(b * EXPERT_BLOCK >= total)
            def _():
                cp = pltpu.make_async_copy(
                    zero_ref, xs_hbm.at[pl.ds(b * EXPERT_BLOCK * SUBLANES, EXPERT_BLOCK * SUBLANES)], row_sem)
                cp.start()
                cp.wait()

    idx_cp = pltpu.make_async_copy(dest_hbm.at[pl.ds(i * n_idx, n_idx)], idx_ref, idx_sem)
    idx_cp.start()
    idx_cp.wait()

    def issue(tl, carry):
        src = h_ref.at[pl.ds(pl.multiple_of(tl * SUBLANES, SUBLANES), SUBLANES)]
        for k in range(TOP_K):
            d = pl.multiple_of(idx_ref[tl * TOP_K + k] * SUBLANES, SUBLANES)
            pltpu.make_async_copy(src, xs_hbm.at[pl.ds(d, SUBLANES)], row_sem).start(priority=k % 2)
        return carry

    lax.fori_loop(0, tg, issue, 0)
    for _ in range(TOP_K):
        pltpu.make_async_copy(h_ref, xs_hbm.at[pl.ds(0, tg * SUBLANES)], row_sem).wait()


def _dispatch(pend_i, dest_flat, h, n_rows, n_experts):
    t = h.shape[0] // SUBLANES
    tg = min(DISPATCH_TILE, t)
    kern = functools.partial(_dispatch_kernel, n_experts=n_experts)
    return pl.pallas_call(
        kern,
        out_shape=jax.ShapeDtypeStruct((n_rows * SUBLANES, LANES), h.dtype),
        grid_spec=pltpu.PrefetchScalarGridSpec(
            num_scalar_prefetch=1,
            grid=(t // tg,),
            in_specs=[pl.BlockSpec(memory_space=pl.ANY),
                      pl.BlockSpec((tg * SUBLANES, LANES), lambda i, pend: (i, 0))],
            out_specs=pl.BlockSpec(memory_space=pl.ANY),
            scratch_shapes=[pltpu.SMEM((tg * TOP_K,), I32), pltpu.VMEM((EXPERT_BLOCK * SUBLANES, LANES), h.dtype),
                            pltpu.SemaphoreType.DMA, pltpu.SemaphoreType.DMA]),
        compiler_params=pltpu.CompilerParams(dimension_semantics=("arbitrary",), has_side_effects=True,
                                             vmem_limit_bytes=VMEM_LIMIT_BYTES),
        name="dispatch",
    )(pend_i, dest_flat, h)


def _expert_kernel(be_ref, nu_ref, pend_ref, xs_ref, wg_hbm, bg_ref, wu_hbm, bu_ref, wd_hbm, bd_ref, y_ref,
                   wg_f, wu_f, wd_f, wg_b, wu_b, wd_b, xstage_ref, ystage_ref, slot_ref, sems):
    i = pl.program_id(0)
    used = i < nu_ref[0]
    e = be_ref[i]

    def fetch(expert, slot):
        return [pltpu.make_async_copy(src.at[expert], dst.at[slot], sems.at[slot])
                for src, dst in ((wg_hbm, wg_f), (wu_hbm, wu_f), (wd_hbm, wd_f))]

    @pl.when(i == 0)
    def _():
        slot_ref[0] = 0
        for cp in fetch(e, 0):
            cp.start()

    first_of_expert = jnp.logical_or(i == 0, e != be_ref[jnp.maximum(i - 1, 0)])

    @pl.when(jnp.logical_and(used, first_of_expert))
    def _():
        slot = slot_ref[0]
        for cp in fetch(e, slot):
            cp.wait()
        wg_b[...] = wg_f[slot].astype(BF16)
        wu_b[...] = wu_f[slot].astype(BF16)
        wd_b[...] = wd_f[slot].astype(BF16)
        nxt = lax.div(pend_ref[e], EXPERT_BLOCK)

        @pl.when(nxt < nu_ref[0])
        def _():
            for cp in fetch(be_ref[nxt], 1 - slot):
                cp.start()

        slot_ref[0] = 1 - slot

    @pl.when(used)
    def _():
        x = _unpack_token_rows(xstage_ref, xs_ref[...], EXPERT_BLOCK).astype(BF16)
        gate = jnp.minimum(jnp.dot(x, wg_b[...], preferred_element_type=F32) + bg_ref[...], SWIGLU_LIMIT)
        up = jnp.clip(jnp.dot(x, wu_b[...], preferred_element_type=F32) + bu_ref[...],
                      -SWIGLU_LIMIT, SWIGLU_LIMIT)
        glu = gate * _sigmoid(SWIGLU_ALPHA * gate)
        mid = ((up + 1.0) * glu).astype(BF16)
        y = jnp.dot(mid, wd_b[...], preferred_element_type=F32) + bd_ref[...]
        y_ref[...] = _pack_token_rows(ystage_ref, y)

    @pl.when(jnp.logical_not(used))
    def _():
        y_ref[...] = jnp.zeros(y_ref.shape, y_ref.dtype)


def _experts(block_e, n_used, pend_i, xs, w_gate, b_gate, w_up, b_up, w_down, b_down):
    n_rows = xs.shape[0] // SUBLANES
    n_e, d, f = w_gate.shape
    nb = n_rows // EXPERT_BLOCK
    blk = (EXPERT_BLOCK * SUBLANES, LANES)
    stage = pltpu.VMEM((EXPERT_BLOCK * SUBLANES, LANES), F32)
    last = lambda i, be, nu, pend: jnp.maximum(jnp.minimum(i, nu[0] - 1), 0)
    bspec = lambda n: pl.BlockSpec((None, 1, n), lambda i, be, nu, pend: (be[last(i, be, nu, pend)], 0, 0))
    hbm = pl.BlockSpec(memory_space=pl.ANY)
    return pl.pallas_call(
        _expert_kernel,
        out_shape=jax.ShapeDtypeStruct((n_rows * SUBLANES, LANES), BF16),
        grid_spec=pltpu.PrefetchScalarGridSpec(
            num_scalar_prefetch=3,
            grid=(nb,),
            in_specs=[pl.BlockSpec(blk, lambda i, be, nu, pend: (last(i, be, nu, pend), 0)),
                      hbm, bspec(f), hbm, bspec(f), hbm, bspec(d)],
            out_specs=pl.BlockSpec(blk, lambda i, be, nu, pend: (i, 0)),
            scratch_shapes=[pltpu.VMEM((2, d, f), F32), pltpu.VMEM((2, d, f), F32), pltpu.VMEM((2, f, d), F32),
                            pltpu.VMEM((d, f), BF16), pltpu.VMEM((d, f), BF16), pltpu.VMEM((f, d), BF16),
                            stage, stage, pltpu.SMEM((1,), I32), pltpu.SemaphoreType.DMA((2,))]),
        compiler_params=_params("arbitrary"),
        name="experts",
    )(block_e, n_used, pend_i, xs, w_gate, b_gate.reshape(n_e, 1, f), w_up, b_up.reshape(n_e, 1, f),
      w_down, b_down.reshape(n_e, 1, d))


def _combine_kernel(dest_hbm, ys_hbm, tw_ref, x1_ref, gt_ref, g_ref, o_ref, idx_ref, buf_ref, stage_ref,
                    idx_sem, row_sem):
    i = pl.program_id(0)
    n = pl.num_programs(0)
    n_idx = idx_ref.shape[1]
    tc = n_idx // TOP_K
    slot = lax.rem(i, 2)

    def idx_copy(tile, s):
        return pltpu.make_async_copy(dest_hbm.at[pl.ds(tile * n_idx, n_idx)], idx_ref.at[s], idx_sem.at[s])

    def issue_rows(s):
        def issue(tl, carry):
            dst_row = pl.multiple_of(tl * SUBLANES, SUBLANES)
            for k in range(TOP_K):
                d = pl.multiple_of(idx_ref[s, tl * TOP_K + k] * SUBLANES, SUBLANES)
                pltpu.make_async_copy(ys_hbm.at[pl.ds(d, SUBLANES)], buf_ref.at[s, k, pl.ds(dst_row, SUBLANES)],
                                      row_sem.at[s]).start(priority=k % 2)
            return carry

        lax.fori_loop(0, tc, issue, 0)

    @pl.when(i == 0)
    def _():
        idx_copy(0, 0).start()
        idx_copy(0, 0).wait()
        issue_rows(0)

        @pl.when(n > 1)
        def _():
            idx_copy(1, 1).start()

    @pl.when(i + 1 < n)
    def _():
        idx_copy(i + 1, 1 - slot).wait()
        issue_rows(1 - slot)

        @pl.when(i + 2 < n)
        def _():
            idx_copy(i + 2, slot).start()

    for k in range(TOP_K):
        pltpu.make_async_copy(ys_hbm.at[pl.ds(0, tc * SUBLANES)], buf_ref.at[slot, k], row_sem.at[slot]).wait()

    tw = tw_ref[...]
    ffn = tw[:, 0:1] * _unpack_token_rows(stage_ref.at[0], buf_ref[slot, 0], tc)
    for k in range(1, TOP_K):
        ffn = ffn + tw[:, k:k + 1] * _unpack_token_rows(stage_ref.at[k], buf_ref[slot, k], tc)
    x2 = x1_ref[...] + gt_ref[0] * ffn
    o_ref[...] = _rms(x2) * g_ref[...]


def _combine(dest_flat, ys, topw, x1, gt, g, seq):
    t, d = x1.shape
    tc = min(COMBINE_TILE, seq)
    per_batch = seq // tc
    row = lambda i: (i, 0)
    return pl.pallas_call(
        _combine_kernel,
        out_shape=jax.ShapeDtypeStruct((t, d), F32),
        grid=(t // tc,),
        in_specs=[pl.BlockSpec(memory_space=pl.ANY), pl.BlockSpec(memory_space=pl.ANY),
                  pl.BlockSpec((tc, TOP_K), row), pl.BlockSpec((tc, d), row),
                  pl.BlockSpec((1, 1, d), lambda i: (i // per_batch, 0, 0)), _const_spec((1, d))],
        out_specs=pl.BlockSpec((tc, d), row),
        scratch_shapes=[pltpu.SMEM((2, tc * TOP_K), I32), pltpu.VMEM((2, TOP_K, tc * SUBLANES, LANES), BF16),
                        pltpu.VMEM((TOP_K, tc * SUBLANES, LANES), F32),
                        pltpu.SemaphoreType.DMA((2,)), pltpu.SemaphoreType.DMA((2,))],
        compiler_params=_params("arbitrary"),
        name="combine",
    )(dest_flat, ys, topw, x1, gt, g)


def _layer(x2, mod, batch, seq, norm1_g, w_in, conv_w, conv_b, dt_bias, a_log, d_skip, ssm_norm_g,
           gla_wg2, gla_bg, gla_norm_g, w_out, norm2_g, w_router, b_router,
           w_gate, b_gate, w_up, b_up, w_down, b_down):
    t, d = x2.shape
    n_heads = dt_bias.shape[0]
    d_ssm = n_heads * SSM_HEAD_DIM
    cd = conv_w.shape[1]
    dkt = gla_wg2.shape[1]
    dvt = w_out.shape[0] - d_ssm
    n_experts = w_router.shape[1]

    sh1, sc1, gt1, sh2, sc2, gt2 = [m.reshape(batch, 1, d) for m in jnp.split(mod[:batch], 6, axis=1)]

    sizes = (d_ssm, cd, n_heads, dkt, dkt, dvt, GLA_GATE_RANK, dvt)
    offs = [0]
    for s in sizes:
        offs.append(offs[-1] + s)
    col = lambda j: w_in[:, offs[j]:offs[j + 1]]
    pad = jnp.zeros((d, LANES - n_heads - GLA_GATE_RANK), F32)
    w_small = jnp.concatenate([col(2), col(6), pad], axis=1)
    weights = [col(0), col(1), col(3), col(4), col(5), col(7), w_small]
    weights = [w.astype(BF16) for w in weights]
    z, xbc, q, k, v, r, small = _inproj(x2, sc1, sh1, norm1_g.reshape(1, d), weights,
                                        [BF16] * 6 + [F32], seq)

    d_skip_e = jnp.repeat(d_skip, SSM_HEAD_DIM).reshape(1, d_ssm)
    y = _ssd(xbc, z, small, conv_w, conv_b, dt_bias, a_log, d_skip_e, ssm_norm_g, batch, seq)
    o = _gla(q, k, v, r, small, gla_wg2, gla_bg, gla_norm_g, batch, seq, gate_col=n_heads)

    x1, h2, topi, topw = _outproj(y, o, x2, gt1, sc2, sh2, norm2_g.reshape(1, d),
                                        w_out[:d_ssm].astype(BF16), w_out[d_ssm:].astype(BF16),
                                        w_router, b_router, seq)

    n_blocks = (t * TOP_K) // EXPERT_BLOCK + n_experts
    dest, block_e, pend = _route(topi, n_experts, n_blocks)
    pend_i = pend[0, :n_experts].astype(I32)
    n_used = (pend_i[n_experts - 1:] // EXPERT_BLOCK).astype(I32)
    dest_flat = dest.reshape(t * TOP_K)
    xs = _dispatch(pend_i, dest_flat, h2, n_blocks * EXPERT_BLOCK, n_experts)
    ys = _experts(block_e[:n_blocks, 0], n_used, pend_i, xs, w_gate, b_gate, w_up, b_up, w_down, b_down)
    return dest_flat, ys, topw, x1, gt2


def kernel(x, c, w_ada, b_ada, norm1_g, w_in, conv_w, conv_b, dt_bias, a_log, d_skip, ssm_norm_g, gla_wg2,
           gla_bg, gla_norm_g, w_out, norm2_g, w_router, b_router, w_gate, b_gate, w_up, b_up, w_down, b_down,
           final_norm_g):
    batch, seq, d = x.shape
    assert w_ada.shape[0] == 1, "single-layer trunk"
    assert d == SUBLANES * LANES, "token rows are moved as one (8, 128) f32 tile each"
    assert seq % min(seq, max(TOKEN_TILE, SSM_CHUNK, GLA_STEP_ROWS, COMBINE_TILE, DISPATCH_TILE)) == 0
    assert seq % max(SSM_CHUNK, GLA_STEP_ROWS) == 0
    x2 = x.reshape(batch * seq, d)
    c_pad = jnp.zeros((SUBLANES, d), F32).at[:batch].set(c)
    mod = _ada(c_pad, w_ada[0], b_ada)
    dest_flat, ys, topw, x1, gt2 = _layer(
        x2, mod, batch, seq, norm1_g[0], w_in[0], conv_w[0], conv_b[0], dt_bias[0], a_log[0], d_skip[0],
        ssm_norm_g[0], gla_wg2[0], gla_bg[0], gla_norm_g[0], w_out[0], norm2_g[0], w_router[0], b_router[0],
        w_gate[0], b_gate[0], w_up[0], b_up[0], w_down[0], b_down[0])
    out = _combine(dest_flat, ys, topw, x1, gt2, final_norm_g.reshape(1, d), seq)
    return out.reshape(batch, seq, d)
```

```python
import functools

import jax
import jax.numpy as jnp
from jax import lax
from jax.experimental import pallas as pl
from jax.experimental.pallas import tpu as pltpu

F32 = jnp.float32
BF16 = jnp.bfloat16
I32 = jnp.int32
HIGHEST = lax.Precision.HIGHEST

EPS = 1e-6
SSM_HEAD_DIM = 64
SSM_GROUPS = 2
SSM_STATE = 128
SSM_CONV = 4
SSM_CHUNK = 128
GLA_HEADS = 4
GLA_GATE_RANK = 16
GLA_GATE_NORM = 16.0
GLA_CHUNK = 64
TOP_K = 4
SWIGLU_LIMIT = 7.0
SWIGLU_ALPHA = 1.702

LANES = 128
SUBLANES = 8
VMEM_LIMIT_BYTES = 56 * 1024 * 1024

TOKEN_TILE = 512
SSD_STEP_ROWS = 256
CONV_CARRY = 16
GLA_STEP_ROWS = 256
ROUTE_TILE = 512
EXPERT_BLOCK = 256
DISPATCH_TILE = 512
COMBINE_TILE = 256

_NT = (((1,), (1,)), ((), ()))
_TN = (((0,), (0,)), ((), ()))


def _sigmoid(v):
    return 0.5 * jnp.tanh(0.5 * v) + 0.5


def _silu(v):
    return v * _sigmoid(v)


def _softplus(v):
    return jnp.maximum(v, 0.0) + jnp.log1p(jnp.exp(-jnp.abs(v)))


def _log_sigmoid(v):
    return jnp.minimum(v, 0.0) - jnp.log(1.0 + jnp.exp(-jnp.abs(v)))


def _rms(v):
    return v * lax.rsqrt(jnp.mean(v * v, axis=-1, keepdims=True) + EPS)


def _pieces(a, n):
    out = []
    for _ in range(n - 1):
        p = a.astype(BF16)
        out.append(p)
        a = a - p.astype(F32)
    out.append(a.astype(BF16))
    return out


def _dot_pieces(a, b, n, dims=None):
    dims = dims or (((a.ndim - 1,), (0,)), ((), ()))
    return sum(lax.dot_general(p, b, dims, preferred_element_type=F32) for p in _pieces(a, n))


def _token_rows_load(ref, rows):
    return jnp.concatenate([ref[pl.ds(s, rows, stride=SUBLANES), :] for s in range(SUBLANES)], axis=1)


def _token_rows_store(ref, v):
    rows = v.shape[0]
    for s in range(SUBLANES):
        ref[pl.ds(s, rows, stride=SUBLANES), :] = v[:, s * LANES:(s + 1) * LANES]


def _params(*semantics):
    return pltpu.CompilerParams(dimension_semantics=semantics, vmem_limit_bytes=VMEM_LIMIT_BYTES)


def _const_spec(shape):
    nd = len(shape)
    return pl.BlockSpec(shape, lambda *_: (0,) * nd)


def _ada_kernel(c_ref, w_ref, b_ref, o_ref):
    cond = _silu(c_ref[...])
    o_ref[...] = jnp.dot(cond, w_ref[...], precision=HIGHEST, preferred_element_type=F32) + b_ref[...]


def _ada(c_pad, w_ada, b_ada):
    rows, d = c_pad.shape
    n = w_ada.shape[1]
    tn = d
    return pl.pallas_call(
        _ada_kernel,
        out_shape=jax.ShapeDtypeStruct((rows, n), F32),
        grid=(n // tn,),
        in_specs=[pl.BlockSpec((rows, d), lambda j: (0, 0)),
                  pl.BlockSpec((d, tn), lambda j: (0, j)),
                  pl.BlockSpec((1, tn), lambda j: (0, j))],
        out_specs=pl.BlockSpec((rows, tn), lambda j: (0, j)),
        compiler_params=_params("arbitrary"),
        name="ada",
    )(c_pad, w_ada, b_ada)


def _inproj_kernel(x_ref, sc_ref, sh_ref, g_ref, *refs):
    n_w = len(refs) // 2
    w_refs, o_refs = refs[:n_w], refs[n_w:]
    h = (_rms(x_ref[...]) * g_ref[...]) * (1.0 + sc_ref[0]) + sh_ref[0]
    hb = h.astype(BF16)
    for w_ref, o_ref in zip(w_refs, o_refs):
        o_ref[...] = jnp.dot(hb, w_ref[...], preferred_element_type=F32).astype(o_ref.dtype)


def _inproj(x2, sc, sh, g, weights, out_dtypes, seq):
    t, d = x2.shape
    tm = min(TOKEN_TILE, seq)
    per_batch = seq // tm
    mod_spec = pl.BlockSpec((1, 1, d), lambda i: (i // per_batch, 0, 0))
    return pl.pallas_call(
        _inproj_kernel,
        out_shape=[jax.ShapeDtypeStruct((t, w.shape[1]), dt) for w, dt in zip(weights, out_dtypes)],
        grid=(t // tm,),
        in_specs=[pl.BlockSpec((tm, d), lambda i: (i, 0)), mod_spec, mod_spec, _const_spec((1, d))]
                 + [_const_spec(w.shape) for w in weights],
        out_specs=[pl.BlockSpec((tm, w.shape[1]), lambda i: (i, 0)) for w in weights],
        compiler_params=_params("arbitrary"),
        name="inproj",
    )(x2, sc, sh, g, *weights)


def _ssd_kernel(tail_ref, xbc_ref, z_ref, sm_ref, cw_ref, cb_ref, dtb_r_ref, dtb_c_ref, alog_r_ref, alog_c_ref,
                dsk_ref, g_ref, y_ref, st_ref, *, n_heads, d_ssm):
    L = SSM_CHUNK
    P = SSM_HEAD_DIM
    N = SSM_STATE
    G = SSM_GROUPS
    gw = d_ssm // G
    R = xbc_ref.shape[0]
    c = pl.program_id(1)

    @pl.when(c == 0)
    def _():
        st_ref[...] = jnp.zeros(st_ref.shape, F32)

    tail = tail_ref[...]
    tail = jnp.where(c == 0, jnp.zeros_like(tail), tail)
    conv_in = jnp.concatenate([tail, xbc_ref[...]], axis=0)

    ri = lax.broadcasted_iota(I32, (L, L), 0)
    ci = lax.broadcasted_iota(I32, (L, L), 1)
    causal = ri >= ci
    causal_b = causal.astype(BF16)
    upper_b = (ri <= ci).astype(BF16)
    wr = lax.broadcasted_iota(I32, ((SSM_CONV - 1) * L, CONV_CARRY + L), 0)
    wc = lax.broadcasted_iota(I32, ((SSM_CONV - 1) * L, CONV_CARRY + L), 1)
    shifts = (wc == (wr % L) + CONV_CARRY - (SSM_CONV - 1) + wr // L).astype(BF16)
    hh = lax.broadcasted_iota(I32, (n_heads, d_ssm), 0)
    jj = lax.broadcasted_iota(I32, (n_heads, d_ssm), 1)
    expand = ((jj // P) == hh).astype(BF16)
    expand2 = jnp.concatenate([expand, expand], axis=0)
    lane = lax.broadcasted_iota(I32, (L, LANES), 1)
    first_half = lane < P
    heads_per_group = n_heads // G
    a_row = -jnp.exp(alog_r_ref[...])
    a_col = -jnp.exp(alog_c_ref[...])

    for s in range(R // L):
        rows = slice(s * L, (s + 1) * L)
        window = conv_in[s * L:s * L + CONV_CARRY + L, :]
        taps = jnp.dot(shifts, window, preferred_element_type=F32)
        acc = cb_ref[...] + cw_ref[SSM_CONV - 1:SSM_CONV, :] * window[CONV_CARRY:, :].astype(F32)
        for j in range(SSM_CONV - 1):
            acc = acc + cw_ref[j:j + 1, :] * taps[j * L:(j + 1) * L, :]
        act = _silu(acc)
        xs = act[:, :d_ssm]
        bm = act[:, d_ssm:d_ssm + G * N].astype(BF16)
        cm = act[:, d_ssm + G * N:].astype(BF16)

        sm = sm_ref[rows, :]
        dt_col = _softplus(sm[:, :n_heads] + dtb_r_ref[...])
        dt_row = _softplus(sm.T[:n_heads, :] + dtb_c_ref[...])
        cs_col = sum(jnp.dot(causal_b, p, preferred_element_type=F32) for p in _pieces(dt_col * a_row, 3))
        cs_row = _dot_pieces(dt_row * a_col, upper_b, 3)
        cs_last = cs_col[L - 1:L, :]

        per_head = jnp.concatenate([dt_col, jnp.exp(cs_col), jnp.exp(cs_last - cs_col)], axis=0)
        per_ch = jnp.dot(jnp.concatenate(_pieces(per_head, 2), axis=1), expand2,
                         preferred_element_type=F32)
        dt_e, ecs_e, dte_e = per_ch[0:L], per_ch[L:2 * L], per_ch[2 * L:3 * L]
        chunk_decay = ecs_e[L - 1:L, :]

        xdt = xs * dt_e
        xdt_b = xdt.astype(BF16)
        xdec_b = (xdt * dte_e).astype(BF16)

        y_parts = []
        y_off_parts = []
        for g in range(G):
            bm_g = bm[:, g * N:(g + 1) * N]
            cm_g = cm[:, g * N:(g + 1) * N]
            cb = lax.dot_general(cm_g, bm_g, _NT, preferred_element_type=F32)
            prev = st_ref[g]
            y_off_parts.append(jnp.dot(cm_g, prev.astype(BF16), preferred_element_type=F32))
            s_new = lax.dot_general(bm_g, xdec_b[:, g * gw:(g + 1) * gw], _TN, preferred_element_type=F32)
            st_ref[g] = prev * chunk_decay[:, g * gw:(g + 1) * gw] + s_new
            for p in range(heads_per_group // 2):
                h0 = g * heads_per_group + 2 * p
                ms = []
                for h in (h0, h0 + 1):
                    diff = cs_col[:, h:h + 1] - cs_row[h:h + 1, :]
                    ms.append((cb * jnp.exp(jnp.where(causal, diff, -jnp.inf))).astype(BF16))
                lhs = jnp.concatenate(ms, axis=1)
                xp = xdt_b[:, h0 * P:(h0 + 2) * P]
                zero = jnp.zeros_like(xp)
                rhs = jnp.concatenate([jnp.where(first_half, xp, zero), jnp.where(first_half, zero, xp)], axis=0)
                y_parts.append(jnp.dot(lhs, rhs, preferred_element_type=F32))
        y = jnp.concatenate(y_parts, axis=1) + jnp.concatenate(y_off_parts, axis=1) * ecs_e + dsk_ref[...] * xs
        y = y * _silu(z_ref[rows, :].astype(F32))
        y = jnp.concatenate([_rms(y[:, g * gw:(g + 1) * gw]) for g in range(G)], axis=1) * g_ref[...]
        y_ref[rows, :] = y.astype(y_ref.dtype)


def _ssd(xbc, z, small, conv_w, conv_b, dt_bias, a_log, d_skip_e, norm_g, batch, seq):
    t, cd = xbc.shape
    d_ssm = z.shape[1]
    n_heads = dt_bias.shape[0]
    L = min(SSD_STEP_ROWS, seq)
    nc = seq // L
    row = lambda b, c: (b * nc + c, 0)
    tail = lambda b, c: (jnp.maximum((b * nc + c) * (L // CONV_CARRY) - 1, 0), 0)
    kern = functools.partial(_ssd_kernel, n_heads=n_heads, d_ssm=d_ssm)
    return pl.pallas_call(
        kern,
        out_shape=jax.ShapeDtypeStruct((t, d_ssm), BF16),
        grid=(batch, nc),
        in_specs=[pl.BlockSpec((CONV_CARRY, cd), tail),
                  pl.BlockSpec((L, cd), row), pl.BlockSpec((L, d_ssm), row), pl.BlockSpec((L, LANES), row),
                  _const_spec(conv_w.shape), _const_spec((1, cd)),
                  _const_spec((1, n_heads)), _const_spec((n_heads, 1)),
                  _const_spec((1, n_heads)), _const_spec((n_heads, 1)),
                  _const_spec((1, d_ssm)), _const_spec((1, d_ssm))],
        out_specs=pl.BlockSpec((L, d_ssm), row),
        scratch_shapes=[pltpu.VMEM((SSM_GROUPS, SSM_STATE, d_ssm // SSM_GROUPS), F32)],
        compiler_params=_params("arbitrary", "arbitrary"),
        name="ssd",
    )(xbc, xbc, z, small, conv_w, conv_b.reshape(1, cd), dt_bias.reshape(1, n_heads), dt_bias.reshape(n_heads, 1),
      a_log.reshape(1, n_heads), a_log.reshape(n_heads, 1), d_skip_e, norm_g.reshape(1, d_ssm))


def _gla_kernel(q_ref, k_ref, v_ref, r_ref, sm_ref, wg2_ref, bg_ref, gn_ref, o_ref, st_ref, *, gate_col):
    L = GLA_CHUNK
    H = GLA_HEADS
    dk = q_ref.shape[1] // H
    dv = v_ref.shape[1] // H
    c = pl.program_id(1)

    @pl.when(c == 0)
    def _():
        st_ref[...] = jnp.zeros(st_ref.shape, F32)

    R = q_ref.shape[0]
    n_chunks = R // L
    ri = lax.broadcasted_iota(I32, (R, R), 0)
    ci = lax.broadcasted_iota(I32, (R, R), 1)
    same_chunk = (ri // L) == (ci // L)
    causal = jnp.logical_and(same_chunk, ri >= ci)
    tril = causal.astype(BF16)
    later = jnp.logical_and(same_chunk, ri < ci).astype(BF16)
    sr = lax.broadcasted_iota(I32, (R, n_chunks * LANES), 0)
    sc = lax.broadcasted_iota(I32, (R, n_chunks * LANES), 1)
    last_rows = (sr == (sc // LANES) * L + (L - 1)).astype(BF16)
    wg2_hi, wg2_lo = _pieces(wg2_ref[...], 2)

    q = q_ref[...].astype(F32) * (dk ** -0.5)
    k = k_ref[...].astype(F32)
    v = v_ref[...]
    r = r_ref[...].astype(F32)
    g_hi, g_lo = _pieces(sm_ref[:, gate_col:gate_col + GLA_GATE_RANK], 2)
    pre = (jnp.dot(g_hi, wg2_hi, preferred_element_type=F32) + jnp.dot(g_lo, wg2_hi, preferred_element_type=F32)
           + jnp.dot(g_hi, wg2_lo, preferred_element_type=F32)) + bg_ref[...]
    gk_p = _pieces(_log_sigmoid(pre) / GLA_GATE_NORM, 3)
    bcum = sum(jnp.dot(tril, p, preferred_element_type=F32) for p in gk_p)
    to_end = sum(jnp.dot(later, p, preferred_element_type=F32) for p in gk_p)
    q_t = (q * jnp.exp(bcum)).astype(BF16)
    k_t = (k * jnp.exp(-bcum)).astype(BF16)
    k_dec = (k * jnp.exp(to_end)).astype(BF16)
    dcol = jnp.exp(_dot_pieces(bcum, last_rows, 3, dims=_TN))
    outs = []
    for h in range(H):
        ks = slice(h * dk, (h + 1) * dk)
        vs = slice(h * dv, (h + 1) * dv)
        att = lax.dot_general(q_t[:, ks], k_t[:, ks], _NT, preferred_element_type=F32)
        att = jnp.where(causal, att, 0.0).astype(BF16)
        o = jnp.dot(att, v[:, vs], preferred_element_type=F32)
        state = st_ref[h]
        inter = []
        for c in range(n_chunks):
            rows = slice(c * L, (c + 1) * L)
            inter.append(jnp.dot(q_t[rows, ks], state.astype(BF16), preferred_element_type=F32))
            s_new = lax.dot_general(k_dec[rows, ks], v[rows, vs], _TN, preferred_element_type=F32)
            dec = dcol[ks, c * LANES:(c + 1) * LANES]
            state = state * jnp.concatenate([dec] * (dv // LANES), axis=1) + s_new
        st_ref[h] = state
        o = o + jnp.concatenate(inter, axis=0)
        outs.append(_rms(o) * gn_ref[...] * _silu(r[:, vs]))
    o_ref[...] = jnp.concatenate(outs, axis=1).astype(o_ref.dtype)


def _gla(q, k, v, r, small, wg2, bg, norm_g, batch, seq, gate_col):
    t, dkt = q.shape
    dvt = v.shape[1]
    rows = min(GLA_STEP_ROWS, seq)
    nc = seq // rows
    row = lambda b, c: (b * nc + c, 0)
    kern = functools.partial(_gla_kernel, gate_col=gate_col)
    return pl.pallas_call(
        kern,
        out_shape=jax.ShapeDtypeStruct((t, dvt), BF16),
        grid=(batch, nc),
        in_specs=[pl.BlockSpec((rows, dkt), row), pl.BlockSpec((rows, dkt), row), pl.BlockSpec((rows, dvt), row),
                  pl.BlockSpec((rows, dvt), row), pl.BlockSpec((rows, LANES), row),
                  _const_spec(wg2.shape), _const_spec((1, dkt)), _const_spec((1, dvt // GLA_HEADS))],
        out_specs=pl.BlockSpec((rows, dvt), row),
        scratch_shapes=[pltpu.VMEM((GLA_HEADS, dkt // GLA_HEADS, dvt // GLA_HEADS), F32)],
        compiler_params=_params("arbitrary", "arbitrary"),
        name="gla",
    )(q, k, v, r, small, wg2, bg.reshape(1, dkt), norm_g.reshape(1, dvt // GLA_HEADS))


def _outproj_kernel(y_ref, o_ref, x_ref, gt_ref, sc_ref, sh_ref, g_ref, wy_ref, wo_ref, wr_ref, br_ref,
                    x1_ref, h_ref, ti_ref, tw_ref, cnt_ref):
    mix = (jnp.dot(y_ref[...], wy_ref[...], preferred_element_type=F32)
           + jnp.dot(o_ref[...], wo_ref[...], preferred_element_type=F32))
    x1 = x_ref[...] + gt_ref[0] * mix
    x1_ref[...] = x1
    h = (_rms(x1) * g_ref[...]) * (1.0 + sc_ref[0]) + sh_ref[0]
    _token_rows_store(h_ref, h)
    n_e = br_ref.shape[0]
    h_hi, h_lo = _pieces(h, 2)
    wr = wr_ref[...]
    hw = lax.dot_general(wr, h_hi, _NT, preferred_element_type=F32)
    logits = (hw[:n_e] + hw[n_e:] + lax.dot_general(wr[:n_e], h_lo, _NT, preferred_element_type=F32)) + br_ref[...]
    expert = lax.broadcasted_iota(I32, logits.shape, 0)
    vals, idxs = [], []
    counts = jnp.zeros(logits.shape, F32)
    for _ in range(TOP_K):
        m = jnp.max(logits, axis=0, keepdims=True)
        idx = jnp.min(jnp.where(logits == m, expert, n_e), axis=0, keepdims=True)
        vals.append(m)
        idxs.append(idx)
        chosen = expert == idx
        counts = counts + chosen.astype(F32)
        logits = jnp.where(chosen, -jnp.inf, logits)
    exps = [jnp.exp(v - vals[0]) for v in vals]
    denom = functools.reduce(lambda a, b: a + b, exps)
    ti_ref[...] = jnp.concatenate(idxs, axis=0)
    tw_ref[...] = jnp.concatenate([e / denom for e in exps], axis=0)

    @pl.when(pl.program_id(0) == 0)
    def _():
        cnt_ref[...] = jnp.zeros(cnt_ref.shape, F32)

    cnt_ref[...] = cnt_ref[...] + jnp.sum(counts, axis=1, keepdims=True)


def _outproj(y, o, x2, gt, sc, sh, g, wy, wo, w_router, b_router, seq):
    t, d = x2.shape
    n_e = w_router.shape[1]
    tm = min(TOKEN_TILE, seq)
    per_batch = seq // tm
    row = lambda i: (i, 0)
    mod_spec = pl.BlockSpec((1, 1, d), lambda i: (i // per_batch, 0, 0))
    wr_t = w_router.T
    wr_hi = wr_t.astype(BF16)
    wr_cat = jnp.concatenate([wr_hi, (wr_t - wr_hi.astype(F32)).astype(BF16)], axis=0)
    col = lambda i: (0, i)
    return pl.pallas_call(
        _outproj_kernel,
        out_shape=[jax.ShapeDtypeStruct((t, d), F32), jax.ShapeDtypeStruct((t * SUBLANES, LANES), F32),
                   jax.ShapeDtypeStruct((TOP_K, t), I32), jax.ShapeDtypeStruct((TOP_K, t), F32),
                   jax.ShapeDtypeStruct((n_e, LANES), F32)],
        grid=(t // tm,),
        in_specs=[pl.BlockSpec((tm, y.shape[1]), row), pl.BlockSpec((tm, o.shape[1]), row),
                  pl.BlockSpec((tm, d), row), mod_spec, mod_spec, mod_spec, _const_spec((1, d)),
                  _const_spec(wy.shape), _const_spec(wo.shape), _const_spec(wr_cat.shape),
                  _const_spec((n_e, 1))],
        out_specs=[pl.BlockSpec((tm, d), row), pl.BlockSpec((tm * SUBLANES, LANES), row),
                   pl.BlockSpec((TOP_K, tm), col), pl.BlockSpec((TOP_K, tm), col),
                   _const_spec((n_e, LANES))],
        compiler_params=_params("arbitrary"),
        name="outproj",
    )(y, o, x2, gt, sc, sh, g, wy, wo, wr_cat, b_router.reshape(n_e, 1))


def _route_kernel(ti_ref, cnt_ref, dest_ref, be_ref, pend_ref, run_ref, *, n_blocks_pad):
    i = pl.program_id(0)
    n_e = cnt_ref.shape[0]
    tr = ti_ref.shape[1]

    @pl.when(i == 0)
    def _():
        counts = cnt_ref[...]
        padded = jnp.ceil(counts / EXPERT_BLOCK) * EXPERT_BLOCK
        ri = lax.broadcasted_iota(I32, (n_e, n_e), 0)
        ci = lax.broadcasted_iota(I32, (n_e, n_e), 1)
        pend = jnp.dot((ri >= ci).astype(F32), padded, precision=HIGHEST, preferred_element_type=F32)
        pend_ref[...] = pend
        run_ref[...] = pend - padded
        start = (lax.broadcasted_iota(I32, (n_e, n_blocks_pad), 1) * EXPERT_BLOCK).astype(F32)
        be = jnp.sum((pend[:, 0:1] <= start).astype(F32), axis=0, keepdims=True)
        be_ref[...] = jnp.minimum(be, n_e - 1).astype(I32)

    ti = ti_ref[...]
    expert = lax.broadcasted_iota(I32, (n_e, tr), 0)
    onehots = [expert == ti[k:k + 1, :] for k in range(TOP_K)]
    cnt = functools.reduce(lambda a, b: a + b, [oh.astype(F32) for oh in onehots])
    ri = lax.broadcasted_iota(I32, (tr, tr), 0)
    ci = lax.broadcasted_iota(I32, (tr, tr), 1)
    before = jnp.dot(cnt.astype(BF16), (ri < ci).astype(BF16), preferred_element_type=F32)
    base = run_ref[:, 0:1] + before
    dest = [jnp.sum(jnp.where(oh, base, 0.0), axis=0, keepdims=True) for oh in onehots]
    dest_ref[...] = jnp.concatenate(dest, axis=0).astype(I32)
    run_ref[...] = run_ref[...] + jnp.sum(cnt, axis=1, keepdims=True)


def _route(topi_t, counts, n_blocks):
    t = topi_t.shape[1]
    n_e = counts.shape[0]
    tr = min(ROUTE_TILE, t)
    n_blocks_pad = -(-n_blocks // LANES) * LANES
    kern = functools.partial(_route_kernel, n_blocks_pad=n_blocks_pad)
    return pl.pallas_call(
        kern,
        out_shape=[jax.ShapeDtypeStruct((TOP_K, t), I32), jax.ShapeDtypeStruct((1, n_blocks_pad), I32),
                   jax.ShapeDtypeStruct((n_e, LANES), F32)],
        grid=(t // tr,),
        in_specs=[pl.BlockSpec((TOP_K, tr), lambda i: (0, i)), _const_spec((n_e, LANES))],
        out_specs=[pl.BlockSpec((TOP_K, tr), lambda i: (0, i)), _const_spec((1, n_blocks_pad)),
                   _const_spec((n_e, LANES))],
        scratch_shapes=[pltpu.VMEM((n_e, LANES), F32)],
        compiler_params=_params("arbitrary"),
        name="route",
    )(topi_t, counts)


def _dispatch_kernel(pend_ref, dest_hbm, h_ref, xs_hbm, idx_ref, zero_ref, idx_sem, row_sem, *, n_experts):
    i = pl.program_id(0)
    n_idx = idx_ref.shape[0]
    tg = n_idx // TOP_K

    @pl.when(i == 0)
    def _():
        zero_ref[...] = jnp.zeros(zero_ref.shape, zero_ref.dtype)
        for e in range(n_experts):
            end = pend_ref[e]
            prev = pend_ref[e - 1] if e > 0 else 0

            @pl.when(end > prev)
            def _():
                start = pl.multiple_of((end - EXPERT_BLOCK) * SUBLANES, EXPERT_BLOCK * SUBLANES)
                cp = pltpu.make_async_copy(zero_ref, xs_hbm.at[pl.ds(start, EXPERT_BLOCK * SUBLANES)], row_sem)
                cp.start()
                cp.wait()

        n_blocks = xs_hbm.shape[0] // (EXPERT_BLOCK * SUBLANES)
        total = pend_ref[n_experts - 1]
        for b in range(n_blocks - n_experts, n_blocks):
            @pl.when(b * EXPERT_BLOCK >= total)
            def _():
                cp = pltpu.make_async_copy(
                    zero_ref, xs_hbm.at[pl.ds(b * EXPERT_BLOCK * SUBLANES, EXPERT_BLOCK * SUBLANES)], row_sem)
                cp.start()
                cp.wait()

    idx_cp = pltpu.make_async_copy(dest_hbm.at[pl.ds(i * n_idx, n_idx)], idx_ref, idx_sem)
    idx_cp.start()
    idx_cp.wait()

    def issue(tl, carry):
        src = h_ref.at[pl.ds(pl.multiple_of(tl * SUBLANES, SUBLANES), SUBLANES)]
        for k in range(TOP_K):
            d = pl.multiple_of(idx_ref[tl * TOP_K + k] * SUBLANES, SUBLANES)
            pltpu.make_async_copy(src, xs_hbm.at[pl.ds(d, SUBLANES)], row_sem).start(priority=k % 2)
        return carry

    lax.fori_loop(0, tg, issue, 0)
    for _ in range(TOP_K):
        pltpu.make_async_copy(h_ref, xs_hbm.at[pl.ds(0, tg * SUBLANES)], row_sem).wait()


def _dispatch(pend_i, dest_flat, h, n_rows, n_experts):
    t = h.shape[0] // SUBLANES
    tg = min(DISPATCH_TILE, t)
    kern = functools.partial(_dispatch_kernel, n_experts=n_experts)
    return pl.pallas_call(
        kern,
        out_shape=jax.ShapeDtypeStruct((n_rows * SUBLANES, LANES), h.dtype),
        grid_spec=pltpu.PrefetchScalarGridSpec(
            num_scalar_prefetch=1,
            grid=(t // tg,),
            in_specs=[pl.BlockSpec(memory_space=pl.ANY),
                      pl.BlockSpec((tg * SUBLANES, LANES), lambda i, pend: (i, 0))],
            out_specs=pl.BlockSpec(memory_space=pl.ANY),
            scratch_shapes=[pltpu.SMEM((tg * TOP_K,), I32), pltpu.VMEM((EXPERT_BLOCK * SUBLANES, LANES), h.dtype),
                            pltpu.SemaphoreType.DMA, pltpu.SemaphoreType.DMA]),
        compiler_params=pltpu.CompilerParams(dimension_semantics=("arbitrary",), has_side_effects=True,
                                             vmem_limit_bytes=VMEM_LIMIT_BYTES),
        name="dispatch",
    )(pend_i, dest_flat, h)


def _expert_kernel(be_ref, nu_ref, pend_ref, xs_ref, wg_hbm, bg_ref, wu_hbm, bu_ref, wd_hbm, bd_ref, y_ref,
                   wg_f, wu_f, wd_f, wg_b, wu_b, wd_b, slot_ref, sems):
    i = pl.program_id(0)
    used = i < nu_ref[0]
    e = be_ref[i]

    def fetch(expert, slot):
        return [pltpu.make_async_copy(src.at[expert], dst.at[slot], sems.at[slot])
                for src, dst in ((wg_hbm, wg_f), (wu_hbm, wu_f), (wd_hbm, wd_f))]

    @pl.when(i == 0)
    def _():
        slot_ref[0] = 0
        for cp in fetch(e, 0):
            cp.start()

    first_of_expert = jnp.logical_or(i == 0, e != be_ref[jnp.maximum(i - 1, 0)])

    @pl.when(jnp.logical_and(used, first_of_expert))
    def _():
        slot = slot_ref[0]
        for cp in fetch(e, slot):
            cp.wait()
        wg_b[...] = wg_f[slot].astype(BF16)
        wu_b[...] = wu_f[slot].astype(BF16)
        wd_b[...] = wd_f[slot].astype(BF16)
        nxt = lax.div(pend_ref[e], EXPERT_BLOCK)

        @pl.when(nxt < nu_ref[0])
        def _():
            for cp in fetch(be_ref[nxt], 1 - slot):
                cp.start()

        slot_ref[0] = 1 - slot

    @pl.when(used)
    def _():
        x = _token_rows_load(xs_ref, EXPERT_BLOCK).astype(BF16)
        gate = jnp.minimum(jnp.dot(x, wg_b[...], preferred_element_type=F32) + bg_ref[...], SWIGLU_LIMIT)
        up = jnp.clip(jnp.dot(x, wu_b[...], preferred_element_type=F32) + bu_ref[...],
                      -SWIGLU_LIMIT, SWIGLU_LIMIT)
        glu = gate * _sigmoid(SWIGLU_ALPHA * gate)
        mid = ((up + 1.0) * glu).astype(BF16)
        y = jnp.dot(mid, wd_b[...], preferred_element_type=F32) + bd_ref[...]
        _token_rows_store(y_ref, y)

    @pl.when(jnp.logical_not(used))
    def _():
        y_ref[...] = jnp.zeros(y_ref.shape, y_ref.dtype)


def _experts(block_e, n_used, pend_i, xs, w_gate, b_gate, w_up, b_up, w_down, b_down):
    n_rows = xs.shape[0] // SUBLANES
    n_e, d, f = w_gate.shape
    nb = n_rows // EXPERT_BLOCK
    blk = (EXPERT_BLOCK * SUBLANES, LANES)
    last = lambda i, be, nu, pend: jnp.maximum(jnp.minimum(i, nu[0] - 1), 0)
    bspec = lambda n: pl.BlockSpec((None, 1, n), lambda i, be, nu, pend: (be[last(i, be, nu, pend)], 0, 0))
    hbm = pl.BlockSpec(memory_space=pl.ANY)
    return pl.pallas_call(
        _expert_kernel,
        out_shape=jax.ShapeDtypeStruct((n_rows * SUBLANES, LANES), F32),
        grid_spec=pltpu.PrefetchScalarGridSpec(
            num_scalar_prefetch=3,
            grid=(nb,),
            in_specs=[pl.BlockSpec(blk, lambda i, be, nu, pend: (last(i, be, nu, pend), 0)),
                      hbm, bspec(f), hbm, bspec(f), hbm, bspec(d)],
            out_specs=pl.BlockSpec(blk, lambda i, be, nu, pend: (i, 0)),
            scratch_shapes=[pltpu.VMEM((2, d, f), F32), pltpu.VMEM((2, d, f), F32), pltpu.VMEM((2, f, d), F32),
                            pltpu.VMEM((d, f), BF16), pltpu.VMEM((d, f), BF16), pltpu.VMEM((f, d), BF16),
                            pltpu.SMEM((1,), I32), pltpu.SemaphoreType.DMA((2,))]),
        compiler_params=_params("arbitrary"),
        name="experts",
    )(block_e, n_used, pend_i, xs, w_gate, b_gate.reshape(n_e, 1, f), w_up, b_up.reshape(n_e, 1, f),
      w_down, b_down.reshape(n_e, 1, d))


def _combine_kernel(dest_hbm, ys_hbm, tw_ref, x1_ref, gt_ref, g_ref, o_ref, idx_ref, buf_ref, idx_sem, row_sem):
    i = pl.program_id(0)
    n = pl.num_programs(0)
    n_idx = idx_ref.shape[1]
    tc = n_idx // TOP_K
    slot = lax.rem(i, 2)

    def idx_copy(tile, s):
        return pltpu.make_async_copy(dest_hbm.at[pl.ds(tile * n_idx, n_idx)], idx_ref.at[s], idx_sem.at[s])

    def issue_rows(s):
        def issue(tl, carry):
            dst_row = pl.multiple_of(tl * SUBLANES, SUBLANES)
            for k in range(TOP_K):
                d = pl.multiple_of(idx_ref[s, tl * TOP_K + k] * SUBLANES, SUBLANES)
                pltpu.make_async_copy(ys_hbm.at[pl.ds(d, SUBLANES)], buf_ref.at[s, k, pl.ds(dst_row, SUBLANES)],
                                      row_sem.at[s]).start(priority=k % 2)
            return carry

        lax.fori_loop(0, tc, issue, 0)

    @pl.when(i == 0)
    def _():
        idx_copy(0, 0).start()
        idx_copy(0, 0).wait()
        issue_rows(0)

        @pl.when(n > 1)
        def _():
            idx_copy(1, 1).start()

    @pl.when(i + 1 < n)
    def _():
        idx_copy(i + 1, 1 - slot).wait()
        issue_rows(1 - slot)

        @pl.when(i + 2 < n)
        def _():
            idx_copy(i + 2, slot).start()

    for k in range(TOP_K):
        pltpu.make_async_copy(ys_hbm.at[pl.ds(0, tc * SUBLANES)], buf_ref.at[slot, k], row_sem.at[slot]).wait()

    tw = tw_ref[...]
    ffn = tw[:, 0:1] * _token_rows_load(buf_ref.at[slot, 0], tc)
    for k in range(1, TOP_K):
        ffn = ffn + tw[:, k:k + 1] * _token_rows_load(buf_ref.at[slot, k], tc)
    x2 = x1_ref[...] + gt_ref[0] * ffn
    o_ref[...] = _rms(x2) * g_ref[...]


def _combine(dest_flat, ys, topw, x1, gt, g, seq):
    t, d = x1.shape
    tc = min(COMBINE_TILE, seq)
    per_batch = seq // tc
    row = lambda i: (i, 0)
    return pl.pallas_call(
        _combine_kernel,
        out_shape=jax.ShapeDtypeStruct((t, d), F32),
        grid=(t // tc,),
        in_specs=[pl.BlockSpec(memory_space=pl.ANY), pl.BlockSpec(memory_space=pl.ANY),
                  pl.BlockSpec((tc, TOP_K), row), pl.BlockSpec((tc, d), row),
                  pl.BlockSpec((1, 1, d), lambda i: (i // per_batch, 0, 0)), _const_spec((1, d))],
        out_specs=pl.BlockSpec((tc, d), row),
        scratch_shapes=[pltpu.SMEM((2, tc * TOP_K), I32), pltpu.VMEM((2, TOP_K, tc * SUBLANES, LANES), F32),
                        pltpu.SemaphoreType.DMA((2,)), pltpu.SemaphoreType.DMA((2,))],
        compiler_params=_params("arbitrary"),
        name="combine",
    )(dest_flat, ys, topw, x1, gt, g)


def _layer(x2, mod, batch, seq, norm1_g, w_in, conv_w, conv_b, dt_bias, a_log, d_skip, ssm_norm_g,
           gla_wg2, gla_bg, gla_norm_g, w_out, norm2_g, w_router, b_router,
           w_gate, b_gate, w_up, b_up, w_down, b_down):
    t, d = x2.shape
    n_heads = dt_bias.shape[0]
    d_ssm = n_heads * SSM_HEAD_DIM
    cd = conv_w.shape[1]
    dkt = gla_wg2.shape[1]
    dvt = w_out.shape[0] - d_ssm
    n_experts = w_router.shape[1]

    sh1, sc1, gt1, sh2, sc2, gt2 = [m.reshape(batch, 1, d) for m in jnp.split(mod[:batch], 6, axis=1)]

    sizes = (d_ssm, cd, n_heads, dkt, dkt, dvt, GLA_GATE_RANK, dvt)
    offs = [0]
    for s in sizes:
        offs.append(offs[-1] + s)
    col = lambda j: w_in[:, offs[j]:offs[j + 1]]
    pad = jnp.zeros((d, LANES - n_heads - GLA_GATE_RANK), F32)
    w_small = jnp.concatenate([col(2), col(6), pad], axis=1)
    weights = [col(0), col(1), col(3), col(4), col(5), col(7), w_small]
    weights = [w.astype(BF16) for w in weights]
    z, xbc, q, k, v, r, small = _inproj(x2, sc1, sh1, norm1_g.reshape(1, d), weights,
                                        [BF16] * 6 + [F32], seq)

    d_skip_e = jnp.repeat(d_skip, SSM_HEAD_DIM).reshape(1, d_ssm)
    y = _ssd(xbc, z, small, conv_w, conv_b, dt_bias, a_log, d_skip_e, ssm_norm_g, batch, seq)
    o = _gla(q, k, v, r, small, gla_wg2, gla_bg, gla_norm_g, batch, seq, gate_col=n_heads)

    x1, h2, topi_t, topw_t, counts = _outproj(y, o, x2, gt1, sc2, sh2, norm2_g.reshape(1, d),
                                        w_out[:d_ssm].astype(BF16), w_out[d_ssm:].astype(BF16),
                                        w_router, b_router, seq)

    n_blocks = (t * TOP_K) // EXPERT_BLOCK + n_experts
    dest_t, block_e, pend = _route(topi_t, counts, n_blocks)
    pend_i = pend[:, 0].astype(I32)
    n_used = (pend_i[n_experts - 1:] // EXPERT_BLOCK).astype(I32)
    dest_flat = dest_t.T.reshape(t * TOP_K)
    topw = topw_t.T
    xs = _dispatch(pend_i, dest_flat, h2, n_blocks * EXPERT_BLOCK, n_experts)
    ys = _experts(block_e[0, :n_blocks], n_used, pend_i, xs, w_gate, b_gate, w_up, b_up, w_down, b_down)
    return dest_flat, ys, topw, x1, gt2


def kernel(x, c, w_ada, b_ada, norm1_g, w_in, conv_w, conv_b, dt_bias, a_log, d_skip, ssm_norm_g, gla_wg2,
           gla_bg, gla_norm_g, w_out, norm2_g, w_router, b_router, w_gate, b_gate, w_up, b_up, w_down, b_down,
           final_norm_g):
    batch, seq, d = x.shape
    assert w_ada.shape[0] == 1, "single-layer trunk"
    assert d == SUBLANES * LANES, "token rows are moved as one (8, 128) f32 tile each"
    assert seq % min(seq, max(TOKEN_TILE, SSM_CHUNK, GLA_STEP_ROWS, COMBINE_TILE, DISPATCH_TILE)) == 0
    assert seq % max(SSM_CHUNK, GLA_STEP_ROWS) == 0
    x2 = x.reshape(batch * seq, d)
    c_pad = jnp.zeros((SUBLANES, d), F32).at[:batch].set(c)
    mod = _ada(c_pad, w_ada[0], b_ada)
    dest_flat, ys, topw, x1, gt2 = _layer(
        x2, mod, batch, seq, norm1_g[0], w_in[0], conv_w[0], conv_b[0], dt_bias[0], a_log[0], d_skip[0],
        ssm_norm_g[0], gla_wg2[0], gla_bg[0], gla_norm_g[0], w_out[0], norm2_g[0], w_router[0], b_router[0],
        w_gate[0], b_gate[0], w_up[0], b_up[0], w_down[0], b_down[0])
    out = _combine(dest_flat, ys, topw, x1, gt2, final_norm_g.reshape(1, d), seq)
    return out.reshape(batch, seq, d)
```

```python
import functools

import jax
import jax.numpy as jnp
from jax import lax
from jax.experimental import pallas as pl
from jax.experimental.pallas import tpu as pltpu

F32 = jnp.float32
BF16 = jnp.bfloat16
I32 = jnp.int32
HIGHEST = lax.Precision.HIGHEST

EPS = 1e-6
SSM_HEAD_DIM = 64
SSM_GROUPS = 2
SSM_STATE = 128
SSM_CONV = 4
SSM_CHUNK = 128
GLA_HEADS = 4
GLA_GATE_RANK = 16
GLA_GATE_NORM = 16.0
GLA_CHUNK = 64
TOP_K = 4
SWIGLU_LIMIT = 7.0
SWIGLU_ALPHA = 1.702

LANES = 128
SUBLANES = 8
VMEM_LIMIT_BYTES = 56 * 1024 * 1024

TOKEN_TILE = 512
SSD_STEP_ROWS = 512
CONV_CARRY = 16
GLA_STEP_ROWS = 512
GLA_GROUP_ROWS = 256
ROUTE_TILE = 512
EXPERT_BLOCK = 256
DISPATCH_TILE = 512
COMBINE_TILE = 256

_NT = (((1,), (1,)), ((), ()))
_TN = (((0,), (0,)), ((), ()))


def _sigmoid(v):
    return 0.5 * jnp.tanh(0.5 * v) + 0.5


def _silu(v):
    return v * _sigmoid(v)


def _softplus(v):
    return jnp.maximum(v, 0.0) + jnp.log1p(jnp.exp(-jnp.abs(v)))


def _log_sigmoid(v):
    return jnp.minimum(v, 0.0) - jnp.log(1.0 + jnp.exp(-jnp.abs(v)))


def _rms(v):
    return v * lax.rsqrt(jnp.mean(v * v, axis=-1, keepdims=True) + EPS)


def _pieces(a, n):
    out = []
    for _ in range(n - 1):
        p = a.astype(BF16)
        out.append(p)
        a = a - p.astype(F32)
    out.append(a.astype(BF16))
    return out


def _dot_pieces(a, b, n, dims=None):
    dims = dims or (((a.ndim - 1,), (0,)), ((), ()))
    return sum(lax.dot_general(p, b, dims, preferred_element_type=F32) for p in _pieces(a, n))


def _token_rows_load(ref, rows):
    return jnp.concatenate([ref[pl.ds(s, rows, stride=SUBLANES), :] for s in range(SUBLANES)], axis=1)


def _token_rows_store(ref, v):
    rows = v.shape[0]
    for s in range(SUBLANES):
        ref[pl.ds(s, rows, stride=SUBLANES), :] = v[:, s * LANES:(s + 1) * LANES]


def _params(*semantics):
    return pltpu.CompilerParams(dimension_semantics=semantics, vmem_limit_bytes=VMEM_LIMIT_BYTES)


def _const_spec(shape):
    nd = len(shape)
    return pl.BlockSpec(shape, lambda *_: (0,) * nd)


def _ada_kernel(c_ref, w_ref, b_ref, o_ref):
    cond = _silu(c_ref[...])
    o_ref[...] = jnp.dot(cond, w_ref[...], precision=HIGHEST, preferred_element_type=F32) + b_ref[...]


def _ada(c_pad, w_ada, b_ada):
    rows, d = c_pad.shape
    n = w_ada.shape[1]
    tn = d
    return pl.pallas_call(
        _ada_kernel,
        out_shape=jax.ShapeDtypeStruct((rows, n), F32),
        grid=(n // tn,),
        in_specs=[pl.BlockSpec((rows, d), lambda j: (0, 0)),
                  pl.BlockSpec((d, tn), lambda j: (0, j)),
                  pl.BlockSpec((1, tn), lambda j: (0, j))],
        out_specs=pl.BlockSpec((rows, tn), lambda j: (0, j)),
        compiler_params=_params("arbitrary"),
        name="ada",
    )(c_pad, w_ada, b_ada)


def _inproj_kernel(x_ref, sc_ref, sh_ref, g_ref, w_ref, *o_refs):
    h = (_rms(x_ref[...]) * g_ref[...]) * (1.0 + sc_ref[0]) + sh_ref[0]
    hb = h.astype(BF16)
    off = 0
    for o_ref in o_refs:
        n = o_ref.shape[1]
        o_ref[...] = jnp.dot(hb, w_ref[:, off:off + n], preferred_element_type=F32).astype(o_ref.dtype)
        off += n


def _inproj(x2, sc, sh, g, w_all, widths, out_dtypes, seq):
    t, d = x2.shape
    tm = min(TOKEN_TILE, seq)
    per_batch = seq // tm
    assert sum(widths) == w_all.shape[1] and all(n % LANES == 0 for n in widths)
    mod_spec = pl.BlockSpec((1, 1, d), lambda i: (i // per_batch, 0, 0))
    return pl.pallas_call(
        _inproj_kernel,
        out_shape=[jax.ShapeDtypeStruct((t, n), dt) for n, dt in zip(widths, out_dtypes)],
        grid=(t // tm,),
        in_specs=[pl.BlockSpec((tm, d), lambda i: (i, 0)), mod_spec, mod_spec, _const_spec((1, d)),
                  _const_spec(w_all.shape)],
        out_specs=[pl.BlockSpec((tm, n), lambda i: (i, 0)) for n in widths],
        compiler_params=_params("arbitrary"),
        name="inproj",
    )(x2, sc, sh, g, w_all)


def _ssd_kernel(tail_ref, xbc_ref, z_ref, sm_ref, cw_ref, cb_ref, dtb_r_ref, dtb_c_ref, alog_r_ref, alog_c_ref,
                dsk_ref, g_ref, y_ref, st_ref, *, n_heads, d_ssm):
    L = SSM_CHUNK
    P = SSM_HEAD_DIM
    N = SSM_STATE
    G = SSM_GROUPS
    gw = d_ssm // G
    R = xbc_ref.shape[0]
    c = pl.program_id(1)

    @pl.when(c == 0)
    def _():
        st_ref[...] = jnp.zeros(st_ref.shape, F32)

    tail = tail_ref[...]
    tail = jnp.where(c == 0, jnp.zeros_like(tail), tail)
    conv_in = jnp.concatenate([tail, xbc_ref[...]], axis=0)

    ri = lax.broadcasted_iota(I32, (L, L), 0)
    ci = lax.broadcasted_iota(I32, (L, L), 1)
    causal = ri >= ci
    causal3 = jnp.concatenate([causal.astype(BF16)] * 3, axis=1)
    upper3 = jnp.concatenate([(ri <= ci).astype(BF16)] * 3, axis=0)
    wr = lax.broadcasted_iota(I32, ((SSM_CONV - 1) * L, CONV_CARRY + L), 0)
    wc = lax.broadcasted_iota(I32, ((SSM_CONV - 1) * L, CONV_CARRY + L), 1)
    shifts = (wc == (wr % L) + CONV_CARRY - (SSM_CONV - 1) + wr // L).astype(BF16)
    hh = lax.broadcasted_iota(I32, (n_heads, d_ssm), 0)
    jj = lax.broadcasted_iota(I32, (n_heads, d_ssm), 1)
    expand = ((jj // P) == hh).astype(BF16)
    expand2 = jnp.concatenate([expand, expand], axis=0)
    lane = lax.broadcasted_iota(I32, (L, LANES), 1)
    first_half = lane < P
    heads_per_group = n_heads // G
    a_row = -jnp.exp(alog_r_ref[...])
    a_col = -jnp.exp(alog_c_ref[...])

    for s in range(R // L):
        rows = slice(s * L, (s + 1) * L)
        window = conv_in[s * L:s * L + CONV_CARRY + L, :]
        taps = jnp.dot(shifts, window, preferred_element_type=F32)
        acc = cb_ref[...] + cw_ref[SSM_CONV - 1:SSM_CONV, :] * window[CONV_CARRY:, :].astype(F32)
        for j in range(SSM_CONV - 1):
            acc = acc + cw_ref[j:j + 1, :] * taps[j * L:(j + 1) * L, :]
        act = _silu(acc)
        xs = act[:, :d_ssm]
        bm = act[:, d_ssm:d_ssm + G * N].astype(BF16)
        cm = act[:, d_ssm + G * N:].astype(BF16)

        sm = sm_ref[rows, :]
        dt_col = _softplus(sm[:, :n_heads] + dtb_r_ref[...])
        dt_row = _softplus(sm.T[:n_heads, :] + dtb_c_ref[...])
        cs_col = jnp.dot(causal3, jnp.concatenate(_pieces(dt_col * a_row, 3), axis=0),
                         preferred_element_type=F32)
        cs_row = jnp.dot(jnp.concatenate(_pieces(dt_row * a_col, 3), axis=1), upper3,
                         preferred_element_type=F32)
        cs_last = cs_col[L - 1:L, :]

        per_head = jnp.concatenate([dt_col, jnp.exp(cs_col), jnp.exp(cs_last - cs_col)], axis=0)
        per_ch = jnp.dot(jnp.concatenate(_pieces(per_head, 2), axis=1), expand2,
                         preferred_element_type=F32)
        dt_e, ecs_e, dte_e = per_ch[0:L], per_ch[L:2 * L], per_ch[2 * L:3 * L]
        chunk_decay = ecs_e[L - 1:L, :]

        xdt = xs * dt_e
        xdt_b = xdt.astype(BF16)
        xdec_b = (xdt * dte_e).astype(BF16)

        y_parts = []
        y_off_parts = []
        for g in range(G):
            bm_g = bm[:, g * N:(g + 1) * N]
            cm_g = cm[:, g * N:(g + 1) * N]
            cb = lax.dot_general(cm_g, bm_g, _NT, preferred_element_type=F32)
            prev = st_ref[g]
            y_off_parts.append(jnp.dot(cm_g, prev.astype(BF16), preferred_element_type=F32))
            s_new = lax.dot_general(bm_g, xdec_b[:, g * gw:(g + 1) * gw], _TN, preferred_element_type=F32)
            st_ref[g] = prev * chunk_decay[:, g * gw:(g + 1) * gw] + s_new
            for p in range(heads_per_group // 2):
                h0 = g * heads_per_group + 2 * p
                ms = []
                for h in (h0, h0 + 1):
                    diff = cs_col[:, h:h + 1] - cs_row[h:h + 1, :]
                    ms.append((cb * jnp.exp(jnp.where(causal, diff, -jnp.inf))).astype(BF16))
                lhs = jnp.concatenate(ms, axis=1)
                xp = xdt_b[:, h0 * P:(h0 + 2) * P]
                zero = jnp.zeros_like(xp)
                rhs = jnp.concatenate([jnp.where(first_half, xp, zero), jnp.where(first_half, zero, xp)], axis=0)
                y_parts.append(jnp.dot(lhs, rhs, preferred_element_type=F32))
        y = jnp.concatenate(y_parts, axis=1) + jnp.concatenate(y_off_parts, axis=1) * ecs_e + dsk_ref[...] * xs
        y = y * _silu(z_ref[rows, :].astype(F32))
        y = jnp.concatenate([_rms(y[:, g * gw:(g + 1) * gw]) for g in range(G)], axis=1) * g_ref[...]
        y_ref[rows, :] = y.astype(y_ref.dtype)


def _ssd(xbc, z, small, conv_w, conv_b, dt_bias, a_log, d_skip_e, norm_g, batch, seq):
    t, cd = xbc.shape
    d_ssm = z.shape[1]
    n_heads = dt_bias.shape[0]
    L = min(SSD_STEP_ROWS, seq)
    nc = seq // L
    row = lambda b, c: (b * nc + c, 0)
    tail = lambda b, c: (jnp.maximum((b * nc + c) * (L // CONV_CARRY) - 1, 0), 0)
    kern = functools.partial(_ssd_kernel, n_heads=n_heads, d_ssm=d_ssm)
    return pl.pallas_call(
        kern,
        out_shape=jax.ShapeDtypeStruct((t, d_ssm), BF16),
        grid=(batch, nc),
        in_specs=[pl.BlockSpec((CONV_CARRY, cd), tail),
                  pl.BlockSpec((L, cd), row), pl.BlockSpec((L, d_ssm), row), pl.BlockSpec((L, LANES), row),
                  _const_spec(conv_w.shape), _const_spec((1, cd)),
                  _const_spec((1, n_heads)), _const_spec((n_heads, 1)),
                  _const_spec((1, n_heads)), _const_spec((n_heads, 1)),
                  _const_spec((1, d_ssm)), _const_spec((1, d_ssm))],
        out_specs=pl.BlockSpec((L, d_ssm), row),
        scratch_shapes=[pltpu.VMEM((SSM_GROUPS, SSM_STATE, d_ssm // SSM_GROUPS), F32)],
        compiler_params=_params("arbitrary", "arbitrary"),
        name="ssd",
    )(xbc, xbc, z, small, conv_w, conv_b.reshape(1, cd), dt_bias.reshape(1, n_heads), dt_bias.reshape(n_heads, 1),
      a_log.reshape(1, n_heads), a_log.reshape(n_heads, 1), d_skip_e, norm_g.reshape(1, d_ssm))


def _gla_kernel(q_ref, k_ref, v_ref, r_ref, sm_ref, wg2_ref, bg_ref, gn_ref, o_ref, st_ref, *, gate_col):
    L = GLA_CHUNK
    H = GLA_HEADS
    dk = q_ref.shape[1] // H
    dv = v_ref.shape[1] // H
    c = pl.program_id(1)

    @pl.when(c == 0)
    def _():
        st_ref[...] = jnp.zeros(st_ref.shape, F32)

    R = min(GLA_GROUP_ROWS, q_ref.shape[0])
    n_chunks = R // L
    ri = lax.broadcasted_iota(I32, (R, R), 0)
    ci = lax.broadcasted_iota(I32, (R, R), 1)
    same_chunk = (ri // L) == (ci // L)
    causal = jnp.logical_and(same_chunk, ri >= ci)
    tril = causal.astype(BF16)
    later = jnp.logical_and(same_chunk, ri < ci).astype(BF16)
    sr = lax.broadcasted_iota(I32, (R, n_chunks * LANES), 0)
    sc = lax.broadcasted_iota(I32, (R, n_chunks * LANES), 1)
    last_rows = (sr == (sc // LANES) * L + (L - 1)).astype(BF16)
    wg2_hi, wg2_lo = _pieces(wg2_ref[...], 2)

    tril3 = jnp.concatenate([jnp.concatenate([tril] * 3, axis=1), jnp.concatenate([later] * 3, axis=1)], axis=0)
    last3 = jnp.concatenate([last_rows] * 3, axis=0)
    wg3 = jnp.concatenate([wg2_hi, wg2_hi, wg2_lo], axis=0)

    for gi in range(q_ref.shape[0] // R):
        rs = slice(gi * R, (gi + 1) * R)
        q = q_ref[rs, :].astype(F32) * (dk ** -0.5)
        k = k_ref[rs, :].astype(F32)
        v = v_ref[rs, :]
        r = r_ref[rs, :].astype(F32)
        g_hi, g_lo = _pieces(sm_ref[rs, gate_col:gate_col + GLA_GATE_RANK], 2)
        pre = jnp.dot(jnp.concatenate([g_hi, g_lo, g_hi], axis=1), wg3, preferred_element_type=F32) + bg_ref[...]
        gk3 = jnp.concatenate(_pieces(_log_sigmoid(pre) / GLA_GATE_NORM, 3), axis=0)
        sums = jnp.dot(tril3, gk3, preferred_element_type=F32)
        bcum = sums[:R]
        to_end = sums[R:]
        q_t = (q * jnp.exp(bcum)).astype(BF16)
        k_t = (k * jnp.exp(-bcum)).astype(BF16)
        k_dec = (k * jnp.exp(to_end)).astype(BF16)
        dcol = jnp.exp(lax.dot_general(jnp.concatenate(_pieces(bcum, 3), axis=0), last3, _TN,
                                       preferred_element_type=F32))
        outs = []
        for h in range(H):
            ks = slice(h * dk, (h + 1) * dk)
            vs = slice(h * dv, (h + 1) * dv)
            att = lax.dot_general(q_t[:, ks], k_t[:, ks], _NT, preferred_element_type=F32)
            att = jnp.where(causal, att, 0.0).astype(BF16)
            o = jnp.dot(att, v[:, vs], preferred_element_type=F32)
            state = st_ref[h]
            inter = []
            for c in range(n_chunks):
                rows = slice(c * L, (c + 1) * L)
                inter.append(jnp.dot(q_t[rows, ks], state.astype(BF16), preferred_element_type=F32))
                s_new = lax.dot_general(k_dec[rows, ks], v[rows, vs], _TN, preferred_element_type=F32)
                dec = dcol[ks, c * LANES:(c + 1) * LANES]
                state = state * jnp.concatenate([dec] * (dv // LANES), axis=1) + s_new
            st_ref[h] = state
            o = o + jnp.concatenate(inter, axis=0)
            outs.append(_rms(o) * gn_ref[...] * _silu(r[:, vs]))
        o_ref[rs, :] = jnp.concatenate(outs, axis=1).astype(o_ref.dtype)


def _gla(q, k, v, r, small, wg2, bg, norm_g, batch, seq, gate_col):
    t, dkt = q.shape
    dvt = v.shape[1]
    rows = min(GLA_STEP_ROWS, seq)
    nc = seq // rows
    row = lambda b, c: (b * nc + c, 0)
    kern = functools.partial(_gla_kernel, gate_col=gate_col)
    return pl.pallas_call(
        kern,
        out_shape=jax.ShapeDtypeStruct((t, dvt), BF16),
        grid=(batch, nc),
        in_specs=[pl.BlockSpec((rows, dkt), row), pl.BlockSpec((rows, dkt), row), pl.BlockSpec((rows, dvt), row),
                  pl.BlockSpec((rows, dvt), row), pl.BlockSpec((rows, LANES), row),
                  _const_spec(wg2.shape), _const_spec((1, dkt)), _const_spec((1, dvt // GLA_HEADS))],
        out_specs=pl.BlockSpec((rows, dvt), row),
        scratch_shapes=[pltpu.VMEM((GLA_HEADS, dkt // GLA_HEADS, dvt // GLA_HEADS), F32)],
        compiler_params=_params("arbitrary", "arbitrary"),
        name="gla",
    )(q, k, v, r, small, wg2, bg.reshape(1, dkt), norm_g.reshape(1, dvt // GLA_HEADS))


def _outproj_kernel(y_ref, o_ref, x_ref, gt_ref, sc_ref, sh_ref, g_ref, wy_ref, wo_ref, wr_ref, br_ref,
                    x1_ref, h_ref, ti_ref, tw_ref, cnt_ref):
    mix = (jnp.dot(y_ref[...], wy_ref[...], preferred_element_type=F32)
           + jnp.dot(o_ref[...], wo_ref[...], preferred_element_type=F32))
    x1 = x_ref[...] + gt_ref[0] * mix
    x1_ref[...] = x1
    h = (_rms(x1) * g_ref[...]) * (1.0 + sc_ref[0]) + sh_ref[0]
    _token_rows_store(h_ref, h)
    n_e = br_ref.shape[0]
    h_hi, h_lo = _pieces(h, 2)
    wr = wr_ref[...]
    hw = lax.dot_general(wr, h_hi, _NT, preferred_element_type=F32)
    logits = (hw[:n_e] + hw[n_e:] + lax.dot_general(wr[:n_e], h_lo, _NT, preferred_element_type=F32)) + br_ref[...]
    expert = lax.broadcasted_iota(I32, logits.shape, 0)
    vals, idxs = [], []
    counts = jnp.zeros(logits.shape, F32)
    for _ in range(TOP_K):
        m = jnp.max(logits, axis=0, keepdims=True)
        idx = jnp.min(jnp.where(logits == m, expert, n_e), axis=0, keepdims=True)
        vals.append(m)
        idxs.append(idx)
        chosen = expert == idx
        counts = counts + chosen.astype(F32)
        logits = jnp.where(chosen, -jnp.inf, logits)
    exps = [jnp.exp(v - vals[0]) for v in vals]
    denom = functools.reduce(lambda a, b: a + b, exps)
    ti_ref[...] = jnp.concatenate(idxs, axis=0)
    tw_ref[...] = jnp.concatenate([e / denom for e in exps], axis=0)

    @pl.when(pl.program_id(0) == 0)
    def _():
        cnt_ref[...] = jnp.zeros(cnt_ref.shape, F32)

    cnt_ref[...] = cnt_ref[...] + jnp.sum(counts, axis=1, keepdims=True)


def _outproj(y, o, x2, gt, sc, sh, g, wy, wo, w_router, b_router, seq):
    t, d = x2.shape
    n_e = w_router.shape[1]
    tm = min(TOKEN_TILE, seq)
    per_batch = seq // tm
    row = lambda i: (i, 0)
    mod_spec = pl.BlockSpec((1, 1, d), lambda i: (i // per_batch, 0, 0))
    wr_t = w_router.T
    wr_hi = wr_t.astype(BF16)
    wr_cat = jnp.concatenate([wr_hi, (wr_t - wr_hi.astype(F32)).astype(BF16)], axis=0)
    col = lambda i: (0, i)
    return pl.pallas_call(
        _outproj_kernel,
        out_shape=[jax.ShapeDtypeStruct((t, d), F32), jax.ShapeDtypeStruct((t * SUBLANES, LANES), F32),
                   jax.ShapeDtypeStruct((TOP_K, t), I32), jax.ShapeDtypeStruct((TOP_K, t), F32),
                   jax.ShapeDtypeStruct((n_e, LANES), F32)],
        grid=(t // tm,),
        in_specs=[pl.BlockSpec((tm, y.shape[1]), row), pl.BlockSpec((tm, o.shape[1]), row),
                  pl.BlockSpec((tm, d), row), mod_spec, mod_spec, mod_spec, _const_spec((1, d)),
                  _const_spec(wy.shape), _const_spec(wo.shape), _const_spec(wr_cat.shape),
                  _const_spec((n_e, 1))],
        out_specs=[pl.BlockSpec((tm, d), row), pl.BlockSpec((tm * SUBLANES, LANES), row),
                   pl.BlockSpec((TOP_K, tm), col), pl.BlockSpec((TOP_K, tm), col),
                   _const_spec((n_e, LANES))],
        compiler_params=_params("arbitrary"),
        name="outproj",
    )(y, o, x2, gt, sc, sh, g, wy, wo, wr_cat, b_router.reshape(n_e, 1))


def _route_kernel(ti_ref, cnt_ref, dest_ref, be_ref, pend_ref, run_ref, *, n_blocks_pad):
    i = pl.program_id(0)
    n_e = cnt_ref.shape[0]
    tr = ti_ref.shape[1]

    @pl.when(i == 0)
    def _():
        counts = cnt_ref[...]
        padded = jnp.ceil(counts / EXPERT_BLOCK) * EXPERT_BLOCK
        ri = lax.broadcasted_iota(I32, (n_e, n_e), 0)
        ci = lax.broadcasted_iota(I32, (n_e, n_e), 1)
        pend = jnp.dot((ri >= ci).astype(F32), padded, precision=HIGHEST, preferred_element_type=F32)
        pend_ref[...] = pend
        run_ref[...] = pend - padded
        start = (lax.broadcasted_iota(I32, (n_e, n_blocks_pad), 1) * EXPERT_BLOCK).astype(F32)
        be = jnp.sum((pend[:, 0:1] <= start).astype(F32), axis=0, keepdims=True)
        be_ref[...] = jnp.minimum(be, n_e - 1).astype(I32)

    ti = ti_ref[...]
    expert = lax.broadcasted_iota(I32, (n_e, tr), 0)
    onehots = [expert == ti[k:k + 1, :] for k in range(TOP_K)]
    cnt = functools.reduce(lambda a, b: a + b, [oh.astype(F32) for oh in onehots])
    ri = lax.broadcasted_iota(I32, (tr, tr), 0)
    ci = lax.broadcasted_iota(I32, (tr, tr), 1)
    before = jnp.dot(cnt.astype(BF16), (ri < ci).astype(BF16), preferred_element_type=F32)
    base = run_ref[:, 0:1] + before
    dest = [jnp.sum(jnp.where(oh, base, 0.0), axis=0, keepdims=True) for oh in onehots]
    dest_ref[...] = jnp.concatenate(dest, axis=0).astype(I32)
    run_ref[...] = run_ref[...] + jnp.sum(cnt, axis=1, keepdims=True)


def _route(topi_t, counts, n_blocks):
    t = topi_t.shape[1]
    n_e = counts.shape[0]
    tr = min(ROUTE_TILE, t)
    n_blocks_pad = -(-n_blocks // LANES) * LANES
    kern = functools.partial(_route_kernel, n_blocks_pad=n_blocks_pad)
    return pl.pallas_call(
        kern,
        out_shape=[jax.ShapeDtypeStruct((TOP_K, t), I32), jax.ShapeDtypeStruct((1, n_blocks_pad), I32),
                   jax.ShapeDtypeStruct((n_e, LANES), F32)],
        grid=(t // tr,),
        in_specs=[pl.BlockSpec((TOP_K, tr), lambda i: (0, i)), _const_spec((n_e, LANES))],
        out_specs=[pl.BlockSpec((TOP_K, tr), lambda i: (0, i)), _const_spec((1, n_blocks_pad)),
                   _const_spec((n_e, LANES))],
        scratch_shapes=[pltpu.VMEM((n_e, LANES), F32)],
        compiler_params=_params("arbitrary"),
        name="route",
    )(topi_t, counts)


def _dispatch_kernel(pend_ref, dest_hbm, h_ref, xs_hbm, idx_ref, zero_ref, idx_sem, row_sem, *, n_experts):
    i = pl.program_id(0)
    n_idx = idx_ref.shape[0]
    tg = n_idx // TOP_K

    @pl.when(i == 0)
    def _():
        zero_ref[...] = jnp.zeros(zero_ref.shape, zero_ref.dtype)
        for e in range(n_experts):
            end = pend_ref[e]
            prev = pend_ref[e - 1] if e > 0 else 0

            @pl.when(end > prev)
            def _():
                start = pl.multiple_of((end - EXPERT_BLOCK) * SUBLANES, EXPERT_BLOCK * SUBLANES)
                cp = pltpu.make_async_copy(zero_ref, xs_hbm.at[pl.ds(start, EXPERT_BLOCK * SUBLANES)], row_sem)
                cp.start()
                cp.wait()

        n_blocks = xs_hbm.shape[0] // (EXPERT_BLOCK * SUBLANES)
        total = pend_ref[n_experts - 1]
        for b in range(n_blocks - n_experts, n_blocks):
            @pl.when(b * EXPERT_BLOCK >= total)
            def _():
                cp = pltpu.make_async_copy(
                    zero_ref, xs_hbm.at[pl.ds(b * EXPERT_BLOCK * SUBLANES, EXPERT_BLOCK * SUBLANES)], row_sem)
                cp.start()
                cp.wait()

    idx_cp = pltpu.make_async_copy(dest_hbm.at[pl.ds(i * n_idx, n_idx)], idx_ref, idx_sem)
    idx_cp.start()
    idx_cp.wait()

    def issue(tl, carry):
        src = h_ref.at[pl.ds(pl.multiple_of(tl * SUBLANES, SUBLANES), SUBLANES)]
        for k in range(TOP_K):
            d = pl.multiple_of(idx_ref[tl * TOP_K + k] * SUBLANES, SUBLANES)
            pltpu.make_async_copy(src, xs_hbm.at[pl.ds(d, SUBLANES)], row_sem).start(priority=k % 2)
        return carry

    lax.fori_loop(0, tg, issue, 0)
    for _ in range(TOP_K):
        pltpu.make_async_copy(h_ref, xs_hbm.at[pl.ds(0, tg * SUBLANES)], row_sem).wait()


def _dispatch(pend_i, dest_flat, h, n_rows, n_experts):
    t = h.shape[0] // SUBLANES
    tg = min(DISPATCH_TILE, t)
    kern = functools.partial(_dispatch_kernel, n_experts=n_experts)
    return pl.pallas_call(
        kern,
        out_shape=jax.ShapeDtypeStruct((n_rows * SUBLANES, LANES), h.dtype),
        grid_spec=pltpu.PrefetchScalarGridSpec(
            num_scalar_prefetch=1,
            grid=(t // tg,),
            in_specs=[pl.BlockSpec(memory_space=pl.ANY),
                      pl.BlockSpec((tg * SUBLANES, LANES), lambda i, pend: (i, 0))],
            out_specs=pl.BlockSpec(memory_space=pl.ANY),
            scratch_shapes=[pltpu.SMEM((tg * TOP_K,), I32), pltpu.VMEM((EXPERT_BLOCK * SUBLANES, LANES), h.dtype),
                            pltpu.SemaphoreType.DMA, pltpu.SemaphoreType.DMA]),
        compiler_params=pltpu.CompilerParams(dimension_semantics=("arbitrary",), has_side_effects=True,
                                             vmem_limit_bytes=VMEM_LIMIT_BYTES),
        name="dispatch",
    )(pend_i, dest_flat, h)


def _expert_kernel(be_ref, nu_ref, pend_ref, xs_ref, wg_hbm, bg_ref, wu_hbm, bu_ref, wd_hbm, bd_ref, y_ref,
                   wg_f, wu_f, wd_f, wg_b, wu_b, wd_b, slot_ref, sems):
    i = pl.program_id(0)
    used = i < nu_ref[0]
    e = be_ref[i]

    def fetch(expert, slot):
        return [pltpu.make_async_copy(src.at[expert], dst.at[slot], sems.at[slot])
                for src, dst in ((wg_hbm, wg_f), (wu_hbm, wu_f), (wd_hbm, wd_f))]

    @pl.when(i == 0)
    def _():
        slot_ref[0] = 0
        for cp in fetch(e, 0):
            cp.start()

    first_of_expert = jnp.logical_or(i == 0, e != be_ref[jnp.maximum(i - 1, 0)])

    @pl.when(jnp.logical_and(used, first_of_expert))
    def _():
        slot = slot_ref[0]
        for cp in fetch(e, slot):
            cp.wait()
        wg_b[...] = wg_f[slot].astype(BF16)
        wu_b[...] = wu_f[slot].astype(BF16)
        wd_b[...] = wd_f[slot].astype(BF16)
        nxt = lax.div(pend_ref[e], EXPERT_BLOCK)

        @pl.when(nxt < nu_ref[0])
        def _():
            for cp in fetch(be_ref[nxt], 1 - slot):
                cp.start()

        slot_ref[0] = 1 - slot

    @pl.when(used)
    def _():
        x = _token_rows_load(xs_ref, EXPERT_BLOCK).astype(BF16)
        gate = jnp.minimum(jnp.dot(x, wg_b[...], preferred_element_type=F32) + bg_ref[...], SWIGLU_LIMIT)
        up = jnp.clip(jnp.dot(x, wu_b[...], preferred_element_type=F32) + bu_ref[...],
                      -SWIGLU_LIMIT, SWIGLU_LIMIT)
        glu = gate * _sigmoid(SWIGLU_ALPHA * gate)
        mid = ((up + 1.0) * glu).astype(BF16)
        y = jnp.dot(mid, wd_b[...], preferred_element_type=F32) + bd_ref[...]
        _token_rows_store(y_ref, y)

    @pl.when(jnp.logical_not(used))
    def _():
        y_ref[...] = jnp.zeros(y_ref.shape, y_ref.dtype)


def _experts(block_e, n_used, pend_i, xs, w_gate, b_gate, w_up, b_up, w_down, b_down):
    n_rows = xs.shape[0] // SUBLANES
    n_e, d, f = w_gate.shape
    nb = n_rows // EXPERT_BLOCK
    blk = (EXPERT_BLOCK * SUBLANES, LANES)
    last = lambda i, be, nu, pend: jnp.maximum(jnp.minimum(i, nu[0] - 1), 0)
    bspec = lambda n: pl.BlockSpec((None, 1, n), lambda i, be, nu, pend: (be[last(i, be, nu, pend)], 0, 0))
    hbm = pl.BlockSpec(memory_space=pl.ANY)
    return pl.pallas_call(
        _expert_kernel,
        out_shape=jax.ShapeDtypeStruct((n_rows * SUBLANES, LANES), F32),
        grid_spec=pltpu.PrefetchScalarGridSpec(
            num_scalar_prefetch=3,
            grid=(nb,),
            in_specs=[pl.BlockSpec(blk, lambda i, be, nu, pend: (last(i, be, nu, pend), 0)),
                      hbm, bspec(f), hbm, bspec(f), hbm, bspec(d)],
            out_specs=pl.BlockSpec(blk, lambda i, be, nu, pend: (i, 0)),
            scratch_shapes=[pltpu.VMEM((2, d, f), F32), pltpu.VMEM((2, d, f), F32), pltpu.VMEM((2, f, d), F32),
                            pltpu.VMEM((d, f), BF16), pltpu.VMEM((d, f), BF16), pltpu.VMEM((f, d), BF16),
                            pltpu.SMEM((1,), I32), pltpu.SemaphoreType.DMA((2,))]),
        compiler_params=_params("arbitrary"),
        name="experts",
    )(block_e, n_used, pend_i, xs, w_gate, b_gate.reshape(n_e, 1, f), w_up, b_up.reshape(n_e, 1, f),
      w_down, b_down.reshape(n_e, 1, d))


def _combine_kernel(dest_hbm, ys_hbm, tw_ref, x1_ref, gt_ref, g_ref, o_ref, idx_ref, buf_ref, idx_sem, row_sem):
    i = pl.program_id(0)
    n = pl.num_programs(0)
    n_idx = idx_ref.shape[1]
    tc = n_idx // TOP_K
    slot = lax.rem(i, 2)

    def idx_copy(tile, s):
        return pltpu.make_async_copy(dest_hbm.at[pl.ds(tile * n_idx, n_idx)], idx_ref.at[s], idx_sem.at[s])

    def issue_rows(s):
        def issue(tl, carry):
            dst_row = pl.multiple_of(tl * SUBLANES, SUBLANES)
            for k in range(TOP_K):
                d = pl.multiple_of(idx_ref[s, tl * TOP_K + k] * SUBLANES, SUBLANES)
                pltpu.make_async_copy(ys_hbm.at[pl.ds(d, SUBLANES)], buf_ref.at[s, k, pl.ds(dst_row, SUBLANES)],
                                      row_sem.at[s]).start(priority=k % 2)
            return carry

        lax.fori_loop(0, tc, issue, 0)

    @pl.when(i == 0)
    def _():
        idx_copy(0, 0).start()
        idx_copy(0, 0).wait()
        issue_rows(0)

        @pl.when(n > 1)
        def _():
            idx_copy(1, 1).start()

    @pl.when(i + 1 < n)
    def _():
        idx_copy(i + 1, 1 - slot).wait()
        issue_rows(1 - slot)

        @pl.when(i + 2 < n)
        def _():
            idx_copy(i + 2, slot).start()

    for k in range(TOP_K):
        pltpu.make_async_copy(ys_hbm.at[pl.ds(0, tc * SUBLANES)], buf_ref.at[slot, k], row_sem.at[slot]).wait()

    tw = tw_ref[...]
    ffn = tw[:, 0:1] * _token_rows_load(buf_ref.at[slot, 0], tc)
    for k in range(1, TOP_K):
        ffn = ffn + tw[:, k:k + 1] * _token_rows_load(buf_ref.at[slot, k], tc)
    x2 = x1_ref[...] + gt_ref[0] * ffn
    o_ref[...] = _rms(x2) * g_ref[...]


def _combine(dest_flat, ys, topw, x1, gt, g, seq):
    t, d = x1.shape
    tc = min(COMBINE_TILE, seq)
    per_batch = seq // tc
    row = lambda i: (i, 0)
    return pl.pallas_call(
        _combine_kernel,
        out_shape=jax.ShapeDtypeStruct((t, d), F32),
        grid=(t // tc,),
        in_specs=[pl.BlockSpec(memory_space=pl.ANY), pl.BlockSpec(memory_space=pl.ANY),
                  pl.BlockSpec((tc, TOP_K), row), pl.BlockSpec((tc, d), row),
                  pl.BlockSpec((1, 1, d), lambda i: (i // per_batch, 0, 0)), _const_spec((1, d))],
        out_specs=pl.BlockSpec((tc, d), row),
        scratch_shapes=[pltpu.SMEM((2, tc * TOP_K), I32), pltpu.VMEM((2, TOP_K, tc * SUBLANES, LANES), F32),
                        pltpu.SemaphoreType.DMA((2,)), pltpu.SemaphoreType.DMA((2,))],
        compiler_params=_params("arbitrary"),
        name="combine",
    )(dest_flat, ys, topw, x1, gt, g)


def _layer(x2, mod, batch, seq, norm1_g, w_in, conv_w, conv_b, dt_bias, a_log, d_skip, ssm_norm_g,
           gla_wg2, gla_bg, gla_norm_g, w_out, norm2_g, w_router, b_router,
           w_gate, b_gate, w_up, b_up, w_down, b_down):
    t, d = x2.shape
    n_heads = dt_bias.shape[0]
    d_ssm = n_heads * SSM_HEAD_DIM
    cd = conv_w.shape[1]
    dkt = gla_wg2.shape[1]
    dvt = w_out.shape[0] - d_ssm
    n_experts = w_router.shape[1]

    sh1, sc1, gt1, sh2, sc2, gt2 = [m.reshape(batch, 1, d) for m in jnp.split(mod[:batch], 6, axis=1)]

    sizes = (d_ssm, cd, n_heads, dkt, dkt, dvt, GLA_GATE_RANK, dvt)
    offs = [0]
    for s in sizes:
        offs.append(offs[-1] + s)
    col = lambda j: w_in[:, offs[j]:offs[j + 1]]
    pad = jnp.zeros((d, LANES - n_heads - GLA_GATE_RANK), F32)
    w_all = jnp.concatenate([col(0), col(1), col(3), col(4), col(5), col(7), col(2), col(6), pad],
                            axis=1).astype(BF16)
    z, xbc, q, k, v, r, small = _inproj(x2, sc1, sh1, norm1_g.reshape(1, d), w_all,
                                        (d_ssm, cd, dkt, dkt, dvt, dvt, LANES), [BF16] * 6 + [F32], seq)

    d_skip_e = jnp.repeat(d_skip, SSM_HEAD_DIM).reshape(1, d_ssm)
    y = _ssd(xbc, z, small, conv_w, conv_b, dt_bias, a_log, d_skip_e, ssm_norm_g, batch, seq)
    o = _gla(q, k, v, r, small, gla_wg2, gla_bg, gla_norm_g, batch, seq, gate_col=n_heads)

    x1, h2, topi_t, topw_t, counts = _outproj(y, o, x2, gt1, sc2, sh2, norm2_g.reshape(1, d),
                                        w_out[:d_ssm].astype(BF16), w_out[d_ssm:].astype(BF16),
                                        w_router, b_router, seq)

    n_blocks = (t * TOP_K) // EXPERT_BLOCK + n_experts
    dest_t, block_e, pend = _route(topi_t, counts, n_blocks)
    pend_i = pend[:, 0].astype(I32)
    n_used = (pend_i[n_experts - 1:] // EXPERT_BLOCK).astype(I32)
    dest_flat = dest_t.T.reshape(t * TOP_K)
    topw = topw_t.T
    xs = _dispatch(pend_i, dest_flat, h2, n_blocks * EXPERT_BLOCK, n_experts)
    ys = _experts(block_e[0, :n_blocks], n_used, pend_i, xs, w_gate, b_gate, w_up, b_up, w_down, b_down)
    return dest_flat, ys, topw, x1, gt2


def kernel(x, c, w_ada, b_ada, norm1_g, w_in, conv_w, conv_b, dt_bias, a_log, d_skip, ssm_norm_g, gla_wg2,
           gla_bg, gla_norm_g, w_out, norm2_g, w_router, b_router, w_gate, b_gate, w_up, b_up, w_down, b_down,
           final_norm_g):
    batch, seq, d = x.shape
    assert w_ada.shape[0] == 1, "single-layer trunk"
    assert d == SUBLANES * LANES, "token rows are moved as one (8, 128) f32 tile each"
    assert seq % min(seq, max(TOKEN_TILE, SSM_CHUNK, GLA_STEP_ROWS, COMBINE_TILE, DISPATCH_TILE)) == 0
    assert seq % max(SSM_CHUNK, GLA_STEP_ROWS) == 0
    x2 = x.reshape(batch * seq, d)
    c_pad = jnp.zeros((SUBLANES, d), F32).at[:batch].set(c)
    mod = _ada(c_pad, w_ada[0], b_ada)
    dest_flat, ys, topw, x1, gt2 = _layer(
        x2, mod, batch, seq, norm1_g[0], w_in[0], conv_w[0], conv_b[0], dt_bias[0], a_log[0], d_skip[0],
        ssm_norm_g[0], gla_wg2[0], gla_bg[0], gla_norm_g[0], w_out[0], norm2_g[0], w_router[0], b_router[0],
        w_gate[0], b_gate[0], w_up[0], b_up[0], w_down[0], b_down[0])
    out = _combine(dest_flat, ys, topw, x1, gt2, final_norm_g.reshape(1, d), seq)
    return out.reshape(batch, seq, d)
```

```python
import functools

import jax
import jax.numpy as jnp
from jax import lax
from jax.experimental import pallas as pl
from jax.experimental.pallas import tpu as pltpu

F32 = jnp.float32
BF16 = jnp.bfloat16
I32 = jnp.int32
HIGHEST = lax.Precision.HIGHEST

EPS = 1e-6
SSM_HEAD_DIM = 64
SSM_GROUPS = 2
SSM_STATE = 128
SSM_CONV = 4
SSM_CHUNK = 128
GLA_HEADS = 4
GLA_GATE_RANK = 16
GLA_GATE_NORM = 16.0
GLA_CHUNK = 64
TOP_K = 4
SWIGLU_LIMIT = 7.0
SWIGLU_ALPHA = 1.702

LANES = 128
SUBLANES = 8
VMEM_LIMIT_BYTES = 56 * 1024 * 1024

TOKEN_TILE = 512
SSD_STEP_ROWS = 512
CONV_CARRY = 16
GLA_STEP_ROWS = 512
GLA_GROUP_ROWS = 256
ROUTE_TILE = 512
EXPERT_BLOCK = 512
DISPATCH_TILE = 512
COMBINE_TILE = 256

_NT = (((1,), (1,)), ((), ()))
_TN = (((0,), (0,)), ((), ()))


def _sigmoid(v):
    return 0.5 * jnp.tanh(0.5 * v) + 0.5


def _silu(v):
    return v * _sigmoid(v)


def _softplus(v):
    return jnp.maximum(v, 0.0) + jnp.log1p(jnp.exp(-jnp.abs(v)))


def _log_sigmoid(v):
    return jnp.minimum(v, 0.0) - jnp.log(1.0 + jnp.exp(-jnp.abs(v)))


def _rms(v):
    return v * lax.rsqrt(jnp.mean(v * v, axis=-1, keepdims=True) + EPS)


def _pieces(a, n):
    out = []
    for _ in range(n - 1):
        p = a.astype(BF16)
        out.append(p)
        a = a - p.astype(F32)
    out.append(a.astype(BF16))
    return out


def _dot_pieces(a, b, n, dims=None):
    dims = dims or (((a.ndim - 1,), (0,)), ((), ()))
    return sum(lax.dot_general(p, b, dims, preferred_element_type=F32) for p in _pieces(a, n))


def _token_rows_load(ref, rows):
    return jnp.concatenate([ref[pl.ds(s, rows, stride=SUBLANES), :] for s in range(SUBLANES)], axis=1)


def _token_rows_store(ref, v):
    rows = v.shape[0]
    for s in range(SUBLANES):
        ref[pl.ds(s, rows, stride=SUBLANES), :] = v[:, s * LANES:(s + 1) * LANES]


def _params(*semantics):
    return pltpu.CompilerParams(dimension_semantics=semantics, vmem_limit_bytes=VMEM_LIMIT_BYTES)


def _const_spec(shape):
    nd = len(shape)
    return pl.BlockSpec(shape, lambda *_: (0,) * nd)


def _ada_kernel(c_ref, w_ref, b_ref, o_ref):
    cond = _silu(c_ref[...])
    o_ref[...] = jnp.dot(cond, w_ref[...], precision=HIGHEST, preferred_element_type=F32) + b_ref[...]


def _ada(c_pad, w_ada, b_ada):
    rows, d = c_pad.shape
    n = w_ada.shape[1]
    tn = d
    return pl.pallas_call(
        _ada_kernel,
        out_shape=jax.ShapeDtypeStruct((rows, n), F32),
        grid=(n // tn,),
        in_specs=[pl.BlockSpec((rows, d), lambda j: (0, 0)),
                  pl.BlockSpec((d, tn), lambda j: (0, j)),
                  pl.BlockSpec((1, tn), lambda j: (0, j))],
        out_specs=pl.BlockSpec((rows, tn), lambda j: (0, j)),
        compiler_params=_params("arbitrary"),
        name="ada",
    )(c_pad, w_ada, b_ada)


def _inproj_kernel(x_ref, sc_ref, sh_ref, g_ref, w_ref, *refs, sections):
    o_refs, ws_ref = refs[:-1], refs[-1]

    @pl.when(pl.program_id(0) == 0)
    def _():
        off = 0
        for o_ref, pieces in zip(o_refs, sections):
            n = o_ref.shape[1]
            used = 0
            for src, width in pieces:
                ws_ref[:, off + used:off + used + width] = w_ref[:, src:src + width]
                used += width
            if used < n:
                ws_ref[:, off + used:off + n] = jnp.zeros((ws_ref.shape[0], n - used), ws_ref.dtype)
            off += n

    h = (_rms(x_ref[...]) * g_ref[...]) * (1.0 + sc_ref[0]) + sh_ref[0]
    hb = h.astype(BF16)
    off = 0
    for o_ref in o_refs:
        n = o_ref.shape[1]
        o_ref[...] = jnp.dot(hb, ws_ref[:, off:off + n], preferred_element_type=F32).astype(o_ref.dtype)
        off += n


def _inproj(x2, sc, sh, g, w_bf, sections, widths, out_dtypes, seq):
    t, d = x2.shape
    tm = min(TOKEN_TILE, seq)
    per_batch = seq // tm
    assert all(n % LANES == 0 for n in widths)
    mod_spec = pl.BlockSpec((1, 1, d), lambda i: (i // per_batch, 0, 0))
    kern = functools.partial(_inproj_kernel, sections=sections)
    return pl.pallas_call(
        kern,
        out_shape=[jax.ShapeDtypeStruct((t, n), dt) for n, dt in zip(widths, out_dtypes)],
        grid=(t // tm,),
        in_specs=[pl.BlockSpec((tm, d), lambda i: (i, 0)), mod_spec, mod_spec, _const_spec((1, d)),
                  pl.BlockSpec(w_bf.shape, lambda i: (0, 0), pipeline_mode=pl.Buffered(1))],
        out_specs=[pl.BlockSpec((tm, n), lambda i: (i, 0)) for n in widths],
        scratch_shapes=[pltpu.VMEM((d, sum(widths)), BF16)],
        compiler_params=_params("arbitrary"),
        name="inproj",
    )(x2, sc, sh, g, w_bf)


def _ssd_kernel(tail_ref, xbc_ref, z_ref, sm_ref, cw_ref, cb_ref, dtb_r_ref, dtb_c_ref, alog_r_ref, alog_c_ref,
                dsk_ref, g_ref, y_ref, st_ref, *, n_heads, d_ssm):
    L = SSM_CHUNK
    P = SSM_HEAD_DIM
    N = SSM_STATE
    G = SSM_GROUPS
    gw = d_ssm // G
    R = xbc_ref.shape[0]
    c = pl.program_id(1)

    @pl.when(c == 0)
    def _():
        st_ref[...] = jnp.zeros(st_ref.shape, F32)

    tail = tail_ref[...]
    tail = jnp.where(c == 0, jnp.zeros_like(tail), tail)
    conv_in = jnp.concatenate([tail, xbc_ref[...]], axis=0)

    ri = lax.broadcasted_iota(I32, (L, L), 0)
    ci = lax.broadcasted_iota(I32, (L, L), 1)
    causal = ri >= ci
    causal3 = jnp.concatenate([causal.astype(BF16)] * 3, axis=1)
    upper3 = jnp.concatenate([(ri <= ci).astype(BF16)] * 3, axis=0)
    wr = lax.broadcasted_iota(I32, ((SSM_CONV - 1) * L, CONV_CARRY + L), 0)
    wc = lax.broadcasted_iota(I32, ((SSM_CONV - 1) * L, CONV_CARRY + L), 1)
    shifts = (wc == (wr % L) + CONV_CARRY - (SSM_CONV - 1) + wr // L).astype(BF16)
    hh = lax.broadcasted_iota(I32, (n_heads, d_ssm), 0)
    jj = lax.broadcasted_iota(I32, (n_heads, d_ssm), 1)
    expand = ((jj // P) == hh).astype(BF16)
    expand2 = jnp.concatenate([expand, expand], axis=0)
    lane = lax.broadcasted_iota(I32, (L, LANES), 1)
    first_half = lane < P
    heads_per_group = n_heads // G
    a_row = -jnp.exp(alog_r_ref[...])
    a_col = -jnp.exp(alog_c_ref[...])

    for s in range(R // L):
        rows = slice(s * L, (s + 1) * L)
        window = conv_in[s * L:s * L + CONV_CARRY + L, :]
        taps = jnp.dot(shifts, window, preferred_element_type=F32)
        acc = cb_ref[...] + cw_ref[SSM_CONV - 1:SSM_CONV, :] * window[CONV_CARRY:, :].astype(F32)
        for j in range(SSM_CONV - 1):
            acc = acc + cw_ref[j:j + 1, :] * taps[j * L:(j + 1) * L, :]
        act = _silu(acc)
        xs = act[:, :d_ssm]
        bm = act[:, d_ssm:d_ssm + G * N].astype(BF16)
        cm = act[:, d_ssm + G * N:].astype(BF16)

        sm = sm_ref[rows, :]
        dt_col = _softplus(sm[:, :n_heads] + dtb_r_ref[...])
        dt_row = _softplus(sm.T[:n_heads, :] + dtb_c_ref[...])
        cs_col = jnp.dot(causal3, jnp.concatenate(_pieces(dt_col * a_row, 3), axis=0),
                         preferred_element_type=F32)
        cs_row = jnp.dot(jnp.concatenate(_pieces(dt_row * a_col, 3), axis=1), upper3,
                         preferred_element_type=F32)
        cs_last = cs_col[L - 1:L, :]

        per_head = jnp.concatenate([dt_col, jnp.exp(cs_col), jnp.exp(cs_last - cs_col)], axis=0)
        per_ch = jnp.dot(jnp.concatenate(_pieces(per_head, 2), axis=1), expand2,
                         preferred_element_type=F32)
        dt_e, ecs_e, dte_e = per_ch[0:L], per_ch[L:2 * L], per_ch[2 * L:3 * L]
        chunk_decay = ecs_e[L - 1:L, :]

        xdt = xs * dt_e
        xdt_b = xdt.astype(BF16)
        xdec_b = (xdt * dte_e).astype(BF16)

        y_parts = []
        y_off_parts = []
        for g in range(G):
            bm_g = bm[:, g * N:(g + 1) * N]
            cm_g = cm[:, g * N:(g + 1) * N]
            cb = lax.dot_general(cm_g, bm_g, _NT, preferred_element_type=F32)
            prev = st_ref[g]
            y_off_parts.append(jnp.dot(cm_g, prev.astype(BF16), preferred_element_type=F32))
            s_new = lax.dot_general(bm_g, xdec_b[:, g * gw:(g + 1) * gw], _TN, preferred_element_type=F32)
            st_ref[g] = prev * chunk_decay[:, g * gw:(g + 1) * gw] + s_new
            for p in range(heads_per_group // 2):
                h0 = g * heads_per_group + 2 * p
                ms = []
                for h in (h0, h0 + 1):
                    diff = cs_col[:, h:h + 1] - cs_row[h:h + 1, :]
                    ms.append((cb * jnp.exp(jnp.where(causal, diff, -jnp.inf))).astype(BF16))
                lhs = jnp.concatenate(ms, axis=1)
                xp = xdt_b[:, h0 * P:(h0 + 2) * P]
                zero = jnp.zeros_like(xp)
                rhs = jnp.concatenate([jnp.where(first_half, xp, zero), jnp.where(first_half, zero, xp)], axis=0)
                y_parts.append(jnp.dot(lhs, rhs, preferred_element_type=F32))
        y = jnp.concatenate(y_parts, axis=1) + jnp.concatenate(y_off_parts, axis=1) * ecs_e + dsk_ref[...] * xs
        y = y * _silu(z_ref[rows, :].astype(F32))
        y = jnp.concatenate([_rms(y[:, g * gw:(g + 1) * gw]) for g in range(G)], axis=1) * g_ref[...]
        y_ref[rows, :] = y.astype(y_ref.dtype)


def _ssd(xbc, z, small, conv_w, conv_b, dt_bias, a_log, d_skip_e, norm_g, batch, seq):
    t, cd = xbc.shape
    d_ssm = z.shape[1]
    n_heads = dt_bias.shape[0]
    L = min(SSD_STEP_ROWS, seq)
    nc = seq // L
    row = lambda b, c: (b * nc + c, 0)
    tail = lambda b, c: (jnp.maximum((b * nc + c) * (L // CONV_CARRY) - 1, 0), 0)
    kern = functools.partial(_ssd_kernel, n_heads=n_heads, d_ssm=d_ssm)
    return pl.pallas_call(
        kern,
        out_shape=jax.ShapeDtypeStruct((t, d_ssm), BF16),
        grid=(batch, nc),
        in_specs=[pl.BlockSpec((CONV_CARRY, cd), tail),
                  pl.BlockSpec((L, cd), row), pl.BlockSpec((L, d_ssm), row), pl.BlockSpec((L, LANES), row),
                  _const_spec(conv_w.shape), _const_spec((1, cd)),
                  _const_spec((1, n_heads)), _const_spec((n_heads, 1)),
                  _const_spec((1, n_heads)), _const_spec((n_heads, 1)),
                  _const_spec((1, d_ssm)), _const_spec((1, d_ssm))],
        out_specs=pl.BlockSpec((L, d_ssm), row),
        scratch_shapes=[pltpu.VMEM((SSM_GROUPS, SSM_STATE, d_ssm // SSM_GROUPS), F32)],
        compiler_params=_params("arbitrary", "arbitrary"),
        name="ssd",
    )(xbc, xbc, z, small, conv_w, conv_b.reshape(1, cd), dt_bias.reshape(1, n_heads), dt_bias.reshape(n_heads, 1),
      a_log.reshape(1, n_heads), a_log.reshape(n_heads, 1), d_skip_e, norm_g.reshape(1, d_ssm))


def _gla_kernel(q_ref, k_ref, v_ref, r_ref, sm_ref, wg2_ref, bg_ref, gn_ref, o_ref, st_ref, *, gate_col):
    L = GLA_CHUNK
    H = GLA_HEADS
    dk = q_ref.shape[1] // H
    dv = v_ref.shape[1] // H
    c = pl.program_id(1)

    @pl.when(c == 0)
    def _():
        st_ref[...] = jnp.zeros(st_ref.shape, F32)

    R = min(GLA_GROUP_ROWS, q_ref.shape[0])
    n_chunks = R // L
    ri = lax.broadcasted_iota(I32, (R, R), 0)
    ci = lax.broadcasted_iota(I32, (R, R), 1)
    same_chunk = (ri // L) == (ci // L)
    causal = jnp.logical_and(same_chunk, ri >= ci)
    tril = causal.astype(BF16)
    later = jnp.logical_and(same_chunk, ri < ci).astype(BF16)
    sr = lax.broadcasted_iota(I32, (R, n_chunks * LANES), 0)
    sc = lax.broadcasted_iota(I32, (R, n_chunks * LANES), 1)
    last_rows = (sr == (sc // LANES) * L + (L - 1)).astype(BF16)
    wg2_hi, wg2_lo = _pieces(wg2_ref[...], 2)

    tril3 = jnp.concatenate([jnp.concatenate([tril] * 3, axis=1), jnp.concatenate([later] * 3, axis=1)], axis=0)
    last3 = jnp.concatenate([last_rows] * 3, axis=0)
    wg3 = jnp.concatenate([wg2_hi, wg2_hi, wg2_lo], axis=0)

    for gi in range(q_ref.shape[0] // R):
        rs = slice(gi * R, (gi + 1) * R)
        q = q_ref[rs, :].astype(F32) * (dk ** -0.5)
        k = k_ref[rs, :].astype(F32)
        v = v_ref[rs, :]
        r = r_ref[rs, :].astype(F32)
        g_hi, g_lo = _pieces(sm_ref[rs, gate_col:gate_col + GLA_GATE_RANK], 2)
        pre = jnp.dot(jnp.concatenate([g_hi, g_lo, g_hi], axis=1), wg3, preferred_element_type=F32) + bg_ref[...]
        gk3 = jnp.concatenate(_pieces(_log_sigmoid(pre) / GLA_GATE_NORM, 3), axis=0)
        sums = jnp.dot(tril3, gk3, preferred_element_type=F32)
        bcum = sums[:R]
        to_end = sums[R:]
        q_t = (q * jnp.exp(bcum)).astype(BF16)
        k_t = (k * jnp.exp(-bcum)).astype(BF16)
        k_dec = (k * jnp.exp(to_end)).astype(BF16)
        dcol = jnp.exp(lax.dot_general(jnp.concatenate(_pieces(bcum, 3), axis=0), last3, _TN,
                                       preferred_element_type=F32))
        outs = []
        for h in range(H):
            ks = slice(h * dk, (h + 1) * dk)
            vs = slice(h * dv, (h + 1) * dv)
            att = lax.dot_general(q_t[:, ks], k_t[:, ks], _NT, preferred_element_type=F32)
            att = jnp.where(causal, att, 0.0).astype(BF16)
            o = jnp.dot(att, v[:, vs], preferred_element_type=F32)
            state = st_ref[h]
            inter = []
            for c in range(n_chunks):
                rows = slice(c * L, (c + 1) * L)
                inter.append(jnp.dot(q_t[rows, ks], state.astype(BF16), preferred_element_type=F32))
                s_new = lax.dot_general(k_dec[rows, ks], v[rows, vs], _TN, preferred_element_type=F32)
                dec = dcol[ks, c * LANES:(c + 1) * LANES]
                state = state * jnp.concatenate([dec] * (dv // LANES), axis=1) + s_new
            st_ref[h] = state
            o = o + jnp.concatenate(inter, axis=0)
            outs.append(_rms(o) * gn_ref[...] * _silu(r[:, vs]))
        o_ref[rs, :] = jnp.concatenate(outs, axis=1).astype(o_ref.dtype)


def _gla(q, k, v, r, small, wg2, bg, norm_g, batch, seq, gate_col):
    t, dkt = q.shape
    dvt = v.shape[1]
    rows = min(GLA_STEP_ROWS, seq)
    nc = seq // rows
    row = lambda b, c: (b * nc + c, 0)
    kern = functools.partial(_gla_kernel, gate_col=gate_col)
    return pl.pallas_call(
        kern,
        out_shape=jax.ShapeDtypeStruct((t, dvt), BF16),
        grid=(batch, nc),
        in_specs=[pl.BlockSpec((rows, dkt), row), pl.BlockSpec((rows, dkt), row), pl.BlockSpec((rows, dvt), row),
                  pl.BlockSpec((rows, dvt), row), pl.BlockSpec((rows, LANES), row),
                  _const_spec(wg2.shape), _const_spec((1, dkt)), _const_spec((1, dvt // GLA_HEADS))],
        out_specs=pl.BlockSpec((rows, dvt), row),
        scratch_shapes=[pltpu.VMEM((GLA_HEADS, dkt // GLA_HEADS, dvt // GLA_HEADS), F32)],
        compiler_params=_params("arbitrary", "arbitrary"),
        name="gla",
    )(q, k, v, r, small, wg2, bg.reshape(1, dkt), norm_g.reshape(1, dvt // GLA_HEADS))


def _outproj_kernel(y_ref, o_ref, x_ref, gt_ref, sc_ref, sh_ref, g_ref, wy_ref, wo_ref, wr_ref, br_ref,
                    x1_ref, h_ref, ti_ref, tw_ref, cnt_ref):
    mix = (jnp.dot(y_ref[...], wy_ref[...], preferred_element_type=F32)
           + jnp.dot(o_ref[...], wo_ref[...], preferred_element_type=F32))
    x1 = x_ref[...] + gt_ref[0] * mix
    x1_ref[...] = x1
    h = (_rms(x1) * g_ref[...]) * (1.0 + sc_ref[0]) + sh_ref[0]
    _token_rows_store(h_ref, h)
    n_e = br_ref.shape[0]
    h_hi, h_lo = _pieces(h, 2)
    wr = wr_ref[...]
    hw = lax.dot_general(wr, h_hi, _NT, preferred_element_type=F32)
    logits = (hw[:n_e] + hw[n_e:] + lax.dot_general(wr[:n_e], h_lo, _NT, preferred_element_type=F32)) + br_ref[...]
    expert = lax.broadcasted_iota(I32, logits.shape, 0)
    vals, idxs = [], []
    counts = jnp.zeros(logits.shape, F32)
    for _ in range(TOP_K):
        m = jnp.max(logits, axis=0, keepdims=True)
        idx = jnp.min(jnp.where(logits == m, expert, n_e), axis=0, keepdims=True)
        vals.append(m)
        idxs.append(idx)
        chosen = expert == idx
        counts = counts + chosen.astype(F32)
        logits = jnp.where(chosen, -jnp.inf, logits)
    exps = [jnp.exp(v - vals[0]) for v in vals]
    denom = functools.reduce(lambda a, b: a + b, exps)
    ti_ref[...] = jnp.concatenate(idxs, axis=0)
    tw_ref[...] = jnp.concatenate([e / denom for e in exps], axis=0)

    @pl.when(pl.program_id(0) == 0)
    def _():
        cnt_ref[...] = jnp.zeros(cnt_ref.shape, F32)

    cnt_ref[...] = cnt_ref[...] + jnp.sum(counts, axis=1, keepdims=True)


def _outproj(y, o, x2, gt, sc, sh, g, wy, wo, w_router, b_router, seq):
    t, d = x2.shape
    n_e = w_router.shape[1]
    tm = min(TOKEN_TILE, seq)
    per_batch = seq // tm
    row = lambda i: (i, 0)
    mod_spec = pl.BlockSpec((1, 1, d), lambda i: (i // per_batch, 0, 0))
    wr_t = w_router.T
    wr_hi = wr_t.astype(BF16)
    wr_cat = jnp.concatenate([wr_hi, (wr_t - wr_hi.astype(F32)).astype(BF16)], axis=0)
    col = lambda i: (0, i)
    return pl.pallas_call(
        _outproj_kernel,
        out_shape=[jax.ShapeDtypeStruct((t, d), F32), jax.ShapeDtypeStruct((t * SUBLANES, LANES), F32),
                   jax.ShapeDtypeStruct((TOP_K, t), I32), jax.ShapeDtypeStruct((TOP_K, t), F32),
                   jax.ShapeDtypeStruct((n_e, LANES), F32)],
        grid=(t // tm,),
        in_specs=[pl.BlockSpec((tm, y.shape[1]), row), pl.BlockSpec((tm, o.shape[1]), row),
                  pl.BlockSpec((tm, d), row), mod_spec, mod_spec, mod_spec, _const_spec((1, d)),
                  _const_spec(wy.shape), _const_spec(wo.shape), _const_spec(wr_cat.shape),
                  _const_spec((n_e, 1))],
        out_specs=[pl.BlockSpec((tm, d), row), pl.BlockSpec((tm * SUBLANES, LANES), row),
                   pl.BlockSpec((TOP_K, tm), col), pl.BlockSpec((TOP_K, tm), col),
                   _const_spec((n_e, LANES))],
        compiler_params=_params("arbitrary"),
        name="outproj",
    )(y, o, x2, gt, sc, sh, g, wy, wo, wr_cat, b_router.reshape(n_e, 1))


def _route_kernel(ti_ref, cnt_ref, dest_ref, be_ref, pend_ref, run_ref, *, n_blocks_pad):
    i = pl.program_id(0)
    n_e = cnt_ref.shape[0]
    tr = ti_ref.shape[1]

    @pl.when(i == 0)
    def _():
        counts = cnt_ref[...]
        padded = jnp.ceil(counts / EXPERT_BLOCK) * EXPERT_BLOCK
        ri = lax.broadcasted_iota(I32, (n_e, n_e), 0)
        ci = lax.broadcasted_iota(I32, (n_e, n_e), 1)
        pend = jnp.dot((ri >= ci).astype(F32), padded, precision=HIGHEST, preferred_element_type=F32)
        pend_ref[...] = pend
        run_ref[...] = pend - padded
        start = (lax.broadcasted_iota(I32, (n_e, n_blocks_pad), 1) * EXPERT_BLOCK).astype(F32)
        be = jnp.sum((pend[:, 0:1] <= start).astype(F32), axis=0, keepdims=True)
        be_ref[...] = jnp.minimum(be, n_e - 1).astype(I32)

    ti = ti_ref[...]
    expert = lax.broadcasted_iota(I32, (n_e, tr), 0)
    onehots = [expert == ti[k:k + 1, :] for k in range(TOP_K)]
    cnt = functools.reduce(lambda a, b: a + b, [oh.astype(F32) for oh in onehots])
    ri = lax.broadcasted_iota(I32, (tr, tr), 0)
    ci = lax.broadcasted_iota(I32, (tr, tr), 1)
    before = jnp.dot(cnt.astype(BF16), (ri < ci).astype(BF16), preferred_element_type=F32)
    base = run_ref[:, 0:1] + before
    dest = [jnp.sum(jnp.where(oh, base, 0.0), axis=0, keepdims=True) for oh in onehots]
    dest_ref[...] = jnp.concatenate(dest, axis=0).astype(I32)
    run_ref[...] = run_ref[...] + jnp.sum(cnt, axis=1, keepdims=True)


def _route(topi_t, counts, n_blocks):
    t = topi_t.shape[1]
    n_e = counts.shape[0]
    tr = min(ROUTE_TILE, t)
    n_blocks_pad = -(-n_blocks // LANES) * LANES
    kern = functools.partial(_route_kernel, n_blocks_pad=n_blocks_pad)
    return pl.pallas_call(
        kern,
        out_shape=[jax.ShapeDtypeStruct((TOP_K, t), I32), jax.ShapeDtypeStruct((1, n_blocks_pad), I32),
                   jax.ShapeDtypeStruct((n_e, LANES), F32)],
        grid=(t // tr,),
        in_specs=[pl.BlockSpec((TOP_K, tr), lambda i: (0, i)), _const_spec((n_e, LANES))],
        out_specs=[pl.BlockSpec((TOP_K, tr), lambda i: (0, i)), _const_spec((1, n_blocks_pad)),
                   _const_spec((n_e, LANES))],
        scratch_shapes=[pltpu.VMEM((n_e, LANES), F32)],
        compiler_params=_params("arbitrary"),
        name="route",
    )(topi_t, counts)


def _dispatch_kernel(pend_ref, dest_hbm, h_ref, xs_hbm, idx_ref, zero_ref, idx_sem, row_sem, *, n_experts):
    i = pl.program_id(0)
    tg = idx_ref.shape[1]

    @pl.when(i == 0)
    def _():
        zero_ref[...] = jnp.zeros(zero_ref.shape, zero_ref.dtype)
        for e in range(n_experts):
            end = pend_ref[e]
            prev = pend_ref[e - 1] if e > 0 else 0

            @pl.when(end > prev)
            def _():
                start = pl.multiple_of((end - EXPERT_BLOCK) * SUBLANES, EXPERT_BLOCK * SUBLANES)
                cp = pltpu.make_async_copy(zero_ref, xs_hbm.at[pl.ds(start, EXPERT_BLOCK * SUBLANES)], row_sem)
                cp.start()
                cp.wait()

        n_blocks = xs_hbm.shape[0] // (EXPERT_BLOCK * SUBLANES)
        total = pend_ref[n_experts - 1]
        for b in range(n_blocks - n_experts, n_blocks):
            @pl.when(b * EXPERT_BLOCK >= total)
            def _():
                cp = pltpu.make_async_copy(
                    zero_ref, xs_hbm.at[pl.ds(b * EXPERT_BLOCK * SUBLANES, EXPERT_BLOCK * SUBLANES)], row_sem)
                cp.start()
                cp.wait()

    idx_cp = pltpu.make_async_copy(dest_hbm.at[:, pl.ds(pl.multiple_of(i * tg, tg), tg)], idx_ref, idx_sem)
    idx_cp.start()
    idx_cp.wait()

    def issue(tl, carry):
        src = h_ref.at[pl.ds(pl.multiple_of(tl * SUBLANES, SUBLANES), SUBLANES)]
        for k in range(TOP_K):
            d = pl.multiple_of(idx_ref[k, tl] * SUBLANES, SUBLANES)
            pltpu.make_async_copy(src, xs_hbm.at[pl.ds(d, SUBLANES)], row_sem).start(priority=k % 2)
        return carry

    lax.fori_loop(0, tg, issue, 0)
    for _ in range(TOP_K):
        pltpu.make_async_copy(h_ref, xs_hbm.at[pl.ds(0, tg * SUBLANES)], row_sem).wait()


def _dispatch(pend_i, dest_t, h, n_rows, n_experts):
    t = h.shape[0] // SUBLANES
    tg = min(DISPATCH_TILE, t)
    kern = functools.partial(_dispatch_kernel, n_experts=n_experts)
    return pl.pallas_call(
        kern,
        out_shape=jax.ShapeDtypeStruct((n_rows * SUBLANES, LANES), h.dtype),
        grid_spec=pltpu.PrefetchScalarGridSpec(
            num_scalar_prefetch=1,
            grid=(t // tg,),
            in_specs=[pl.BlockSpec(memory_space=pl.ANY),
                      pl.BlockSpec((tg * SUBLANES, LANES), lambda i, pend: (i, 0))],
            out_specs=pl.BlockSpec(memory_space=pl.ANY),
            scratch_shapes=[pltpu.SMEM((TOP_K, tg), I32), pltpu.VMEM((EXPERT_BLOCK * SUBLANES, LANES), h.dtype),
                            pltpu.SemaphoreType.DMA, pltpu.SemaphoreType.DMA]),
        compiler_params=pltpu.CompilerParams(dimension_semantics=("arbitrary",), has_side_effects=True,
                                             vmem_limit_bytes=VMEM_LIMIT_BYTES),
        name="dispatch",
    )(pend_i, dest_t, h)


def _expert_kernel(be_ref, nu_ref, pend_ref, xs_ref, wg_hbm, bg_ref, wu_hbm, bu_ref, wd_hbm, bd_ref, y_ref,
                   wg_f, wu_f, wd_f, wg_b, wu_b, wd_b, slot_ref, sems):
    i = pl.program_id(0)
    used = i < nu_ref[0]
    e = be_ref[i]

    def fetch(expert, slot):
        return [pltpu.make_async_copy(src.at[expert], dst.at[slot], sems.at[slot])
                for src, dst in ((wg_hbm, wg_f), (wu_hbm, wu_f), (wd_hbm, wd_f))]

    @pl.when(i == 0)
    def _():
        slot_ref[0] = 0
        for cp in fetch(e, 0):
            cp.start()

    first_of_expert = jnp.logical_or(i == 0, e != be_ref[jnp.maximum(i - 1, 0)])

    @pl.when(jnp.logical_and(used, first_of_expert))
    def _():
        slot = slot_ref[0]
        for cp in fetch(e, slot):
            cp.wait()
        wg_b[...] = wg_f[slot].astype(BF16)
        wu_b[...] = wu_f[slot].astype(BF16)
        wd_b[...] = wd_f[slot].astype(BF16)
        nxt = lax.div(pend_ref[e], EXPERT_BLOCK)

        @pl.when(nxt < nu_ref[0])
        def _():
            for cp in fetch(be_ref[nxt], 1 - slot):
                cp.start()

        slot_ref[0] = 1 - slot

    @pl.when(used)
    def _():
        x = _token_rows_load(xs_ref, EXPERT_BLOCK).astype(BF16)
        gate = jnp.minimum(jnp.dot(x, wg_b[...], preferred_element_type=F32) + bg_ref[...], SWIGLU_LIMIT)
        up = jnp.clip(jnp.dot(x, wu_b[...], preferred_element_type=F32) + bu_ref[...],
                      -SWIGLU_LIMIT, SWIGLU_LIMIT)
        glu = gate * _sigmoid(SWIGLU_ALPHA * gate)
        mid = ((up + 1.0) * glu).astype(BF16)
        y = jnp.dot(mid, wd_b[...], preferred_element_type=F32) + bd_ref[...]
        _token_rows_store(y_ref, y)

    @pl.when(jnp.logical_not(used))
    def _():
        y_ref[...] = jnp.zeros(y_ref.shape, y_ref.dtype)


def _experts(block_e, n_used, pend_i, xs, w_gate, b_gate, w_up, b_up, w_down, b_down):
    n_rows = xs.shape[0] // SUBLANES
    n_e, d, f = w_gate.shape
    nb = n_rows // EXPERT_BLOCK
    blk = (EXPERT_BLOCK * SUBLANES, LANES)
    last = lambda i, be, nu, pend: jnp.maximum(jnp.minimum(i, nu[0] - 1), 0)
    bspec = lambda n: pl.BlockSpec((None, 1, n), lambda i, be, nu, pend: (be[last(i, be, nu, pend)], 0, 0))
    hbm = pl.BlockSpec(memory_space=pl.ANY)
    return pl.pallas_call(
        _expert_kernel,
        out_shape=jax.ShapeDtypeStruct((n_rows * SUBLANES, LANES), F32),
        grid_spec=pltpu.PrefetchScalarGridSpec(
            num_scalar_prefetch=3,
            grid=(nb,),
            in_specs=[pl.BlockSpec(blk, lambda i, be, nu, pend: (last(i, be, nu, pend), 0)),
                      hbm, bspec(f), hbm, bspec(f), hbm, bspec(d)],
            out_specs=pl.BlockSpec(blk, lambda i, be, nu, pend: (i, 0)),
            scratch_shapes=[pltpu.VMEM((2, d, f), F32), pltpu.VMEM((2, d, f), F32), pltpu.VMEM((2, f, d), F32),
                            pltpu.VMEM((d, f), BF16), pltpu.VMEM((d, f), BF16), pltpu.VMEM((f, d), BF16),
                            pltpu.SMEM((1,), I32), pltpu.SemaphoreType.DMA((2,))]),
        compiler_params=_params("arbitrary"),
        name="experts",
    )(block_e, n_used, pend_i, xs, w_gate, b_gate.reshape(n_e, 1, f), w_up, b_up.reshape(n_e, 1, f),
      w_down, b_down.reshape(n_e, 1, d))


def _combine_kernel(dest_hbm, ys_hbm, tw_ref, x1_ref, gt_ref, g_ref, o_ref, idx_ref, buf_ref, idx_sem, row_sem):
    i = pl.program_id(0)
    n = pl.num_programs(0)
    tc = idx_ref.shape[1]
    slot = lax.rem(i, 2)

    def idx_copy(tile, s):
        return pltpu.make_async_copy(dest_hbm.at[:, pl.ds(pl.multiple_of(tile * tc, tc), tc)],
                                     idx_ref.at[pl.ds(pl.multiple_of(s * TOP_K, TOP_K), TOP_K)], idx_sem.at[s])

    def issue_rows(s):
        def issue(tl, carry):
            dst_row = pl.multiple_of(tl * SUBLANES, SUBLANES)
            for k in range(TOP_K):
                d = pl.multiple_of(idx_ref[s * TOP_K + k, tl] * SUBLANES, SUBLANES)
                pltpu.make_async_copy(ys_hbm.at[pl.ds(d, SUBLANES)], buf_ref.at[s, k, pl.ds(dst_row, SUBLANES)],
                                      row_sem.at[s]).start(priority=k % 2)
            return carry

        lax.fori_loop(0, tc, issue, 0)

    @pl.when(i == 0)
    def _():
        idx_copy(0, 0).start()
        idx_copy(0, 0).wait()
        issue_rows(0)

        @pl.when(n > 1)
        def _():
            idx_copy(1, 1).start()

    @pl.when(i + 1 < n)
    def _():
        idx_copy(i + 1, 1 - slot).wait()
        issue_rows(1 - slot)

        @pl.when(i + 2 < n)
        def _():
            idx_copy(i + 2, slot).start()

    for k in range(TOP_K):
        pltpu.make_async_copy(ys_hbm.at[pl.ds(0, tc * SUBLANES)], buf_ref.at[slot, k], row_sem.at[slot]).wait()

    tw = tw_ref[...]
    ffn = tw[:, 0:1] * _token_rows_load(buf_ref.at[slot, 0], tc)
    for k in range(1, TOP_K):
        ffn = ffn + tw[:, k:k + 1] * _token_rows_load(buf_ref.at[slot, k], tc)
    x2 = x1_ref[...] + gt_ref[0] * ffn
    o_ref[...] = _rms(x2) * g_ref[...]


def _combine(dest_t, ys, topw, x1, gt, g, seq):
    t, d = x1.shape
    tc = min(COMBINE_TILE, seq)
    per_batch = seq // tc
    row = lambda i: (i, 0)
    return pl.pallas_call(
        _combine_kernel,
        out_shape=jax.ShapeDtypeStruct((t, d), F32),
        grid=(t // tc,),
        in_specs=[pl.BlockSpec(memory_space=pl.ANY), pl.BlockSpec(memory_space=pl.ANY),
                  pl.BlockSpec((tc, TOP_K), row), pl.BlockSpec((tc, d), row),
                  pl.BlockSpec((1, 1, d), lambda i: (i // per_batch, 0, 0)), _const_spec((1, d))],
        out_specs=pl.BlockSpec((tc, d), row),
        scratch_shapes=[pltpu.SMEM((2 * TOP_K, tc), I32), pltpu.VMEM((2, TOP_K, tc * SUBLANES, LANES), F32),
                        pltpu.SemaphoreType.DMA((2,)), pltpu.SemaphoreType.DMA((2,))],
        compiler_params=_params("arbitrary"),
        name="combine",
    )(dest_t, ys, topw, x1, gt, g)


def _layer(x2, mod, batch, seq, norm1_g, w_in, conv_w, conv_b, dt_bias, a_log, d_skip, ssm_norm_g,
           gla_wg2, gla_bg, gla_norm_g, w_out, norm2_g, w_router, b_router,
           w_gate, b_gate, w_up, b_up, w_down, b_down):
    t, d = x2.shape
    n_heads = dt_bias.shape[0]
    d_ssm = n_heads * SSM_HEAD_DIM
    cd = conv_w.shape[1]
    dkt = gla_wg2.shape[1]
    dvt = w_out.shape[0] - d_ssm
    n_experts = w_router.shape[1]

    sh1, sc1, gt1, sh2, sc2, gt2 = [m.reshape(batch, 1, d) for m in jnp.split(mod[:batch], 6, axis=1)]

    sizes = (d_ssm, cd, n_heads, dkt, dkt, dvt, GLA_GATE_RANK, dvt)
    offs = [0]
    for s in sizes:
        offs.append(offs[-1] + s)
    piece = lambda j: (offs[j], sizes[j])
    sections = ((piece(0),), (piece(1),), (piece(3),), (piece(4),), (piece(5),), (piece(7),), (piece(2), piece(6)))
    z, xbc, q, k, v, r, small = _inproj(x2, sc1, sh1, norm1_g.reshape(1, d), w_in.astype(BF16), sections,
                                        (d_ssm, cd, dkt, dkt, dvt, dvt, LANES), [BF16] * 6 + [F32], seq)

    d_skip_e = jnp.repeat(d_skip, SSM_HEAD_DIM).reshape(1, d_ssm)
    y = _ssd(xbc, z, small, conv_w, conv_b, dt_bias, a_log, d_skip_e, ssm_norm_g, batch, seq)
    o = _gla(q, k, v, r, small, gla_wg2, gla_bg, gla_norm_g, batch, seq, gate_col=n_heads)

    x1, h2, topi_t, topw_t, counts = _outproj(y, o, x2, gt1, sc2, sh2, norm2_g.reshape(1, d),
                                        w_out[:d_ssm].astype(BF16), w_out[d_ssm:].astype(BF16),
                                        w_router, b_router, seq)

    n_blocks = (t * TOP_K) // EXPERT_BLOCK + n_experts
    dest_t, block_e, pend = _route(topi_t, counts, n_blocks)
    pend_i = pend[:, 0].astype(I32)
    n_used = (pend_i[n_experts - 1:] // EXPERT_BLOCK).astype(I32)
    topw = topw_t.T
    xs = _dispatch(pend_i, dest_t, h2, n_blocks * EXPERT_BLOCK, n_experts)
    ys = _experts(block_e[0, :n_blocks], n_used, pend_i, xs, w_gate, b_gate, w_up, b_up, w_down, b_down)
    return dest_t, ys, topw, x1, gt2


def kernel(x, c, w_ada, b_ada, norm1_g, w_in, conv_w, conv_b, dt_bias, a_log, d_skip, ssm_norm_g, gla_wg2,
           gla_bg, gla_norm_g, w_out, norm2_g, w_router, b_router, w_gate, b_gate, w_up, b_up, w_down, b_down,
           final_norm_g):
    batch, seq, d = x.shape
    assert w_ada.shape[0] == 1, "single-layer trunk"
    assert d == SUBLANES * LANES, "token rows are moved as one (8, 128) f32 tile each"
    assert seq % min(seq, max(TOKEN_TILE, SSM_CHUNK, GLA_STEP_ROWS, COMBINE_TILE, DISPATCH_TILE)) == 0
    assert seq % max(SSM_CHUNK, GLA_STEP_ROWS) == 0
    x2 = x.reshape(batch * seq, d)
    c_pad = jnp.zeros((SUBLANES, d), F32).at[:batch].set(c)
    mod = _ada(c_pad, w_ada[0], b_ada)
    dest_t, ys, topw, x1, gt2 = _layer(
        x2, mod, batch, seq, norm1_g[0], w_in[0], conv_w[0], conv_b[0], dt_bias[0], a_log[0], d_skip[0],
        ssm_norm_g[0], gla_wg2[0], gla_bg[0], gla_norm_g[0], w_out[0], norm2_g[0], w_router[0], b_router[0],
        w_gate[0], b_gate[0], w_up[0], b_up[0], w_down[0], b_down[0])
    out = _combine(dest_t, ys, topw, x1, gt2, final_norm_g.reshape(1, d), seq)
    return out.reshape(batch, seq, d)
```

```python
import functools

import jax
import jax.numpy as jnp
from jax import lax
from jax.experimental import pallas as pl
from jax.experimental.pallas import tpu as pltpu

F32 = jnp.float32
BF16 = jnp.bfloat16
I32 = jnp.int32
HIGHEST = lax.Precision.HIGHEST

EPS = 1e-6
SSM_HEAD_DIM = 64
SSM_GROUPS = 2
SSM_STATE = 128
SSM_CONV = 4
SSM_CHUNK = 128
GLA_HEADS = 4
GLA_GATE_RANK = 16
GLA_GATE_NORM = 16.0
GLA_CHUNK = 64
TOP_K = 4
SWIGLU_LIMIT = 7.0
SWIGLU_ALPHA = 1.702

LANES = 128
SUBLANES = 8
VMEM_LIMIT_BYTES = 56 * 1024 * 1024

TOKEN_TILE = 512
SSD_STEP_ROWS = 512
CONV_CARRY = 16
GLA_STEP_ROWS = 512
GLA_GROUP_ROWS = 256
ROUTE_TILE = 512
EXPERT_BLOCK = 512
DISPATCH_TILE = 512
COMBINE_TILE = 256

_NT = (((1,), (1,)), ((), ()))
_TN = (((0,), (0,)), ((), ()))


def _sigmoid(v):
    return 0.5 * jnp.tanh(0.5 * v) + 0.5


def _silu(v):
    return v * _sigmoid(v)


def _softplus(v):
    return jnp.maximum(v, 0.0) + jnp.log1p(jnp.exp(-jnp.abs(v)))


def _log_sigmoid(v):
    return jnp.minimum(v, 0.0) - jnp.log(1.0 + jnp.exp(-jnp.abs(v)))


def _rms(v):
    return v * lax.rsqrt(jnp.mean(v * v, axis=-1, keepdims=True) + EPS)


def _pieces(a, n):
    out = []
    for _ in range(n - 1):
        p = a.astype(BF16)
        out.append(p)
        a = a - p.astype(F32)
    out.append(a.astype(BF16))
    return out


def _dot_pieces(a, b, n, dims=None):
    dims = dims or (((a.ndim - 1,), (0,)), ((), ()))
    return sum(lax.dot_general(p, b, dims, preferred_element_type=F32) for p in _pieces(a, n))


def _token_rows_load(ref, rows):
    return jnp.concatenate([ref[pl.ds(s, rows, stride=SUBLANES), :] for s in range(SUBLANES)], axis=1)


def _token_rows_store(ref, v):
    rows = v.shape[0]
    for s in range(SUBLANES):
        ref[pl.ds(s, rows, stride=SUBLANES), :] = v[:, s * LANES:(s + 1) * LANES]


def _params(*semantics):
    return pltpu.CompilerParams(dimension_semantics=semantics, vmem_limit_bytes=VMEM_LIMIT_BYTES)


def _const_spec(shape):
    nd = len(shape)
    return pl.BlockSpec(shape, lambda *_: (0,) * nd)


def _ada_kernel(c_ref, w_ref, b_ref, o_ref):
    cond = _silu(c_ref[...])
    o_ref[...] = jnp.dot(cond, w_ref[...], precision=HIGHEST, preferred_element_type=F32) + b_ref[...]


def _ada(c_pad, w_ada, b_ada):
    rows, d = c_pad.shape
    n = w_ada.shape[1]
    tn = d
    return pl.pallas_call(
        _ada_kernel,
        out_shape=jax.ShapeDtypeStruct((rows, n), F32),
        grid=(n // tn,),
        in_specs=[pl.BlockSpec((rows, d), lambda j: (0, 0)),
                  pl.BlockSpec((d, tn), lambda j: (0, j)),
                  pl.BlockSpec((1, tn), lambda j: (0, j))],
        out_specs=pl.BlockSpec((rows, tn), lambda j: (0, j)),
        compiler_params=_params("arbitrary"),
        name="ada",
    )(c_pad, w_ada, b_ada)


def _inproj_kernel(x_ref, sc_ref, sh_ref, g_ref, w_ref, *refs, sections):
    o_refs, ws_ref = refs[:-1], refs[-1]

    @pl.when(pl.program_id(0) == 0)
    def _():
        off = 0
        for o_ref, pieces in zip(o_refs, sections):
            n = o_ref.shape[1]
            used = 0
            for src, width in pieces:
                ws_ref[:, off + used:off + used + width] = w_ref[:, src:src + width]
                used += width
            if used < n:
                ws_ref[:, off + used:off + n] = jnp.zeros((ws_ref.shape[0], n - used), ws_ref.dtype)
            off += n

    h = (_rms(x_ref[...]) * g_ref[...]) * (1.0 + sc_ref[0]) + sh_ref[0]
    hb = h.astype(BF16)
    off = 0
    for o_ref in o_refs:
        n = o_ref.shape[1]
        o_ref[...] = jnp.dot(hb, ws_ref[:, off:off + n], preferred_element_type=F32).astype(o_ref.dtype)
        off += n


def _inproj(x2, sc, sh, g, w_bf, sections, widths, out_dtypes, seq):
    t, d = x2.shape
    tm = min(TOKEN_TILE, seq)
    per_batch = seq // tm
    assert all(n % LANES == 0 for n in widths)
    mod_spec = pl.BlockSpec((1, 1, d), lambda i: (i // per_batch, 0, 0))
    kern = functools.partial(_inproj_kernel, sections=sections)
    return pl.pallas_call(
        kern,
        out_shape=[jax.ShapeDtypeStruct((t, n), dt) for n, dt in zip(widths, out_dtypes)],
        grid=(t // tm,),
        in_specs=[pl.BlockSpec((tm, d), lambda i: (i, 0)), mod_spec, mod_spec, _const_spec((1, d)),
                  pl.BlockSpec(w_bf.shape, lambda i: (0, 0), pipeline_mode=pl.Buffered(1))],
        out_specs=[pl.BlockSpec((tm, n), lambda i: (i, 0)) for n in widths],
        scratch_shapes=[pltpu.VMEM((d, sum(widths)), BF16)],
        compiler_params=_params("arbitrary"),
        name="inproj",
    )(x2, sc, sh, g, w_bf)


def _ssd_kernel(tail_ref, xbc_ref, z_ref, sm_ref, cw_ref, cb_ref, dtb_r_ref, dtb_c_ref, alog_r_ref, alog_c_ref,
                dsk_ref, g_ref, y_ref, st_ref, *, n_heads, d_ssm):
    L = SSM_CHUNK
    P = SSM_HEAD_DIM
    N = SSM_STATE
    G = SSM_GROUPS
    gw = d_ssm // G
    R = xbc_ref.shape[0]
    c = pl.program_id(1)

    @pl.when(c == 0)
    def _():
        st_ref[...] = jnp.zeros(st_ref.shape, F32)

    tail = tail_ref[...]
    tail = jnp.where(c == 0, jnp.zeros_like(tail), tail)
    conv_in = jnp.concatenate([tail, xbc_ref[...]], axis=0)

    ri = lax.broadcasted_iota(I32, (L, L), 0)
    ci = lax.broadcasted_iota(I32, (L, L), 1)
    causal = ri >= ci
    causal3 = jnp.concatenate([causal.astype(BF16)] * 3, axis=1)
    upper3 = jnp.concatenate([(ri <= ci).astype(BF16)] * 3, axis=0)
    wr = lax.broadcasted_iota(I32, ((SSM_CONV - 1) * L, CONV_CARRY + L), 0)
    wc = lax.broadcasted_iota(I32, ((SSM_CONV - 1) * L, CONV_CARRY + L), 1)
    shifts = (wc == (wr % L) + CONV_CARRY - (SSM_CONV - 1) + wr // L).astype(BF16)
    hh = lax.broadcasted_iota(I32, (n_heads, d_ssm), 0)
    jj = lax.broadcasted_iota(I32, (n_heads, d_ssm), 1)
    expand = ((jj // P) == hh).astype(BF16)
    expand2 = jnp.concatenate([expand, expand], axis=0)
    lane = lax.broadcasted_iota(I32, (L, LANES), 1)
    first_half = lane < P
    heads_per_group = n_heads // G
    a_row = -jnp.exp(alog_r_ref[...])
    a_col = -jnp.exp(alog_c_ref[...])

    for s in range(R // L):
        rows = slice(s * L, (s + 1) * L)
        window = conv_in[s * L:s * L + CONV_CARRY + L, :]
        taps = jnp.dot(shifts, window, preferred_element_type=F32)
        acc = cb_ref[...] + cw_ref[SSM_CONV - 1:SSM_CONV, :] * window[CONV_CARRY:, :].astype(F32)
        for j in range(SSM_CONV - 1):
            acc = acc + cw_ref[j:j + 1, :] * taps[j * L:(j + 1) * L, :]
        act = _silu(acc)
        xs = act[:, :d_ssm]
        bm = act[:, d_ssm:d_ssm + G * N].astype(BF16)
        cm = act[:, d_ssm + G * N:].astype(BF16)

        sm = sm_ref[rows, :]
        dt_col = _softplus(sm[:, :n_heads] + dtb_r_ref[...])
        dt_row = _softplus(sm.T[:n_heads, :] + dtb_c_ref[...])
        cs_col = jnp.dot(causal3, jnp.concatenate(_pieces(dt_col * a_row, 3), axis=0),
                         preferred_element_type=F32)
        cs_row = jnp.dot(jnp.concatenate(_pieces(dt_row * a_col, 3), axis=1), upper3,
                         preferred_element_type=F32)
        cs_last = cs_col[L - 1:L, :]

        per_head = jnp.concatenate([dt_col, jnp.exp(cs_col), jnp.exp(cs_last - cs_col)], axis=0)
        per_ch = jnp.dot(jnp.concatenate(_pieces(per_head, 2), axis=1), expand2,
                         preferred_element_type=F32)
        dt_e, ecs_e, dte_e = per_ch[0:L], per_ch[L:2 * L], per_ch[2 * L:3 * L]
        chunk_decay = ecs_e[L - 1:L, :]

        xdt = xs * dt_e
        xdt_b = xdt.astype(BF16)
        xdec_b = (xdt * dte_e).astype(BF16)

        y_parts = []
        y_off_parts = []
        for g in range(G):
            bm_g = bm[:, g * N:(g + 1) * N]
            cm_g = cm[:, g * N:(g + 1) * N]
            cb = lax.dot_general(cm_g, bm_g, _NT, preferred_element_type=F32)
            prev = st_ref[g]
            y_off_parts.append(jnp.dot(cm_g, prev.astype(BF16), preferred_element_type=F32))
            s_new = lax.dot_general(bm_g, xdec_b[:, g * gw:(g + 1) * gw], _TN, preferred_element_type=F32)
            st_ref[g] = prev * chunk_decay[:, g * gw:(g + 1) * gw] + s_new
            for p in range(heads_per_group // 2):
                h0 = g * heads_per_group + 2 * p
                ms = []
                for h in (h0, h0 + 1):
                    diff = cs_col[:, h:h + 1] - cs_row[h:h + 1, :]
                    ms.append((cb * jnp.exp(jnp.where(causal, diff, -jnp.inf))).astype(BF16))
                lhs = jnp.concatenate(ms, axis=1)
                xp = xdt_b[:, h0 * P:(h0 + 2) * P]
                zero = jnp.zeros_like(xp)
                rhs = jnp.concatenate([jnp.where(first_half, xp, zero), jnp.where(first_half, zero, xp)], axis=0)
                y_parts.append(jnp.dot(lhs, rhs, preferred_element_type=F32))
        y = jnp.concatenate(y_parts, axis=1) + jnp.concatenate(y_off_parts, axis=1) * ecs_e + dsk_ref[...] * xs
        y = y * _silu(z_ref[rows, :].astype(F32))
        y = jnp.concatenate([_rms(y[:, g * gw:(g + 1) * gw]) for g in range(G)], axis=1) * g_ref[...]
        y_ref[rows, :] = y.astype(y_ref.dtype)


def _ssd(xbc, z, small, conv_w, conv_b, dt_bias, a_log, d_skip_e, norm_g, batch, seq):
    t, cd = xbc.shape
    d_ssm = z.shape[1]
    n_heads = dt_bias.shape[0]
    L = min(SSD_STEP_ROWS, seq)
    nc = seq // L
    row = lambda b, c: (b * nc + c, 0)
    tail = lambda b, c: (jnp.maximum((b * nc + c) * (L // CONV_CARRY) - 1, 0), 0)
    kern = functools.partial(_ssd_kernel, n_heads=n_heads, d_ssm=d_ssm)
    return pl.pallas_call(
        kern,
        out_shape=jax.ShapeDtypeStruct((t, d_ssm), BF16),
        grid=(batch, nc),
        in_specs=[pl.BlockSpec((CONV_CARRY, cd), tail),
                  pl.BlockSpec((L, cd), row), pl.BlockSpec((L, d_ssm), row), pl.BlockSpec((L, LANES), row),
                  _const_spec(conv_w.shape), _const_spec((1, cd)),
                  _const_spec((1, n_heads)), _const_spec((n_heads, 1)),
                  _const_spec((1, n_heads)), _const_spec((n_heads, 1)),
                  _const_spec((1, d_ssm)), _const_spec((1, d_ssm))],
        out_specs=pl.BlockSpec((L, d_ssm), row),
        scratch_shapes=[pltpu.VMEM((SSM_GROUPS, SSM_STATE, d_ssm // SSM_GROUPS), F32)],
        compiler_params=_params("arbitrary", "arbitrary"),
        name="ssd",
    )(xbc, xbc, z, small, conv_w, conv_b.reshape(1, cd), dt_bias.reshape(1, n_heads), dt_bias.reshape(n_heads, 1),
      a_log.reshape(1, n_heads), a_log.reshape(n_heads, 1), d_skip_e, norm_g.reshape(1, d_ssm))


def _gla_kernel(q_ref, k_ref, v_ref, r_ref, sm_ref, wg2_ref, bg_ref, gn_ref, o_ref, st_ref, *, gate_col):
    L = GLA_CHUNK
    H = GLA_HEADS
    dk = q_ref.shape[1] // H
    dv = v_ref.shape[1] // H
    c = pl.program_id(1)

    @pl.when(c == 0)
    def _():
        st_ref[...] = jnp.zeros(st_ref.shape, F32)

    R = min(GLA_GROUP_ROWS, q_ref.shape[0])
    n_chunks = R // L
    ri = lax.broadcasted_iota(I32, (R, R), 0)
    ci = lax.broadcasted_iota(I32, (R, R), 1)
    same_chunk = (ri // L) == (ci // L)
    causal = jnp.logical_and(same_chunk, ri >= ci)
    tril = causal.astype(BF16)
    later = jnp.logical_and(same_chunk, ri < ci).astype(BF16)
    sr = lax.broadcasted_iota(I32, (R, n_chunks * LANES), 0)
    sc = lax.broadcasted_iota(I32, (R, n_chunks * LANES), 1)
    last_rows = (sr == (sc // LANES) * L + (L - 1)).astype(BF16)
    wg2_hi, wg2_lo = _pieces(wg2_ref[...], 2)

    tril3 = jnp.concatenate([jnp.concatenate([tril] * 3, axis=1), jnp.concatenate([later] * 3, axis=1)], axis=0)
    last3 = jnp.concatenate([last_rows] * 3, axis=0)
    wg3 = jnp.concatenate([wg2_hi, wg2_hi, wg2_lo], axis=0)

    for gi in range(q_ref.shape[0] // R):
        rs = slice(gi * R, (gi + 1) * R)
        q = q_ref[rs, :].astype(F32) * (dk ** -0.5)
        k = k_ref[rs, :].astype(F32)
        v = v_ref[rs, :]
        r = r_ref[rs, :].astype(F32)
        g_hi, g_lo = _pieces(sm_ref[rs, gate_col:gate_col + GLA_GATE_RANK], 2)
        pre = jnp.dot(jnp.concatenate([g_hi, g_lo, g_hi], axis=1), wg3, preferred_element_type=F32) + bg_ref[...]
        gk3 = jnp.concatenate(_pieces(_log_sigmoid(pre) / GLA_GATE_NORM, 3), axis=0)
        sums = jnp.dot(tril3, gk3, preferred_element_type=F32)
        bcum = sums[:R]
        to_end = sums[R:]
        q_t = (q * jnp.exp(bcum)).astype(BF16)
        k_t = (k * jnp.exp(-bcum)).astype(BF16)
        k_dec = (k * jnp.exp(to_end)).astype(BF16)
        dcol = jnp.exp(lax.dot_general(jnp.concatenate(_pieces(bcum, 3), axis=0), last3, _TN,
                                       preferred_element_type=F32))
        outs = []
        for h in range(H):
            ks = slice(h * dk, (h + 1) * dk)
            vs = slice(h * dv, (h + 1) * dv)
            att = lax.dot_general(q_t[:, ks], k_t[:, ks], _NT, preferred_element_type=F32)
            att = jnp.where(causal, att, 0.0).astype(BF16)
            o = jnp.dot(att, v[:, vs], preferred_element_type=F32)
            state = st_ref[h]
            inter = []
            for c in range(n_chunks):
                rows = slice(c * L, (c + 1) * L)
                inter.append(jnp.dot(q_t[rows, ks], state.astype(BF16), preferred_element_type=F32))
                s_new = lax.dot_general(k_dec[rows, ks], v[rows, vs], _TN, preferred_element_type=F32)
                dec = dcol[ks, c * LANES:(c + 1) * LANES]
                state = state * jnp.concatenate([dec] * (dv // LANES), axis=1) + s_new
            st_ref[h] = state
            o = o + jnp.concatenate(inter, axis=0)
            outs.append(_rms(o) * gn_ref[...] * _silu(r[:, vs]))
        o_ref[rs, :] = jnp.concatenate(outs, axis=1).astype(o_ref.dtype)


def _gla(q, k, v, r, small, wg2, bg, norm_g, batch, seq, gate_col):
    t, dkt = q.shape
    dvt = v.shape[1]
    rows = min(GLA_STEP_ROWS, seq)
    nc = seq // rows
    row = lambda b, c: (b * nc + c, 0)
    kern = functools.partial(_gla_kernel, gate_col=gate_col)
    return pl.pallas_call(
        kern,
        out_shape=jax.ShapeDtypeStruct((t, dvt), BF16),
        grid=(batch, nc),
        in_specs=[pl.BlockSpec((rows, dkt), row), pl.BlockSpec((rows, dkt), row), pl.BlockSpec((rows, dvt), row),
                  pl.BlockSpec((rows, dvt), row), pl.BlockSpec((rows, LANES), row),
                  _const_spec(wg2.shape), _const_spec((1, dkt)), _const_spec((1, dvt // GLA_HEADS))],
        out_specs=pl.BlockSpec((rows, dvt), row),
        scratch_shapes=[pltpu.VMEM((GLA_HEADS, dkt // GLA_HEADS, dvt // GLA_HEADS), F32)],
        compiler_params=_params("arbitrary", "arbitrary"),
        name="gla",
    )(q, k, v, r, small, wg2, bg.reshape(1, dkt), norm_g.reshape(1, dvt // GLA_HEADS))


def _outproj_kernel(y_ref, o_ref, x_ref, gt_ref, sc_ref, sh_ref, g_ref, wy_ref, wo_ref, wr_ref, br_ref,
                    x1_ref, h_ref, ti_ref, tw_ref, cnt_ref):
    mix = (jnp.dot(y_ref[...], wy_ref[...], preferred_element_type=F32)
           + jnp.dot(o_ref[...], wo_ref[...], preferred_element_type=F32))
    x1 = x_ref[...] + gt_ref[0] * mix
    x1_ref[...] = x1
    h = (_rms(x1) * g_ref[...]) * (1.0 + sc_ref[0]) + sh_ref[0]
    _token_rows_store(h_ref, h)
    n_e = br_ref.shape[0]
    h_hi, h_lo = _pieces(h, 2)
    wr = wr_ref[...]
    hw = lax.dot_general(wr, h_hi, _NT, preferred_element_type=F32)
    logits = (hw[:n_e] + hw[n_e:] + lax.dot_general(wr[:n_e], h_lo, _NT, preferred_element_type=F32)) + br_ref[...]
    expert = lax.broadcasted_iota(I32, logits.shape, 0)
    vals, idxs = [], []
    counts = jnp.zeros(logits.shape, F32)
    for _ in range(TOP_K):
        m = jnp.max(logits, axis=0, keepdims=True)
        idx = jnp.min(jnp.where(logits == m, expert, n_e), axis=0, keepdims=True)
        vals.append(m)
        idxs.append(idx)
        chosen = expert == idx
        counts = counts + chosen.astype(F32)
        logits = jnp.where(chosen, -jnp.inf, logits)
    exps = [jnp.exp(v - vals[0]) for v in vals]
    denom = functools.reduce(lambda a, b: a + b, exps)
    ti_ref[...] = jnp.concatenate(idxs, axis=0)
    tw_ref[...] = jnp.concatenate([e / denom for e in exps], axis=0)

    @pl.when(pl.program_id(0) == 0)
    def _():
        cnt_ref[...] = jnp.zeros(cnt_ref.shape, F32)

    cnt_ref[...] = cnt_ref[...] + jnp.sum(counts, axis=1, keepdims=True)


def _outproj(y, o, x2, gt, sc, sh, g, wy, wo, w_router, b_router, seq):
    t, d = x2.shape
    n_e = w_router.shape[1]
    tm = min(TOKEN_TILE, seq)
    per_batch = seq // tm
    row = lambda i: (i, 0)
    mod_spec = pl.BlockSpec((1, 1, d), lambda i: (i // per_batch, 0, 0))
    wr_t = w_router.T
    wr_hi = wr_t.astype(BF16)
    wr_cat = jnp.concatenate([wr_hi, (wr_t - wr_hi.astype(F32)).astype(BF16)], axis=0)
    col = lambda i: (0, i)
    return pl.pallas_call(
        _outproj_kernel,
        out_shape=[jax.ShapeDtypeStruct((t, d), F32), jax.ShapeDtypeStruct((t * SUBLANES, LANES), F32),
                   jax.ShapeDtypeStruct((TOP_K, t), I32), jax.ShapeDtypeStruct((TOP_K, t), F32),
                   jax.ShapeDtypeStruct((n_e, LANES), F32)],
        grid=(t // tm,),
        in_specs=[pl.BlockSpec((tm, y.shape[1]), row), pl.BlockSpec((tm, o.shape[1]), row),
                  pl.BlockSpec((tm, d), row), mod_spec, mod_spec, mod_spec, _const_spec((1, d)),
                  _const_spec(wy.shape), _const_spec(wo.shape), _const_spec(wr_cat.shape),
                  _const_spec((n_e, 1))],
        out_specs=[pl.BlockSpec((tm, d), row), pl.BlockSpec((tm * SUBLANES, LANES), row),
                   pl.BlockSpec((TOP_K, tm), col), pl.BlockSpec((TOP_K, tm), col),
                   _const_spec((n_e, LANES))],
        compiler_params=_params("arbitrary"),
        name="outproj",
    )(y, o, x2, gt, sc, sh, g, wy, wo, wr_cat, b_router.reshape(n_e, 1))


def _route_kernel(ti_ref, cnt_ref, dest_ref, be_ref, pend_ref, run_ref, *, n_blocks_pad):
    i = pl.program_id(0)
    n_e = cnt_ref.shape[0]
    tr = ti_ref.shape[1]

    @pl.when(i == 0)
    def _():
        counts = cnt_ref[...]
        padded = jnp.ceil(counts / EXPERT_BLOCK) * EXPERT_BLOCK
        ri = lax.broadcasted_iota(I32, (n_e, n_e), 0)
        ci = lax.broadcasted_iota(I32, (n_e, n_e), 1)
        pend = jnp.dot((ri >= ci).astype(F32), padded, precision=HIGHEST, preferred_element_type=F32)
        pend_ref[...] = pend
        run_ref[...] = pend - padded
        start = (lax.broadcasted_iota(I32, (n_e, n_blocks_pad), 1) * EXPERT_BLOCK).astype(F32)
        be = jnp.sum((pend[:, 0:1] <= start).astype(F32), axis=0, keepdims=True)
        be_ref[...] = jnp.minimum(be, n_e - 1).astype(I32)

    ti = ti_ref[...]
    expert = lax.broadcasted_iota(I32, (n_e, tr), 0)
    onehots = [expert == ti[k:k + 1, :] for k in range(TOP_K)]
    cnt = functools.reduce(lambda a, b: a + b, [oh.astype(F32) for oh in onehots])
    ri = lax.broadcasted_iota(I32, (tr, tr), 0)
    ci = lax.broadcasted_iota(I32, (tr, tr), 1)
    before = jnp.dot(cnt.astype(BF16), (ri < ci).astype(BF16), preferred_element_type=F32)
    base = run_ref[:, 0:1] + before
    dest = [jnp.sum(jnp.where(oh, base, 0.0), axis=0, keepdims=True) for oh in onehots]
    dest_ref[...] = jnp.concatenate(dest, axis=0).astype(I32)
    run_ref[...] = run_ref[...] + jnp.sum(cnt, axis=1, keepdims=True)


def _route(topi_t, counts, n_blocks):
    t = topi_t.shape[1]
    n_e = counts.shape[0]
    tr = min(ROUTE_TILE, t)
    n_blocks_pad = -(-n_blocks // LANES) * LANES
    kern = functools.partial(_route_kernel, n_blocks_pad=n_blocks_pad)
    return pl.pallas_call(
        kern,
        out_shape=[jax.ShapeDtypeStruct((TOP_K, t), I32), jax.ShapeDtypeStruct((1, n_blocks_pad), I32),
                   jax.ShapeDtypeStruct((n_e, LANES), F32)],
        grid=(t // tr,),
        in_specs=[pl.BlockSpec((TOP_K, tr), lambda i: (0, i)), _const_spec((n_e, LANES))],
        out_specs=[pl.BlockSpec((TOP_K, tr), lambda i: (0, i)), _const_spec((1, n_blocks_pad)),
                   _const_spec((n_e, LANES))],
        scratch_shapes=[pltpu.VMEM((n_e, LANES), F32)],
        compiler_params=_params("arbitrary"),
        name="route",
    )(topi_t, counts)


def _dispatch_kernel(pend_ref, dest_hbm, h_ref, xs_hbm, idx_ref, zero_ref, idx_sem, row_sem, *, n_experts):
    i = pl.program_id(0)
    tg = idx_ref.shape[1]

    @pl.when(i == 0)
    def _():
        zero_ref[...] = jnp.zeros(zero_ref.shape, zero_ref.dtype)
        for e in range(n_experts):
            end = pend_ref[e]
            prev = pend_ref[e - 1] if e > 0 else 0

            @pl.when(end > prev)
            def _():
                start = pl.multiple_of((end - EXPERT_BLOCK) * SUBLANES, EXPERT_BLOCK * SUBLANES)
                cp = pltpu.make_async_copy(zero_ref, xs_hbm.at[pl.ds(start, EXPERT_BLOCK * SUBLANES)], row_sem)
                cp.start()
                cp.wait()

        n_blocks = xs_hbm.shape[0] // (EXPERT_BLOCK * SUBLANES)
        total = pend_ref[n_experts - 1]
        for b in range(n_blocks - n_experts, n_blocks):
            @pl.when(b * EXPERT_BLOCK >= total)
            def _():
                cp = pltpu.make_async_copy(
                    zero_ref, xs_hbm.at[pl.ds(b * EXPERT_BLOCK * SUBLANES, EXPERT_BLOCK * SUBLANES)], row_sem)
                cp.start()
                cp.wait()

    idx_cp = pltpu.make_async_copy(dest_hbm.at[:, pl.ds(pl.multiple_of(i * tg, tg), tg)], idx_ref, idx_sem)
    idx_cp.start()
    idx_cp.wait()

    def issue(tl, carry):
        src = h_ref.at[pl.ds(pl.multiple_of(tl * SUBLANES, SUBLANES), SUBLANES)]
        for k in range(TOP_K):
            d = pl.multiple_of(idx_ref[k, tl] * SUBLANES, SUBLANES)
            pltpu.make_async_copy(src, xs_hbm.at[pl.ds(d, SUBLANES)], row_sem).start(priority=k % 2)
        return carry

    lax.fori_loop(0, tg, issue, 0)
    for _ in range(TOP_K):
        pltpu.make_async_copy(h_ref, xs_hbm.at[pl.ds(0, tg * SUBLANES)], row_sem).wait()


def _dispatch(pend_i, dest_t, h, n_rows, n_experts):
    t = h.shape[0] // SUBLANES
    tg = min(DISPATCH_TILE, t)
    kern = functools.partial(_dispatch_kernel, n_experts=n_experts)
    return pl.pallas_call(
        kern,
        out_shape=jax.ShapeDtypeStruct((n_rows * SUBLANES, LANES), h.dtype),
        grid_spec=pltpu.PrefetchScalarGridSpec(
            num_scalar_prefetch=1,
            grid=(t // tg,),
            in_specs=[pl.BlockSpec(memory_space=pl.ANY),
                      pl.BlockSpec((tg * SUBLANES, LANES), lambda i, pend: (i, 0))],
            out_specs=pl.BlockSpec(memory_space=pl.ANY),
            scratch_shapes=[pltpu.SMEM((TOP_K, tg), I32), pltpu.VMEM((EXPERT_BLOCK * SUBLANES, LANES), h.dtype),
                            pltpu.SemaphoreType.DMA, pltpu.SemaphoreType.DMA]),
        compiler_params=pltpu.CompilerParams(dimension_semantics=("arbitrary",), has_side_effects=True,
                                             vmem_limit_bytes=VMEM_LIMIT_BYTES),
        name="dispatch",
    )(pend_i, dest_t, h)


def _expert_kernel(be_ref, nu_ref, pend_ref, xs_ref, wg_hbm, bg_ref, wu_hbm, bu_ref, wd_hbm, bd_ref, y_ref,
                   wg_f, wu_f, wd_f, wg_b, wu_b, wd_b, slot_ref, sems):
    i = pl.program_id(0)
    used = i < nu_ref[0]
    e = be_ref[i]

    def fetch(expert, slot):
        return [pltpu.make_async_copy(src.at[expert], dst.at[slot], sems.at[slot])
                for src, dst in ((wg_hbm, wg_f), (wu_hbm, wu_f), (wd_hbm, wd_f))]

    @pl.when(i == 0)
    def _():
        slot_ref[0] = 0
        for cp in fetch(e, 0):
            cp.start()

    first_of_expert = jnp.logical_or(i == 0, e != be_ref[jnp.maximum(i - 1, 0)])

    @pl.when(jnp.logical_and(used, first_of_expert))
    def _():
        slot = slot_ref[0]
        for cp in fetch(e, slot):
            cp.wait()
        wg_b[...] = wg_f[slot].astype(BF16)
        wu_b[...] = wu_f[slot].astype(BF16)
        wd_b[...] = wd_f[slot].astype(BF16)
        nxt = lax.div(pend_ref[e], EXPERT_BLOCK)

        @pl.when(nxt < nu_ref[0])
        def _():
            for cp in fetch(be_ref[nxt], 1 - slot):
                cp.start()

        slot_ref[0] = 1 - slot

    @pl.when(used)
    def _():
        x = _token_rows_load(xs_ref, EXPERT_BLOCK).astype(BF16)
        gate = jnp.minimum(jnp.dot(x, wg_b[...], preferred_element_type=F32) + bg_ref[...], SWIGLU_LIMIT)
        up = jnp.clip(jnp.dot(x, wu_b[...], preferred_element_type=F32) + bu_ref[...],
                      -SWIGLU_LIMIT, SWIGLU_LIMIT)
        glu = gate * _sigmoid(SWIGLU_ALPHA * gate)
        mid = ((up + 1.0) * glu).astype(BF16)
        y = jnp.dot(mid, wd_b[...], preferred_element_type=F32) + bd_ref[...]
        _token_rows_store(y_ref, y)

    @pl.when(jnp.logical_not(used))
    def _():
        y_ref[...] = jnp.zeros(y_ref.shape, y_ref.dtype)


def _experts(block_e, n_used, pend_i, xs, w_gate, b_gate, w_up, b_up, w_down, b_down):
    n_rows = xs.shape[0] // SUBLANES
    n_e, d, f = w_gate.shape
    nb = n_rows // EXPERT_BLOCK
    blk = (EXPERT_BLOCK * SUBLANES, LANES)
    last = lambda i, be, nu, pend: jnp.maximum(jnp.minimum(i, nu[0] - 1), 0)
    bspec = lambda n: pl.BlockSpec((None, 1, n), lambda i, be, nu, pend: (be[last(i, be, nu, pend)], 0, 0))
    hbm = pl.BlockSpec(memory_space=pl.ANY)
    return pl.pallas_call(
        _expert_kernel,
        out_shape=jax.ShapeDtypeStruct((n_rows * SUBLANES, LANES), F32),
        grid_spec=pltpu.PrefetchScalarGridSpec(
            num_scalar_prefetch=3,
            grid=(nb,),
            in_specs=[pl.BlockSpec(blk, lambda i, be, nu, pend: (last(i, be, nu, pend), 0)),
                      hbm, bspec(f), hbm, bspec(f), hbm, bspec(d)],
            out_specs=pl.BlockSpec(blk, lambda i, be, nu, pend: (i, 0)),
            scratch_shapes=[pltpu.VMEM((2, d, f), F32), pltpu.VMEM((2, d, f), F32), pltpu.VMEM((2, f, d), F32),
                            pltpu.VMEM((d, f), BF16), pltpu.VMEM((d, f), BF16), pltpu.VMEM((f, d), BF16),
                            pltpu.SMEM((1,), I32), pltpu.SemaphoreType.DMA((2,))]),
        compiler_params=_params("arbitrary"),
        name="experts",
    )(block_e, n_used, pend_i, xs, w_gate, b_gate.reshape(n_e, 1, f), w_up, b_up.reshape(n_e, 1, f),
      w_down, b_down.reshape(n_e, 1, d))


def _combine_kernel(dest_hbm, ys_hbm, tw_ref, x1_ref, gt_ref, g_ref, o_ref, idx_ref, buf_ref, idx_sem, row_sem):
    i = pl.program_id(0)
    n = pl.num_programs(0)
    tc = idx_ref.shape[1]

    def idx_copy(tile, s):
        return pltpu.make_async_copy(dest_hbm.at[:, pl.ds(pl.multiple_of(tile * tc, tc), tc)],
                                     idx_ref.at[pl.ds(s * TOP_K, TOP_K)], idx_sem.at[s])

    def issue_rows(s):
        def issue(tl, carry):
            dst_row = pl.multiple_of(tl * SUBLANES, SUBLANES)
            for k in range(TOP_K):
                d = pl.multiple_of(idx_ref[s * TOP_K + k, tl] * SUBLANES, SUBLANES)
                pltpu.make_async_copy(ys_hbm.at[pl.ds(d, SUBLANES)], buf_ref.at[s, k, pl.ds(dst_row, SUBLANES)],
                                      row_sem.at[s]).start(priority=k % 2)
            return carry

        lax.fori_loop(0, tc, issue, 0)

    @pl.when(i == 0)
    def _():
        idx_copy(0, 0).start()
        idx_copy(0, 0).wait()
        issue_rows(0)

        @pl.when(n > 1)
        def _():
            idx_copy(1, 1).start()

    def step(slot):
        @pl.when(i + 1 < n)
        def _():
            idx_copy(i + 1, 1 - slot).wait()
            issue_rows(1 - slot)

            @pl.when(i + 2 < n)
            def _():
                idx_copy(i + 2, slot).start()

        for k in range(TOP_K):
            pltpu.make_async_copy(ys_hbm.at[pl.ds(0, tc * SUBLANES)], buf_ref.at[slot, k], row_sem.at[slot]).wait()

        tw = tw_ref[...]
        ffn = tw[:, 0:1] * _token_rows_load(buf_ref.at[slot, 0], tc)
        for k in range(1, TOP_K):
            ffn = ffn + tw[:, k:k + 1] * _token_rows_load(buf_ref.at[slot, k], tc)
        x2 = x1_ref[...] + gt_ref[0] * ffn
        o_ref[...] = _rms(x2) * g_ref[...]

    for parity in range(2):
        pl.when(lax.rem(i, 2) == parity)(functools.partial(step, parity))


def _combine(dest_t, ys, topw, x1, gt, g, seq):
    t, d = x1.shape
    tc = min(COMBINE_TILE, seq)
    per_batch = seq // tc
    row = lambda i: (i, 0)
    return pl.pallas_call(
        _combine_kernel,
        out_shape=jax.ShapeDtypeStruct((t, d), F32),
        grid=(t // tc,),
        in_specs=[pl.BlockSpec(memory_space=pl.ANY), pl.BlockSpec(memory_space=pl.ANY),
                  pl.BlockSpec((tc, TOP_K), row), pl.BlockSpec((tc, d), row),
                  pl.BlockSpec((1, 1, d), lambda i: (i // per_batch, 0, 0)), _const_spec((1, d))],
        out_specs=pl.BlockSpec((tc, d), row),
        scratch_shapes=[pltpu.SMEM((2 * TOP_K, tc), I32), pltpu.VMEM((2, TOP_K, tc * SUBLANES, LANES), F32),
                        pltpu.SemaphoreType.DMA((2,)), pltpu.SemaphoreType.DMA((2,))],
        compiler_params=_params("arbitrary"),
        name="combine",
    )(dest_t, ys, topw, x1, gt, g)


def _layer(x2, mod, batch, seq, norm1_g, w_in, conv_w, conv_b, dt_bias, a_log, d_skip, ssm_norm_g,
           gla_wg2, gla_bg, gla_norm_g, w_out, norm2_g, w_router, b_router,
           w_gate, b_gate, w_up, b_up, w_down, b_down):
    t, d = x2.shape
    n_heads = dt_bias.shape[0]
    d_ssm = n_heads * SSM_HEAD_DIM
    cd = conv_w.shape[1]
    dkt = gla_wg2.shape[1]
    dvt = w_out.shape[0] - d_ssm
    n_experts = w_router.shape[1]

    sh1, sc1, gt1, sh2, sc2, gt2 = [m.reshape(batch, 1, d) for m in jnp.split(mod[:batch], 6, axis=1)]

    sizes = (d_ssm, cd, n_heads, dkt, dkt, dvt, GLA_GATE_RANK, dvt)
    offs = [0]
    for s in sizes:
        offs.append(offs[-1] + s)
    piece = lambda j: (offs[j], sizes[j])
    sections = ((piece(0),), (piece(1),), (piece(3),), (piece(4),), (piece(5),), (piece(7),), (piece(2), piece(6)))
    z, xbc, q, k, v, r, small = _inproj(x2, sc1, sh1, norm1_g.reshape(1, d), w_in.astype(BF16), sections,
                                        (d_ssm, cd, dkt, dkt, dvt, dvt, LANES), [BF16] * 6 + [F32], seq)

    d_skip_e = jnp.repeat(d_skip, SSM_HEAD_DIM).reshape(1, d_ssm)
    y = _ssd(xbc, z, small, conv_w, conv_b, dt_bias, a_log, d_skip_e, ssm_norm_g, batch, seq)
    o = _gla(q, k, v, r, small, gla_wg2, gla_bg, gla_norm_g, batch, seq, gate_col=n_heads)

    x1, h2, topi_t, topw_t, counts = _outproj(y, o, x2, gt1, sc2, sh2, norm2_g.reshape(1, d),
                                        w_out[:d_ssm].astype(BF16), w_out[d_ssm:].astype(BF16),
                                        w_router, b_router, seq)

    n_blocks = (t * TOP_K) // EXPERT_BLOCK + n_experts
    dest_t, block_e, pend = _route(topi_t, counts, n_blocks)
    pend_i = pend[:, 0].astype(I32)
    n_used = (pend_i[n_experts - 1:] // EXPERT_BLOCK).astype(I32)
    topw = topw_t.T
    xs = _dispatch(pend_i, dest_t, h2, n_blocks * EXPERT_BLOCK, n_experts)
    ys = _experts(block_e[0, :n_blocks], n_used, pend_i, xs, w_gate, b_gate, w_up, b_up, w_down, b_down)
    return dest_t, ys, topw, x1, gt2


def kernel(x, c, w_ada, b_ada, norm1_g, w_in, conv_w, conv_b, dt_bias, a_log, d_skip, ssm_norm_g, gla_wg2,
           gla_bg, gla_norm_g, w_out, norm2_g, w_router, b_router, w_gate, b_gate, w_up, b_up, w_down, b_down,
           final_norm_g):
    batch, seq, d = x.shape
    assert w_ada.shape[0] == 1, "single-layer trunk"
    assert d == SUBLANES * LANES, "token rows are moved as one (8, 128) f32 tile each"
    assert seq % min(seq, max(TOKEN_TILE, SSM_CHUNK, GLA_STEP_ROWS, COMBINE_TILE, DISPATCH_TILE)) == 0
    assert seq % max(SSM_CHUNK, GLA_STEP_ROWS) == 0
    x2 = x.reshape(batch * seq, d)
    c_pad = jnp.zeros((SUBLANES, d), F32).at[:batch].set(c)
    mod = _ada(c_pad, w_ada[0], b_ada)
    dest_t, ys, topw, x1, gt2 = _layer(
        x2, mod, batch, seq, norm1_g[0], w_in[0], conv_w[0], conv_b[0], dt_bias[0], a_log[0], d_skip[0],
        ssm_norm_g[0], gla_wg2[0], gla_bg[0], gla_norm_g[0], w_out[0], norm2_g[0], w_router[0], b_router[0],
        w_gate[0], b_gate[0], w_up[0], b_up[0], w_down[0], b_down[0])
    out = _combine(dest_t, ys, topw, x1, gt2, final_norm_g.reshape(1, d), seq)
    return out.reshape(batch, seq, d)
```

```python
import functools

import jax
import jax.numpy as jnp
from jax import lax
from jax.experimental import pallas as pl
from jax.experimental.pallas import tpu as pltpu

F32 = jnp.float32
BF16 = jnp.bfloat16
I32 = jnp.int32
HIGHEST = lax.Precision.HIGHEST

EPS = 1e-6
SSM_HEAD_DIM = 64
SSM_GROUPS = 2
SSM_STATE = 128
SSM_CONV = 4
SSM_CHUNK = 128
GLA_HEADS = 4
GLA_GATE_RANK = 16
GLA_GATE_NORM = 16.0
GLA_CHUNK = 64
TOP_K = 4
SWIGLU_LIMIT = 7.0
SWIGLU_ALPHA = 1.702

LANES = 128
SUBLANES = 8
VMEM_LIMIT_BYTES = 56 * 1024 * 1024

TOKEN_TILE = 512
SSD_STEP_ROWS = 512
CONV_CARRY = 16
GLA_STEP_ROWS = 512
GLA_GROUP_ROWS = 256
ROUTE_TILE = 512
EXPERT_BLOCK = 512
DISPATCH_TILE = 512
COMBINE_TILE = 512

_NT = (((1,), (1,)), ((), ()))
_TN = (((0,), (0,)), ((), ()))


def _sigmoid(v):
    return 0.5 * jnp.tanh(0.5 * v) + 0.5


def _silu(v):
    return v * _sigmoid(v)


def _softplus(v):
    return jnp.maximum(v, 0.0) + jnp.log1p(jnp.exp(-jnp.abs(v)))


def _log_sigmoid(v):
    return jnp.minimum(v, 0.0) - jnp.log(1.0 + jnp.exp(-jnp.abs(v)))


def _rms(v):
    return v * lax.rsqrt(jnp.mean(v * v, axis=-1, keepdims=True) + EPS)


def _pieces(a, n):
    out = []
    for _ in range(n - 1):
        p = a.astype(BF16)
        out.append(p)
        a = a - p.astype(F32)
    out.append(a.astype(BF16))
    return out


def _dot_pieces(a, b, n, dims=None):
    dims = dims or (((a.ndim - 1,), (0,)), ((), ()))
    return sum(lax.dot_general(p, b, dims, preferred_element_type=F32) for p in _pieces(a, n))


def _token_rows_load(ref, rows):
    return jnp.concatenate([ref[pl.ds(s, rows, stride=SUBLANES), :] for s in range(SUBLANES)], axis=1)


def _token_rows_store(ref, v):
    rows = v.shape[0]
    for s in range(SUBLANES):
        ref[pl.ds(s, rows, stride=SUBLANES), :] = v[:, s * LANES:(s + 1) * LANES]


def _params(*semantics):
    return pltpu.CompilerParams(dimension_semantics=semantics, vmem_limit_bytes=VMEM_LIMIT_BYTES)


def _const_spec(shape):
    nd = len(shape)
    return pl.BlockSpec(shape, lambda *_: (0,) * nd)


def _ada_kernel(c_ref, w_ref, b_ref, o_ref):
    cond = _silu(c_ref[...])
    o_ref[...] = jnp.dot(cond, w_ref[...], precision=HIGHEST, preferred_element_type=F32) + b_ref[...]


def _ada(c_pad, w_ada, b_ada):
    rows, d = c_pad.shape
    n = w_ada.shape[1]
    tn = d
    return pl.pallas_call(
        _ada_kernel,
        out_shape=jax.ShapeDtypeStruct((rows, n), F32),
        grid=(n // tn,),
        in_specs=[pl.BlockSpec((rows, d), lambda j: (0, 0)),
                  pl.BlockSpec((d, tn), lambda j: (0, j)),
                  pl.BlockSpec((1, tn), lambda j: (0, j))],
        out_specs=pl.BlockSpec((rows, tn), lambda j: (0, j)),
        compiler_params=_params("arbitrary"),
        name="ada",
    )(c_pad, w_ada, b_ada)


def _inproj_kernel(x_ref, sc_ref, sh_ref, g_ref, w_ref, *refs, sections):
    o_refs, ws_ref = refs[:-1], refs[-1]

    @pl.when(pl.program_id(0) == 0)
    def _():
        off = 0
        for o_ref, pieces in zip(o_refs, sections):
            n = o_ref.shape[1]
            used = 0
            for src, width in pieces:
                ws_ref[:, off + used:off + used + width] = w_ref[:, src:src + width]
                used += width
            if used < n:
                ws_ref[:, off + used:off + n] = jnp.zeros((ws_ref.shape[0], n - used), ws_ref.dtype)
            off += n

    h = (_rms(x_ref[...]) * g_ref[...]) * (1.0 + sc_ref[0]) + sh_ref[0]
    hb = h.astype(BF16)
    off = 0
    for o_ref in o_refs:
        n = o_ref.shape[1]
        o_ref[...] = jnp.dot(hb, ws_ref[:, off:off + n], preferred_element_type=F32).astype(o_ref.dtype)
        off += n


def _inproj(x2, sc, sh, g, w_bf, sections, widths, out_dtypes, seq):
    t, d = x2.shape
    tm = min(TOKEN_TILE, seq)
    per_batch = seq // tm
    assert all(n % LANES == 0 for n in widths)
    mod_spec = pl.BlockSpec((1, 1, d), lambda i: (i // per_batch, 0, 0))
    kern = functools.partial(_inproj_kernel, sections=sections)
    return pl.pallas_call(
        kern,
        out_shape=[jax.ShapeDtypeStruct((t, n), dt) for n, dt in zip(widths, out_dtypes)],
        grid=(t // tm,),
        in_specs=[pl.BlockSpec((tm, d), lambda i: (i, 0)), mod_spec, mod_spec, _const_spec((1, d)),
                  pl.BlockSpec(w_bf.shape, lambda i: (0, 0), pipeline_mode=pl.Buffered(1))],
        out_specs=[pl.BlockSpec((tm, n), lambda i: (i, 0)) for n in widths],
        scratch_shapes=[pltpu.VMEM((d, sum(widths)), BF16)],
        compiler_params=_params("arbitrary"),
        name="inproj",
    )(x2, sc, sh, g, w_bf)


def _ssd_kernel(tail_ref, xbc_ref, z_ref, sm_ref, cw_ref, cb_ref, dtb_r_ref, dtb_c_ref, alog_r_ref, alog_c_ref,
                dsk_ref, g_ref, y_ref, st_ref, *, n_heads, d_ssm):
    L = SSM_CHUNK
    P = SSM_HEAD_DIM
    N = SSM_STATE
    G = SSM_GROUPS
    gw = d_ssm // G
    R = xbc_ref.shape[0]
    c = pl.program_id(1)

    @pl.when(c == 0)
    def _():
        st_ref[...] = jnp.zeros(st_ref.shape, F32)

    tail = tail_ref[...]
    tail = jnp.where(c == 0, jnp.zeros_like(tail), tail)
    conv_in = jnp.concatenate([tail, xbc_ref[...]], axis=0)

    ri = lax.broadcasted_iota(I32, (L, L), 0)
    ci = lax.broadcasted_iota(I32, (L, L), 1)
    causal = ri >= ci
    causal3 = jnp.concatenate([causal.astype(BF16)] * 3, axis=1)
    upper3 = jnp.concatenate([(ri <= ci).astype(BF16)] * 3, axis=0)
    wr = lax.broadcasted_iota(I32, ((SSM_CONV - 1) * L, CONV_CARRY + L), 0)
    wc = lax.broadcasted_iota(I32, ((SSM_CONV - 1) * L, CONV_CARRY + L), 1)
    shifts = (wc == (wr % L) + CONV_CARRY - (SSM_CONV - 1) + wr // L).astype(BF16)
    hh = lax.broadcasted_iota(I32, (n_heads, d_ssm), 0)
    jj = lax.broadcasted_iota(I32, (n_heads, d_ssm), 1)
    expand = ((jj // P) == hh).astype(BF16)
    expand2 = jnp.concatenate([expand, expand], axis=0)
    lane = lax.broadcasted_iota(I32, (L, LANES), 1)
    first_half = lane < P
    heads_per_group = n_heads // G
    a_row = -jnp.exp(alog_r_ref[...])
    a_col = -jnp.exp(alog_c_ref[...])

    for s in range(R // L):
        rows = slice(s * L, (s + 1) * L)
        window = conv_in[s * L:s * L + CONV_CARRY + L, :]
        taps = jnp.dot(shifts, window, preferred_element_type=F32)
        acc = cb_ref[...] + cw_ref[SSM_CONV - 1:SSM_CONV, :] * window[CONV_CARRY:, :].astype(F32)
        for j in range(SSM_CONV - 1):
            acc = acc + cw_ref[j:j + 1, :] * taps[j * L:(j + 1) * L, :]
        act = _silu(acc)
        xs = act[:, :d_ssm]
        bm = act[:, d_ssm:d_ssm + G * N].astype(BF16)
        cm = act[:, d_ssm + G * N:].astype(BF16)

        sm = sm_ref[rows, :]
        dt_col = _softplus(sm[:, :n_heads] + dtb_r_ref[...])
        dt_row = _softplus(sm.T[:n_heads, :] + dtb_c_ref[...])
        cs_col = jnp.dot(causal3, jnp.concatenate(_pieces(dt_col * a_row, 3), axis=0),
                         preferred_element_type=F32)
        cs_row = jnp.dot(jnp.concatenate(_pieces(dt_row * a_col, 3), axis=1), upper3,
                         preferred_element_type=F32)
        cs_last = cs_col[L - 1:L, :]

        per_head = jnp.concatenate([dt_col, jnp.exp(cs_col), jnp.exp(cs_last - cs_col)], axis=0)
        per_ch = jnp.dot(jnp.concatenate(_pieces(per_head, 2), axis=1), expand2,
                         preferred_element_type=F32)
        dt_e, ecs_e, dte_e = per_ch[0:L], per_ch[L:2 * L], per_ch[2 * L:3 * L]
        chunk_decay = ecs_e[L - 1:L, :]

        xdt = xs * dt_e
        xdt_b = xdt.astype(BF16)
        xdec_b = (xdt * dte_e).astype(BF16)

        y_parts = []
        y_off_parts = []
        for g in range(G):
            bm_g = bm[:, g * N:(g + 1) * N]
            cm_g = cm[:, g * N:(g + 1) * N]
            cb = lax.dot_general(cm_g, bm_g, _NT, preferred_element_type=F32)
            prev = st_ref[g]
            y_off_parts.append(jnp.dot(cm_g, prev.astype(BF16), preferred_element_type=F32))
            s_new = lax.dot_general(bm_g, xdec_b[:, g * gw:(g + 1) * gw], _TN, preferred_element_type=F32)
            st_ref[g] = prev * chunk_decay[:, g * gw:(g + 1) * gw] + s_new
            for p in range(heads_per_group // 2):
                h0 = g * heads_per_group + 2 * p
                ms = []
                for h in (h0, h0 + 1):
                    diff = cs_col[:, h:h + 1] - cs_row[h:h + 1, :]
                    ms.append((cb * jnp.exp(jnp.where(causal, diff, -jnp.inf))).astype(BF16))
                lhs = jnp.concatenate(ms, axis=1)
                xp = xdt_b[:, h0 * P:(h0 + 2) * P]
                zero = jnp.zeros_like(xp)
                rhs = jnp.concatenate([jnp.where(first_half, xp, zero), jnp.where(first_half, zero, xp)], axis=0)
                y_parts.append(jnp.dot(lhs, rhs, preferred_element_type=F32))
        y = jnp.concatenate(y_parts, axis=1) + jnp.concatenate(y_off_parts, axis=1) * ecs_e + dsk_ref[...] * xs
        y = y * _silu(z_ref[rows, :].astype(F32))
        y = jnp.concatenate([_rms(y[:, g * gw:(g + 1) * gw]) for g in range(G)], axis=1) * g_ref[...]
        y_ref[rows, :] = y.astype(y_ref.dtype)


def _ssd(xbc, z, small, conv_w, conv_b, dt_bias, a_log, d_skip_e, norm_g, batch, seq):
    t, cd = xbc.shape
    d_ssm = z.shape[1]
    n_heads = dt_bias.shape[0]
    L = min(SSD_STEP_ROWS, seq)
    nc = seq // L
    row = lambda b, c: (b * nc + c, 0)
    tail = lambda b, c: (jnp.maximum((b * nc + c) * (L // CONV_CARRY) - 1, 0), 0)
    kern = functools.partial(_ssd_kernel, n_heads=n_heads, d_ssm=d_ssm)
    return pl.pallas_call(
        kern,
        out_shape=jax.ShapeDtypeStruct((t, d_ssm), BF16),
        grid=(batch, nc),
        in_specs=[pl.BlockSpec((CONV_CARRY, cd), tail),
                  pl.BlockSpec((L, cd), row), pl.BlockSpec((L, d_ssm), row), pl.BlockSpec((L, LANES), row),
                  _const_spec(conv_w.shape), _const_spec((1, cd)),
                  _const_spec((1, n_heads)), _const_spec((n_heads, 1)),
                  _const_spec((1, n_heads)), _const_spec((n_heads, 1)),
                  _const_spec((1, d_ssm)), _const_spec((1, d_ssm))],
        out_specs=pl.BlockSpec((L, d_ssm), row),
        scratch_shapes=[pltpu.VMEM((SSM_GROUPS, SSM_STATE, d_ssm // SSM_GROUPS), F32)],
        compiler_params=_params("arbitrary", "arbitrary"),
        name="ssd",
    )(xbc, xbc, z, small, conv_w, conv_b.reshape(1, cd), dt_bias.reshape(1, n_heads), dt_bias.reshape(n_heads, 1),
      a_log.reshape(1, n_heads), a_log.reshape(n_heads, 1), d_skip_e, norm_g.reshape(1, d_ssm))


def _gla_kernel(q_ref, k_ref, v_ref, r_ref, sm_ref, wg2_ref, bg_ref, gn_ref, o_ref, st_ref, *, gate_col):
    L = GLA_CHUNK
    H = GLA_HEADS
    dk = q_ref.shape[1] // H
    dv = v_ref.shape[1] // H
    c = pl.program_id(1)

    @pl.when(c == 0)
    def _():
        st_ref[...] = jnp.zeros(st_ref.shape, F32)

    R = min(GLA_GROUP_ROWS, q_ref.shape[0])
    n_chunks = R // L
    ri = lax.broadcasted_iota(I32, (R, R), 0)
    ci = lax.broadcasted_iota(I32, (R, R), 1)
    same_chunk = (ri // L) == (ci // L)
    causal = jnp.logical_and(same_chunk, ri >= ci)
    tril = causal.astype(BF16)
    later = jnp.logical_and(same_chunk, ri < ci).astype(BF16)
    sr = lax.broadcasted_iota(I32, (R, n_chunks * LANES), 0)
    sc = lax.broadcasted_iota(I32, (R, n_chunks * LANES), 1)
    last_rows = (sr == (sc // LANES) * L + (L - 1)).astype(BF16)
    wg2_hi, wg2_lo = _pieces(wg2_ref[...], 2)

    tril3 = jnp.concatenate([jnp.concatenate([tril] * 3, axis=1), jnp.concatenate([later] * 3, axis=1)], axis=0)
    last3 = jnp.concatenate([last_rows] * 3, axis=0)
    wg3 = jnp.concatenate([wg2_hi, wg2_hi, wg2_lo], axis=0)

    for gi in range(q_ref.shape[0] // R):
        rs = slice(gi * R, (gi + 1) * R)
        q = q_ref[rs, :].astype(F32) * (dk ** -0.5)
        k = k_ref[rs, :].astype(F32)
        v = v_ref[rs, :]
        r = r_ref[rs, :].astype(F32)
        g_hi, g_lo = _pieces(sm_ref[rs, gate_col:gate_col + GLA_GATE_RANK], 2)
        pre = jnp.dot(jnp.concatenate([g_hi, g_lo, g_hi], axis=1), wg3, preferred_element_type=F32) + bg_ref[...]
        gk3 = jnp.concatenate(_pieces(_log_sigmoid(pre) / GLA_GATE_NORM, 3), axis=0)
        sums = jnp.dot(tril3, gk3, preferred_element_type=F32)
        bcum = sums[:R]
        to_end = sums[R:]
        q_t = (q * jnp.exp(bcum)).astype(BF16)
        k_t = (k * jnp.exp(-bcum)).astype(BF16)
        k_dec = (k * jnp.exp(to_end)).astype(BF16)
        dcol = jnp.exp(lax.dot_general(jnp.concatenate(_pieces(bcum, 3), axis=0), last3, _TN,
                                       preferred_element_type=F32))
        outs = []
        for h in range(H):
            ks = slice(h * dk, (h + 1) * dk)
            vs = slice(h * dv, (h + 1) * dv)
            att = lax.dot_general(q_t[:, ks], k_t[:, ks], _NT, preferred_element_type=F32)
            att = jnp.where(causal, att, 0.0).astype(BF16)
            o = jnp.dot(att, v[:, vs], preferred_element_type=F32)
            state = st_ref[h]
            inter = []
            for c in range(n_chunks):
                rows = slice(c * L, (c + 1) * L)
                inter.append(jnp.dot(q_t[rows, ks], state.astype(BF16), preferred_element_type=F32))
                s_new = lax.dot_general(k_dec[rows, ks], v[rows, vs], _TN, preferred_element_type=F32)
                dec = dcol[ks, c * LANES:(c + 1) * LANES]
                state = state * jnp.concatenate([dec] * (dv // LANES), axis=1) + s_new
            st_ref[h] = state
            o = o + jnp.concatenate(inter, axis=0)
            outs.append(_rms(o) * gn_ref[...] * _silu(r[:, vs]))
        o_ref[rs, :] = jnp.concatenate(outs, axis=1).astype(o_ref.dtype)


def _gla(q, k, v, r, small, wg2, bg, norm_g, batch, seq, gate_col):
    t, dkt = q.shape
    dvt = v.shape[1]
    rows = min(GLA_STEP_ROWS, seq)
    nc = seq // rows
    row = lambda b, c: (b * nc + c, 0)
    kern = functools.partial(_gla_kernel, gate_col=gate_col)
    return pl.pallas_call(
        kern,
        out_shape=jax.ShapeDtypeStruct((t, dvt), BF16),
        grid=(batch, nc),
        in_specs=[pl.BlockSpec((rows, dkt), row), pl.BlockSpec((rows, dkt), row), pl.BlockSpec((rows, dvt), row),
                  pl.BlockSpec((rows, dvt), row), pl.BlockSpec((rows, LANES), row),
                  _const_spec(wg2.shape), _const_spec((1, dkt)), _const_spec((1, dvt // GLA_HEADS))],
        out_specs=pl.BlockSpec((rows, dvt), row),
        scratch_shapes=[pltpu.VMEM((GLA_HEADS, dkt // GLA_HEADS, dvt // GLA_HEADS), F32)],
        compiler_params=_params("arbitrary", "arbitrary"),
        name="gla",
    )(q, k, v, r, small, wg2, bg.reshape(1, dkt), norm_g.reshape(1, dvt // GLA_HEADS))


def _outproj_kernel(y_ref, o_ref, x_ref, gt_ref, sc_ref, sh_ref, g_ref, wy_ref, wo_ref, wr_ref, br_ref,
                    x1_ref, h_ref, ti_ref, tw_ref, cnt_ref):
    mix = (jnp.dot(y_ref[...], wy_ref[...], preferred_element_type=F32)
           + jnp.dot(o_ref[...], wo_ref[...], preferred_element_type=F32))
    x1 = x_ref[...] + gt_ref[0] * mix
    x1_ref[...] = x1
    h = (_rms(x1) * g_ref[...]) * (1.0 + sc_ref[0]) + sh_ref[0]
    _token_rows_store(h_ref, h)
    n_e = br_ref.shape[0]
    h_hi, h_lo = _pieces(h, 2)
    wr = wr_ref[...]
    hw = lax.dot_general(wr, h_hi, _NT, preferred_element_type=F32)
    logits = (hw[:n_e] + hw[n_e:] + lax.dot_general(wr[:n_e], h_lo, _NT, preferred_element_type=F32)) + br_ref[...]
    expert = lax.broadcasted_iota(I32, logits.shape, 0)
    vals, idxs = [], []
    counts = jnp.zeros(logits.shape, F32)
    for _ in range(TOP_K):
        m = jnp.max(logits, axis=0, keepdims=True)
        idx = jnp.min(jnp.where(logits == m, expert, n_e), axis=0, keepdims=True)
        vals.append(m)
        idxs.append(idx)
        chosen = expert == idx
        counts = counts + chosen.astype(F32)
        logits = jnp.where(chosen, -jnp.inf, logits)
    exps = [jnp.exp(v - vals[0]) for v in vals]
    denom = functools.reduce(lambda a, b: a + b, exps)
    ti_ref[...] = jnp.concatenate(idxs, axis=0)
    tw_ref[...] = jnp.concatenate([e / denom for e in exps], axis=0)

    @pl.when(pl.program_id(0) == 0)
    def _():
        cnt_ref[...] = jnp.zeros(cnt_ref.shape, F32)

    cnt_ref[...] = cnt_ref[...] + jnp.sum(counts, axis=1, keepdims=True)


def _outproj(y, o, x2, gt, sc, sh, g, wy, wo, w_router, b_router, seq):
    t, d = x2.shape
    n_e = w_router.shape[1]
    tm = min(TOKEN_TILE, seq)
    per_batch = seq // tm
    row = lambda i: (i, 0)
    mod_spec = pl.BlockSpec((1, 1, d), lambda i: (i // per_batch, 0, 0))
    wr_t = w_router.T
    wr_hi = wr_t.astype(BF16)
    wr_cat = jnp.concatenate([wr_hi, (wr_t - wr_hi.astype(F32)).astype(BF16)], axis=0)
    col = lambda i: (0, i)
    return pl.pallas_call(
        _outproj_kernel,
        out_shape=[jax.ShapeDtypeStruct((t, d), F32), jax.ShapeDtypeStruct((t * SUBLANES, LANES), F32),
                   jax.ShapeDtypeStruct((TOP_K, t), I32), jax.ShapeDtypeStruct((TOP_K, t), F32),
                   jax.ShapeDtypeStruct((n_e, LANES), F32)],
        grid=(t // tm,),
        in_specs=[pl.BlockSpec((tm, y.shape[1]), row), pl.BlockSpec((tm, o.shape[1]), row),
                  pl.BlockSpec((tm, d), row), mod_spec, mod_spec, mod_spec, _const_spec((1, d)),
                  _const_spec(wy.shape), _const_spec(wo.shape), _const_spec(wr_cat.shape),
                  _const_spec((n_e, 1))],
        out_specs=[pl.BlockSpec((tm, d), row), pl.BlockSpec((tm * SUBLANES, LANES), row),
                   pl.BlockSpec((TOP_K, tm), col), pl.BlockSpec((TOP_K, tm), col),
                   _const_spec((n_e, LANES))],
        compiler_params=_params("arbitrary"),
        name="outproj",
    )(y, o, x2, gt, sc, sh, g, wy, wo, wr_cat, b_router.reshape(n_e, 1))


def _route_kernel(ti_ref, cnt_ref, dest_ref, be_ref, pend_ref, run_ref, *, n_blocks_pad):
    i = pl.program_id(0)
    n_e = cnt_ref.shape[0]
    tr = ti_ref.shape[1]

    @pl.when(i == 0)
    def _():
        counts = cnt_ref[...]
        padded = jnp.ceil(counts / EXPERT_BLOCK) * EXPERT_BLOCK
        ri = lax.broadcasted_iota(I32, (n_e, n_e), 0)
        ci = lax.broadcasted_iota(I32, (n_e, n_e), 1)
        pend = jnp.dot((ri >= ci).astype(F32), padded, precision=HIGHEST, preferred_element_type=F32)
        pend_ref[...] = pend
        run_ref[...] = pend - padded
        start = (lax.broadcasted_iota(I32, (n_e, n_blocks_pad), 1) * EXPERT_BLOCK).astype(F32)
        be = jnp.sum((pend[:, 0:1] <= start).astype(F32), axis=0, keepdims=True)
        be_ref[...] = jnp.minimum(be, n_e - 1).astype(I32)

    ti = ti_ref[...]
    expert = lax.broadcasted_iota(I32, (n_e, tr), 0)
    onehots = [expert == ti[k:k + 1, :] for k in range(TOP_K)]
    cnt = functools.reduce(lambda a, b: a + b, [oh.astype(F32) for oh in onehots])
    ri = lax.broadcasted_iota(I32, (tr, tr), 0)
    ci = lax.broadcasted_iota(I32, (tr, tr), 1)
    before = jnp.dot(cnt.astype(BF16), (ri < ci).astype(BF16), preferred_element_type=F32)
    base = run_ref[:, 0:1] + before
    dest = [jnp.sum(jnp.where(oh, base, 0.0), axis=0, keepdims=True) for oh in onehots]
    dest_ref[...] = jnp.concatenate(dest, axis=0).astype(I32)
    run_ref[...] = run_ref[...] + jnp.sum(cnt, axis=1, keepdims=True)


def _route(topi_t, counts, n_blocks):
    t = topi_t.shape[1]
    n_e = counts.shape[0]
    tr = min(ROUTE_TILE, t)
    n_blocks_pad = -(-n_blocks // LANES) * LANES
    kern = functools.partial(_route_kernel, n_blocks_pad=n_blocks_pad)
    return pl.pallas_call(
        kern,
        out_shape=[jax.ShapeDtypeStruct((TOP_K, t), I32), jax.ShapeDtypeStruct((1, n_blocks_pad), I32),
                   jax.ShapeDtypeStruct((n_e, LANES), F32)],
        grid=(t // tr,),
        in_specs=[pl.BlockSpec((TOP_K, tr), lambda i: (0, i)), _const_spec((n_e, LANES))],
        out_specs=[pl.BlockSpec((TOP_K, tr), lambda i: (0, i)), _const_spec((1, n_blocks_pad)),
                   _const_spec((n_e, LANES))],
        scratch_shapes=[pltpu.VMEM((n_e, LANES), F32)],
        compiler_params=_params("arbitrary"),
        name="route",
    )(topi_t, counts)


def _dispatch_kernel(pend_ref, dest_hbm, h_ref, xs_hbm, idx_ref, zero_ref, idx_sem, row_sem, *, n_experts):
    i = pl.program_id(0)
    tg = idx_ref.shape[1]

    @pl.when(i == 0)
    def _():
        zero_ref[...] = jnp.zeros(zero_ref.shape, zero_ref.dtype)
        for e in range(n_experts):
            end = pend_ref[e]
            prev = pend_ref[e - 1] if e > 0 else 0

            @pl.when(end > prev)
            def _():
                start = pl.multiple_of((end - EXPERT_BLOCK) * SUBLANES, EXPERT_BLOCK * SUBLANES)
                cp = pltpu.make_async_copy(zero_ref, xs_hbm.at[pl.ds(start, EXPERT_BLOCK * SUBLANES)], row_sem)
                cp.start()
                cp.wait()

        n_blocks = xs_hbm.shape[0] // (EXPERT_BLOCK * SUBLANES)
        total = pend_ref[n_experts - 1]
        for b in range(n_blocks - n_experts, n_blocks):
            @pl.when(b * EXPERT_BLOCK >= total)
            def _():
                cp = pltpu.make_async_copy(
                    zero_ref, xs_hbm.at[pl.ds(b * EXPERT_BLOCK * SUBLANES, EXPERT_BLOCK * SUBLANES)], row_sem)
                cp.start()
                cp.wait()

    idx_cp = pltpu.make_async_copy(dest_hbm.at[:, pl.ds(pl.multiple_of(i * tg, tg), tg)], idx_ref, idx_sem)
    idx_cp.start()
    idx_cp.wait()

    def issue(tl, carry):
        src = h_ref.at[pl.ds(pl.multiple_of(tl * SUBLANES, SUBLANES), SUBLANES)]
        for k in range(TOP_K):
            d = pl.multiple_of(idx_ref[k, tl] * SUBLANES, SUBLANES)
            pltpu.make_async_copy(src, xs_hbm.at[pl.ds(d, SUBLANES)], row_sem).start(priority=k % 2)
        return carry

    lax.fori_loop(0, tg, issue, 0)
    for _ in range(TOP_K):
        pltpu.make_async_copy(h_ref, xs_hbm.at[pl.ds(0, tg * SUBLANES)], row_sem).wait()


def _dispatch(pend_i, dest_t, h, n_rows, n_experts):
    t = h.shape[0] // SUBLANES
    tg = min(DISPATCH_TILE, t)
    kern = functools.partial(_dispatch_kernel, n_experts=n_experts)
    return pl.pallas_call(
        kern,
        out_shape=jax.ShapeDtypeStruct((n_rows * SUBLANES, LANES), h.dtype),
        grid_spec=pltpu.PrefetchScalarGridSpec(
            num_scalar_prefetch=1,
            grid=(t // tg,),
            in_specs=[pl.BlockSpec(memory_space=pl.ANY),
                      pl.BlockSpec((tg * SUBLANES, LANES), lambda i, pend: (i, 0))],
            out_specs=pl.BlockSpec(memory_space=pl.ANY),
            scratch_shapes=[pltpu.SMEM((TOP_K, tg), I32), pltpu.VMEM((EXPERT_BLOCK * SUBLANES, LANES), h.dtype),
                            pltpu.SemaphoreType.DMA, pltpu.SemaphoreType.DMA]),
        compiler_params=pltpu.CompilerParams(dimension_semantics=("arbitrary",), has_side_effects=True,
                                             vmem_limit_bytes=VMEM_LIMIT_BYTES),
        name="dispatch",
    )(pend_i, dest_t, h)


def _expert_kernel(be_ref, nu_ref, pend_ref, xs_ref, wg_hbm, bg_ref, wu_hbm, bu_ref, wd_hbm, bd_ref, y_ref,
                   wg_f, wu_f, wd_f, wg_b, wu_b, wd_b, slot_ref, sems):
    i = pl.program_id(0)
    used = i < nu_ref[0]
    e = be_ref[i]

    def fetch(expert, slot):
        return [pltpu.make_async_copy(src.at[expert], dst.at[slot], sems.at[slot])
                for src, dst in ((wg_hbm, wg_f), (wu_hbm, wu_f), (wd_hbm, wd_f))]

    @pl.when(i == 0)
    def _():
        slot_ref[0] = 0
        for cp in fetch(e, 0):
            cp.start()

    first_of_expert = jnp.logical_or(i == 0, e != be_ref[jnp.maximum(i - 1, 0)])

    @pl.when(jnp.logical_and(used, first_of_expert))
    def _():
        slot = slot_ref[0]
        for cp in fetch(e, slot):
            cp.wait()
        wg_b[...] = wg_f[slot].astype(BF16)
        wu_b[...] = wu_f[slot].astype(BF16)
        wd_b[...] = wd_f[slot].astype(BF16)
        nxt = lax.div(pend_ref[e], EXPERT_BLOCK)

        @pl.when(nxt < nu_ref[0])
        def _():
            for cp in fetch(be_ref[nxt], 1 - slot):
                cp.start()

        slot_ref[0] = 1 - slot

    def ffn(rows):
        x = _token_rows_load(xs_ref, rows).astype(BF16)
        gate = jnp.minimum(jnp.dot(x, wg_b[...], preferred_element_type=F32) + bg_ref[...], SWIGLU_LIMIT)
        up = jnp.clip(jnp.dot(x, wu_b[...], preferred_element_type=F32) + bu_ref[...],
                      -SWIGLU_LIMIT, SWIGLU_LIMIT)
        glu = gate * _sigmoid(SWIGLU_ALPHA * gate)
        mid = ((up + 1.0) * glu).astype(BF16)
        y = jnp.dot(mid, wd_b[...], preferred_element_type=F32) + bd_ref[...]
        _token_rows_store(y_ref, y)
        if rows < EXPERT_BLOCK:
            y_ref[rows * SUBLANES:, :] = jnp.zeros(((EXPERT_BLOCK - rows) * SUBLANES, LANES), y_ref.dtype)

    valid = pend_ref[pend_ref.shape[0] // 2 + e] - i * EXPERT_BLOCK
    half_full = valid <= EXPERT_BLOCK // 2

    @pl.when(jnp.logical_and(used, jnp.logical_not(half_full)))
    def _():
        ffn(EXPERT_BLOCK)

    @pl.when(jnp.logical_and(used, half_full))
    def _():
        ffn(EXPERT_BLOCK // 2)

    @pl.when(jnp.logical_not(used))
    def _():
        y_ref[...] = jnp.zeros(y_ref.shape, y_ref.dtype)


def _experts(block_e, n_used, pend_i, xs, w_gate, b_gate, w_up, b_up, w_down, b_down):
    n_rows = xs.shape[0] // SUBLANES
    n_e, d, f = w_gate.shape
    nb = n_rows // EXPERT_BLOCK
    blk = (EXPERT_BLOCK * SUBLANES, LANES)
    last = lambda i, be, nu, pend: jnp.maximum(jnp.minimum(i, nu[0] - 1), 0)
    bspec = lambda n: pl.BlockSpec((None, 1, n), lambda i, be, nu, pend: (be[last(i, be, nu, pend)], 0, 0))
    hbm = pl.BlockSpec(memory_space=pl.ANY)
    return pl.pallas_call(
        _expert_kernel,
        out_shape=jax.ShapeDtypeStruct((n_rows * SUBLANES, LANES), F32),
        grid_spec=pltpu.PrefetchScalarGridSpec(
            num_scalar_prefetch=3,
            grid=(nb,),
            in_specs=[pl.BlockSpec(blk, lambda i, be, nu, pend: (last(i, be, nu, pend), 0)),
                      hbm, bspec(f), hbm, bspec(f), hbm, bspec(d)],
            out_specs=pl.BlockSpec(blk, lambda i, be, nu, pend: (i, 0)),
            scratch_shapes=[pltpu.VMEM((2, d, f), F32), pltpu.VMEM((2, d, f), F32), pltpu.VMEM((2, f, d), F32),
                            pltpu.VMEM((d, f), BF16), pltpu.VMEM((d, f), BF16), pltpu.VMEM((f, d), BF16),
                            pltpu.SMEM((1,), I32), pltpu.SemaphoreType.DMA((2,))]),
        compiler_params=_params("arbitrary"),
        name="experts",
    )(block_e, n_used, pend_i, xs, w_gate, b_gate.reshape(n_e, 1, f), w_up, b_up.reshape(n_e, 1, f),
      w_down, b_down.reshape(n_e, 1, d))


def _combine_kernel(dest_hbm, ys_hbm, tw_ref, x1_ref, gt_ref, g_ref, o_ref, idx_ref, buf_ref, idx_sem, row_sem):
    i = pl.program_id(0)
    n = pl.num_programs(0)
    tc = idx_ref.shape[1]

    def idx_copy(tile, s):
        return pltpu.make_async_copy(dest_hbm.at[:, pl.ds(pl.multiple_of(tile * tc, tc), tc)],
                                     idx_ref.at[pl.ds(s * TOP_K, TOP_K)], idx_sem.at[s])

    def issue_rows(s):
        def issue(tl, carry):
            dst_row = pl.multiple_of(tl * SUBLANES, SUBLANES)
            for k in range(TOP_K):
                d = pl.multiple_of(idx_ref[s * TOP_K + k, tl] * SUBLANES, SUBLANES)
                pltpu.make_async_copy(ys_hbm.at[pl.ds(d, SUBLANES)], buf_ref.at[s, k, pl.ds(dst_row, SUBLANES)],
                                      row_sem.at[s]).start(priority=k % 2)
            return carry

        lax.fori_loop(0, tc, issue, 0)

    @pl.when(i == 0)
    def _():
        idx_copy(0, 0).start()
        idx_copy(0, 0).wait()
        issue_rows(0)

        @pl.when(n > 1)
        def _():
            idx_copy(1, 1).start()

    def step(slot):
        @pl.when(i + 1 < n)
        def _():
            idx_copy(i + 1, 1 - slot).wait()
            issue_rows(1 - slot)

            @pl.when(i + 2 < n)
            def _():
                idx_copy(i + 2, slot).start()

        for k in range(TOP_K):
            pltpu.make_async_copy(ys_hbm.at[pl.ds(0, tc * SUBLANES)], buf_ref.at[slot, k], row_sem.at[slot]).wait()

        tw = tw_ref[...]
        ffn = tw[:, 0:1] * _token_rows_load(buf_ref.at[slot, 0], tc)
        for k in range(1, TOP_K):
            ffn = ffn + tw[:, k:k + 1] * _token_rows_load(buf_ref.at[slot, k], tc)
        x2 = x1_ref[...] + gt_ref[0] * ffn
        o_ref[...] = _rms(x2) * g_ref[...]

    for parity in range(2):
        pl.when(lax.rem(i, 2) == parity)(functools.partial(step, parity))


def _combine(dest_t, ys, topw, x1, gt, g, seq):
    t, d = x1.shape
    tc = min(COMBINE_TILE, seq)
    per_batch = seq // tc
    row = lambda i: (i, 0)
    return pl.pallas_call(
        _combine_kernel,
        out_shape=jax.ShapeDtypeStruct((t, d), F32),
        grid=(t // tc,),
        in_specs=[pl.BlockSpec(memory_space=pl.ANY), pl.BlockSpec(memory_space=pl.ANY),
                  pl.BlockSpec((tc, TOP_K), row), pl.BlockSpec((tc, d), row),
                  pl.BlockSpec((1, 1, d), lambda i: (i // per_batch, 0, 0)), _const_spec((1, d))],
        out_specs=pl.BlockSpec((tc, d), row),
        scratch_shapes=[pltpu.SMEM((2 * TOP_K, tc), I32), pltpu.VMEM((2, TOP_K, tc * SUBLANES, LANES), F32),
                        pltpu.SemaphoreType.DMA((2,)), pltpu.SemaphoreType.DMA((2,))],
        compiler_params=_params("arbitrary"),
        name="combine",
    )(dest_t, ys, topw, x1, gt, g)


def _layer(x2, mod, batch, seq, norm1_g, w_in, conv_w, conv_b, dt_bias, a_log, d_skip, ssm_norm_g,
           gla_wg2, gla_bg, gla_norm_g, w_out, norm2_g, w_router, b_router,
           w_gate, b_gate, w_up, b_up, w_down, b_down):
    t, d = x2.shape
    n_heads = dt_bias.shape[0]
    d_ssm = n_heads * SSM_HEAD_DIM
    cd = conv_w.shape[1]
    dkt = gla_wg2.shape[1]
    dvt = w_out.shape[0] - d_ssm
    n_experts = w_router.shape[1]

    sh1, sc1, gt1, sh2, sc2, gt2 = [m.reshape(batch, 1, d) for m in jnp.split(mod[:batch], 6, axis=1)]

    sizes = (d_ssm, cd, n_heads, dkt, dkt, dvt, GLA_GATE_RANK, dvt)
    offs = [0]
    for s in sizes:
        offs.append(offs[-1] + s)
    piece = lambda j: (offs[j], sizes[j])
    sections = ((piece(0),), (piece(1),), (piece(3),), (piece(4),), (piece(5),), (piece(7),), (piece(2), piece(6)))
    z, xbc, q, k, v, r, small = _inproj(x2, sc1, sh1, norm1_g.reshape(1, d), w_in.astype(BF16), sections,
                                        (d_ssm, cd, dkt, dkt, dvt, dvt, LANES), [BF16] * 6 + [F32], seq)

    d_skip_e = jnp.repeat(d_skip, SSM_HEAD_DIM).reshape(1, d_ssm)
    y = _ssd(xbc, z, small, conv_w, conv_b, dt_bias, a_log, d_skip_e, ssm_norm_g, batch, seq)
    o = _gla(q, k, v, r, small, gla_wg2, gla_bg, gla_norm_g, batch, seq, gate_col=n_heads)

    x1, h2, topi_t, topw_t, counts = _outproj(y, o, x2, gt1, sc2, sh2, norm2_g.reshape(1, d),
                                        w_out[:d_ssm].astype(BF16), w_out[d_ssm:].astype(BF16),
                                        w_router, b_router, seq)

    n_blocks = (t * TOP_K) // EXPERT_BLOCK + n_experts
    dest_t, block_e, pend = _route(topi_t, counts, n_blocks)
    pend_i = pend[:, 0].astype(I32)
    n_used = (pend_i[n_experts - 1:] // EXPERT_BLOCK).astype(I32)
    topw = topw_t.T
    xs = _dispatch(pend_i, dest_t, h2, n_blocks * EXPERT_BLOCK, n_experts)
    real_end = jnp.concatenate([jnp.zeros((1,), I32), pend_i[:-1]]) + counts[:, 0].astype(I32)
    seg_ends = jnp.concatenate([pend_i, real_end])
    ys = _experts(block_e[0, :n_blocks], n_used, seg_ends, xs, w_gate, b_gate, w_up, b_up, w_down, b_down)
    return dest_t, ys, topw, x1, gt2


def kernel(x, c, w_ada, b_ada, norm1_g, w_in, conv_w, conv_b, dt_bias, a_log, d_skip, ssm_norm_g, gla_wg2,
           gla_bg, gla_norm_g, w_out, norm2_g, w_router, b_router, w_gate, b_gate, w_up, b_up, w_down, b_down,
           final_norm_g):
    batch, seq, d = x.shape
    assert w_ada.shape[0] == 1, "single-layer trunk"
    assert d == SUBLANES * LANES, "token rows are moved as one (8, 128) f32 tile each"
    assert seq % min(seq, max(TOKEN_TILE, SSM_CHUNK, GLA_STEP_ROWS, COMBINE_TILE, DISPATCH_TILE)) == 0
    assert seq % max(SSM_CHUNK, GLA_STEP_ROWS) == 0
    x2 = x.reshape(batch * seq, d)
    c_pad = jnp.zeros((SUBLANES, d), F32).at[:batch].set(c)
    mod = _ada(c_pad, w_ada[0], b_ada)
    dest_t, ys, topw, x1, gt2 = _layer(
        x2, mod, batch, seq, norm1_g[0], w_in[0], conv_w[0], conv_b[0], dt_bias[0], a_log[0], d_skip[0],
        ssm_norm_g[0], gla_wg2[0], gla_bg[0], gla_norm_g[0], w_out[0], norm2_g[0], w_router[0], b_router[0],
        w_gate[0], b_gate[0], w_up[0], b_up[0], w_down[0], b_down[0])
    out = _combine(dest_t, ys, topw, x1, gt2, final_norm_g.reshape(1, d), seq)
    return out.reshape(batch, seq, d)
```

```python
import functools

import jax
import jax.numpy as jnp
from jax import lax
from jax.experimental import pallas as pl
from jax.experimental.pallas import tpu as pltpu

F32 = jnp.float32
BF16 = jnp.bfloat16
I32 = jnp.int32
HIGHEST = lax.Precision.HIGHEST

EPS = 1e-6
SSM_HEAD_DIM = 64
SSM_GROUPS = 2
SSM_STATE = 128
SSM_CONV = 4
SSM_CHUNK = 128
GLA_HEADS = 4
GLA_GATE_RANK = 16
GLA_GATE_NORM = 16.0
GLA_CHUNK = 64
TOP_K = 4
SWIGLU_LIMIT = 7.0
SWIGLU_ALPHA = 1.702

LANES = 128
SUBLANES = 8
VMEM_LIMIT_BYTES = 56 * 1024 * 1024

TOKEN_TILE = 512
SSD_STEP_ROWS = 512
CONV_CARRY = 16
GLA_STEP_ROWS = 512
GLA_GROUP_ROWS = 256
ROUTE_TILE = 512
EXPERT_BLOCK = 512
EXPERT_PATHS = 4
DISPATCH_TILE = 512
COMBINE_TILE = 512

_NT = (((1,), (1,)), ((), ()))
_TN = (((0,), (0,)), ((), ()))


def _sigmoid(v):
    return 0.5 * jnp.tanh(0.5 * v) + 0.5


def _silu(v):
    return v * _sigmoid(v)


def _softplus(v):
    return jnp.maximum(v, 0.0) + jnp.log1p(jnp.exp(-jnp.abs(v)))


def _log_sigmoid(v):
    return jnp.minimum(v, 0.0) - jnp.log(1.0 + jnp.exp(-jnp.abs(v)))


def _rms(v):
    return v * lax.rsqrt(jnp.mean(v * v, axis=-1, keepdims=True) + EPS)


def _pieces(a, n):
    out = []
    for _ in range(n - 1):
        p = a.astype(BF16)
        out.append(p)
        a = a - p.astype(F32)
    out.append(a.astype(BF16))
    return out


def _dot_pieces(a, b, n, dims=None):
    dims = dims or (((a.ndim - 1,), (0,)), ((), ()))
    return sum(lax.dot_general(p, b, dims, preferred_element_type=F32) for p in _pieces(a, n))


def _token_rows_load(ref, rows):
    return jnp.concatenate([ref[pl.ds(s, rows, stride=SUBLANES), :] for s in range(SUBLANES)], axis=1)


def _token_rows_store(ref, v):
    rows = v.shape[0]
    for s in range(SUBLANES):
        ref[pl.ds(s, rows, stride=SUBLANES), :] = v[:, s * LANES:(s + 1) * LANES]


def _params(*semantics):
    return pltpu.CompilerParams(dimension_semantics=semantics, vmem_limit_bytes=VMEM_LIMIT_BYTES)


def _const_spec(shape):
    nd = len(shape)
    return pl.BlockSpec(shape, lambda *_: (0,) * nd)


def _ada_kernel(c_ref, w_ref, b_ref, o_ref):
    cond = _silu(c_ref[...])
    o_ref[...] = jnp.dot(cond, w_ref[...], precision=HIGHEST, preferred_element_type=F32) + b_ref[...]


def _ada(c_pad, w_ada, b_ada):
    rows, d = c_pad.shape
    n = w_ada.shape[1]
    tn = d
    return pl.pallas_call(
        _ada_kernel,
        out_shape=jax.ShapeDtypeStruct((rows, n), F32),
        grid=(n // tn,),
        in_specs=[pl.BlockSpec((rows, d), lambda j: (0, 0)),
                  pl.BlockSpec((d, tn), lambda j: (0, j)),
                  pl.BlockSpec((1, tn), lambda j: (0, j))],
        out_specs=pl.BlockSpec((rows, tn), lambda j: (0, j)),
        compiler_params=_params("arbitrary"),
        name="ada",
    )(c_pad, w_ada, b_ada)


def _inproj_kernel(x_ref, sc_ref, sh_ref, g_ref, w_ref, *refs, sections):
    o_refs, ws_ref = refs[:-1], refs[-1]

    @pl.when(pl.program_id(0) == 0)
    def _():
        off = 0
        for o_ref, pieces in zip(o_refs, sections):
            n = o_ref.shape[1]
            used = 0
            for src, width in pieces:
                ws_ref[:, off + used:off + used + width] = w_ref[:, src:src + width]
                used += width
            if used < n:
                ws_ref[:, off + used:off + n] = jnp.zeros((ws_ref.shape[0], n - used), ws_ref.dtype)
            off += n

    h = (_rms(x_ref[...]) * g_ref[...]) * (1.0 + sc_ref[0]) + sh_ref[0]
    hb = h.astype(BF16)
    off = 0
    for o_ref in o_refs:
        n = o_ref.shape[1]
        o_ref[...] = jnp.dot(hb, ws_ref[:, off:off + n], preferred_element_type=F32).astype(o_ref.dtype)
        off += n


def _inproj(x2, sc, sh, g, w_bf, sections, widths, out_dtypes, seq):
    t, d = x2.shape
    tm = min(TOKEN_TILE, seq)
    per_batch = seq // tm
    assert all(n % LANES == 0 for n in widths)
    mod_spec = pl.BlockSpec((1, 1, d), lambda i: (i // per_batch, 0, 0))
    kern = functools.partial(_inproj_kernel, sections=sections)
    return pl.pallas_call(
        kern,
        out_shape=[jax.ShapeDtypeStruct((t, n), dt) for n, dt in zip(widths, out_dtypes)],
        grid=(t // tm,),
        in_specs=[pl.BlockSpec((tm, d), lambda i: (i, 0)), mod_spec, mod_spec, _const_spec((1, d)),
                  pl.BlockSpec(w_bf.shape, lambda i: (0, 0), pipeline_mode=pl.Buffered(1))],
        out_specs=[pl.BlockSpec((tm, n), lambda i: (i, 0)) for n in widths],
        scratch_shapes=[pltpu.VMEM((d, sum(widths)), BF16)],
        compiler_params=_params("arbitrary"),
        name="inproj",
    )(x2, sc, sh, g, w_bf)


def _ssd_kernel(tail_ref, xbc_ref, z_ref, sm_ref, cw_ref, cb_ref, dtb_r_ref, dtb_c_ref, alog_r_ref, alog_c_ref,
                dsk_ref, g_ref, y_ref, st_ref, *, n_heads, d_ssm):
    L = SSM_CHUNK
    P = SSM_HEAD_DIM
    N = SSM_STATE
    G = SSM_GROUPS
    gw = d_ssm // G
    R = xbc_ref.shape[0]
    c = pl.program_id(1)

    @pl.when(c == 0)
    def _():
        st_ref[...] = jnp.zeros(st_ref.shape, F32)

    tail = tail_ref[...]
    tail = jnp.where(c == 0, jnp.zeros_like(tail), tail)
    conv_in = jnp.concatenate([tail, xbc_ref[...]], axis=0)

    ri = lax.broadcasted_iota(I32, (L, L), 0)
    ci = lax.broadcasted_iota(I32, (L, L), 1)
    causal = ri >= ci
    causal3 = jnp.concatenate([causal.astype(BF16)] * 3, axis=1)
    upper3 = jnp.concatenate([(ri <= ci).astype(BF16)] * 3, axis=0)
    wr = lax.broadcasted_iota(I32, ((SSM_CONV - 1) * L, CONV_CARRY + L), 0)
    wc = lax.broadcasted_iota(I32, ((SSM_CONV - 1) * L, CONV_CARRY + L), 1)
    shifts = (wc == (wr % L) + CONV_CARRY - (SSM_CONV - 1) + wr // L).astype(BF16)
    hh = lax.broadcasted_iota(I32, (n_heads, d_ssm), 0)
    jj = lax.broadcasted_iota(I32, (n_heads, d_ssm), 1)
    expand = ((jj // P) == hh).astype(BF16)
    expand2 = jnp.concatenate([expand, expand], axis=0)
    lane = lax.broadcasted_iota(I32, (L, LANES), 1)
    first_half = lane < P
    heads_per_group = n_heads // G
    a_row = -jnp.exp(alog_r_ref[...])
    a_col = -jnp.exp(alog_c_ref[...])

    for s in range(R // L):
        rows = slice(s * L, (s + 1) * L)
        window = conv_in[s * L:s * L + CONV_CARRY + L, :]
        taps = jnp.dot(shifts, window, preferred_element_type=F32)
        acc = cb_ref[...] + cw_ref[SSM_CONV - 1:SSM_CONV, :] * window[CONV_CARRY:, :].astype(F32)
        for j in range(SSM_CONV - 1):
            acc = acc + cw_ref[j:j + 1, :] * taps[j * L:(j + 1) * L, :]
        act = _silu(acc)
        xs = act[:, :d_ssm]
        bm = act[:, d_ssm:d_ssm + G * N].astype(BF16)
        cm = act[:, d_ssm + G * N:].astype(BF16)

        sm = sm_ref[rows, :]
        dt_col = _softplus(sm[:, :n_heads] + dtb_r_ref[...])
        dt_row = _softplus(sm.T[:n_heads, :] + dtb_c_ref[...])
        cs_col = jnp.dot(causal3, jnp.concatenate(_pieces(dt_col * a_row, 3), axis=0),
                         preferred_element_type=F32)
        cs_row = jnp.dot(jnp.concatenate(_pieces(dt_row * a_col, 3), axis=1), upper3,
                         preferred_element_type=F32)
        cs_last = cs_col[L - 1:L, :]

        per_head = jnp.concatenate([dt_col, jnp.exp(cs_col), jnp.exp(cs_last - cs_col)], axis=0)
        per_ch = jnp.dot(jnp.concatenate(_pieces(per_head, 2), axis=1), expand2,
                         preferred_element_type=F32)
        dt_e, ecs_e, dte_e = per_ch[0:L], per_ch[L:2 * L], per_ch[2 * L:3 * L]
        chunk_decay = ecs_e[L - 1:L, :]

        xdt = xs * dt_e
        xdt_b = xdt.astype(BF16)
        xdec_b = (xdt * dte_e).astype(BF16)

        y_parts = []
        y_off_parts = []
        for g in range(G):
            bm_g = bm[:, g * N:(g + 1) * N]
            cm_g = cm[:, g * N:(g + 1) * N]
            cb = lax.dot_general(cm_g, bm_g, _NT, preferred_element_type=F32)
            prev = st_ref[g]
            y_off_parts.append(jnp.dot(cm_g, prev.astype(BF16), preferred_element_type=F32))
            s_new = lax.dot_general(bm_g, xdec_b[:, g * gw:(g + 1) * gw], _TN, preferred_element_type=F32)
            st_ref[g] = prev * chunk_decay[:, g * gw:(g + 1) * gw] + s_new
            for p in range(heads_per_group // 2):
                h0 = g * heads_per_group + 2 * p
                ms = []
                for h in (h0, h0 + 1):
                    diff = cs_col[:, h:h + 1] - cs_row[h:h + 1, :]
                    ms.append((cb * jnp.exp(jnp.where(causal, diff, -jnp.inf))).astype(BF16))
                lhs = jnp.concatenate(ms, axis=1)
                xp = xdt_b[:, h0 * P:(h0 + 2) * P]
                zero = jnp.zeros_like(xp)
                rhs = jnp.concatenate([jnp.where(first_half, xp, zero), jnp.where(first_half, zero, xp)], axis=0)
                y_parts.append(jnp.dot(lhs, rhs, preferred_element_type=F32))
        y = jnp.concatenate(y_parts, axis=1) + jnp.concatenate(y_off_parts, axis=1) * ecs_e + dsk_ref[...] * xs
        y = y * _silu(z_ref[rows, :].astype(F32))
        y = jnp.concatenate([_rms(y[:, g * gw:(g + 1) * gw]) for g in range(G)], axis=1) * g_ref[...]
        y_ref[rows, :] = y.astype(y_ref.dtype)


def _ssd(xbc, z, small, conv_w, conv_b, dt_bias, a_log, d_skip_e, norm_g, batch, seq):
    t, cd = xbc.shape
    d_ssm = z.shape[1]
    n_heads = dt_bias.shape[0]
    L = min(SSD_STEP_ROWS, seq)
    nc = seq // L
    row = lambda b, c: (b * nc + c, 0)
    tail = lambda b, c: (jnp.maximum((b * nc + c) * (L // CONV_CARRY) - 1, 0), 0)
    kern = functools.partial(_ssd_kernel, n_heads=n_heads, d_ssm=d_ssm)
    return pl.pallas_call(
        kern,
        out_shape=jax.ShapeDtypeStruct((t, d_ssm), BF16),
        grid=(batch, nc),
        in_specs=[pl.BlockSpec((CONV_CARRY, cd), tail),
                  pl.BlockSpec((L, cd), row), pl.BlockSpec((L, d_ssm), row), pl.BlockSpec((L, LANES), row),
                  _const_spec(conv_w.shape), _const_spec((1, cd)),
                  _const_spec((1, n_heads)), _const_spec((n_heads, 1)),
                  _const_spec((1, n_heads)), _const_spec((n_heads, 1)),
                  _const_spec((1, d_ssm)), _const_spec((1, d_ssm))],
        out_specs=pl.BlockSpec((L, d_ssm), row),
        scratch_shapes=[pltpu.VMEM((SSM_GROUPS, SSM_STATE, d_ssm // SSM_GROUPS), F32)],
        compiler_params=_params("arbitrary", "arbitrary"),
        name="ssd",
    )(xbc, xbc, z, small, conv_w, conv_b.reshape(1, cd), dt_bias.reshape(1, n_heads), dt_bias.reshape(n_heads, 1),
      a_log.reshape(1, n_heads), a_log.reshape(n_heads, 1), d_skip_e, norm_g.reshape(1, d_ssm))


def _gla_kernel(q_ref, k_ref, v_ref, r_ref, sm_ref, wg2_ref, bg_ref, gn_ref, o_ref, st_ref, *, gate_col):
    L = GLA_CHUNK
    H = GLA_HEADS
    dk = q_ref.shape[1] // H
    dv = v_ref.shape[1] // H
    c = pl.program_id(1)

    @pl.when(c == 0)
    def _():
        st_ref[...] = jnp.zeros(st_ref.shape, F32)

    R = min(GLA_GROUP_ROWS, q_ref.shape[0])
    n_chunks = R // L
    ri = lax.broadcasted_iota(I32, (R, R), 0)
    ci = lax.broadcasted_iota(I32, (R, R), 1)
    same_chunk = (ri // L) == (ci // L)
    causal = jnp.logical_and(same_chunk, ri >= ci)
    tril = causal.astype(BF16)
    later = jnp.logical_and(same_chunk, ri < ci).astype(BF16)
    sr = lax.broadcasted_iota(I32, (R, n_chunks * LANES), 0)
    sc = lax.broadcasted_iota(I32, (R, n_chunks * LANES), 1)
    last_rows = (sr == (sc // LANES) * L + (L - 1)).astype(BF16)
    wg2_hi, wg2_lo = _pieces(wg2_ref[...], 2)

    tril3 = jnp.concatenate([jnp.concatenate([tril] * 3, axis=1), jnp.concatenate([later] * 3, axis=1)], axis=0)
    last3 = jnp.concatenate([last_rows] * 3, axis=0)
    wg3 = jnp.concatenate([wg2_hi, wg2_hi, wg2_lo], axis=0)

    for gi in range(q_ref.shape[0] // R):
        rs = slice(gi * R, (gi + 1) * R)
        q = q_ref[rs, :].astype(F32) * (dk ** -0.5)
        k = k_ref[rs, :].astype(F32)
        v = v_ref[rs, :]
        r = r_ref[rs, :].astype(F32)
        g_hi, g_lo = _pieces(sm_ref[rs, gate_col:gate_col + GLA_GATE_RANK], 2)
        pre = jnp.dot(jnp.concatenate([g_hi, g_lo, g_hi], axis=1), wg3, preferred_element_type=F32) + bg_ref[...]
        gk3 = jnp.concatenate(_pieces(_log_sigmoid(pre) / GLA_GATE_NORM, 3), axis=0)
        sums = jnp.dot(tril3, gk3, preferred_element_type=F32)
        bcum = sums[:R]
        to_end = sums[R:]
        q_t = (q * jnp.exp(bcum)).astype(BF16)
        k_t = (k * jnp.exp(-bcum)).astype(BF16)
        k_dec = (k * jnp.exp(to_end)).astype(BF16)
        dcol = jnp.exp(lax.dot_general(jnp.concatenate(_pieces(bcum, 3), axis=0), last3, _TN,
                                       preferred_element_type=F32))
        outs = []
        for h in range(H):
            ks = slice(h * dk, (h + 1) * dk)
            vs = slice(h * dv, (h + 1) * dv)
            att = lax.dot_general(q_t[:, ks], k_t[:, ks], _NT, preferred_element_type=F32)
            att = jnp.where(causal, att, 0.0).astype(BF16)
            o = jnp.dot(att, v[:, vs], preferred_element_type=F32)
            state = st_ref[h]
            inter = []
            for c in range(n_chunks):
                rows = slice(c * L, (c + 1) * L)
                inter.append(jnp.dot(q_t[rows, ks], state.astype(BF16), preferred_element_type=F32))
                s_new = lax.dot_general(k_dec[rows, ks], v[rows, vs], _TN, preferred_element_type=F32)
                dec = dcol[ks, c * LANES:(c + 1) * LANES]
                state = state * jnp.concatenate([dec] * (dv // LANES), axis=1) + s_new
            st_ref[h] = state
            o = o + jnp.concatenate(inter, axis=0)
            outs.append(_rms(o) * gn_ref[...] * _silu(r[:, vs]))
        o_ref[rs, :] = jnp.concatenate(outs, axis=1).astype(o_ref.dtype)


def _gla(q, k, v, r, small, wg2, bg, norm_g, batch, seq, gate_col):
    t, dkt = q.shape
    dvt = v.shape[1]
    rows = min(GLA_STEP_ROWS, seq)
    nc = seq // rows
    row = lambda b, c: (b * nc + c, 0)
    kern = functools.partial(_gla_kernel, gate_col=gate_col)
    return pl.pallas_call(
        kern,
        out_shape=jax.ShapeDtypeStruct((t, dvt), BF16),
        grid=(batch, nc),
        in_specs=[pl.BlockSpec((rows, dkt), row), pl.BlockSpec((rows, dkt), row), pl.BlockSpec((rows, dvt), row),
                  pl.BlockSpec((rows, dvt), row), pl.BlockSpec((rows, LANES), row),
                  _const_spec(wg2.shape), _const_spec((1, dkt)), _const_spec((1, dvt // GLA_HEADS))],
        out_specs=pl.BlockSpec((rows, dvt), row),
        scratch_shapes=[pltpu.VMEM((GLA_HEADS, dkt // GLA_HEADS, dvt // GLA_HEADS), F32)],
        compiler_params=_params("arbitrary", "arbitrary"),
        name="gla",
    )(q, k, v, r, small, wg2, bg.reshape(1, dkt), norm_g.reshape(1, dvt // GLA_HEADS))


def _outproj_kernel(y_ref, o_ref, x_ref, gt_ref, sc_ref, sh_ref, g_ref, wy_ref, wo_ref, wr_ref, br_ref,
                    x1_ref, h_ref, ti_ref, tw_ref, cnt_ref):
    mix = (jnp.dot(y_ref[...], wy_ref[...], preferred_element_type=F32)
           + jnp.dot(o_ref[...], wo_ref[...], preferred_element_type=F32))
    x1 = x_ref[...] + gt_ref[0] * mix
    x1_ref[...] = x1
    h = (_rms(x1) * g_ref[...]) * (1.0 + sc_ref[0]) + sh_ref[0]
    _token_rows_store(h_ref, h)
    n_e = br_ref.shape[0]
    h_hi, h_lo = _pieces(h, 2)
    wr = wr_ref[...]
    hw = lax.dot_general(wr, h_hi, _NT, preferred_element_type=F32)
    logits = (hw[:n_e] + hw[n_e:] + lax.dot_general(wr[:n_e], h_lo, _NT, preferred_element_type=F32)) + br_ref[...]
    expert = lax.broadcasted_iota(I32, logits.shape, 0)
    vals, idxs = [], []
    counts = jnp.zeros(logits.shape, F32)
    for _ in range(TOP_K):
        m = jnp.max(logits, axis=0, keepdims=True)
        idx = jnp.min(jnp.where(logits == m, expert, n_e), axis=0, keepdims=True)
        vals.append(m)
        idxs.append(idx)
        chosen = expert == idx
        counts = counts + chosen.astype(F32)
        logits = jnp.where(chosen, -jnp.inf, logits)
    exps = [jnp.exp(v - vals[0]) for v in vals]
    denom = functools.reduce(lambda a, b: a + b, exps)
    ti_ref[...] = jnp.concatenate(idxs, axis=0)
    tw_ref[...] = jnp.concatenate([e / denom for e in exps], axis=0)

    @pl.when(pl.program_id(0) == 0)
    def _():
        cnt_ref[...] = jnp.zeros(cnt_ref.shape, F32)

    cnt_ref[...] = cnt_ref[...] + jnp.sum(counts, axis=1, keepdims=True)


def _outproj(y, o, x2, gt, sc, sh, g, wy, wo, w_router, b_router, seq):
    t, d = x2.shape
    n_e = w_router.shape[1]
    tm = min(TOKEN_TILE, seq)
    per_batch = seq // tm
    row = lambda i: (i, 0)
    mod_spec = pl.BlockSpec((1, 1, d), lambda i: (i // per_batch, 0, 0))
    wr_t = w_router.T
    wr_hi = wr_t.astype(BF16)
    wr_cat = jnp.concatenate([wr_hi, (wr_t - wr_hi.astype(F32)).astype(BF16)], axis=0)
    col = lambda i: (0, i)
    return pl.pallas_call(
        _outproj_kernel,
        out_shape=[jax.ShapeDtypeStruct((t, d), F32), jax.ShapeDtypeStruct((t * SUBLANES, LANES), F32),
                   jax.ShapeDtypeStruct((TOP_K, t), I32), jax.ShapeDtypeStruct((TOP_K, t), F32),
                   jax.ShapeDtypeStruct((n_e, LANES), F32)],
        grid=(t // tm,),
        in_specs=[pl.BlockSpec((tm, y.shape[1]), row), pl.BlockSpec((tm, o.shape[1]), row),
                  pl.BlockSpec((tm, d), row), mod_spec, mod_spec, mod_spec, _const_spec((1, d)),
                  _const_spec(wy.shape), _const_spec(wo.shape), _const_spec(wr_cat.shape),
                  _const_spec((n_e, 1))],
        out_specs=[pl.BlockSpec((tm, d), row), pl.BlockSpec((tm * SUBLANES, LANES), row),
                   pl.BlockSpec((TOP_K, tm), col), pl.BlockSpec((TOP_K, tm), col),
                   _const_spec((n_e, LANES))],
        compiler_params=_params("arbitrary"),
        name="outproj",
    )(y, o, x2, gt, sc, sh, g, wy, wo, wr_cat, b_router.reshape(n_e, 1))


def _route_kernel(ti_ref, cnt_ref, dest_ref, be_ref, pend_ref, run_ref, *, n_blocks_pad):
    i = pl.program_id(0)
    n_e = cnt_ref.shape[0]
    tr = ti_ref.shape[1]

    @pl.when(i == 0)
    def _():
        counts = cnt_ref[...]
        padded = jnp.ceil(counts / EXPERT_BLOCK) * EXPERT_BLOCK
        ri = lax.broadcasted_iota(I32, (n_e, n_e), 0)
        ci = lax.broadcasted_iota(I32, (n_e, n_e), 1)
        pend = jnp.dot((ri >= ci).astype(F32), padded, precision=HIGHEST, preferred_element_type=F32)
        pend_ref[...] = pend
        run_ref[...] = pend - padded
        start = (lax.broadcasted_iota(I32, (n_e, n_blocks_pad), 1) * EXPERT_BLOCK).astype(F32)
        be = jnp.sum((pend[:, 0:1] <= start).astype(F32), axis=0, keepdims=True)
        be_ref[...] = jnp.minimum(be, n_e - 1).astype(I32)

    ti = ti_ref[...]
    expert = lax.broadcasted_iota(I32, (n_e, tr), 0)
    onehots = [expert == ti[k:k + 1, :] for k in range(TOP_K)]
    cnt = functools.reduce(lambda a, b: a + b, [oh.astype(F32) for oh in onehots])
    ri = lax.broadcasted_iota(I32, (tr, tr), 0)
    ci = lax.broadcasted_iota(I32, (tr, tr), 1)
    before = jnp.dot(cnt.astype(BF16), (ri < ci).astype(BF16), preferred_element_type=F32)
    base = run_ref[:, 0:1] + before
    dest = [jnp.sum(jnp.where(oh, base, 0.0), axis=0, keepdims=True) for oh in onehots]
    dest_ref[...] = jnp.concatenate(dest, axis=0).astype(I32)
    run_ref[...] = run_ref[...] + jnp.sum(cnt, axis=1, keepdims=True)


def _route(topi_t, counts, n_blocks):
    t = topi_t.shape[1]
    n_e = counts.shape[0]
    tr = min(ROUTE_TILE, t)
    n_blocks_pad = -(-n_blocks // LANES) * LANES
    kern = functools.partial(_route_kernel, n_blocks_pad=n_blocks_pad)
    return pl.pallas_call(
        kern,
        out_shape=[jax.ShapeDtypeStruct((TOP_K, t), I32), jax.ShapeDtypeStruct((1, n_blocks_pad), I32),
                   jax.ShapeDtypeStruct((n_e, LANES), F32)],
        grid=(t // tr,),
        in_specs=[pl.BlockSpec((TOP_K, tr), lambda i: (0, i)), _const_spec((n_e, LANES))],
        out_specs=[pl.BlockSpec((TOP_K, tr), lambda i: (0, i)), _const_spec((1, n_blocks_pad)),
                   _const_spec((n_e, LANES))],
        scratch_shapes=[pltpu.VMEM((n_e, LANES), F32)],
        compiler_params=_params("arbitrary"),
        name="route",
    )(topi_t, counts)


def _dispatch_kernel(pend_ref, dest_hbm, h_ref, xs_hbm, idx_ref, zero_ref, idx_sem, row_sem, *, n_experts):
    i = pl.program_id(0)
    tg = idx_ref.shape[1]

    @pl.when(i == 0)
    def _():
        zero_ref[...] = jnp.zeros(zero_ref.shape, zero_ref.dtype)
        for e in range(n_experts):
            end = pend_ref[e]
            prev = pend_ref[e - 1] if e > 0 else 0

            @pl.when(end > prev)
            def _():
                start = pl.multiple_of((end - EXPERT_BLOCK) * SUBLANES, EXPERT_BLOCK * SUBLANES)
                cp = pltpu.make_async_copy(zero_ref, xs_hbm.at[pl.ds(start, EXPERT_BLOCK * SUBLANES)], row_sem)
                cp.start()
                cp.wait()

        n_blocks = xs_hbm.shape[0] // (EXPERT_BLOCK * SUBLANES)
        total = pend_ref[n_experts - 1]
        for b in range(n_blocks - n_experts, n_blocks):
            @pl.when(b * EXPERT_BLOCK >= total)
            def _():
                cp = pltpu.make_async_copy(
                    zero_ref, xs_hbm.at[pl.ds(b * EXPERT_BLOCK * SUBLANES, EXPERT_BLOCK * SUBLANES)], row_sem)
                cp.start()
                cp.wait()

    n = pl.num_programs(0)

    def idx_copy(tile, s):
        return pltpu.make_async_copy(dest_hbm.at[:, pl.ds(pl.multiple_of(tile * tg, tg), tg)],
                                     idx_ref.at[pl.ds(s * TOP_K, TOP_K)], idx_sem.at[s])

    @pl.when(i == 0)
    def _():
        idx_copy(0, 0).start()

    def step(slot):
        idx_copy(i, slot).wait()

        @pl.when(i + 1 < n)
        def _():
            idx_copy(i + 1, 1 - slot).start()

        def issue(tl, carry):
            src = h_ref.at[pl.ds(pl.multiple_of(tl * SUBLANES, SUBLANES), SUBLANES)]
            for k in range(TOP_K):
                d = pl.multiple_of(idx_ref[slot * TOP_K + k, tl] * SUBLANES, SUBLANES)
                pltpu.make_async_copy(src, xs_hbm.at[pl.ds(d, SUBLANES)], row_sem).start(priority=k % 2)
            return carry

        lax.fori_loop(0, tg, issue, 0)

    for parity in range(2):
        pl.when(lax.rem(i, 2) == parity)(functools.partial(step, parity))
    for _ in range(TOP_K):
        pltpu.make_async_copy(h_ref, xs_hbm.at[pl.ds(0, tg * SUBLANES)], row_sem).wait()


def _dispatch(pend_i, dest_t, h, n_rows, n_experts):
    t = h.shape[0] // SUBLANES
    tg = min(DISPATCH_TILE, t)
    kern = functools.partial(_dispatch_kernel, n_experts=n_experts)
    return pl.pallas_call(
        kern,
        out_shape=jax.ShapeDtypeStruct((n_rows * SUBLANES, LANES), h.dtype),
        grid_spec=pltpu.PrefetchScalarGridSpec(
            num_scalar_prefetch=1,
            grid=(t // tg,),
            in_specs=[pl.BlockSpec(memory_space=pl.ANY),
                      pl.BlockSpec((tg * SUBLANES, LANES), lambda i, pend: (i, 0))],
            out_specs=pl.BlockSpec(memory_space=pl.ANY),
            scratch_shapes=[pltpu.SMEM((2 * TOP_K, tg), I32), pltpu.VMEM((EXPERT_BLOCK * SUBLANES, LANES), h.dtype),
                            pltpu.SemaphoreType.DMA((2,)), pltpu.SemaphoreType.DMA]),
        compiler_params=pltpu.CompilerParams(dimension_semantics=("arbitrary",), has_side_effects=True,
                                             vmem_limit_bytes=VMEM_LIMIT_BYTES),
        name="dispatch",
    )(pend_i, dest_t, h)


def _expert_kernel(be_ref, nu_ref, pend_ref, xs_ref, wg_hbm, bg_ref, wu_hbm, bu_ref, wd_hbm, bd_ref, y_ref,
                   wg_f, wu_f, wd_f, wg_b, wu_b, wd_b, slot_ref, sems):
    i = pl.program_id(0)
    used = i < nu_ref[0]
    e = be_ref[i]

    def fetch(expert, slot):
        return [pltpu.make_async_copy(src.at[expert], dst.at[slot], sems.at[slot])
                for src, dst in ((wg_hbm, wg_f), (wu_hbm, wu_f), (wd_hbm, wd_f))]

    @pl.when(i == 0)
    def _():
        slot_ref[0] = 0
        for cp in fetch(e, 0):
            cp.start()

    first_of_expert = jnp.logical_or(i == 0, e != be_ref[jnp.maximum(i - 1, 0)])

    @pl.when(jnp.logical_and(used, first_of_expert))
    def _():
        slot = slot_ref[0]
        for cp in fetch(e, slot):
            cp.wait()
        wg_b[...] = wg_f[slot].astype(BF16)
        wu_b[...] = wu_f[slot].astype(BF16)
        wd_b[...] = wd_f[slot].astype(BF16)
        nxt = lax.div(pend_ref[e], EXPERT_BLOCK)

        @pl.when(nxt < nu_ref[0])
        def _():
            for cp in fetch(be_ref[nxt], 1 - slot):
                cp.start()

        slot_ref[0] = 1 - slot

    def ffn(rows):
        x = _token_rows_load(xs_ref, rows).astype(BF16)
        gate = jnp.minimum(jnp.dot(x, wg_b[...], preferred_element_type=F32) + bg_ref[...], SWIGLU_LIMIT)
        up = jnp.clip(jnp.dot(x, wu_b[...], preferred_element_type=F32) + bu_ref[...],
                      -SWIGLU_LIMIT, SWIGLU_LIMIT)
        glu = gate * _sigmoid(SWIGLU_ALPHA * gate)
        mid = ((up + 1.0) * glu).astype(BF16)
        y = jnp.dot(mid, wd_b[...], preferred_element_type=F32) + bd_ref[...]
        _token_rows_store(y_ref, y)
        if rows < EXPERT_BLOCK:
            y_ref[rows * SUBLANES:, :] = jnp.zeros(((EXPERT_BLOCK - rows) * SUBLANES, LANES), y_ref.dtype)

    valid = pend_ref[pend_ref.shape[0] // 2 + e] - i * EXPERT_BLOCK
    quarter = EXPERT_BLOCK // EXPERT_PATHS
    for p in range(1, EXPERT_PATHS + 1):
        covers = valid <= p * quarter if p < EXPERT_PATHS else True
        needs = valid > (p - 1) * quarter if p > 1 else True
        pl.when(jnp.logical_and(used, jnp.logical_and(covers, needs)))(functools.partial(ffn, p * quarter))

    @pl.when(jnp.logical_not(used))
    def _():
        y_ref[...] = jnp.zeros(y_ref.shape, y_ref.dtype)


def _experts(block_e, n_used, pend_i, xs, w_gate, b_gate, w_up, b_up, w_down, b_down):
    n_rows = xs.shape[0] // SUBLANES
    n_e, d, f = w_gate.shape
    nb = n_rows // EXPERT_BLOCK
    blk = (EXPERT_BLOCK * SUBLANES, LANES)
    last = lambda i, be, nu, pend: jnp.maximum(jnp.minimum(i, nu[0] - 1), 0)
    bspec = lambda n: pl.BlockSpec((None, 1, n), lambda i, be, nu, pend: (be[last(i, be, nu, pend)], 0, 0))
    hbm = pl.BlockSpec(memory_space=pl.ANY)
    return pl.pallas_call(
        _expert_kernel,
        out_shape=jax.ShapeDtypeStruct((n_rows * SUBLANES, LANES), F32),
        grid_spec=pltpu.PrefetchScalarGridSpec(
            num_scalar_prefetch=3,
            grid=(nb,),
            in_specs=[pl.BlockSpec(blk, lambda i, be, nu, pend: (last(i, be, nu, pend), 0)),
                      hbm, bspec(f), hbm, bspec(f), hbm, bspec(d)],
            out_specs=pl.BlockSpec(blk, lambda i, be, nu, pend: (i, 0)),
            scratch_shapes=[pltpu.VMEM((2, d, f), F32), pltpu.VMEM((2, d, f), F32), pltpu.VMEM((2, f, d), F32),
                            pltpu.VMEM((d, f), BF16), pltpu.VMEM((d, f), BF16), pltpu.VMEM((f, d), BF16),
                            pltpu.SMEM((1,), I32), pltpu.SemaphoreType.DMA((2,))]),
        compiler_params=_params("arbitrary"),
        name="experts",
    )(block_e, n_used, pend_i, xs, w_gate, b_gate.reshape(n_e, 1, f), w_up, b_up.reshape(n_e, 1, f),
      w_down, b_down.reshape(n_e, 1, d))


def _combine_kernel(dest_hbm, ys_hbm, tw_ref, x1_ref, gt_ref, g_ref, o_ref, idx_ref, buf_ref, idx_sem, row_sem):
    i = pl.program_id(0)
    n = pl.num_programs(0)
    tc = idx_ref.shape[1]

    def idx_copy(tile, s):
        return pltpu.make_async_copy(dest_hbm.at[:, pl.ds(pl.multiple_of(tile * tc, tc), tc)],
                                     idx_ref.at[pl.ds(s * TOP_K, TOP_K)], idx_sem.at[s])

    def issue_rows(s):
        def issue(tl, carry):
            dst_row = pl.multiple_of(tl * SUBLANES, SUBLANES)
            for k in range(TOP_K):
                d = pl.multiple_of(idx_ref[s * TOP_K + k, tl] * SUBLANES, SUBLANES)
                pltpu.make_async_copy(ys_hbm.at[pl.ds(d, SUBLANES)], buf_ref.at[s, k, pl.ds(dst_row, SUBLANES)],
                                      row_sem.at[s]).start(priority=k % 2)
            return carry

        lax.fori_loop(0, tc, issue, 0)

    @pl.when(i == 0)
    def _():
        idx_copy(0, 0).start()
        idx_copy(0, 0).wait()
        issue_rows(0)

        @pl.when(n > 1)
        def _():
            idx_copy(1, 1).start()

    def step(slot):
        @pl.when(i + 1 < n)
        def _():
            idx_copy(i + 1, 1 - slot).wait()
            issue_rows(1 - slot)

            @pl.when(i + 2 < n)
            def _():
                idx_copy(i + 2, slot).start()

        for k in range(TOP_K):
            pltpu.make_async_copy(ys_hbm.at[pl.ds(0, tc * SUBLANES)], buf_ref.at[slot, k], row_sem.at[slot]).wait()

        tw = tw_ref[...]
        ffn = tw[:, 0:1] * _token_rows_load(buf_ref.at[slot, 0], tc)
        for k in range(1, TOP_K):
            ffn = ffn + tw[:, k:k + 1] * _token_rows_load(buf_ref.at[slot, k], tc)
        x2 = x1_ref[...] + gt_ref[0] * ffn
        o_ref[...] = _rms(x2) * g_ref[...]

    for parity in range(2):
        pl.when(lax.rem(i, 2) == parity)(functools.partial(step, parity))


def _combine(dest_t, ys, topw, x1, gt, g, seq):
    t, d = x1.shape
    tc = min(COMBINE_TILE, seq)
    per_batch = seq // tc
    row = lambda i: (i, 0)
    return pl.pallas_call(
        _combine_kernel,
        out_shape=jax.ShapeDtypeStruct((t, d), F32),
        grid=(t // tc,),
        in_specs=[pl.BlockSpec(memory_space=pl.ANY), pl.BlockSpec(memory_space=pl.ANY),
                  pl.BlockSpec((tc, TOP_K), row), pl.BlockSpec((tc, d), row),
                  pl.BlockSpec((1, 1, d), lambda i: (i // per_batch, 0, 0)), _const_spec((1, d))],
        out_specs=pl.BlockSpec((tc, d), row),
        scratch_shapes=[pltpu.SMEM((2 * TOP_K, tc), I32), pltpu.VMEM((2, TOP_K, tc * SUBLANES, LANES), F32),
                        pltpu.SemaphoreType.DMA((2,)), pltpu.SemaphoreType.DMA((2,))],
        compiler_params=_params("arbitrary"),
        name="combine",
    )(dest_t, ys, topw, x1, gt, g)


def _layer(x2, mod, batch, seq, norm1_g, w_in, conv_w, conv_b, dt_bias, a_log, d_skip, ssm_norm_g,
           gla_wg2, gla_bg, gla_norm_g, w_out, norm2_g, w_router, b_router,
           w_gate, b_gate, w_up, b_up, w_down, b_down):
    t, d = x2.shape
    n_heads = dt_bias.shape[0]
    d_ssm = n_heads * SSM_HEAD_DIM
    cd = conv_w.shape[1]
    dkt = gla_wg2.shape[1]
    dvt = w_out.shape[0] - d_ssm
    n_experts = w_router.shape[1]

    sh1, sc1, gt1, sh2, sc2, gt2 = [m.reshape(batch, 1, d) for m in jnp.split(mod[:batch], 6, axis=1)]

    sizes = (d_ssm, cd, n_heads, dkt, dkt, dvt, GLA_GATE_RANK, dvt)
    offs = [0]
    for s in sizes:
        offs.append(offs[-1] + s)
    piece = lambda j: (offs[j], sizes[j])
    sections = ((piece(0),), (piece(1),), (piece(3),), (piece(4),), (piece(5),), (piece(7),), (piece(2), piece(6)))
    z, xbc, q, k, v, r, small = _inproj(x2, sc1, sh1, norm1_g.reshape(1, d), w_in.astype(BF16), sections,
                                        (d_ssm, cd, dkt, dkt, dvt, dvt, LANES), [BF16] * 6 + [F32], seq)

    d_skip_e = jnp.repeat(d_skip, SSM_HEAD_DIM).reshape(1, d_ssm)
    y = _ssd(xbc, z, small, conv_w, conv_b, dt_bias, a_log, d_skip_e, ssm_norm_g, batch, seq)
    o = _gla(q, k, v, r, small, gla_wg2, gla_bg, gla_norm_g, batch, seq, gate_col=n_heads)

    x1, h2, topi_t, topw_t, counts = _outproj(y, o, x2, gt1, sc2, sh2, norm2_g.reshape(1, d),
                                        w_out[:d_ssm].astype(BF16), w_out[d_ssm:].astype(BF16),
                                        w_router, b_router, seq)

    n_blocks = (t * TOP_K) // EXPERT_BLOCK + n_experts
    dest_t, block_e, pend = _route(topi_t, counts, n_blocks)
    pend_i = pend[:, 0].astype(I32)
    n_used = (pend_i[n_experts - 1:] // EXPERT_BLOCK).astype(I32)
    topw = topw_t.T
    xs = _dispatch(pend_i, dest_t, h2, n_blocks * EXPERT_BLOCK, n_experts)
    real_end = jnp.concatenate([jnp.zeros((1,), I32), pend_i[:-1]]) + counts[:, 0].astype(I32)
    seg_ends = jnp.concatenate([pend_i, real_end])
    ys = _experts(block_e[0, :n_blocks], n_used, seg_ends, xs, w_gate, b_gate, w_up, b_up, w_down, b_down)
    return dest_t, ys, topw, x1, gt2


def kernel(x, c, w_ada, b_ada, norm1_g, w_in, conv_w, conv_b, dt_bias, a_log, d_skip, ssm_norm_g, gla_wg2,
           gla_bg, gla_norm_g, w_out, norm2_g, w_router, b_router, w_gate, b_gate, w_up, b_up, w_down, b_down,
           final_norm_g):
    batch, seq, d = x.shape
    assert w_ada.shape[0] == 1, "single-layer trunk"
    assert d == SUBLANES * LANES, "token rows are moved as one (8, 128) f32 tile each"
    assert seq % min(seq, max(TOKEN_TILE, SSM_CHUNK, GLA_STEP_ROWS, COMBINE_TILE, DISPATCH_TILE)) == 0
    assert seq % max(SSM_CHUNK, GLA_STEP_ROWS) == 0
    x2 = x.reshape(batch * seq, d)
    c_pad = jnp.zeros((SUBLANES, d), F32).at[:batch].set(c)
    mod = _ada(c_pad, w_ada[0], b_ada)
    dest_t, ys, topw, x1, gt2 = _layer(
        x2, mod, batch, seq, norm1_g[0], w_in[0], conv_w[0], conv_b[0], dt_bias[0], a_log[0], d_skip[0],
        ssm_norm_g[0], gla_wg2[0], gla_bg[0], gla_norm_g[0], w_out[0], norm2_g[0], w_router[0], b_router[0],
        w_gate[0], b_gate[0], w_up[0], b_up[0], w_down[0], b_down[0])
    out = _combine(dest_t, ys, topw, x1, gt2, final_norm_g.reshape(1, d), seq)
    return out.reshape(batch, seq, d)
```

```python
import functools

import jax
import jax.numpy as jnp
from jax import lax
from jax.experimental import pallas as pl
from jax.experimental.pallas import tpu as pltpu

F32 = jnp.float32
BF16 = jnp.bfloat16
I32 = jnp.int32
HIGHEST = lax.Precision.HIGHEST

EPS = 1e-6
SSM_HEAD_DIM = 64
SSM_GROUPS = 2
SSM_STATE = 128
SSM_CONV = 4
SSM_CHUNK = 128
GLA_HEADS = 4
GLA_GATE_RANK = 16
GLA_GATE_NORM = 16.0
GLA_CHUNK = 64
TOP_K = 4
SWIGLU_LIMIT = 7.0
SWIGLU_ALPHA = 1.702

LANES = 128
SUBLANES = 8
VMEM_LIMIT_BYTES = 56 * 1024 * 1024

TOKEN_TILE = 512
SSD_STEP_ROWS = 512
CONV_CARRY = 16
GLA_STEP_ROWS = 512
GLA_GROUP_ROWS = 256
ROUTE_TILE = 512
EXPERT_BLOCK = 512
EXPERT_PATHS = 4
DISPATCH_TILE = 512
COMBINE_TILE = 512

_NT = (((1,), (1,)), ((), ()))
_TN = (((0,), (0,)), ((), ()))


def _sigmoid(v):
    return 0.5 * jnp.tanh(0.5 * v) + 0.5


def _silu(v):
    return v * _sigmoid(v)


def _softplus(v):
    return jnp.maximum(v, 0.0) + jnp.log1p(jnp.exp(-jnp.abs(v)))


def _log_sigmoid(v):
    return jnp.minimum(v, 0.0) - jnp.log(1.0 + jnp.exp(-jnp.abs(v)))


def _rms(v):
    return v * lax.rsqrt(jnp.mean(v * v, axis=-1, keepdims=True) + EPS)


def _pieces(a, n):
    out = []
    for _ in range(n - 1):
        p = a.astype(BF16)
        out.append(p)
        a = a - p.astype(F32)
    out.append(a.astype(BF16))
    return out


def _dot_pieces(a, b, n, dims=None):
    dims = dims or (((a.ndim - 1,), (0,)), ((), ()))
    return sum(lax.dot_general(p, b, dims, preferred_element_type=F32) for p in _pieces(a, n))


def _token_rows_load(ref, rows):
    return jnp.concatenate([ref[pl.ds(s, rows, stride=SUBLANES), :] for s in range(SUBLANES)], axis=1)


def _token_rows_store(ref, v):
    rows = v.shape[0]
    for s in range(SUBLANES):
        ref[pl.ds(s, rows, stride=SUBLANES), :] = v[:, s * LANES:(s + 1) * LANES]


def _params(*semantics):
    return pltpu.CompilerParams(dimension_semantics=semantics, vmem_limit_bytes=VMEM_LIMIT_BYTES)


def _const_spec(shape):
    nd = len(shape)
    return pl.BlockSpec(shape, lambda *_: (0,) * nd)


def _ada_kernel(c_ref, w_ref, b_ref, o_ref):
    cond = _silu(c_ref[...])
    o_ref[...] = jnp.dot(cond, w_ref[...], precision=HIGHEST, preferred_element_type=F32) + b_ref[...]


def _ada(c_pad, w_ada, b_ada):
    rows, d = c_pad.shape
    n = w_ada.shape[1]
    tn = d
    return pl.pallas_call(
        _ada_kernel,
        out_shape=jax.ShapeDtypeStruct((rows, n), F32),
        grid=(n // tn,),
        in_specs=[pl.BlockSpec((rows, d), lambda j: (0, 0)),
                  pl.BlockSpec((d, tn), lambda j: (0, j)),
                  pl.BlockSpec((1, tn), lambda j: (0, j))],
        out_specs=pl.BlockSpec((rows, tn), lambda j: (0, j)),
        compiler_params=_params("arbitrary"),
        name="ada",
    )(c_pad, w_ada, b_ada)


def _inproj_kernel(x_ref, sc_ref, sh_ref, g_ref, w_ref, *refs, sections):
    o_refs, ws_ref = refs[:-1], refs[-1]

    @pl.when(pl.program_id(0) == 0)
    def _():
        off = 0
        for o_ref, pieces in zip(o_refs, sections):
            n = o_ref.shape[1]
            used = 0
            for src, width in pieces:
                ws_ref[:, off + used:off + used + width] = w_ref[:, src:src + width]
                used += width
            if used < n:
                ws_ref[:, off + used:off + n] = jnp.zeros((ws_ref.shape[0], n - used), ws_ref.dtype)
            off += n

    h = (_rms(x_ref[...]) * g_ref[...]) * (1.0 + sc_ref[0]) + sh_ref[0]
    hb = h.astype(BF16)
    off = 0
    for o_ref in o_refs:
        n = o_ref.shape[1]
        o_ref[...] = jnp.dot(hb, ws_ref[:, off:off + n], preferred_element_type=F32).astype(o_ref.dtype)
        off += n


def _inproj(x2, sc, sh, g, w_bf, sections, widths, out_dtypes, seq):
    t, d = x2.shape
    tm = min(TOKEN_TILE, seq)
    per_batch = seq // tm
    assert all(n % LANES == 0 for n in widths)
    mod_spec = pl.BlockSpec((1, 1, d), lambda i: (i // per_batch, 0, 0))
    kern = functools.partial(_inproj_kernel, sections=sections)
    return pl.pallas_call(
        kern,
        out_shape=[jax.ShapeDtypeStruct((t, n), dt) for n, dt in zip(widths, out_dtypes)],
        grid=(t // tm,),
        in_specs=[pl.BlockSpec((tm, d), lambda i: (i, 0)), mod_spec, mod_spec, _const_spec((1, d)),
                  pl.BlockSpec(w_bf.shape, lambda i: (0, 0), pipeline_mode=pl.Buffered(1))],
        out_specs=[pl.BlockSpec((tm, n), lambda i: (i, 0)) for n in widths],
        scratch_shapes=[pltpu.VMEM((d, sum(widths)), BF16)],
        compiler_params=_params("arbitrary"),
        name="inproj",
    )(x2, sc, sh, g, w_bf)


def _ssd_kernel(tail_ref, xbc_ref, z_ref, sm_ref, cw_ref, cb_ref, dtb_r_ref, dtb_c_ref, alog_r_ref, alog_c_ref,
                dsk_ref, g_ref, y_ref, st_ref, *, n_heads, d_ssm):
    L = SSM_CHUNK
    P = SSM_HEAD_DIM
    N = SSM_STATE
    G = SSM_GROUPS
    gw = d_ssm // G
    R = xbc_ref.shape[0]
    c = pl.program_id(1)

    @pl.when(c == 0)
    def _():
        st_ref[...] = jnp.zeros(st_ref.shape, F32)

    tail = tail_ref[...]
    tail = jnp.where(c == 0, jnp.zeros_like(tail), tail)
    conv_in = jnp.concatenate([tail, xbc_ref[...]], axis=0)

    ri = lax.broadcasted_iota(I32, (L, L), 0)
    ci = lax.broadcasted_iota(I32, (L, L), 1)
    causal = ri >= ci
    causal3 = jnp.concatenate([causal.astype(BF16)] * 3, axis=1)
    upper3 = jnp.concatenate([(ri <= ci).astype(BF16)] * 3, axis=0)
    wr = lax.broadcasted_iota(I32, ((SSM_CONV - 1) * L, CONV_CARRY + L), 0)
    wc = lax.broadcasted_iota(I32, ((SSM_CONV - 1) * L, CONV_CARRY + L), 1)
    shifts = (wc == (wr % L) + CONV_CARRY - (SSM_CONV - 1) + wr // L).astype(BF16)
    hh = lax.broadcasted_iota(I32, (n_heads, d_ssm), 0)
    jj = lax.broadcasted_iota(I32, (n_heads, d_ssm), 1)
    expand = ((jj // P) == hh).astype(BF16)
    expand2 = jnp.concatenate([expand, expand], axis=0)
    lane = lax.broadcasted_iota(I32, (L, LANES), 1)
    first_half = lane < P
    heads_per_group = n_heads // G
    a_row = -jnp.exp(alog_r_ref[...])
    a_col = -jnp.exp(alog_c_ref[...])

    for s in range(R // L):
        rows = slice(s * L, (s + 1) * L)
        window = conv_in[s * L:s * L + CONV_CARRY + L, :]
        taps = jnp.dot(shifts, window, preferred_element_type=F32)
        acc = cb_ref[...] + cw_ref[SSM_CONV - 1:SSM_CONV, :] * window[CONV_CARRY:, :].astype(F32)
        for j in range(SSM_CONV - 1):
            acc = acc + cw_ref[j:j + 1, :] * taps[j * L:(j + 1) * L, :]
        act = _silu(acc)
        xs = act[:, :d_ssm]
        bm = act[:, d_ssm:d_ssm + G * N].astype(BF16)
        cm = act[:, d_ssm + G * N:].astype(BF16)

        sm = sm_ref[rows, :]
        dt_col = _softplus(sm[:, :n_heads] + dtb_r_ref[...])
        dt_row = _softplus(sm.T[:n_heads, :] + dtb_c_ref[...])
        cs_col = jnp.dot(causal3, jnp.concatenate(_pieces(dt_col * a_row, 3), axis=0),
                         preferred_element_type=F32)
        cs_row = jnp.dot(jnp.concatenate(_pieces(dt_row * a_col, 3), axis=1), upper3,
                         preferred_element_type=F32)
        cs_last = cs_col[L - 1:L, :]

        per_head = jnp.concatenate([dt_col, jnp.exp(cs_col), jnp.exp(cs_last - cs_col)], axis=0)
        per_ch = jnp.dot(jnp.concatenate(_pieces(per_head, 2), axis=1), expand2,
                         preferred_element_type=F32)
        dt_e, ecs_e, dte_e = per_ch[0:L], per_ch[L:2 * L], per_ch[2 * L:3 * L]
        chunk_decay = ecs_e[L - 1:L, :]

        xdt = xs * dt_e
        xdt_b = xdt.astype(BF16)
        xdec_b = (xdt * dte_e).astype(BF16)

        y_parts = []
        y_off_parts = []
        for g in range(G):
            bm_g = bm[:, g * N:(g + 1) * N]
            cm_g = cm[:, g * N:(g + 1) * N]
            cb = lax.dot_general(cm_g, bm_g, _NT, preferred_element_type=F32)
            prev = st_ref[g]
            y_off_parts.append(jnp.dot(cm_g, prev.astype(BF16), preferred_element_type=F32))
            s_new = lax.dot_general(bm_g, xdec_b[:, g * gw:(g + 1) * gw], _TN, preferred_element_type=F32)
            st_ref[g] = prev * chunk_decay[:, g * gw:(g + 1) * gw] + s_new
            for p in range(heads_per_group // 2):
                h0 = g * heads_per_group + 2 * p
                ms = []
                for h in (h0, h0 + 1):
                    diff = cs_col[:, h:h + 1] - cs_row[h:h + 1, :]
                    ms.append((cb * jnp.exp(jnp.where(causal, diff, -jnp.inf))).astype(BF16))
                lhs = jnp.concatenate(ms, axis=1)
                xp = xdt_b[:, h0 * P:(h0 + 2) * P]
                zero = jnp.zeros_like(xp)
                rhs = jnp.concatenate([jnp.where(first_half, xp, zero), jnp.where(first_half, zero, xp)], axis=0)
                y_parts.append(jnp.dot(lhs, rhs, preferred_element_type=F32))
        y = jnp.concatenate(y_parts, axis=1) + jnp.concatenate(y_off_parts, axis=1) * ecs_e + dsk_ref[...] * xs
        y = y * _silu(z_ref[rows, :].astype(F32))
        y = jnp.concatenate([_rms(y[:, g * gw:(g + 1) * gw]) for g in range(G)], axis=1) * g_ref[...]
        y_ref[rows, :] = y.astype(y_ref.dtype)


def _ssd(xbc, z, small, conv_w, conv_b, dt_bias, a_log, d_skip_e, norm_g, batch, seq):
    t, cd = xbc.shape
    d_ssm = z.shape[1]
    n_heads = dt_bias.shape[0]
    L = min(SSD_STEP_ROWS, seq)
    nc = seq // L
    row = lambda b, c: (b * nc + c, 0)
    tail = lambda b, c: (jnp.maximum((b * nc + c) * (L // CONV_CARRY) - 1, 0), 0)
    kern = functools.partial(_ssd_kernel, n_heads=n_heads, d_ssm=d_ssm)
    return pl.pallas_call(
        kern,
        out_shape=jax.ShapeDtypeStruct((t, d_ssm), BF16),
        grid=(batch, nc),
        in_specs=[pl.BlockSpec((CONV_CARRY, cd), tail),
                  pl.BlockSpec((L, cd), row), pl.BlockSpec((L, d_ssm), row), pl.BlockSpec((L, LANES), row),
                  _const_spec(conv_w.shape), _const_spec((1, cd)),
                  _const_spec((1, n_heads)), _const_spec((n_heads, 1)),
                  _const_spec((1, n_heads)), _const_spec((n_heads, 1)),
                  _const_spec((1, d_ssm)), _const_spec((1, d_ssm))],
        out_specs=pl.BlockSpec((L, d_ssm), row),
        scratch_shapes=[pltpu.VMEM((SSM_GROUPS, SSM_STATE, d_ssm // SSM_GROUPS), F32)],
        compiler_params=_params("arbitrary", "arbitrary"),
        name="ssd",
    )(xbc, xbc, z, small, conv_w, conv_b.reshape(1, cd), dt_bias.reshape(1, n_heads), dt_bias.reshape(n_heads, 1),
      a_log.reshape(1, n_heads), a_log.reshape(n_heads, 1), d_skip_e, norm_g.reshape(1, d_ssm))


def _gla_kernel(q_ref, k_ref, v_ref, r_ref, sm_ref, wg2_ref, bg_ref, gn_ref, o_ref, st_ref, *, gate_col):
    L = GLA_CHUNK
    H = GLA_HEADS
    dk = q_ref.shape[1] // H
    dv = v_ref.shape[1] // H
    c = pl.program_id(1)

    @pl.when(c == 0)
    def _():
        st_ref[...] = jnp.zeros(st_ref.shape, F32)

    R = min(GLA_GROUP_ROWS, q_ref.shape[0])
    n_chunks = R // L
    ri = lax.broadcasted_iota(I32, (R, R), 0)
    ci = lax.broadcasted_iota(I32, (R, R), 1)
    same_chunk = (ri // L) == (ci // L)
    causal = jnp.logical_and(same_chunk, ri >= ci)
    tril = causal.astype(BF16)
    later = jnp.logical_and(same_chunk, ri < ci).astype(BF16)
    sr = lax.broadcasted_iota(I32, (R, n_chunks * LANES), 0)
    sc = lax.broadcasted_iota(I32, (R, n_chunks * LANES), 1)
    last_rows = (sr == (sc // LANES) * L + (L - 1)).astype(BF16)
    wg2_hi, wg2_lo = _pieces(wg2_ref[...], 2)

    tril3 = jnp.concatenate([jnp.concatenate([tril] * 3, axis=1), jnp.concatenate([later] * 3, axis=1)], axis=0)
    last3 = jnp.concatenate([last_rows] * 3, axis=0)
    wg3 = jnp.concatenate([wg2_hi, wg2_hi, wg2_lo], axis=0)

    for gi in range(q_ref.shape[0] // R):
        rs = slice(gi * R, (gi + 1) * R)
        q = q_ref[rs, :].astype(F32) * (dk ** -0.5)
        k = k_ref[rs, :].astype(F32)
        v = v_ref[rs, :]
        r = r_ref[rs, :].astype(F32)
        g_hi, g_lo = _pieces(sm_ref[rs, gate_col:gate_col + GLA_GATE_RANK], 2)
        pre = jnp.dot(jnp.concatenate([g_hi, g_lo, g_hi], axis=1), wg3, preferred_element_type=F32) + bg_ref[...]
        gk3 = jnp.concatenate(_pieces(_log_sigmoid(pre) / GLA_GATE_NORM, 3), axis=0)
        sums = jnp.dot(tril3, gk3, preferred_element_type=F32)
        bcum = sums[:R]
        to_end = sums[R:]
        q_t = (q * jnp.exp(bcum)).astype(BF16)
        k_t = (k * jnp.exp(-bcum)).astype(BF16)
        k_dec = (k * jnp.exp(to_end)).astype(BF16)
        dcol = jnp.exp(lax.dot_general(jnp.concatenate(_pieces(bcum, 3), axis=0), last3, _TN,
                                       preferred_element_type=F32))
        outs = []
        for h in range(H):
            ks = slice(h * dk, (h + 1) * dk)
            vs = slice(h * dv, (h + 1) * dv)
            att = lax.dot_general(q_t[:, ks], k_t[:, ks], _NT, preferred_element_type=F32)
            att = jnp.where(causal, att, 0.0).astype(BF16)
            o = jnp.dot(att, v[:, vs], preferred_element_type=F32)
            state = st_ref[h]
            inter = []
            for c in range(n_chunks):
                rows = slice(c * L, (c + 1) * L)
                inter.append(jnp.dot(q_t[rows, ks], state.astype(BF16), preferred_element_type=F32))
                s_new = lax.dot_general(k_dec[rows, ks], v[rows, vs], _TN, preferred_element_type=F32)
                dec = dcol[ks, c * LANES:(c + 1) * LANES]
                state = state * jnp.concatenate([dec] * (dv // LANES), axis=1) + s_new
            st_ref[h] = state
            o = o + jnp.concatenate(inter, axis=0)
            outs.append(_rms(o) * gn_ref[...] * _silu(r[:, vs]))
        o_ref[rs, :] = jnp.concatenate(outs, axis=1).astype(o_ref.dtype)


def _gla(q, k, v, r, small, wg2, bg, norm_g, batch, seq, gate_col):
    t, dkt = q.shape
    dvt = v.shape[1]
    rows = min(GLA_STEP_ROWS, seq)
    nc = seq // rows
    row = lambda b, c: (b * nc + c, 0)
    kern = functools.partial(_gla_kernel, gate_col=gate_col)
    return pl.pallas_call(
        kern,
        out_shape=jax.ShapeDtypeStruct((t, dvt), BF16),
        grid=(batch, nc),
        in_specs=[pl.BlockSpec((rows, dkt), row), pl.BlockSpec((rows, dkt), row), pl.BlockSpec((rows, dvt), row),
                  pl.BlockSpec((rows, dvt), row), pl.BlockSpec((rows, LANES), row),
                  _const_spec(wg2.shape), _const_spec((1, dkt)), _const_spec((1, dvt // GLA_HEADS))],
        out_specs=pl.BlockSpec((rows, dvt), row),
        scratch_shapes=[pltpu.VMEM((GLA_HEADS, dkt // GLA_HEADS, dvt // GLA_HEADS), F32)],
        compiler_params=_params("arbitrary", "arbitrary"),
        name="gla",
    )(q, k, v, r, small, wg2, bg.reshape(1, dkt), norm_g.reshape(1, dvt // GLA_HEADS))


def _outproj_kernel(y_ref, o_ref, x_ref, gt_ref, sc_ref, sh_ref, g_ref, wy_ref, wo_ref, wr_ref, br_ref,
                    x1_ref, h_ref, ti_ref, tw_ref, cnt_ref):
    mix = (jnp.dot(y_ref[...], wy_ref[...], preferred_element_type=F32)
           + jnp.dot(o_ref[...], wo_ref[...], preferred_element_type=F32))
    x1 = x_ref[...] + gt_ref[0] * mix
    x1_ref[...] = x1
    h = (_rms(x1) * g_ref[...]) * (1.0 + sc_ref[0]) + sh_ref[0]
    _token_rows_store(h_ref, h)
    n_e = br_ref.shape[0]
    h_hi, h_lo = _pieces(h, 2)
    wr = wr_ref[...]
    hw = lax.dot_general(wr, h_hi, _NT, preferred_element_type=F32)
    logits = (hw[:n_e] + hw[n_e:] + lax.dot_general(wr[:n_e], h_lo, _NT, preferred_element_type=F32)) + br_ref[...]
    expert = lax.broadcasted_iota(I32, logits.shape, 0)
    vals, idxs = [], []
    counts = jnp.zeros(logits.shape, F32)
    for _ in range(TOP_K):
        m = jnp.max(logits, axis=0, keepdims=True)
        idx = jnp.min(jnp.where(logits == m, expert, n_e), axis=0, keepdims=True)
        vals.append(m)
        idxs.append(idx)
        chosen = expert == idx
        counts = counts + chosen.astype(F32)
        logits = jnp.where(chosen, -jnp.inf, logits)
    exps = [jnp.exp(v - vals[0]) for v in vals]
    denom = functools.reduce(lambda a, b: a + b, exps)
    ti_ref[...] = jnp.concatenate(idxs, axis=0)
    tw_ref[...] = jnp.concatenate([e / denom for e in exps], axis=0)

    @pl.when(pl.program_id(0) == 0)
    def _():
        cnt_ref[...] = jnp.zeros(cnt_ref.shape, F32)

    cnt_ref[...] = cnt_ref[...] + jnp.sum(counts, axis=1, keepdims=True)


def _outproj(y, o, x2, gt, sc, sh, g, wy, wo, w_router, b_router, seq):
    t, d = x2.shape
    n_e = w_router.shape[1]
    tm = min(TOKEN_TILE, seq)
    per_batch = seq // tm
    row = lambda i: (i, 0)
    mod_spec = pl.BlockSpec((1, 1, d), lambda i: (i // per_batch, 0, 0))
    wr_t = w_router.T
    wr_hi = wr_t.astype(BF16)
    wr_cat = jnp.concatenate([wr_hi, (wr_t - wr_hi.astype(F32)).astype(BF16)], axis=0)
    col = lambda i: (0, i)
    return pl.pallas_call(
        _outproj_kernel,
        out_shape=[jax.ShapeDtypeStruct((t, d), F32), jax.ShapeDtypeStruct((t * SUBLANES, LANES), F32),
                   jax.ShapeDtypeStruct((TOP_K, t), I32), jax.ShapeDtypeStruct((TOP_K, t), F32),
                   jax.ShapeDtypeStruct((n_e, LANES), F32)],
        grid=(t // tm,),
        in_specs=[pl.BlockSpec((tm, y.shape[1]), row), pl.BlockSpec((tm, o.shape[1]), row),
                  pl.BlockSpec((tm, d), row), mod_spec, mod_spec, mod_spec, _const_spec((1, d)),
                  _const_spec(wy.shape), _const_spec(wo.shape), _const_spec(wr_cat.shape),
                  _const_spec((n_e, 1))],
        out_specs=[pl.BlockSpec((tm, d), row), pl.BlockSpec((tm * SUBLANES, LANES), row),
                   pl.BlockSpec((TOP_K, tm), col), pl.BlockSpec((TOP_K, tm), col),
                   _const_spec((n_e, LANES))],
        compiler_params=_params("arbitrary"),
        name="outproj",
    )(y, o, x2, gt, sc, sh, g, wy, wo, wr_cat, b_router.reshape(n_e, 1))


def _route_kernel(ti_ref, cnt_ref, dest_ref, be_ref, pend_ref, run_ref, *, n_blocks_pad):
    i = pl.program_id(0)
    n_e = cnt_ref.shape[0]
    tr = ti_ref.shape[1]

    @pl.when(i == 0)
    def _():
        counts = cnt_ref[...]
        padded = jnp.ceil(counts / EXPERT_BLOCK) * EXPERT_BLOCK
        ri = lax.broadcasted_iota(I32, (n_e, n_e), 0)
        ci = lax.broadcasted_iota(I32, (n_e, n_e), 1)
        pend = jnp.dot((ri >= ci).astype(F32), padded, precision=HIGHEST, preferred_element_type=F32)
        pend_ref[...] = pend
        run_ref[...] = pend - padded
        start = (lax.broadcasted_iota(I32, (n_e, n_blocks_pad), 1) * EXPERT_BLOCK).astype(F32)
        be = jnp.sum((pend[:, 0:1] <= start).astype(F32), axis=0, keepdims=True)
        be_ref[...] = jnp.minimum(be, n_e - 1).astype(I32)

    ti = ti_ref[...]
    expert = lax.broadcasted_iota(I32, (n_e, tr), 0)
    onehots = [expert == ti[k:k + 1, :] for k in range(TOP_K)]
    cnt = functools.reduce(lambda a, b: a + b, [oh.astype(F32) for oh in onehots])
    ri = lax.broadcasted_iota(I32, (tr, tr), 0)
    ci = lax.broadcasted_iota(I32, (tr, tr), 1)
    before = jnp.dot(cnt.astype(BF16), (ri < ci).astype(BF16), preferred_element_type=F32)
    base = run_ref[:, 0:1] + before
    dest = [jnp.sum(jnp.where(oh, base, 0.0), axis=0, keepdims=True) for oh in onehots]
    dest_ref[...] = jnp.concatenate(dest, axis=0).astype(I32)
    run_ref[...] = run_ref[...] + jnp.sum(cnt, axis=1, keepdims=True)


def _route(topi_t, counts, n_blocks):
    t = topi_t.shape[1]
    n_e = counts.shape[0]
    tr = min(ROUTE_TILE, t)
    n_blocks_pad = -(-n_blocks // LANES) * LANES
    kern = functools.partial(_route_kernel, n_blocks_pad=n_blocks_pad)
    return pl.pallas_call(
        kern,
        out_shape=[jax.ShapeDtypeStruct((TOP_K, t), I32), jax.ShapeDtypeStruct((1, n_blocks_pad), I32),
                   jax.ShapeDtypeStruct((n_e, LANES), F32)],
        grid=(t // tr,),
        in_specs=[pl.BlockSpec((TOP_K, tr), lambda i: (0, i)), _const_spec((n_e, LANES))],
        out_specs=[pl.BlockSpec((TOP_K, tr), lambda i: (0, i)), _const_spec((1, n_blocks_pad)),
                   _const_spec((n_e, LANES))],
        scratch_shapes=[pltpu.VMEM((n_e, LANES), F32)],
        compiler_params=_params("arbitrary"),
        name="route",
    )(topi_t, counts)


def _dispatch_kernel(pend_ref, dest_hbm, h_ref, wg_hbm, wu_hbm, wd_hbm, xs_hbm, wgb_hbm, wub_hbm, wdb_hbm,
                     idx_ref, zero_ref, fg_ref, fu_ref, fd_ref, bg_ref, bu_ref, bd_ref, idx_sem, row_sem, w_sem,
                     *, n_experts, experts_per_step):
    i = pl.program_id(0)
    tg = idx_ref.shape[1]
    mats = ((wg_hbm, wgb_hbm, fg_ref, bg_ref), (wu_hbm, wub_hbm, fu_ref, bu_ref), (wd_hbm, wdb_hbm, fd_ref, bd_ref))

    def w_in(expert):
        return [pltpu.make_async_copy(src.at[expert], f32, w_sem.at[0]) for src, _, f32, _ in mats]

    def w_out(expert):
        return [pltpu.make_async_copy(b16, dst.at[expert], w_sem.at[1]) for _, dst, _, b16 in mats]

    def convert(expert):
        for cp in w_in(expert):
            cp.wait()

        @pl.when(expert > 0)
        def _():
            for cp in w_out(expert - 1):
                cp.wait()

        for _, _, f32, b16 in mats:
            b16[...] = f32[...].astype(BF16)
        for cp in w_out(expert):
            cp.start()

        @pl.when(expert == n_experts - 1)
        def _():
            for cp in w_out(expert):
                cp.wait()

    first = i * experts_per_step

    @pl.when(first < n_experts)
    def _():
        for cp in w_in(first):
            cp.start()

    @pl.when(i == 0)
    def _():
        zero_ref[...] = jnp.zeros(zero_ref.shape, zero_ref.dtype)
        for e in range(n_experts):
            end = pend_ref[e]
            prev = pend_ref[e - 1] if e > 0 else 0

            @pl.when(end > prev)
            def _():
                start = pl.multiple_of((end - EXPERT_BLOCK) * SUBLANES, EXPERT_BLOCK * SUBLANES)
                cp = pltpu.make_async_copy(zero_ref, xs_hbm.at[pl.ds(start, EXPERT_BLOCK * SUBLANES)], row_sem)
                cp.start()
                cp.wait()

        n_blocks = xs_hbm.shape[0] // (EXPERT_BLOCK * SUBLANES)
        total = pend_ref[n_experts - 1]
        for b in range(n_blocks - n_experts, n_blocks):
            @pl.when(b * EXPERT_BLOCK >= total)
            def _():
                cp = pltpu.make_async_copy(
                    zero_ref, xs_hbm.at[pl.ds(b * EXPERT_BLOCK * SUBLANES, EXPERT_BLOCK * SUBLANES)], row_sem)
                cp.start()
                cp.wait()

    n = pl.num_programs(0)

    def idx_copy(tile, s):
        return pltpu.make_async_copy(dest_hbm.at[:, pl.ds(pl.multiple_of(tile * tg, tg), tg)],
                                     idx_ref.at[pl.ds(s * TOP_K, TOP_K)], idx_sem.at[s])

    @pl.when(i == 0)
    def _():
        idx_copy(0, 0).start()

    def step(slot):
        idx_copy(i, slot).wait()

        @pl.when(i + 1 < n)
        def _():
            idx_copy(i + 1, 1 - slot).start()

        def issue(tl, carry):
            src = h_ref.at[pl.ds(pl.multiple_of(tl * SUBLANES, SUBLANES), SUBLANES)]
            for k in range(TOP_K):
                d = pl.multiple_of(idx_ref[slot * TOP_K + k, tl] * SUBLANES, SUBLANES)
                pltpu.make_async_copy(src, xs_hbm.at[pl.ds(d, SUBLANES)], row_sem).start(priority=k % 2)
            return carry

        lax.fori_loop(0, tg, issue, 0)

    for parity in range(2):
        pl.when(lax.rem(i, 2) == parity)(functools.partial(step, parity))

    pl.when(first < n_experts)(functools.partial(convert, first))
    for j in range(1, experts_per_step):
        @pl.when(first + j < n_experts)
        def _():
            for cp in w_in(first + j):
                cp.start()
            convert(first + j)

    for _ in range(TOP_K):
        pltpu.make_async_copy(h_ref, xs_hbm.at[pl.ds(0, tg * SUBLANES)], row_sem).wait()


def _dispatch(pend_i, dest_t, h, n_rows, w_gate, w_up, w_down):
    t = h.shape[0] // SUBLANES
    tg = min(DISPATCH_TILE, t)
    n_steps = t // tg
    n_experts = w_gate.shape[0]
    kern = functools.partial(_dispatch_kernel, n_experts=n_experts, experts_per_step=-(-n_experts // n_steps))
    hbm = pl.BlockSpec(memory_space=pl.ANY)
    weights = (w_gate, w_up, w_down)
    return pl.pallas_call(
        kern,
        out_shape=[jax.ShapeDtypeStruct((n_rows * SUBLANES, LANES), h.dtype)]
                  + [jax.ShapeDtypeStruct(w.shape, BF16) for w in weights],
        grid_spec=pltpu.PrefetchScalarGridSpec(
            num_scalar_prefetch=1,
            grid=(n_steps,),
            in_specs=[hbm, pl.BlockSpec((tg * SUBLANES, LANES), lambda i, pend: (i, 0)), hbm, hbm, hbm],
            out_specs=[hbm, hbm, hbm, hbm],
            scratch_shapes=[pltpu.SMEM((2 * TOP_K, tg), I32), pltpu.VMEM((EXPERT_BLOCK * SUBLANES, LANES), h.dtype)]
                           + [pltpu.VMEM(w.shape[1:], F32) for w in weights]
                           + [pltpu.VMEM(w.shape[1:], BF16) for w in weights]
                           + [pltpu.SemaphoreType.DMA((2,)), pltpu.SemaphoreType.DMA, pltpu.SemaphoreType.DMA((2,))]),
        compiler_params=pltpu.CompilerParams(dimension_semantics=("arbitrary",), has_side_effects=True,
                                             vmem_limit_bytes=VMEM_LIMIT_BYTES),
        name="dispatch",
    )(pend_i, dest_t, h, *weights)


def _expert_kernel(be_ref, nu_ref, pend_ref, xs_ref, wg_hbm, bg_ref, wu_hbm, bu_ref, wd_hbm, bd_ref, y_ref,
                   wg_b, wu_b, wd_b, slot_ref, sems):
    i = pl.program_id(0)
    used = i < nu_ref[0]
    e = be_ref[i]

    def fetch(expert, slot):
        return [pltpu.make_async_copy(src.at[expert], dst.at[slot], sems.at[slot])
                for src, dst in ((wg_hbm, wg_b), (wu_hbm, wu_b), (wd_hbm, wd_b))]

    @pl.when(i == 0)
    def _():
        slot_ref[0] = 1
        for cp in fetch(e, 0):
            cp.start()

    first_of_expert = jnp.logical_or(i == 0, e != be_ref[jnp.maximum(i - 1, 0)])

    @pl.when(jnp.logical_and(used, first_of_expert))
    def _():
        slot = 1 - slot_ref[0]
        for cp in fetch(e, slot):
            cp.wait()
        slot_ref[0] = slot
        nxt = lax.div(pend_ref[e], EXPERT_BLOCK)

        @pl.when(nxt < nu_ref[0])
        def _():
            for cp in fetch(be_ref[nxt], 1 - slot):
                cp.start()

    def ffn(rows):
        cur = slot_ref[0]
        x = _token_rows_load(xs_ref, rows).astype(BF16)
        gate = jnp.minimum(jnp.dot(x, wg_b[cur], preferred_element_type=F32) + bg_ref[...], SWIGLU_LIMIT)
        up = jnp.clip(jnp.dot(x, wu_b[cur], preferred_element_type=F32) + bu_ref[...],
                      -SWIGLU_LIMIT, SWIGLU_LIMIT)
        glu = gate * _sigmoid(SWIGLU_ALPHA * gate)
        mid = ((up + 1.0) * glu).astype(BF16)
        y = jnp.dot(mid, wd_b[cur], preferred_element_type=F32) + bd_ref[...]
        _token_rows_store(y_ref, y)
        if rows < EXPERT_BLOCK:
            y_ref[rows * SUBLANES:, :] = jnp.zeros(((EXPERT_BLOCK - rows) * SUBLANES, LANES), y_ref.dtype)

    valid = pend_ref[pend_ref.shape[0] // 2 + e] - i * EXPERT_BLOCK
    quarter = EXPERT_BLOCK // EXPERT_PATHS
    for p in range(1, EXPERT_PATHS + 1):
        covers = valid <= p * quarter if p < EXPERT_PATHS else True
        needs = valid > (p - 1) * quarter if p > 1 else True
        pl.when(jnp.logical_and(used, jnp.logical_and(covers, needs)))(functools.partial(ffn, p * quarter))

    @pl.when(jnp.logical_not(used))
    def _():
        y_ref[...] = jnp.zeros(y_ref.shape, y_ref.dtype)


def _experts(block_e, n_used, pend_i, xs, w_gate, b_gate, w_up, b_up, w_down, b_down):
    n_rows = xs.shape[0] // SUBLANES
    n_e, d, f = w_gate.shape
    nb = n_rows // EXPERT_BLOCK
    blk = (EXPERT_BLOCK * SUBLANES, LANES)
    last = lambda i, be, nu, pend: jnp.maximum(jnp.minimum(i, nu[0] - 1), 0)
    bspec = lambda n: pl.BlockSpec((None, 1, n), lambda i, be, nu, pend: (be[last(i, be, nu, pend)], 0, 0))
    hbm = pl.BlockSpec(memory_space=pl.ANY)
    return pl.pallas_call(
        _expert_kernel,
        out_shape=jax.ShapeDtypeStruct((n_rows * SUBLANES, LANES), F32),
        grid_spec=pltpu.PrefetchScalarGridSpec(
            num_scalar_prefetch=3,
            grid=(nb,),
            in_specs=[pl.BlockSpec(blk, lambda i, be, nu, pend: (last(i, be, nu, pend), 0)),
                      hbm, bspec(f), hbm, bspec(f), hbm, bspec(d)],
            out_specs=pl.BlockSpec(blk, lambda i, be, nu, pend: (i, 0)),
            scratch_shapes=[pltpu.VMEM((2, d, f), BF16), pltpu.VMEM((2, d, f), BF16), pltpu.VMEM((2, f, d), BF16),
                            pltpu.SMEM((1,), I32), pltpu.SemaphoreType.DMA((2,))]),
        compiler_params=_params("arbitrary"),
        name="experts",
    )(block_e, n_used, pend_i, xs, w_gate, b_gate.reshape(n_e, 1, f), w_up, b_up.reshape(n_e, 1, f),
      w_down, b_down.reshape(n_e, 1, d))


def _combine_kernel(dest_hbm, ys_hbm, tw_ref, x1_ref, gt_ref, g_ref, o_ref, idx_ref, buf_ref, idx_sem, row_sem):
    i = pl.program_id(0)
    n = pl.num_programs(0)
    tc = idx_ref.shape[1]

    def idx_copy(tile, s):
        return pltpu.make_async_copy(dest_hbm.at[:, pl.ds(pl.multiple_of(tile * tc, tc), tc)],
                                     idx_ref.at[pl.ds(s * TOP_K, TOP_K)], idx_sem.at[s])

    def issue_rows(s):
        def issue(tl, carry):
            dst_row = pl.multiple_of(tl * SUBLANES, SUBLANES)
            for k in range(TOP_K):
                d = pl.multiple_of(idx_ref[s * TOP_K + k, tl] * SUBLANES, SUBLANES)
                pltpu.make_async_copy(ys_hbm.at[pl.ds(d, SUBLANES)], buf_ref.at[s, k, pl.ds(dst_row, SUBLANES)],
                                      row_sem.at[s]).start(priority=k % 2)
            return carry

        lax.fori_loop(0, tc, issue, 0)

    @pl.when(i == 0)
    def _():
        idx_copy(0, 0).start()
        idx_copy(0, 0).wait()
        issue_rows(0)

        @pl.when(n > 1)
        def _():
            idx_copy(1, 1).start()

    def step(slot):
        @pl.when(i + 1 < n)
        def _():
            idx_copy(i + 1, 1 - slot).wait()
            issue_rows(1 - slot)

            @pl.when(i + 2 < n)
            def _():
                idx_copy(i + 2, slot).start()

        for k in range(TOP_K):
            pltpu.make_async_copy(ys_hbm.at[pl.ds(0, tc * SUBLANES)], buf_ref.at[slot, k], row_sem.at[slot]).wait()

        tw = tw_ref[...]
        ffn = tw[:, 0:1] * _token_rows_load(buf_ref.at[slot, 0], tc)
        for k in range(1, TOP_K):
            ffn = ffn + tw[:, k:k + 1] * _token_rows_load(buf_ref.at[slot, k], tc)
        x2 = x1_ref[...] + gt_ref[0] * ffn
        o_ref[...] = _rms(x2) * g_ref[...]

    for parity in range(2):
        pl.when(lax.rem(i, 2) == parity)(functools.partial(step, parity))


def _combine(dest_t, ys, topw, x1, gt, g, seq):
    t, d = x1.shape
    tc = min(COMBINE_TILE, seq)
    per_batch = seq // tc
    row = lambda i: (i, 0)
    return pl.pallas_call(
        _combine_kernel,
        out_shape=jax.ShapeDtypeStruct((t, d), F32),
        grid=(t // tc,),
        in_specs=[pl.BlockSpec(memory_space=pl.ANY), pl.BlockSpec(memory_space=pl.ANY),
                  pl.BlockSpec((tc, TOP_K), row), pl.BlockSpec((tc, d), row),
                  pl.BlockSpec((1, 1, d), lambda i: (i // per_batch, 0, 0)), _const_spec((1, d))],
        out_specs=pl.BlockSpec((tc, d), row),
        scratch_shapes=[pltpu.SMEM((2 * TOP_K, tc), I32), pltpu.VMEM((2, TOP_K, tc * SUBLANES, LANES), F32),
                        pltpu.SemaphoreType.DMA((2,)), pltpu.SemaphoreType.DMA((2,))],
        compiler_params=_params("arbitrary"),
        name="combine",
    )(dest_t, ys, topw, x1, gt, g)


def _layer(x2, mod, batch, seq, norm1_g, w_in, conv_w, conv_b, dt_bias, a_log, d_skip, ssm_norm_g,
           gla_wg2, gla_bg, gla_norm_g, w_out, norm2_g, w_router, b_router,
           w_gate, b_gate, w_up, b_up, w_down, b_down):
    t, d = x2.shape
    n_heads = dt_bias.shape[0]
    d_ssm = n_heads * SSM_HEAD_DIM
    cd = conv_w.shape[1]
    dkt = gla_wg2.shape[1]
    dvt = w_out.shape[0] - d_ssm
    n_experts = w_router.shape[1]

    sh1, sc1, gt1, sh2, sc2, gt2 = [m.reshape(batch, 1, d) for m in jnp.split(mod[:batch], 6, axis=1)]

    sizes = (d_ssm, cd, n_heads, dkt, dkt, dvt, GLA_GATE_RANK, dvt)
    offs = [0]
    for s in sizes:
        offs.append(offs[-1] + s)
    piece = lambda j: (offs[j], sizes[j])
    sections = ((piece(0),), (piece(1),), (piece(3),), (piece(4),), (piece(5),), (piece(7),), (piece(2), piece(6)))
    z, xbc, q, k, v, r, small = _inproj(x2, sc1, sh1, norm1_g.reshape(1, d), w_in.astype(BF16), sections,
                                        (d_ssm, cd, dkt, dkt, dvt, dvt, LANES), [BF16] * 6 + [F32], seq)

    d_skip_e = jnp.repeat(d_skip, SSM_HEAD_DIM).reshape(1, d_ssm)
    y = _ssd(xbc, z, small, conv_w, conv_b, dt_bias, a_log, d_skip_e, ssm_norm_g, batch, seq)
    o = _gla(q, k, v, r, small, gla_wg2, gla_bg, gla_norm_g, batch, seq, gate_col=n_heads)

    x1, h2, topi_t, topw_t, counts = _outproj(y, o, x2, gt1, sc2, sh2, norm2_g.reshape(1, d),
                                        w_out[:d_ssm].astype(BF16), w_out[d_ssm:].astype(BF16),
                                        w_router, b_router, seq)

    n_blocks = (t * TOP_K) // EXPERT_BLOCK + n_experts
    dest_t, block_e, pend = _route(topi_t, counts, n_blocks)
    pend_i = pend[:, 0].astype(I32)
    n_used = (pend_i[n_experts - 1:] // EXPERT_BLOCK).astype(I32)
    topw = topw_t.T
    xs, w_gate, w_up, w_down = _dispatch(pend_i, dest_t, h2, n_blocks * EXPERT_BLOCK, w_gate, w_up, w_down)
    real_end = jnp.concatenate([jnp.zeros((1,), I32), pend_i[:-1]]) + counts[:, 0].astype(I32)
    seg_ends = jnp.concatenate([pend_i, real_end])
    ys = _experts(block_e[0, :n_blocks], n_used, seg_ends, xs, w_gate, b_gate, w_up, b_up, w_down, b_down)
    return dest_t, ys, topw, x1, gt2


def kernel(x, c, w_ada, b_ada, norm1_g, w_in, conv_w, conv_b, dt_bias, a_log, d_skip, ssm_norm_g, gla_wg2,
           gla_bg, gla_norm_g, w_out, norm2_g, w_router, b_router, w_gate, b_gate, w_up, b_up, w_down, b_down,
           final_norm_g):
    batch, seq, d = x.shape
    assert w_ada.shape[0] == 1, "single-layer trunk"
    assert d == SUBLANES * LANES, "token rows are moved as one (8, 128) f32 tile each"
    assert seq % min(seq, max(TOKEN_TILE, SSM_CHUNK, GLA_STEP_ROWS, COMBINE_TILE, DISPATCH_TILE)) == 0
    assert seq % max(SSM_CHUNK, GLA_STEP_ROWS) == 0
    x2 = x.reshape(batch * seq, d)
    c_pad = jnp.zeros((SUBLANES, d), F32).at[:batch].set(c)
    mod = _ada(c_pad, w_ada[0], b_ada)
    dest_t, ys, topw, x1, gt2 = _layer(
        x2, mod, batch, seq, norm1_g[0], w_in[0], conv_w[0], conv_b[0], dt_bias[0], a_log[0], d_skip[0],
        ssm_norm_g[0], gla_wg2[0], gla_bg[0], gla_norm_g[0], w_out[0], norm2_g[0], w_router[0], b_router[0],
        w_gate[0], b_gate[0], w_up[0], b_up[0], w_down[0], b_down[0])
    out = _combine(dest_t, ys, topw, x1, gt2, final_norm_g.reshape(1, d), seq)
    return out.reshape(batch, seq, d)
```

```python
import functools

import jax
import jax.numpy as jnp
from jax import lax
from jax.experimental import pallas as pl
from jax.experimental.pallas import tpu as pltpu

F32 = jnp.float32
BF16 = jnp.bfloat16
I32 = jnp.int32
HIGHEST = lax.Precision.HIGHEST

EPS = 1e-6
SSM_HEAD_DIM = 64
SSM_GROUPS = 2
SSM_STATE = 128
SSM_CONV = 4
SSM_CHUNK = 128
GLA_HEADS = 4
GLA_GATE_RANK = 16
GLA_GATE_NORM = 16.0
GLA_CHUNK = 64
TOP_K = 4
SWIGLU_LIMIT = 7.0
SWIGLU_ALPHA = 1.702

LANES = 128
SUBLANES = 8
V7X_VMEM_BYTES = 64 * 1024 * 1024
VMEM_LIMIT_BYTES = V7X_VMEM_BYTES - 8 * 1024 * 1024

TOKEN_TILE = 512
SSD_STEP_ROWS = 512
CONV_CARRY = 16
GLA_STEP_ROWS = 512
GLA_GROUP_ROWS = 256
ROUTE_TILE = 512
EXPERT_BLOCK = 512
EXPERT_PATHS = 4
DISPATCH_TILE = 1024
COMBINE_TILE = 512

_NT = (((1,), (1,)), ((), ()))
_TN = (((0,), (0,)), ((), ()))


def _sigmoid(v):
    return 0.5 * jnp.tanh(0.5 * v) + 0.5


def _silu(v):
    return v * _sigmoid(v)


def _softplus(v):
    return jnp.maximum(v, 0.0) + jnp.log1p(jnp.exp(-jnp.abs(v)))


def _log_sigmoid(v):
    return jnp.minimum(v, 0.0) - jnp.log(1.0 + jnp.exp(-jnp.abs(v)))


def _rms(v):
    return v * lax.rsqrt(jnp.mean(v * v, axis=-1, keepdims=True) + EPS)


def _pieces(a, n):
    out = []
    for _ in range(n - 1):
        p = a.astype(BF16)
        out.append(p)
        a = a - p.astype(F32)
    out.append(a.astype(BF16))
    return out


def _token_rows_load(ref, rows):
    return jnp.concatenate([ref[pl.ds(s, rows, stride=SUBLANES), :] for s in range(SUBLANES)], axis=1)


def _token_rows_store(ref, v):
    rows = v.shape[0]
    for s in range(SUBLANES):
        ref[pl.ds(s, rows, stride=SUBLANES), :] = v[:, s * LANES:(s + 1) * LANES]


def _params(*semantics):
    return pltpu.CompilerParams(dimension_semantics=semantics, vmem_limit_bytes=VMEM_LIMIT_BYTES)


def _const_spec(shape):
    nd = len(shape)
    return pl.BlockSpec(shape, lambda *_: (0,) * nd)


def _ada_kernel(c_ref, w_ref, b_ref, o_ref):
    cond = _silu(c_ref[...])
    o_ref[...] = jnp.dot(cond, w_ref[...], precision=HIGHEST, preferred_element_type=F32) + b_ref[...]


def _ada(c_pad, w_ada, b_ada):
    rows, d = c_pad.shape
    n = w_ada.shape[1]
    tn = d
    return pl.pallas_call(
        _ada_kernel,
        out_shape=jax.ShapeDtypeStruct((rows, n), F32),
        grid=(n // tn,),
        in_specs=[pl.BlockSpec((rows, d), lambda j: (0, 0)),
                  pl.BlockSpec((d, tn), lambda j: (0, j)),
                  pl.BlockSpec((1, tn), lambda j: (0, j))],
        out_specs=pl.BlockSpec((rows, tn), lambda j: (0, j)),
        compiler_params=_params("arbitrary"),
        name="ada",
    )(c_pad, w_ada, b_ada)


def _inproj_kernel(x_ref, sc_ref, sh_ref, g_ref, w_ref, *refs, sections):
    o_refs, ws_ref = refs[:-1], refs[-1]

    @pl.when(pl.program_id(0) == 0)
    def _():
        off = 0
        for o_ref, pieces in zip(o_refs, sections):
            n = o_ref.shape[1]
            used = 0
            for src, width in pieces:
                ws_ref[:, off + used:off + used + width] = w_ref[:, src:src + width]
                used += width
            if used < n:
                ws_ref[:, off + used:off + n] = jnp.zeros((ws_ref.shape[0], n - used), ws_ref.dtype)
            off += n

    h = (_rms(x_ref[...]) * g_ref[...]) * (1.0 + sc_ref[0]) + sh_ref[0]
    hb = h.astype(BF16)
    off = 0
    for o_ref in o_refs:
        n = o_ref.shape[1]
        o_ref[...] = jnp.dot(hb, ws_ref[:, off:off + n], preferred_element_type=F32).astype(o_ref.dtype)
        off += n


def _inproj(x2, sc, sh, g, w_bf, sections, widths, out_dtypes, seq):
    t, d = x2.shape
    tm = min(TOKEN_TILE, seq)
    per_batch = seq // tm
    assert all(n % LANES == 0 for n in widths)
    mod_spec = pl.BlockSpec((1, 1, d), lambda i: (i // per_batch, 0, 0))
    kern = functools.partial(_inproj_kernel, sections=sections)
    return pl.pallas_call(
        kern,
        out_shape=[jax.ShapeDtypeStruct((t, n), dt) for n, dt in zip(widths, out_dtypes)],
        grid=(t // tm,),
        in_specs=[pl.BlockSpec((tm, d), lambda i: (i, 0)), mod_spec, mod_spec, _const_spec((1, d)),
                  pl.BlockSpec(w_bf.shape, lambda i: (0, 0), pipeline_mode=pl.Buffered(1))],
        out_specs=[pl.BlockSpec((tm, n), lambda i: (i, 0)) for n in widths],
        scratch_shapes=[pltpu.VMEM((d, sum(widths)), BF16)],
        compiler_params=_params("arbitrary"),
        name="inproj",
    )(x2, sc, sh, g, w_bf)


def _ssd_kernel(tail_ref, xbc_ref, z_ref, sm_ref, cw_ref, cb_ref, dtb_r_ref, dtb_c_ref, alog_r_ref, alog_c_ref,
                dsk_ref, g_ref, y_ref, st_ref, *, n_heads, d_ssm):
    L = SSM_CHUNK
    P = SSM_HEAD_DIM
    N = SSM_STATE
    G = SSM_GROUPS
    gw = d_ssm // G
    R = xbc_ref.shape[0]
    c = pl.program_id(1)

    @pl.when(c == 0)
    def _():
        st_ref[...] = jnp.zeros(st_ref.shape, F32)

    tail = tail_ref[...]
    tail = jnp.where(c == 0, jnp.zeros_like(tail), tail)
    conv_in = jnp.concatenate([tail, xbc_ref[...]], axis=0)

    ri = lax.broadcasted_iota(I32, (L, L), 0)
    ci = lax.broadcasted_iota(I32, (L, L), 1)
    causal = ri >= ci
    causal3 = jnp.concatenate([causal.astype(BF16)] * 3, axis=1)
    upper3 = jnp.concatenate([(ri <= ci).astype(BF16)] * 3, axis=0)
    wr = lax.broadcasted_iota(I32, ((SSM_CONV - 1) * L, CONV_CARRY + L), 0)
    wc = lax.broadcasted_iota(I32, ((SSM_CONV - 1) * L, CONV_CARRY + L), 1)
    shifts = (wc == (wr % L) + CONV_CARRY - (SSM_CONV - 1) + wr // L).astype(BF16)
    hh = lax.broadcasted_iota(I32, (n_heads, d_ssm), 0)
    jj = lax.broadcasted_iota(I32, (n_heads, d_ssm), 1)
    expand = ((jj // P) == hh).astype(BF16)
    expand2 = jnp.concatenate([expand, expand], axis=0)
    lane = lax.broadcasted_iota(I32, (L, LANES), 1)
    first_half = lane < P
    heads_per_group = n_heads // G
    a_row = -jnp.exp(alog_r_ref[...])
    a_col = -jnp.exp(alog_c_ref[...])

    for s in range(R // L):
        rows = slice(s * L, (s + 1) * L)
        window = conv_in[s * L:s * L + CONV_CARRY + L, :]
        taps = jnp.dot(shifts, window, preferred_element_type=F32)
        acc = cb_ref[...] + cw_ref[SSM_CONV - 1:SSM_CONV, :] * window[CONV_CARRY:, :].astype(F32)
        for j in range(SSM_CONV - 1):
            acc = acc + cw_ref[j:j + 1, :] * taps[j * L:(j + 1) * L, :]
        act = _silu(acc)
        xs = act[:, :d_ssm]
        bm = act[:, d_ssm:d_ssm + G * N].astype(BF16)
        cm = act[:, d_ssm + G * N:].astype(BF16)

        sm = sm_ref[rows, :]
        dt_col = _softplus(sm[:, :n_heads] + dtb_r_ref[...])
        dt_row = _softplus(sm.T[:n_heads, :] + dtb_c_ref[...])
        cs_col = jnp.dot(causal3, jnp.concatenate(_pieces(dt_col * a_row, 3), axis=0),
                         preferred_element_type=F32)
        cs_row = jnp.dot(jnp.concatenate(_pieces(dt_row * a_col, 3), axis=1), upper3,
                         preferred_element_type=F32)
        cs_last = cs_col[L - 1:L, :]

        per_head = jnp.concatenate([dt_col, jnp.exp(cs_col), jnp.exp(cs_last - cs_col)], axis=0)
        per_ch = jnp.dot(jnp.concatenate(_pieces(per_head, 2), axis=1), expand2,
                         preferred_element_type=F32)
        dt_e, ecs_e, dte_e = per_ch[0:L], per_ch[L:2 * L], per_ch[2 * L:3 * L]
        chunk_decay = ecs_e[L - 1:L, :]

        xdt = xs * dt_e
        xdt_b = xdt.astype(BF16)
        xdec_b = (xdt * dte_e).astype(BF16)

        y_parts = []
        y_off_parts = []
        for g in range(G):
            bm_g = bm[:, g * N:(g + 1) * N]
            cm_g = cm[:, g * N:(g + 1) * N]
            cb = lax.dot_general(cm_g, bm_g, _NT, preferred_element_type=F32)
            prev = st_ref[g]
            y_off_parts.append(jnp.dot(cm_g, prev.astype(BF16), preferred_element_type=F32))
            s_new = lax.dot_general(bm_g, xdec_b[:, g * gw:(g + 1) * gw], _TN, preferred_element_type=F32)
            st_ref[g] = prev * chunk_decay[:, g * gw:(g + 1) * gw] + s_new
            for p in range(heads_per_group // 2):
                h0 = g * heads_per_group + 2 * p
                ms = []
                for h in (h0, h0 + 1):
                    diff = cs_col[:, h:h + 1] - cs_row[h:h + 1, :]
                    ms.append((cb * jnp.exp(jnp.where(causal, diff, -jnp.inf))).astype(BF16))
                lhs = jnp.concatenate(ms, axis=1)
                xp = xdt_b[:, h0 * P:(h0 + 2) * P]
                zero = jnp.zeros_like(xp)
                rhs = jnp.concatenate([jnp.where(first_half, xp, zero), jnp.where(first_half, zero, xp)], axis=0)
                y_parts.append(jnp.dot(lhs, rhs, preferred_element_type=F32))
        y = jnp.concatenate(y_parts, axis=1) + jnp.concatenate(y_off_parts, axis=1) * ecs_e + dsk_ref[...] * xs
        y = y * _silu(z_ref[rows, :].astype(F32))
        y = jnp.concatenate([_rms(y[:, g * gw:(g + 1) * gw]) for g in range(G)], axis=1) * g_ref[...]
        y_ref[rows, :] = y.astype(y_ref.dtype)


def _ssd(xbc, z, small, conv_w, conv_b, dt_bias, a_log, d_skip_e, norm_g, batch, seq):
    t, cd = xbc.shape
    d_ssm = z.shape[1]
    n_heads = dt_bias.shape[0]
    L = min(SSD_STEP_ROWS, seq)
    nc = seq // L
    row = lambda b, c: (b * nc + c, 0)
    tail = lambda b, c: (jnp.maximum((b * nc + c) * (L // CONV_CARRY) - 1, 0), 0)
    kern = functools.partial(_ssd_kernel, n_heads=n_heads, d_ssm=d_ssm)
    return pl.pallas_call(
        kern,
        out_shape=jax.ShapeDtypeStruct((t, d_ssm), BF16),
        grid=(batch, nc),
        in_specs=[pl.BlockSpec((CONV_CARRY, cd), tail),
                  pl.BlockSpec((L, cd), row), pl.BlockSpec((L, d_ssm), row), pl.BlockSpec((L, LANES), row),
                  _const_spec(conv_w.shape), _const_spec((1, cd)),
                  _const_spec((1, n_heads)), _const_spec((n_heads, 1)),
                  _const_spec((1, n_heads)), _const_spec((n_heads, 1)),
                  _const_spec((1, d_ssm)), _const_spec((1, d_ssm))],
        out_specs=pl.BlockSpec((L, d_ssm), row),
        scratch_shapes=[pltpu.VMEM((SSM_GROUPS, SSM_STATE, d_ssm // SSM_GROUPS), F32)],
        compiler_params=_params("arbitrary", "arbitrary"),
        name="ssd",
    )(xbc, xbc, z, small, conv_w, conv_b.reshape(1, cd), dt_bias.reshape(1, n_heads), dt_bias.reshape(n_heads, 1),
      a_log.reshape(1, n_heads), a_log.reshape(n_heads, 1), d_skip_e, norm_g.reshape(1, d_ssm))


def _gla_kernel(q_ref, k_ref, v_ref, r_ref, sm_ref, wg2_ref, bg_ref, gn_ref, o_ref, st_ref, *, gate_col):
    L = GLA_CHUNK
    H = GLA_HEADS
    dk = q_ref.shape[1] // H
    dv = v_ref.shape[1] // H
    c = pl.program_id(1)

    @pl.when(c == 0)
    def _():
        st_ref[...] = jnp.zeros(st_ref.shape, F32)

    R = min(GLA_GROUP_ROWS, q_ref.shape[0])
    n_chunks = R // L
    ri = lax.broadcasted_iota(I32, (R, R), 0)
    ci = lax.broadcasted_iota(I32, (R, R), 1)
    same_chunk = (ri // L) == (ci // L)
    causal = jnp.logical_and(same_chunk, ri >= ci)
    tril = causal.astype(BF16)
    later = jnp.logical_and(same_chunk, ri < ci).astype(BF16)
    sr = lax.broadcasted_iota(I32, (R, n_chunks * LANES), 0)
    sc = lax.broadcasted_iota(I32, (R, n_chunks * LANES), 1)
    last_rows = (sr == (sc // LANES) * L + (L - 1)).astype(BF16)
    wg2_hi, wg2_lo = _pieces(wg2_ref[...], 2)

    tril3 = jnp.concatenate([jnp.concatenate([tril] * 3, axis=1), jnp.concatenate([later] * 3, axis=1)], axis=0)
    last3 = jnp.concatenate([last_rows] * 3, axis=0)
    wg3 = jnp.concatenate([wg2_hi, wg2_hi, wg2_lo], axis=0)

    for gi in range(q_ref.shape[0] // R):
        rs = slice(gi * R, (gi + 1) * R)
        q = q_ref[rs, :].astype(F32) * (dk ** -0.5)
        k = k_ref[rs, :].astype(F32)
        v = v_ref[rs, :]
        r = r_ref[rs, :].astype(F32)
        g_hi, g_lo = _pieces(sm_ref[rs, gate_col:gate_col + GLA_GATE_RANK], 2)
        pre = jnp.dot(jnp.concatenate([g_hi, g_lo, g_hi], axis=1), wg3, preferred_element_type=F32) + bg_ref[...]
        gk3 = jnp.concatenate(_pieces(_log_sigmoid(pre) / GLA_GATE_NORM, 3), axis=0)
        sums = jnp.dot(tril3, gk3, preferred_element_type=F32)
        bcum = sums[:R]
        to_end = sums[R:]
        q_t = (q * jnp.exp(bcum)).astype(BF16)
        k_t = (k * jnp.exp(-bcum)).astype(BF16)
        k_dec = (k * jnp.exp(to_end)).astype(BF16)
        dcol = jnp.exp(lax.dot_general(jnp.concatenate(_pieces(bcum, 3), axis=0), last3, _TN,
                                       preferred_element_type=F32))
        outs = []
        for h in range(H):
            ks = slice(h * dk, (h + 1) * dk)
            vs = slice(h * dv, (h + 1) * dv)
            att = lax.dot_general(q_t[:, ks], k_t[:, ks], _NT, preferred_element_type=F32)
            att = jnp.where(causal, att, 0.0).astype(BF16)
            o = jnp.dot(att, v[:, vs], preferred_element_type=F32)
            state = st_ref[h]
            inter = []
            for c in range(n_chunks):
                rows = slice(c * L, (c + 1) * L)
                inter.append(jnp.dot(q_t[rows, ks], state.astype(BF16), preferred_element_type=F32))
                s_new = lax.dot_general(k_dec[rows, ks], v[rows, vs], _TN, preferred_element_type=F32)
                dec = dcol[ks, c * LANES:(c + 1) * LANES]
                state = state * jnp.concatenate([dec] * (dv // LANES), axis=1) + s_new
            st_ref[h] = state
            o = o + jnp.concatenate(inter, axis=0)
            outs.append(_rms(o) * gn_ref[...] * _silu(r[:, vs]))
        o_ref[rs, :] = jnp.concatenate(outs, axis=1).astype(o_ref.dtype)


def _gla(q, k, v, r, small, wg2, bg, norm_g, batch, seq, gate_col):
    t, dkt = q.shape
    dvt = v.shape[1]
    rows = min(GLA_STEP_ROWS, seq)
    nc = seq // rows
    row = lambda b, c: (b * nc + c, 0)
    kern = functools.partial(_gla_kernel, gate_col=gate_col)
    return pl.pallas_call(
        kern,
        out_shape=jax.ShapeDtypeStruct((t, dvt), BF16),
        grid=(batch, nc),
        in_specs=[pl.BlockSpec((rows, dkt), row), pl.BlockSpec((rows, dkt), row), pl.BlockSpec((rows, dvt), row),
                  pl.BlockSpec((rows, dvt), row), pl.BlockSpec((rows, LANES), row),
                  _const_spec(wg2.shape), _const_spec((1, dkt)), _const_spec((1, dvt // GLA_HEADS))],
        out_specs=pl.BlockSpec((rows, dvt), row),
        scratch_shapes=[pltpu.VMEM((GLA_HEADS, dkt // GLA_HEADS, dvt // GLA_HEADS), F32)],
        compiler_params=_params("arbitrary", "arbitrary"),
        name="gla",
    )(q, k, v, r, small, wg2, bg.reshape(1, dkt), norm_g.reshape(1, dvt // GLA_HEADS))


def _outproj_kernel(y_ref, o_ref, x_ref, gt_ref, sc_ref, sh_ref, g_ref, wy_ref, wo_ref, wr_ref, br_ref,
                    x1_ref, h_ref, ti_ref, tw_ref, cnt_ref):
    mix = (jnp.dot(y_ref[...], wy_ref[...], preferred_element_type=F32)
           + jnp.dot(o_ref[...], wo_ref[...], preferred_element_type=F32))
    x1 = x_ref[...] + gt_ref[0] * mix
    x1_ref[...] = x1
    h = (_rms(x1) * g_ref[...]) * (1.0 + sc_ref[0]) + sh_ref[0]
    _token_rows_store(h_ref, h)
    n_e = br_ref.shape[0]
    h_hi, h_lo = _pieces(h, 2)
    wr = wr_ref[...]
    hw = lax.dot_general(wr, h_hi, _NT, preferred_element_type=F32)
    logits = (hw[:n_e] + hw[n_e:] + lax.dot_general(wr[:n_e], h_lo, _NT, preferred_element_type=F32)) + br_ref[...]
    expert = lax.broadcasted_iota(I32, logits.shape, 0)
    vals, idxs = [], []
    counts = jnp.zeros(logits.shape, F32)
    for _ in range(TOP_K):
        m = jnp.max(logits, axis=0, keepdims=True)
        idx = jnp.min(jnp.where(logits == m, expert, n_e), axis=0, keepdims=True)
        vals.append(m)
        idxs.append(idx)
        chosen = expert == idx
        counts = counts + chosen.astype(F32)
        logits = jnp.where(chosen, -jnp.inf, logits)
    exps = [jnp.exp(v - vals[0]) for v in vals]
    denom = functools.reduce(lambda a, b: a + b, exps)
    ti_ref[...] = jnp.concatenate(idxs, axis=0)
    tw_ref[...] = jnp.concatenate([e / denom for e in exps], axis=0)

    @pl.when(pl.program_id(0) == 0)
    def _():
        cnt_ref[...] = jnp.zeros(cnt_ref.shape, F32)

    cnt_ref[...] = cnt_ref[...] + jnp.sum(counts, axis=1, keepdims=True)


def _outproj(y, o, x2, gt, sc, sh, g, wy, wo, w_router, b_router, seq):
    t, d = x2.shape
    n_e = w_router.shape[1]
    tm = min(TOKEN_TILE, seq)
    per_batch = seq // tm
    row = lambda i: (i, 0)
    mod_spec = pl.BlockSpec((1, 1, d), lambda i: (i // per_batch, 0, 0))
    wr_t = w_router.T
    wr_hi = wr_t.astype(BF16)
    wr_cat = jnp.concatenate([wr_hi, (wr_t - wr_hi.astype(F32)).astype(BF16)], axis=0)
    col = lambda i: (0, i)
    return pl.pallas_call(
        _outproj_kernel,
        out_shape=[jax.ShapeDtypeStruct((t, d), F32), jax.ShapeDtypeStruct((t * SUBLANES, LANES), F32),
                   jax.ShapeDtypeStruct((TOP_K, t), I32), jax.ShapeDtypeStruct((TOP_K, t), F32),
                   jax.ShapeDtypeStruct((n_e, LANES), F32)],
        grid=(t // tm,),
        in_specs=[pl.BlockSpec((tm, y.shape[1]), row), pl.BlockSpec((tm, o.shape[1]), row),
                  pl.BlockSpec((tm, d), row), mod_spec, mod_spec, mod_spec, _const_spec((1, d)),
                  _const_spec(wy.shape), _const_spec(wo.shape), _const_spec(wr_cat.shape),
                  _const_spec((n_e, 1))],
        out_specs=[pl.BlockSpec((tm, d), row), pl.BlockSpec((tm * SUBLANES, LANES), row),
                   pl.BlockSpec((TOP_K, tm), col), pl.BlockSpec((TOP_K, tm), col),
                   _const_spec((n_e, LANES))],
        compiler_params=_params("arbitrary"),
        name="outproj",
    )(y, o, x2, gt, sc, sh, g, wy, wo, wr_cat, b_router.reshape(n_e, 1))


def _route_kernel(ti_ref, cnt_ref, dest_ref, be_ref, pend_ref, run_ref, *, n_blocks_pad):
    i = pl.program_id(0)
    n_e = cnt_ref.shape[0]
    tr = ti_ref.shape[1]

    @pl.when(i == 0)
    def _():
        counts = cnt_ref[...]
        padded = jnp.ceil(counts / EXPERT_BLOCK) * EXPERT_BLOCK
        ri = lax.broadcasted_iota(I32, (n_e, n_e), 0)
        ci = lax.broadcasted_iota(I32, (n_e, n_e), 1)
        pend = jnp.dot((ri >= ci).astype(F32), padded, precision=HIGHEST, preferred_element_type=F32)
        pend_ref[...] = pend
        run_ref[...] = pend - padded
        start = (lax.broadcasted_iota(I32, (n_e, n_blocks_pad), 1) * EXPERT_BLOCK).astype(F32)
        be = jnp.sum((pend[:, 0:1] <= start).astype(F32), axis=0, keepdims=True)
        be_ref[...] = jnp.minimum(be, n_e - 1).astype(I32)

    ti = ti_ref[...]
    expert = lax.broadcasted_iota(I32, (n_e, tr), 0)
    onehots = [expert == ti[k:k + 1, :] for k in range(TOP_K)]
    cnt = functools.reduce(lambda a, b: a + b, [oh.astype(F32) for oh in onehots])
    ri = lax.broadcasted_iota(I32, (tr, tr), 0)
    ci = lax.broadcasted_iota(I32, (tr, tr), 1)
    before = jnp.dot(cnt.astype(BF16), (ri < ci).astype(BF16), preferred_element_type=F32)
    base = run_ref[:, 0:1] + before
    dest = [jnp.sum(jnp.where(oh, base, 0.0), axis=0, keepdims=True) for oh in onehots]
    dest_ref[...] = jnp.concatenate(dest, axis=0).astype(I32)
    run_ref[...] = run_ref[...] + jnp.sum(cnt, axis=1, keepdims=True)


def _route(topi_t, counts, n_blocks):
    t = topi_t.shape[1]
    n_e = counts.shape[0]
    tr = min(ROUTE_TILE, t)
    n_blocks_pad = -(-n_blocks // LANES) * LANES
    kern = functools.partial(_route_kernel, n_blocks_pad=n_blocks_pad)
    return pl.pallas_call(
        kern,
        out_shape=[jax.ShapeDtypeStruct((TOP_K, t), I32), jax.ShapeDtypeStruct((1, n_blocks_pad), I32),
                   jax.ShapeDtypeStruct((n_e, LANES), F32)],
        grid=(t // tr,),
        in_specs=[pl.BlockSpec((TOP_K, tr), lambda i: (0, i)), _const_spec((n_e, LANES))],
        out_specs=[pl.BlockSpec((TOP_K, tr), lambda i: (0, i)), _const_spec((1, n_blocks_pad)),
                   _const_spec((n_e, LANES))],
        scratch_shapes=[pltpu.VMEM((n_e, LANES), F32)],
        compiler_params=_params("arbitrary"),
        name="route",
    )(topi_t, counts)


def _dispatch_kernel(pend_ref, dest_hbm, h_ref, xs_hbm, idx_ref, zero_ref, idx_sem, row_sem, *, n_experts):
    i = pl.program_id(0)
    tg = idx_ref.shape[1]

    @pl.when(i == 0)
    def _():
        zero_ref[...] = jnp.zeros(zero_ref.shape, zero_ref.dtype)
        for e in range(n_experts):
            end = pend_ref[e]
            prev = pend_ref[e - 1] if e > 0 else 0

            @pl.when(end > prev)
            def _():
                start = pl.multiple_of((end - EXPERT_BLOCK) * SUBLANES, EXPERT_BLOCK * SUBLANES)
                cp = pltpu.make_async_copy(zero_ref, xs_hbm.at[pl.ds(start, EXPERT_BLOCK * SUBLANES)], row_sem)
                cp.start()
                cp.wait()

        n_blocks = xs_hbm.shape[0] // (EXPERT_BLOCK * SUBLANES)
        total = pend_ref[n_experts - 1]
        for b in range(n_blocks - n_experts, n_blocks):
            @pl.when(b * EXPERT_BLOCK >= total)
            def _():
                cp = pltpu.make_async_copy(
                    zero_ref, xs_hbm.at[pl.ds(b * EXPERT_BLOCK * SUBLANES, EXPERT_BLOCK * SUBLANES)], row_sem)
                cp.start()
                cp.wait()

    n = pl.num_programs(0)

    def idx_copy(tile, s):
        return pltpu.make_async_copy(dest_hbm.at[:, pl.ds(pl.multiple_of(tile * tg, tg), tg)],
                                     idx_ref.at[pl.ds(s * TOP_K, TOP_K)], idx_sem.at[s])

    @pl.when(i == 0)
    def _():
        idx_copy(0, 0).start()

    def step(slot):
        idx_copy(i, slot).wait()

        @pl.when(i + 1 < n)
        def _():
            idx_copy(i + 1, 1 - slot).start()

        def issue(tl, carry):
            src = h_ref.at[pl.ds(pl.multiple_of(tl * SUBLANES, SUBLANES), SUBLANES)]
            for k in range(TOP_K):
                d = pl.multiple_of(idx_ref[slot * TOP_K + k, tl] * SUBLANES, SUBLANES)
                pltpu.make_async_copy(src, xs_hbm.at[pl.ds(d, SUBLANES)], row_sem).start(priority=k % 2)
            return carry

        lax.fori_loop(0, tg, issue, 0)

    for parity in range(2):
        pl.when(lax.rem(i, 2) == parity)(functools.partial(step, parity))
    for _ in range(TOP_K):
        pltpu.make_async_copy(h_ref, xs_hbm.at[pl.ds(0, tg * SUBLANES)], row_sem).wait()


def _dispatch(pend_i, dest_t, h, n_rows, n_experts):
    t = h.shape[0] // SUBLANES
    tg = min(DISPATCH_TILE, t)
    kern = functools.partial(_dispatch_kernel, n_experts=n_experts)
    return pl.pallas_call(
        kern,
        out_shape=jax.ShapeDtypeStruct((n_rows * SUBLANES, LANES), h.dtype),
        grid_spec=pltpu.PrefetchScalarGridSpec(
            num_scalar_prefetch=1,
            grid=(t // tg,),
            in_specs=[pl.BlockSpec(memory_space=pl.ANY),
                      pl.BlockSpec((tg * SUBLANES, LANES), lambda i, pend: (i, 0))],
            out_specs=pl.BlockSpec(memory_space=pl.ANY),
            scratch_shapes=[pltpu.SMEM((2 * TOP_K, tg), I32), pltpu.VMEM((EXPERT_BLOCK * SUBLANES, LANES), h.dtype),
                            pltpu.SemaphoreType.DMA((2,)), pltpu.SemaphoreType.DMA]),
        compiler_params=pltpu.CompilerParams(dimension_semantics=("arbitrary",), has_side_effects=True,
                                             vmem_limit_bytes=VMEM_LIMIT_BYTES),
        name="dispatch",
    )(pend_i, dest_t, h)


def _expert_kernel(be_ref, nu_ref, pend_ref, xs_ref, wg_hbm, bg_ref, wu_hbm, bu_ref, wd_hbm, bd_ref, y_ref,
                   wg_f, wu_f, wd_f, wg_b, wu_b, wd_b, slot_ref, sems):
    i = pl.program_id(0)
    used = i < nu_ref[0]
    e = be_ref[i]

    def fetch(expert, slot):
        return [pltpu.make_async_copy(src.at[expert], dst.at[slot], sems.at[slot])
                for src, dst in ((wg_hbm, wg_f), (wu_hbm, wu_f), (wd_hbm, wd_f))]

    @pl.when(i == 0)
    def _():
        slot_ref[0] = 0
        for cp in fetch(e, 0):
            cp.start()

    first_of_expert = jnp.logical_or(i == 0, e != be_ref[jnp.maximum(i - 1, 0)])

    @pl.when(jnp.logical_and(used, first_of_expert))
    def _():
        slot = slot_ref[0]
        for cp in fetch(e, slot):
            cp.wait()
        wg_b[...] = wg_f[slot].astype(BF16)
        wu_b[...] = wu_f[slot].astype(BF16)
        wd_b[...] = wd_f[slot].astype(BF16)
        nxt = lax.div(pend_ref[e], EXPERT_BLOCK)

        @pl.when(nxt < nu_ref[0])
        def _():
            for cp in fetch(be_ref[nxt], 1 - slot):
                cp.start()

        slot_ref[0] = 1 - slot

    def ffn(rows):
        x = _token_rows_load(xs_ref, rows).astype(BF16)
        gate = jnp.minimum(jnp.dot(x, wg_b[...], preferred_element_type=F32) + bg_ref[...], SWIGLU_LIMIT)
        up = jnp.clip(jnp.dot(x, wu_b[...], preferred_element_type=F32) + bu_ref[...],
                      -SWIGLU_LIMIT, SWIGLU_LIMIT)
        glu = gate * _sigmoid(SWIGLU_ALPHA * gate)
        mid = ((up + 1.0) * glu).astype(BF16)
        y = jnp.dot(mid, wd_b[...], preferred_element_type=F32) + bd_ref[...]
        _token_rows_store(y_ref, y)
        if rows < EXPERT_BLOCK:
            y_ref[rows * SUBLANES:, :] = jnp.zeros(((EXPERT_BLOCK - rows) * SUBLANES, LANES), y_ref.dtype)

    valid = pend_ref[pend_ref.shape[0] // 2 + e] - i * EXPERT_BLOCK
    quarter = EXPERT_BLOCK // EXPERT_PATHS
    for p in range(1, EXPERT_PATHS + 1):
        covers = valid <= p * quarter if p < EXPERT_PATHS else True
        needs = valid > (p - 1) * quarter if p > 1 else True
        pl.when(jnp.logical_and(used, jnp.logical_and(covers, needs)))(functools.partial(ffn, p * quarter))

    @pl.when(jnp.logical_not(used))
    def _():
        y_ref[...] = jnp.zeros(y_ref.shape, y_ref.dtype)


def _experts(block_e, n_used, pend_i, xs, w_gate, b_gate, w_up, b_up, w_down, b_down):
    n_rows = xs.shape[0] // SUBLANES
    n_e, d, f = w_gate.shape
    nb = n_rows // EXPERT_BLOCK
    blk = (EXPERT_BLOCK * SUBLANES, LANES)
    last = lambda i, be, nu, pend: jnp.maximum(jnp.minimum(i, nu[0] - 1), 0)
    bspec = lambda n: pl.BlockSpec((None, 1, n), lambda i, be, nu, pend: (be[last(i, be, nu, pend)], 0, 0))
    hbm = pl.BlockSpec(memory_space=pl.ANY)
    return pl.pallas_call(
        _expert_kernel,
        out_shape=jax.ShapeDtypeStruct((n_rows * SUBLANES, LANES), F32),
        grid_spec=pltpu.PrefetchScalarGridSpec(
            num_scalar_prefetch=3,
            grid=(nb,),
            in_specs=[pl.BlockSpec(blk, lambda i, be, nu, pend: (last(i, be, nu, pend), 0)),
                      hbm, bspec(f), hbm, bspec(f), hbm, bspec(d)],
            out_specs=pl.BlockSpec(blk, lambda i, be, nu, pend: (i, 0)),
            scratch_shapes=[pltpu.VMEM((2, d, f), F32), pltpu.VMEM((2, d, f), F32), pltpu.VMEM((2, f, d), F32),
                            pltpu.VMEM((d, f), BF16), pltpu.VMEM((d, f), BF16), pltpu.VMEM((f, d), BF16),
                            pltpu.SMEM((1,), I32), pltpu.SemaphoreType.DMA((2,))]),
        compiler_params=_params("arbitrary"),
        name="experts",
    )(block_e, n_used, pend_i, xs, w_gate, b_gate.reshape(n_e, 1, f), w_up, b_up.reshape(n_e, 1, f),
      w_down, b_down.reshape(n_e, 1, d))


def _combine_kernel(dest_hbm, ys_hbm, tw_ref, x1_ref, gt_ref, g_ref, o_ref, idx_ref, buf_ref, idx_sem, row_sem):
    i = pl.program_id(0)
    n = pl.num_programs(0)
    tc = idx_ref.shape[1]

    def idx_copy(tile, s):
        return pltpu.make_async_copy(dest_hbm.at[:, pl.ds(pl.multiple_of(tile * tc, tc), tc)],
                                     idx_ref.at[pl.ds(s * TOP_K, TOP_K)], idx_sem.at[s])

    def issue_rows(s):
        def issue(tl, carry):
            dst_row = pl.multiple_of(tl * SUBLANES, SUBLANES)
            for k in range(TOP_K):
                d = pl.multiple_of(idx_ref[s * TOP_K + k, tl] * SUBLANES, SUBLANES)
                pltpu.make_async_copy(ys_hbm.at[pl.ds(d, SUBLANES)], buf_ref.at[s, k, pl.ds(dst_row, SUBLANES)],
                                      row_sem.at[s]).start(priority=k % 2)
            return carry

        lax.fori_loop(0, tc, issue, 0)

    @pl.when(i == 0)
    def _():
        idx_copy(0, 0).start()
        idx_copy(0, 0).wait()
        issue_rows(0)

        @pl.when(n > 1)
        def _():
            idx_copy(1, 1).start()

    def step(slot):
        @pl.when(i + 1 < n)
        def _():
            idx_copy(i + 1, 1 - slot).wait()
            issue_rows(1 - slot)

            @pl.when(i + 2 < n)
            def _():
                idx_copy(i + 2, slot).start()

        for k in range(TOP_K):
            pltpu.make_async_copy(ys_hbm.at[pl.ds(0, tc * SUBLANES)], buf_ref.at[slot, k], row_sem.at[slot]).wait()

        tw = tw_ref[...]
        ffn = tw[:, 0:1] * _token_rows_load(buf_ref.at[slot, 0], tc)
        for k in range(1, TOP_K):
            ffn = ffn + tw[:, k:k + 1] * _token_rows_load(buf_ref.at[slot, k], tc)
        x2 = x1_ref[...] + gt_ref[0] * ffn
        o_ref[...] = _rms(x2) * g_ref[...]

    for parity in range(2):
        pl.when(lax.rem(i, 2) == parity)(functools.partial(step, parity))


def _combine(dest_t, ys, topw, x1, gt, g, seq):
    t, d = x1.shape
    tc = min(COMBINE_TILE, seq)
    per_batch = seq // tc
    row = lambda i: (i, 0)
    return pl.pallas_call(
        _combine_kernel,
        out_shape=jax.ShapeDtypeStruct((t, d), F32),
        grid=(t // tc,),
        in_specs=[pl.BlockSpec(memory_space=pl.ANY), pl.BlockSpec(memory_space=pl.ANY),
                  pl.BlockSpec((tc, TOP_K), row), pl.BlockSpec((tc, d), row),
                  pl.BlockSpec((1, 1, d), lambda i: (i // per_batch, 0, 0)), _const_spec((1, d))],
        out_specs=pl.BlockSpec((tc, d), row),
        scratch_shapes=[pltpu.SMEM((2 * TOP_K, tc), I32), pltpu.VMEM((2, TOP_K, tc * SUBLANES, LANES), F32),
                        pltpu.SemaphoreType.DMA((2,)), pltpu.SemaphoreType.DMA((2,))],
        compiler_params=_params("arbitrary"),
        name="combine",
    )(dest_t, ys, topw, x1, gt, g)


def _layer(x2, mod, batch, seq, norm1_g, w_in, conv_w, conv_b, dt_bias, a_log, d_skip, ssm_norm_g,
           gla_wg2, gla_bg, gla_norm_g, w_out, norm2_g, w_router, b_router,
           w_gate, b_gate, w_up, b_up, w_down, b_down):
    t, d = x2.shape
    n_heads = dt_bias.shape[0]
    d_ssm = n_heads * SSM_HEAD_DIM
    cd = conv_w.shape[1]
    dkt = gla_wg2.shape[1]
    dvt = w_out.shape[0] - d_ssm
    n_experts = w_router.shape[1]

    sh1, sc1, gt1, sh2, sc2, gt2 = [m.reshape(batch, 1, d) for m in jnp.split(mod[:batch], 6, axis=1)]

    sizes = (d_ssm, cd, n_heads, dkt, dkt, dvt, GLA_GATE_RANK, dvt)
    offs = [0]
    for s in sizes:
        offs.append(offs[-1] + s)
    piece = lambda j: (offs[j], sizes[j])
    sections = ((piece(0),), (piece(1),), (piece(3),), (piece(4),), (piece(5),), (piece(7),), (piece(2), piece(6)))
    z, xbc, q, k, v, r, small = _inproj(x2, sc1, sh1, norm1_g.reshape(1, d), w_in.astype(BF16), sections,
                                        (d_ssm, cd, dkt, dkt, dvt, dvt, LANES), [BF16] * 6 + [F32], seq)

    d_skip_e = jnp.repeat(d_skip, SSM_HEAD_DIM).reshape(1, d_ssm)
    y = _ssd(xbc, z, small, conv_w, conv_b, dt_bias, a_log, d_skip_e, ssm_norm_g, batch, seq)
    o = _gla(q, k, v, r, small, gla_wg2, gla_bg, gla_norm_g, batch, seq, gate_col=n_heads)

    x1, h2, topi_t, topw_t, counts = _outproj(y, o, x2, gt1, sc2, sh2, norm2_g.reshape(1, d),
                                        w_out[:d_ssm].astype(BF16), w_out[d_ssm:].astype(BF16),
                                        w_router, b_router, seq)

    n_blocks = (t * TOP_K) // EXPERT_BLOCK + n_experts
    dest_t, block_e, pend = _route(topi_t, counts, n_blocks)
    pend_i = pend[:, 0].astype(I32)
    n_used = (pend_i[n_experts - 1:] // EXPERT_BLOCK).astype(I32)
    topw = topw_t.T
    xs = _dispatch(pend_i, dest_t, h2, n_blocks * EXPERT_BLOCK, n_experts)
    real_end = jnp.concatenate([jnp.zeros((1,), I32), pend_i[:-1]]) + counts[:, 0].astype(I32)
    seg_ends = jnp.concatenate([pend_i, real_end])
    ys = _experts(block_e[0, :n_blocks], n_used, seg_ends, xs, w_gate, b_gate, w_up, b_up, w_down, b_down)
    return dest_t, ys, topw, x1, gt2


def kernel(x, c, w_ada, b_ada, norm1_g, w_in, conv_w, conv_b, dt_bias, a_log, d_skip, ssm_norm_g, gla_wg2,
           gla_bg, gla_norm_g, w_out, norm2_g, w_router, b_router, w_gate, b_gate, w_up, b_up, w_down, b_down,
           final_norm_g):
    batch, seq, d = x.shape
    assert w_ada.shape[0] == 1, "single-layer trunk"
    assert d == SUBLANES * LANES, "token rows are moved as one (8, 128) f32 tile each"
    assert seq % min(seq, max(TOKEN_TILE, SSM_CHUNK, GLA_STEP_ROWS, COMBINE_TILE, DISPATCH_TILE)) == 0
    assert seq % max(SSM_CHUNK, GLA_STEP_ROWS) == 0
    x2 = x.reshape(batch * seq, d)
    c_pad = jnp.zeros((SUBLANES, d), F32).at[:batch].set(c)
    mod = _ada(c_pad, w_ada[0], b_ada)
    dest_t, ys, topw, x1, gt2 = _layer(
        x2, mod, batch, seq, norm1_g[0], w_in[0], conv_w[0], conv_b[0], dt_bias[0], a_log[0], d_skip[0],
        ssm_norm_g[0], gla_wg2[0], gla_bg[0], gla_norm_g[0], w_out[0], norm2_g[0], w_router[0], b_router[0],
        w_gate[0], b_gate[0], w_up[0], b_up[0], w_down[0], b_down[0])
    out = _combine(dest_t, ys, topw, x1, gt2, final_norm_g.reshape(1, d), seq)
    return out.reshape(batch, seq, d)
```

```python
import functools

import jax
import jax.numpy as jnp
from jax import lax
from jax.experimental import pallas as pl
from jax.experimental.pallas import tpu as pltpu

F32 = jnp.float32
BF16 = jnp.bfloat16
I32 = jnp.int32
HIGHEST = lax.Precision.HIGHEST

EPS = 1e-6
SSM_HEAD_DIM = 64
SSM_GROUPS = 2
SSM_STATE = 128
SSM_CONV = 4
SSM_CHUNK = 128
GLA_HEADS = 4
GLA_GATE_RANK = 16
GLA_GATE_NORM = 16.0
GLA_CHUNK = 64
TOP_K = 4
SWIGLU_LIMIT = 7.0
SWIGLU_ALPHA = 1.702

LANES = 128
SUBLANES = 8
V7X_VMEM_BYTES = 64 * 1024 * 1024
VMEM_LIMIT_BYTES = V7X_VMEM_BYTES - 8 * 1024 * 1024

TOKEN_TILE = 512
SSD_STEP_ROWS = 512
CONV_CARRY = 16
GLA_STEP_ROWS = 512
GLA_GROUP_ROWS = 256
ROUTE_TILE = 512
EXPERT_BLOCK = 512
EXPERT_PATHS = 4
DISPATCH_TILE = 1024
COMBINE_TILE = 512

_NT = (((1,), (1,)), ((), ()))
_TN = (((0,), (0,)), ((), ()))


def _sigmoid(v):
    return 0.5 * jnp.tanh(0.5 * v) + 0.5


def _silu(v):
    return v * _sigmoid(v)


def _softplus(v):
    return jnp.maximum(v, 0.0) + jnp.log1p(jnp.exp(-jnp.abs(v)))


def _log_sigmoid(v):
    return jnp.minimum(v, 0.0) - jnp.log(1.0 + jnp.exp(-jnp.abs(v)))


def _rms(v):
    return v * lax.rsqrt(jnp.mean(v * v, axis=-1, keepdims=True) + EPS)


def _pieces(a, n):
    out = []
    for _ in range(n - 1):
        p = a.astype(BF16)
        out.append(p)
        a = a - p.astype(F32)
    out.append(a.astype(BF16))
    return out


def _token_rows_load(ref, rows):
    return jnp.concatenate([ref[pl.ds(s, rows, stride=SUBLANES), :] for s in range(SUBLANES)], axis=1)


def _token_rows_store(ref, v):
    rows = v.shape[0]
    for s in range(SUBLANES):
        ref[pl.ds(s, rows, stride=SUBLANES), :] = v[:, s * LANES:(s + 1) * LANES]


def _params(*semantics):
    return pltpu.CompilerParams(dimension_semantics=semantics, vmem_limit_bytes=VMEM_LIMIT_BYTES)


def _const_spec(shape):
    nd = len(shape)
    return pl.BlockSpec(shape, lambda *_: (0,) * nd)


def _ada_kernel(c_ref, w_ref, b_ref, o_ref):
    cond = _silu(c_ref[...])
    o_ref[...] = jnp.dot(cond, w_ref[...], precision=HIGHEST, preferred_element_type=F32) + b_ref[...]


def _ada(c_pad, w_ada, b_ada):
    rows, d = c_pad.shape
    n = w_ada.shape[1]
    tn = d
    return pl.pallas_call(
        _ada_kernel,
        out_shape=jax.ShapeDtypeStruct((rows, n), F32),
        grid=(n // tn,),
        in_specs=[pl.BlockSpec((rows, d), lambda j: (0, 0)),
                  pl.BlockSpec((d, tn), lambda j: (0, j)),
                  pl.BlockSpec((1, tn), lambda j: (0, j))],
        out_specs=pl.BlockSpec((rows, tn), lambda j: (0, j)),
        compiler_params=_params("arbitrary"),
        name="ada",
    )(c_pad, w_ada, b_ada)


def _inproj_kernel(x_ref, sc_ref, sh_ref, g_ref, w_ref, *refs, sections):
    o_refs, ws_ref = refs[:-1], refs[-1]

    @pl.when(pl.program_id(0) == 0)
    def _():
        off = 0
        for o_ref, pieces in zip(o_refs, sections):
            n = o_ref.shape[1]
            used = 0
            for src, width in pieces:
                ws_ref[:, off + used:off + used + width] = w_ref[:, src:src + width]
                used += width
            if used < n:
                ws_ref[:, off + used:off + n] = jnp.zeros((ws_ref.shape[0], n - used), ws_ref.dtype)
            off += n

    h = (_rms(x_ref[...]) * g_ref[...]) * (1.0 + sc_ref[0]) + sh_ref[0]
    hb = h.astype(BF16)
    off = 0
    for o_ref in o_refs:
        n = o_ref.shape[1]
        o_ref[...] = jnp.dot(hb, ws_ref[:, off:off + n], preferred_element_type=F32).astype(o_ref.dtype)
        off += n


def _inproj(x2, sc, sh, g, w_bf, sections, widths, out_dtypes, seq):
    t, d = x2.shape
    tm = min(TOKEN_TILE, seq)
    per_batch = seq // tm
    assert all(n % LANES == 0 for n in widths)
    mod_spec = pl.BlockSpec((1, 1, d), lambda i: (i // per_batch, 0, 0))
    kern = functools.partial(_inproj_kernel, sections=sections)
    return pl.pallas_call(
        kern,
        out_shape=[jax.ShapeDtypeStruct((t, n), dt) for n, dt in zip(widths, out_dtypes)],
        grid=(t // tm,),
        in_specs=[pl.BlockSpec((tm, d), lambda i: (i, 0)), mod_spec, mod_spec, _const_spec((1, d)),
                  pl.BlockSpec(w_bf.shape, lambda i: (0, 0), pipeline_mode=pl.Buffered(1))],
        out_specs=[pl.BlockSpec((tm, n), lambda i: (i, 0)) for n in widths],
        scratch_shapes=[pltpu.VMEM((d, sum(widths)), BF16)],
        compiler_params=_params("arbitrary"),
        name="inproj",
    )(x2, sc, sh, g, w_bf)


def _ssd_kernel(tail_ref, xbc_ref, z_ref, sm_ref, cw_ref, cb_ref, dtb_r_ref, dtb_c_ref, alog_r_ref, alog_c_ref,
                dsk_ref, g_ref, y_ref, st_ref, *, n_heads, d_ssm):
    L = SSM_CHUNK
    P = SSM_HEAD_DIM
    N = SSM_STATE
    G = SSM_GROUPS
    gw = d_ssm // G
    R = xbc_ref.shape[0]
    c = pl.program_id(1)

    @pl.when(c == 0)
    def _():
        st_ref[...] = jnp.zeros(st_ref.shape, F32)

    tail = tail_ref[...]
    tail = jnp.where(c == 0, jnp.zeros_like(tail), tail)
    conv_in = jnp.concatenate([tail, xbc_ref[...]], axis=0)

    ri = lax.broadcasted_iota(I32, (L, L), 0)
    ci = lax.broadcasted_iota(I32, (L, L), 1)
    causal = ri >= ci
    causal3 = jnp.concatenate([causal.astype(BF16)] * 3, axis=1)
    upper3 = jnp.concatenate([(ri <= ci).astype(BF16)] * 3, axis=0)
    wr = lax.broadcasted_iota(I32, ((SSM_CONV - 1) * L, CONV_CARRY + L), 0)
    wc = lax.broadcasted_iota(I32, ((SSM_CONV - 1) * L, CONV_CARRY + L), 1)
    shifts = (wc == (wr % L) + CONV_CARRY - (SSM_CONV - 1) + wr // L).astype(BF16)
    hh = lax.broadcasted_iota(I32, (n_heads, d_ssm), 0)
    jj = lax.broadcasted_iota(I32, (n_heads, d_ssm), 1)
    expand = ((jj // P) == hh).astype(BF16)
    expand2 = jnp.concatenate([expand, expand], axis=0)
    lane = lax.broadcasted_iota(I32, (L, LANES), 1)
    first_half = lane < P
    heads_per_group = n_heads // G
    a_row = -jnp.exp(alog_r_ref[...])
    a_col = -jnp.exp(alog_c_ref[...])

    for s in range(R // L):
        rows = slice(s * L, (s + 1) * L)
        window = conv_in[s * L:s * L + CONV_CARRY + L, :]
        taps = jnp.dot(shifts, window, preferred_element_type=F32)
        acc = cb_ref[...] + cw_ref[SSM_CONV - 1:SSM_CONV, :] * window[CONV_CARRY:, :].astype(F32)
        for j in range(SSM_CONV - 1):
            acc = acc + cw_ref[j:j + 1, :] * taps[j * L:(j + 1) * L, :]
        act = _silu(acc)
        xs = act[:, :d_ssm]
        bm = act[:, d_ssm:d_ssm + G * N].astype(BF16)
        cm = act[:, d_ssm + G * N:].astype(BF16)

        sm = sm_ref[rows, :]
        dt_col = _softplus(sm[:, :n_heads] + dtb_r_ref[...])
        dt_row = _softplus(sm.T[:n_heads, :] + dtb_c_ref[...])
        cs_col = jnp.dot(causal3, jnp.concatenate(_pieces(dt_col * a_row, 3), axis=0),
                         preferred_element_type=F32)
        cs_row = jnp.dot(jnp.concatenate(_pieces(dt_row * a_col, 3), axis=1), upper3,
                         preferred_element_type=F32)
        cs_last = cs_col[L - 1:L, :]

        per_head = jnp.concatenate([dt_col, jnp.exp(cs_col), jnp.exp(cs_last - cs_col)], axis=0)
        per_ch = jnp.dot(jnp.concatenate(_pieces(per_head, 2), axis=1), expand2,
                         preferred_element_type=F32)
        dt_e, ecs_e, dte_e = per_ch[0:L], per_ch[L:2 * L], per_ch[2 * L:3 * L]
        chunk_decay = ecs_e[L - 1:L, :]

        xdt = xs * dt_e
        xdt_b = xdt.astype(BF16)
        xdec_b = (xdt * dte_e).astype(BF16)

        y_parts = []
        y_off_parts = []
        for g in range(G):
            bm_g = bm[:, g * N:(g + 1) * N]
            cm_g = cm[:, g * N:(g + 1) * N]
            cb = lax.dot_general(cm_g, bm_g, _NT, preferred_element_type=F32)
            prev = st_ref[g]
            y_off_parts.append(jnp.dot(cm_g, prev.astype(BF16), preferred_element_type=F32))
            s_new = lax.dot_general(bm_g, xdec_b[:, g * gw:(g + 1) * gw], _TN, preferred_element_type=F32)
            st_ref[g] = prev * chunk_decay[:, g * gw:(g + 1) * gw] + s_new
            for p in range(heads_per_group // 2):
                h0 = g * heads_per_group + 2 * p
                ms = []
                for h in (h0, h0 + 1):
                    diff = cs_col[:, h:h + 1] - cs_row[h:h + 1, :]
                    ms.append((cb * jnp.exp(jnp.where(causal, diff, -jnp.inf))).astype(BF16))
                lhs = jnp.concatenate(ms, axis=1)
                xp = xdt_b[:, h0 * P:(h0 + 2) * P]
                zero = jnp.zeros_like(xp)
                rhs = jnp.concatenate([jnp.where(first_half, xp, zero), jnp.where(first_half, zero, xp)], axis=0)
                y_parts.append(jnp.dot(lhs, rhs, preferred_element_type=F32))
        y = jnp.concatenate(y_parts, axis=1) + jnp.concatenate(y_off_parts, axis=1) * ecs_e + dsk_ref[...] * xs
        y = y * _silu(z_ref[rows, :].astype(F32))
        y = jnp.concatenate([_rms(y[:, g * gw:(g + 1) * gw]) for g in range(G)], axis=1) * g_ref[...]
        y_ref[rows, :] = y.astype(y_ref.dtype)


def _ssd(xbc, z, small, conv_w, conv_b, dt_bias, a_log, d_skip_e, norm_g, batch, seq):
    t, cd = xbc.shape
    d_ssm = z.shape[1]
    n_heads = dt_bias.shape[0]
    L = min(SSD_STEP_ROWS, seq)
    nc = seq // L
    row = lambda b, c: (b * nc + c, 0)
    tail = lambda b, c: (jnp.maximum((b * nc + c) * (L // CONV_CARRY) - 1, 0), 0)
    kern = functools.partial(_ssd_kernel, n_heads=n_heads, d_ssm=d_ssm)
    return pl.pallas_call(
        kern,
        out_shape=jax.ShapeDtypeStruct((t, d_ssm), BF16),
        grid=(batch, nc),
        in_specs=[pl.BlockSpec((CONV_CARRY, cd), tail),
                  pl.BlockSpec((L, cd), row), pl.BlockSpec((L, d_ssm), row), pl.BlockSpec((L, LANES), row),
                  _const_spec(conv_w.shape), _const_spec((1, cd)),
                  _const_spec((1, n_heads)), _const_spec((n_heads, 1)),
                  _const_spec((1, n_heads)), _const_spec((n_heads, 1)),
                  _const_spec((1, d_ssm)), _const_spec((1, d_ssm))],
        out_specs=pl.BlockSpec((L, d_ssm), row),
        scratch_shapes=[pltpu.VMEM((SSM_GROUPS, SSM_STATE, d_ssm // SSM_GROUPS), F32)],
        compiler_params=_params("arbitrary", "arbitrary"),
        name="ssd",
    )(xbc, xbc, z, small, conv_w, conv_b.reshape(1, cd), dt_bias.reshape(1, n_heads), dt_bias.reshape(n_heads, 1),
      a_log.reshape(1, n_heads), a_log.reshape(n_heads, 1), d_skip_e, norm_g.reshape(1, d_ssm))


def _gla_kernel(q_ref, k_ref, v_ref, r_ref, sm_ref, wg2_ref, bg_ref, gn_ref, o_ref, st_ref, *, gate_col):
    L = GLA_CHUNK
    H = GLA_HEADS
    dk = q_ref.shape[1] // H
    dv = v_ref.shape[1] // H
    c = pl.program_id(1)

    @pl.when(c == 0)
    def _():
        st_ref[...] = jnp.zeros(st_ref.shape, F32)

    R = min(GLA_GROUP_ROWS, q_ref.shape[0])
    n_chunks = R // L
    ri = lax.broadcasted_iota(I32, (R, R), 0)
    ci = lax.broadcasted_iota(I32, (R, R), 1)
    same_chunk = (ri // L) == (ci // L)
    causal = jnp.logical_and(same_chunk, ri >= ci)
    tril = causal.astype(BF16)
    later = jnp.logical_and(same_chunk, ri < ci).astype(BF16)
    sr = lax.broadcasted_iota(I32, (R, n_chunks * LANES), 0)
    sc = lax.broadcasted_iota(I32, (R, n_chunks * LANES), 1)
    last_rows = (sr == (sc // LANES) * L + (L - 1)).astype(BF16)
    wg2_hi, wg2_lo = _pieces(wg2_ref[...], 2)

    tril3 = jnp.concatenate([jnp.concatenate([tril] * 3, axis=1), jnp.concatenate([later] * 3, axis=1)], axis=0)
    last3 = jnp.concatenate([last_rows] * 3, axis=0)
    wg3 = jnp.concatenate([wg2_hi, wg2_hi, wg2_lo], axis=0)

    for gi in range(q_ref.shape[0] // R):
        rs = slice(gi * R, (gi + 1) * R)
        q = q_ref[rs, :].astype(F32) * (dk ** -0.5)
        k = k_ref[rs, :].astype(F32)
        v = v_ref[rs, :]
        r = r_ref[rs, :].astype(F32)
        g_hi, g_lo = _pieces(sm_ref[rs, gate_col:gate_col + GLA_GATE_RANK], 2)
        pre = jnp.dot(jnp.concatenate([g_hi, g_lo, g_hi], axis=1), wg3, preferred_element_type=F32) + bg_ref[...]
        gk3 = jnp.concatenate(_pieces(_log_sigmoid(pre) / GLA_GATE_NORM, 3), axis=0)
        sums = jnp.dot(tril3, gk3, preferred_element_type=F32)
        bcum = sums[:R]
        to_end = sums[R:]
        q_t = (q * jnp.exp(bcum)).astype(BF16)
        k_t = (k * jnp.exp(-bcum)).astype(BF16)
        k_dec = (k * jnp.exp(to_end)).astype(BF16)
        dcol = jnp.exp(lax.dot_general(jnp.concatenate(_pieces(bcum, 3), axis=0), last3, _TN,
                                       preferred_element_type=F32))
        outs = []
        for h in range(H):
            ks = slice(h * dk, (h + 1) * dk)
            vs = slice(h * dv, (h + 1) * dv)
            att = lax.dot_general(q_t[:, ks], k_t[:, ks], _NT, preferred_element_type=F32)
            att = jnp.where(causal, att, 0.0).astype(BF16)
            o = jnp.dot(att, v[:, vs], preferred_element_type=F32)
            state = st_ref[h]
            inter = []
            for c in range(n_chunks):
                rows = slice(c * L, (c + 1) * L)
                inter.append(jnp.dot(q_t[rows, ks], state.astype(BF16), preferred_element_type=F32))
                s_new = lax.dot_general(k_dec[rows, ks], v[rows, vs], _TN, preferred_element_type=F32)
                dec = dcol[ks, c * LANES:(c + 1) * LANES]
                state = state * jnp.concatenate([dec] * (dv // LANES), axis=1) + s_new
            st_ref[h] = state
            o = o + jnp.concatenate(inter, axis=0)
            outs.append(_rms(o) * gn_ref[...] * _silu(r[:, vs]))
        o_ref[rs, :] = jnp.concatenate(outs, axis=1).astype(o_ref.dtype)


def _gla(q, k, v, r, small, wg2, bg, norm_g, batch, seq, gate_col):
    t, dkt = q.shape
    dvt = v.shape[1]
    rows = min(GLA_STEP_ROWS, seq)
    nc = seq // rows
    row = lambda b, c: (b * nc + c, 0)
    kern = functools.partial(_gla_kernel, gate_col=gate_col)
    return pl.pallas_call(
        kern,
        out_shape=jax.ShapeDtypeStruct((t, dvt), BF16),
        grid=(batch, nc),
        in_specs=[pl.BlockSpec((rows, dkt), row), pl.BlockSpec((rows, dkt), row), pl.BlockSpec((rows, dvt), row),
                  pl.BlockSpec((rows, dvt), row), pl.BlockSpec((rows, LANES), row),
                  _const_spec(wg2.shape), _const_spec((1, dkt)), _const_spec((1, dvt // GLA_HEADS))],
        out_specs=pl.BlockSpec((rows, dvt), row),
        scratch_shapes=[pltpu.VMEM((GLA_HEADS, dkt // GLA_HEADS, dvt // GLA_HEADS), F32)],
        compiler_params=_params("arbitrary", "arbitrary"),
        name="gla",
    )(q, k, v, r, small, wg2, bg.reshape(1, dkt), norm_g.reshape(1, dvt // GLA_HEADS))


def _outproj_kernel(y_ref, o_ref, x_ref, gt_ref, sc_ref, sh_ref, g_ref, wy_ref, wo_ref, wr_ref, br_ref,
                    x1_ref, h_ref, ti_ref, tw_ref, cnt_ref):
    mix = (jnp.dot(y_ref[...], wy_ref[...], preferred_element_type=F32)
           + jnp.dot(o_ref[...], wo_ref[...], preferred_element_type=F32))
    x1 = x_ref[...] + gt_ref[0] * mix
    x1_ref[...] = x1
    h = (_rms(x1) * g_ref[...]) * (1.0 + sc_ref[0]) + sh_ref[0]
    _token_rows_store(h_ref, h)
    n_e = br_ref.shape[0]
    h_hi, h_lo = _pieces(h, 2)
    wr = wr_ref[...]
    hw = lax.dot_general(wr, h_hi, _NT, preferred_element_type=F32)
    logits = (hw[:n_e] + hw[n_e:] + lax.dot_general(wr[:n_e], h_lo, _NT, preferred_element_type=F32)) + br_ref[...]
    expert = lax.broadcasted_iota(I32, logits.shape, 0)
    vals, idxs = [], []
    counts = jnp.zeros(logits.shape, F32)
    for _ in range(TOP_K):
        m = jnp.max(logits, axis=0, keepdims=True)
        idx = jnp.min(jnp.where(logits == m, expert, n_e), axis=0, keepdims=True)
        vals.append(m)
        idxs.append(idx)
        chosen = expert == idx
        counts = counts + chosen.astype(F32)
        logits = jnp.where(chosen, -jnp.inf, logits)
    exps = [jnp.exp(v - vals[0]) for v in vals]
    denom = functools.reduce(lambda a, b: a + b, exps)
    ti_ref[...] = jnp.concatenate(idxs, axis=0)
    tw_ref[...] = jnp.concatenate([e / denom for e in exps], axis=0)

    @pl.when(pl.program_id(0) == 0)
    def _():
        cnt_ref[...] = jnp.zeros(cnt_ref.shape, F32)

    cnt_ref[...] = cnt_ref[...] + jnp.sum(counts, axis=1, keepdims=True)


def _outproj(y, o, x2, gt, sc, sh, g, wy, wo, w_router, b_router, seq):
    t, d = x2.shape
    n_e = w_router.shape[1]
    tm = min(TOKEN_TILE, seq)
    per_batch = seq // tm
    row = lambda i: (i, 0)
    mod_spec = pl.BlockSpec((1, 1, d), lambda i: (i // per_batch, 0, 0))
    wr_t = w_router.T
    wr_hi = wr_t.astype(BF16)
    wr_cat = jnp.concatenate([wr_hi, (wr_t - wr_hi.astype(F32)).astype(BF16)], axis=0)
    col = lambda i: (0, i)
    return pl.pallas_call(
        _outproj_kernel,
        out_shape=[jax.ShapeDtypeStruct((t, d), F32), jax.ShapeDtypeStruct((t * SUBLANES, LANES), F32),
                   jax.ShapeDtypeStruct((TOP_K, t), I32), jax.ShapeDtypeStruct((TOP_K, t), F32),
                   jax.ShapeDtypeStruct((n_e, LANES), F32)],
        grid=(t // tm,),
        in_specs=[pl.BlockSpec((tm, y.shape[1]), row), pl.BlockSpec((tm, o.shape[1]), row),
                  pl.BlockSpec((tm, d), row), mod_spec, mod_spec, mod_spec, _const_spec((1, d)),
                  _const_spec(wy.shape), _const_spec(wo.shape), _const_spec(wr_cat.shape),
                  _const_spec((n_e, 1))],
        out_specs=[pl.BlockSpec((tm, d), row), pl.BlockSpec((tm * SUBLANES, LANES), row),
                   pl.BlockSpec((TOP_K, tm), col), pl.BlockSpec((TOP_K, tm), col),
                   _const_spec((n_e, LANES))],
        compiler_params=_params("arbitrary"),
        name="outproj",
    )(y, o, x2, gt, sc, sh, g, wy, wo, wr_cat, b_router.reshape(n_e, 1))


def _route_kernel(ti_ref, cnt_ref, dest_ref, be_ref, pend_ref, run_ref, *, n_blocks_pad):
    i = pl.program_id(0)
    n_e = cnt_ref.shape[0]
    tr = ti_ref.shape[1]

    @pl.when(i == 0)
    def _():
        counts = cnt_ref[...]
        padded = jnp.ceil(counts / EXPERT_BLOCK) * EXPERT_BLOCK
        ri = lax.broadcasted_iota(I32, (n_e, n_e), 0)
        ci = lax.broadcasted_iota(I32, (n_e, n_e), 1)
        pend = jnp.dot((ri >= ci).astype(F32), padded, precision=HIGHEST, preferred_element_type=F32)
        pend_ref[...] = pend
        run_ref[...] = pend - padded
        start = (lax.broadcasted_iota(I32, (n_e, n_blocks_pad), 1) * EXPERT_BLOCK).astype(F32)
        be = jnp.sum((pend[:, 0:1] <= start).astype(F32), axis=0, keepdims=True)
        be_ref[...] = jnp.minimum(be, n_e - 1).astype(I32)

    ti = ti_ref[...]
    expert = lax.broadcasted_iota(I32, (n_e, tr), 0)
    onehots = [expert == ti[k:k + 1, :] for k in range(TOP_K)]
    cnt = functools.reduce(lambda a, b: a + b, [oh.astype(F32) for oh in onehots])
    ri = lax.broadcasted_iota(I32, (tr, tr), 0)
    ci = lax.broadcasted_iota(I32, (tr, tr), 1)
    before = jnp.dot(cnt.astype(BF16), (ri < ci).astype(BF16), preferred_element_type=F32)
    base = run_ref[:, 0:1] + before
    dest = [jnp.sum(jnp.where(oh, base, 0.0), axis=0, keepdims=True) for oh in onehots]
    dest_ref[...] = jnp.concatenate(dest, axis=0).astype(I32)
    run_ref[...] = run_ref[...] + jnp.sum(cnt, axis=1, keepdims=True)


def _route(topi_t, counts, n_blocks):
    t = topi_t.shape[1]
    n_e = counts.shape[0]
    tr = min(ROUTE_TILE, t)
    n_blocks_pad = -(-n_blocks // LANES) * LANES
    kern = functools.partial(_route_kernel, n_blocks_pad=n_blocks_pad)
    return pl.pallas_call(
        kern,
        out_shape=[jax.ShapeDtypeStruct((TOP_K, t), I32), jax.ShapeDtypeStruct((1, n_blocks_pad), I32),
                   jax.ShapeDtypeStruct((n_e, LANES), F32)],
        grid=(t // tr,),
        in_specs=[pl.BlockSpec((TOP_K, tr), lambda i: (0, i)), _const_spec((n_e, LANES))],
        out_specs=[pl.BlockSpec((TOP_K, tr), lambda i: (0, i)), _const_spec((1, n_blocks_pad)),
                   _const_spec((n_e, LANES))],
        scratch_shapes=[pltpu.VMEM((n_e, LANES), F32)],
        compiler_params=_params("arbitrary"),
        name="route",
    )(topi_t, counts)


def _dispatch_kernel(pend_ref, dest_hbm, h_ref, xs_hbm, idx_ref, zero_ref, idx_sem, row_sem, *, n_experts):
    i = pl.program_id(0)
    tg = idx_ref.shape[1]

    @pl.when(i == 0)
    def _():
        zero_ref[...] = jnp.zeros(zero_ref.shape, zero_ref.dtype)
        for e in range(n_experts):
            end = pend_ref[e]
            prev = pend_ref[e - 1] if e > 0 else 0

            @pl.when(end > prev)
            def _():
                start = pl.multiple_of((end - EXPERT_BLOCK) * SUBLANES, EXPERT_BLOCK * SUBLANES)
                cp = pltpu.make_async_copy(zero_ref, xs_hbm.at[pl.ds(start, EXPERT_BLOCK * SUBLANES)], row_sem)
                cp.start()
                cp.wait()

        n_blocks = xs_hbm.shape[0] // (EXPERT_BLOCK * SUBLANES)
        total = pend_ref[n_experts - 1]
        for b in range(n_blocks - n_experts, n_blocks):
            @pl.when(b * EXPERT_BLOCK >= total)
            def _():
                cp = pltpu.make_async_copy(
                    zero_ref, xs_hbm.at[pl.ds(b * EXPERT_BLOCK * SUBLANES, EXPERT_BLOCK * SUBLANES)], row_sem)
                cp.start()
                cp.wait()

    n = pl.num_programs(0)

    def idx_copy(tile, s):
        return pltpu.make_async_copy(dest_hbm.at[:, pl.ds(pl.multiple_of(tile * tg, tg), tg)],
                                     idx_ref.at[pl.ds(s * TOP_K, TOP_K)], idx_sem.at[s])

    @pl.when(i == 0)
    def _():
        idx_copy(0, 0).start()

    def step(slot):
        idx_copy(i, slot).wait()

        @pl.when(i + 1 < n)
        def _():
            idx_copy(i + 1, 1 - slot).start()

        def issue(tl, carry):
            src = h_ref.at[pl.ds(pl.multiple_of(tl * SUBLANES, SUBLANES), SUBLANES)]
            for k in range(TOP_K):
                d = pl.multiple_of(idx_ref[slot * TOP_K + k, tl] * SUBLANES, SUBLANES)
                pltpu.make_async_copy(src, xs_hbm.at[pl.ds(d, SUBLANES)], row_sem).start(priority=k % 2)
            return carry

        lax.fori_loop(0, tg, issue, 0)

    for parity in range(2):
        pl.when(lax.rem(i, 2) == parity)(functools.partial(step, parity))
    for _ in range(TOP_K):
        pltpu.make_async_copy(h_ref, xs_hbm.at[pl.ds(0, tg * SUBLANES)], row_sem).wait()


def _dispatch(pend_i, dest_t, h, n_rows, n_experts):
    t = h.shape[0] // SUBLANES
    tg = min(DISPATCH_TILE, t)
    kern = functools.partial(_dispatch_kernel, n_experts=n_experts)
    return pl.pallas_call(
        kern,
        out_shape=jax.ShapeDtypeStruct((n_rows * SUBLANES, LANES), h.dtype),
        grid_spec=pltpu.PrefetchScalarGridSpec(
            num_scalar_prefetch=1,
            grid=(t // tg,),
            in_specs=[pl.BlockSpec(memory_space=pl.ANY),
                      pl.BlockSpec((tg * SUBLANES, LANES), lambda i, pend: (i, 0))],
            out_specs=pl.BlockSpec(memory_space=pl.ANY),
            scratch_shapes=[pltpu.SMEM((2 * TOP_K, tg), I32), pltpu.VMEM((EXPERT_BLOCK * SUBLANES, LANES), h.dtype),
                            pltpu.SemaphoreType.DMA((2,)), pltpu.SemaphoreType.DMA]),
        compiler_params=pltpu.CompilerParams(dimension_semantics=("arbitrary",), has_side_effects=True,
                                             vmem_limit_bytes=VMEM_LIMIT_BYTES),
        name="dispatch",
    )(pend_i, dest_t, h)


def _expert_kernel(be_ref, nu_ref, pend_ref, xs_ref, wg_hbm, bg_ref, wu_hbm, bu_ref, wd_hbm, bd_ref, y_ref,
                   wg_f, wu_f, wd_f, wg_b, wu_b, wd_b, slot_ref, sems):
    i = pl.program_id(0)
    used = i < nu_ref[0]
    e = be_ref[i]

    def fetch(expert, slot):
        return [pltpu.make_async_copy(src.at[expert], dst.at[slot], sems.at[slot])
                for src, dst in ((wg_hbm, wg_f), (wu_hbm, wu_f), (wd_hbm, wd_f))]

    @pl.when(i == 0)
    def _():
        slot_ref[0] = 0
        for cp in fetch(e, 0):
            cp.start()

    first_of_expert = jnp.logical_or(i == 0, e != be_ref[jnp.maximum(i - 1, 0)])

    @pl.when(jnp.logical_and(used, first_of_expert))
    def _():
        slot = slot_ref[0]
        for cp in fetch(e, slot):
            cp.wait()
        wg_b[...] = wg_f[slot].astype(BF16)
        wu_b[...] = wu_f[slot].astype(BF16)
        wd_b[...] = wd_f[slot].astype(BF16)
        nxt = lax.div(pend_ref[e], EXPERT_BLOCK)

        @pl.when(nxt < nu_ref[0])
        def _():
            for cp in fetch(be_ref[nxt], 1 - slot):
                cp.start(priority=1)

        slot_ref[0] = 1 - slot

    def ffn(rows):
        x = _token_rows_load(xs_ref, rows).astype(BF16)
        gate = jnp.minimum(jnp.dot(x, wg_b[...], preferred_element_type=F32) + bg_ref[...], SWIGLU_LIMIT)
        up = jnp.clip(jnp.dot(x, wu_b[...], preferred_element_type=F32) + bu_ref[...],
                      -SWIGLU_LIMIT, SWIGLU_LIMIT)
        glu = gate * _sigmoid(SWIGLU_ALPHA * gate)
        mid = ((up + 1.0) * glu).astype(BF16)
        y = jnp.dot(mid, wd_b[...], preferred_element_type=F32) + bd_ref[...]
        _token_rows_store(y_ref, y)
        if rows < EXPERT_BLOCK:
            y_ref[rows * SUBLANES:, :] = jnp.zeros(((EXPERT_BLOCK - rows) * SUBLANES, LANES), y_ref.dtype)

    valid = pend_ref[pend_ref.shape[0] // 2 + e] - i * EXPERT_BLOCK
    quarter = EXPERT_BLOCK // EXPERT_PATHS
    for p in range(1, EXPERT_PATHS + 1):
        covers = valid <= p * quarter if p < EXPERT_PATHS else True
        needs = valid > (p - 1) * quarter if p > 1 else True
        pl.when(jnp.logical_and(used, jnp.logical_and(covers, needs)))(functools.partial(ffn, p * quarter))

    @pl.when(jnp.logical_not(used))
    def _():
        y_ref[...] = jnp.zeros(y_ref.shape, y_ref.dtype)


def _experts(block_e, n_used, pend_i, xs, w_gate, b_gate, w_up, b_up, w_down, b_down):
    n_rows = xs.shape[0] // SUBLANES
    n_e, d, f = w_gate.shape
    nb = n_rows // EXPERT_BLOCK
    blk = (EXPERT_BLOCK * SUBLANES, LANES)
    last = lambda i, be, nu, pend: jnp.maximum(jnp.minimum(i, nu[0] - 1), 0)
    bspec = lambda n: pl.BlockSpec((None, 1, n), lambda i, be, nu, pend: (be[last(i, be, nu, pend)], 0, 0))
    hbm = pl.BlockSpec(memory_space=pl.ANY)
    return pl.pallas_call(
        _expert_kernel,
        out_shape=jax.ShapeDtypeStruct((n_rows * SUBLANES, LANES), F32),
        grid_spec=pltpu.PrefetchScalarGridSpec(
            num_scalar_prefetch=3,
            grid=(nb,),
            in_specs=[pl.BlockSpec(blk, lambda i, be, nu, pend: (last(i, be, nu, pend), 0)),
                      hbm, bspec(f), hbm, bspec(f), hbm, bspec(d)],
            out_specs=pl.BlockSpec(blk, lambda i, be, nu, pend: (i, 0)),
            scratch_shapes=[pltpu.VMEM((2, d, f), F32), pltpu.VMEM((2, d, f), F32), pltpu.VMEM((2, f, d), F32),
                            pltpu.VMEM((d, f), BF16), pltpu.VMEM((d, f), BF16), pltpu.VMEM((f, d), BF16),
                            pltpu.SMEM((1,), I32), pltpu.SemaphoreType.DMA((2,))]),
        compiler_params=_params("arbitrary"),
        name="experts",
    )(block_e, n_used, pend_i, xs, w_gate, b_gate.reshape(n_e, 1, f), w_up, b_up.reshape(n_e, 1, f),
      w_down, b_down.reshape(n_e, 1, d))


def _combine_kernel(dest_hbm, ys_hbm, tw_ref, x1_ref, gt_ref, g_ref, o_ref, idx_ref, buf_ref, idx_sem, row_sem):
    i = pl.program_id(0)
    n = pl.num_programs(0)
    tc = idx_ref.shape[1]

    def idx_copy(tile, s):
        return pltpu.make_async_copy(dest_hbm.at[:, pl.ds(pl.multiple_of(tile * tc, tc), tc)],
                                     idx_ref.at[pl.ds(s * TOP_K, TOP_K)], idx_sem.at[s])

    def issue_rows(s):
        def issue(tl, carry):
            dst_row = pl.multiple_of(tl * SUBLANES, SUBLANES)
            for k in range(TOP_K):
                d = pl.multiple_of(idx_ref[s * TOP_K + k, tl] * SUBLANES, SUBLANES)
                pltpu.make_async_copy(ys_hbm.at[pl.ds(d, SUBLANES)], buf_ref.at[s, k, pl.ds(dst_row, SUBLANES)],
                                      row_sem.at[s]).start(priority=k % 2)
            return carry

        lax.fori_loop(0, tc, issue, 0)

    @pl.when(i == 0)
    def _():
        idx_copy(0, 0).start()
        idx_copy(0, 0).wait()
        issue_rows(0)

        @pl.when(n > 1)
        def _():
            idx_copy(1, 1).start()

    def step(slot):
        @pl.when(i + 1 < n)
        def _():
            idx_copy(i + 1, 1 - slot).wait()
            issue_rows(1 - slot)

            @pl.when(i + 2 < n)
            def _():
                idx_copy(i + 2, slot).start()

        for k in range(TOP_K):
            pltpu.make_async_copy(ys_hbm.at[pl.ds(0, tc * SUBLANES)], buf_ref.at[slot, k], row_sem.at[slot]).wait()

        tw = tw_ref[...]
        ffn = tw[:, 0:1] * _token_rows_load(buf_ref.at[slot, 0], tc)
        for k in range(1, TOP_K):
            ffn = ffn + tw[:, k:k + 1] * _token_rows_load(buf_ref.at[slot, k], tc)
        x2 = x1_ref[...] + gt_ref[0] * ffn
        o_ref[...] = _rms(x2) * g_ref[...]

    for parity in range(2):
        pl.when(lax.rem(i, 2) == parity)(functools.partial(step, parity))


def _combine(dest_t, ys, topw, x1, gt, g, seq):
    t, d = x1.shape
    tc = min(COMBINE_TILE, seq)
    per_batch = seq // tc
    row = lambda i: (i, 0)
    return pl.pallas_call(
        _combine_kernel,
        out_shape=jax.ShapeDtypeStruct((t, d), F32),
        grid=(t // tc,),
        in_specs=[pl.BlockSpec(memory_space=pl.ANY), pl.BlockSpec(memory_space=pl.ANY),
                  pl.BlockSpec((tc, TOP_K), row), pl.BlockSpec((tc, d), row),
                  pl.BlockSpec((1, 1, d), lambda i: (i // per_batch, 0, 0)), _const_spec((1, d))],
        out_specs=pl.BlockSpec((tc, d), row),
        scratch_shapes=[pltpu.SMEM((2 * TOP_K, tc), I32), pltpu.VMEM((2, TOP_K, tc * SUBLANES, LANES), F32),
                        pltpu.SemaphoreType.DMA((2,)), pltpu.SemaphoreType.DMA((2,))],
        compiler_params=_params("arbitrary"),
        name="combine",
    )(dest_t, ys, topw, x1, gt, g)


def _layer(x2, mod, batch, seq, norm1_g, w_in, conv_w, conv_b, dt_bias, a_log, d_skip, ssm_norm_g,
           gla_wg2, gla_bg, gla_norm_g, w_out, norm2_g, w_router, b_router,
           w_gate, b_gate, w_up, b_up, w_down, b_down):
    t, d = x2.shape
    n_heads = dt_bias.shape[0]
    d_ssm = n_heads * SSM_HEAD_DIM
    cd = conv_w.shape[1]
    dkt = gla_wg2.shape[1]
    dvt = w_out.shape[0] - d_ssm
    n_experts = w_router.shape[1]

    sh1, sc1, gt1, sh2, sc2, gt2 = [m.reshape(batch, 1, d) for m in jnp.split(mod[:batch], 6, axis=1)]

    sizes = (d_ssm, cd, n_heads, dkt, dkt, dvt, GLA_GATE_RANK, dvt)
    offs = [0]
    for s in sizes:
        offs.append(offs[-1] + s)
    piece = lambda j: (offs[j], sizes[j])
    sections = ((piece(0),), (piece(1),), (piece(3),), (piece(4),), (piece(5),), (piece(7),), (piece(2), piece(6)))
    z, xbc, q, k, v, r, small = _inproj(x2, sc1, sh1, norm1_g.reshape(1, d), w_in.astype(BF16), sections,
                                        (d_ssm, cd, dkt, dkt, dvt, dvt, LANES), [BF16] * 6 + [F32], seq)

    d_skip_e = jnp.repeat(d_skip, SSM_HEAD_DIM).reshape(1, d_ssm)
    y = _ssd(xbc, z, small, conv_w, conv_b, dt_bias, a_log, d_skip_e, ssm_norm_g, batch, seq)
    o = _gla(q, k, v, r, small, gla_wg2, gla_bg, gla_norm_g, batch, seq, gate_col=n_heads)

    x1, h2, topi_t, topw_t, counts = _outproj(y, o, x2, gt1, sc2, sh2, norm2_g.reshape(1, d),
                                        w_out[:d_ssm].astype(BF16), w_out[d_ssm:].astype(BF16),
                                        w_router, b_router, seq)

    n_blocks = (t * TOP_K) // EXPERT_BLOCK + n_experts
    dest_t, block_e, pend = _route(topi_t, counts, n_blocks)
    pend_i = pend[:, 0].astype(I32)
    n_used = (pend_i[n_experts - 1:] // EXPERT_BLOCK).astype(I32)
    topw = topw_t.T
    xs = _dispatch(pend_i, dest_t, h2, n_blocks * EXPERT_BLOCK, n_experts)
    real_end = jnp.concatenate([jnp.zeros((1,), I32), pend_i[:-1]]) + counts[:, 0].astype(I32)
    seg_ends = jnp.concatenate([pend_i, real_end])
    ys = _experts(block_e[0, :n_blocks], n_used, seg_ends, xs, w_gate, b_gate, w_up, b_up, w_down, b_down)
    return dest_t, ys, topw, x1, gt2


def kernel(x, c, w_ada, b_ada, norm1_g, w_in, conv_w, conv_b, dt_bias, a_log, d_skip, ssm_norm_g, gla_wg2,
           gla_bg, gla_norm_g, w_out, norm2_g, w_router, b_router, w_gate, b_gate, w_up, b_up, w_down, b_down,
           final_norm_g):
    batch, seq, d = x.shape
    assert w_ada.shape[0] == 1, "single-layer trunk"
    assert d == SUBLANES * LANES, "token rows are moved as one (8, 128) f32 tile each"
    assert seq % min(seq, max(TOKEN_TILE, SSM_CHUNK, GLA_STEP_ROWS, COMBINE_TILE, DISPATCH_TILE)) == 0
    assert seq % max(SSM_CHUNK, GLA_STEP_ROWS) == 0
    x2 = x.reshape(batch * seq, d)
    c_pad = jnp.zeros((SUBLANES, d), F32).at[:batch].set(c)
    mod = _ada(c_pad, w_ada[0], b_ada)
    dest_t, ys, topw, x1, gt2 = _layer(
        x2, mod, batch, seq, norm1_g[0], w_in[0], conv_w[0], conv_b[0], dt_bias[0], a_log[0], d_skip[0],
        ssm_norm_g[0], gla_wg2[0], gla_bg[0], gla_norm_g[0], w_out[0], norm2_g[0], w_router[0], b_router[0],
        w_gate[0], b_gate[0], w_up[0], b_up[0], w_down[0], b_down[0])
    out = _combine(dest_t, ys, topw, x1, gt2, final_norm_g.reshape(1, d), seq)
    return out.reshape(batch, seq, d)
```

```python
import functools

import jax
import jax.numpy as jnp
from jax import lax
from jax.experimental import pallas as pl
from jax.experimental.pallas import tpu as pltpu

F32 = jnp.float32
BF16 = jnp.bfloat16
I32 = jnp.int32
HIGHEST = lax.Precision.HIGHEST

EPS = 1e-6
SSM_HEAD_DIM = 64
SSM_GROUPS = 2
SSM_STATE = 128
SSM_CONV = 4
SSM_CHUNK = 128
GLA_HEADS = 4
GLA_GATE_RANK = 16
GLA_GATE_NORM = 16.0
GLA_CHUNK = 64
TOP_K = 4
SWIGLU_LIMIT = 7.0
SWIGLU_ALPHA = 1.702

LANES = 128
SUBLANES = 8
V7X_VMEM_BYTES = 64 * 1024 * 1024
VMEM_LIMIT_BYTES = V7X_VMEM_BYTES - 8 * 1024 * 1024

TOKEN_TILE = 512
SSD_STEP_ROWS = 512
CONV_CARRY = 16
GLA_STEP_ROWS = 512
GLA_GROUP_ROWS = 128
ROUTE_TILE = 512
EXPERT_BLOCK = 512
EXPERT_PATHS = 4
DISPATCH_TILE = 1024
COMBINE_TILE = 512

_NT = (((1,), (1,)), ((), ()))
_TN = (((0,), (0,)), ((), ()))


def _sigmoid(v):
    return 0.5 * jnp.tanh(0.5 * v) + 0.5


def _silu(v):
    return v * _sigmoid(v)


def _softplus(v):
    return jnp.maximum(v, 0.0) + jnp.log1p(jnp.exp(-jnp.abs(v)))


def _log_sigmoid(v):
    return jnp.minimum(v, 0.0) - jnp.log(1.0 + jnp.exp(-jnp.abs(v)))


def _rms(v):
    return v * lax.rsqrt(jnp.mean(v * v, axis=-1, keepdims=True) + EPS)


def _pieces(a, n):
    out = []
    for _ in range(n - 1):
        p = a.astype(BF16)
        out.append(p)
        a = a - p.astype(F32)
    out.append(a.astype(BF16))
    return out


def _token_rows_load(ref, rows):
    return jnp.concatenate([ref[pl.ds(s, rows, stride=SUBLANES), :] for s in range(SUBLANES)], axis=1)


def _token_rows_store(ref, v):
    rows = v.shape[0]
    for s in range(SUBLANES):
        ref[pl.ds(s, rows, stride=SUBLANES), :] = v[:, s * LANES:(s + 1) * LANES]


def _params(*semantics):
    return pltpu.CompilerParams(dimension_semantics=semantics, vmem_limit_bytes=VMEM_LIMIT_BYTES)


def _const_spec(shape):
    nd = len(shape)
    return pl.BlockSpec(shape, lambda *_: (0,) * nd)


def _ada_kernel(c_ref, w_ref, b_ref, o_ref):
    cond = _silu(c_ref[...])
    o_ref[...] = jnp.dot(cond, w_ref[...], precision=HIGHEST, preferred_element_type=F32) + b_ref[...]


def _ada(c_pad, w_ada, b_ada):
    rows, d = c_pad.shape
    n = w_ada.shape[1]
    tn = d
    return pl.pallas_call(
        _ada_kernel,
        out_shape=jax.ShapeDtypeStruct((rows, n), F32),
        grid=(n // tn,),
        in_specs=[pl.BlockSpec((rows, d), lambda j: (0, 0)),
                  pl.BlockSpec((d, tn), lambda j: (0, j)),
                  pl.BlockSpec((1, tn), lambda j: (0, j))],
        out_specs=pl.BlockSpec((rows, tn), lambda j: (0, j)),
        compiler_params=_params("arbitrary"),
        name="ada",
    )(c_pad, w_ada, b_ada)


def _inproj_kernel(x_ref, sc_ref, sh_ref, g_ref, w_ref, *refs, sections):
    o_refs, ws_ref = refs[:-1], refs[-1]

    @pl.when(pl.program_id(0) == 0)
    def _():
        off = 0
        for o_ref, pieces in zip(o_refs, sections):
            n = o_ref.shape[1]
            used = 0
            for src, width in pieces:
                ws_ref[:, off + used:off + used + width] = w_ref[:, src:src + width]
                used += width
            if used < n:
                ws_ref[:, off + used:off + n] = jnp.zeros((ws_ref.shape[0], n - used), ws_ref.dtype)
            off += n

    h = (_rms(x_ref[...]) * g_ref[...]) * (1.0 + sc_ref[0]) + sh_ref[0]
    hb = h.astype(BF16)
    off = 0
    for o_ref in o_refs:
        n = o_ref.shape[1]
        o_ref[...] = jnp.dot(hb, ws_ref[:, off:off + n], preferred_element_type=F32).astype(o_ref.dtype)
        off += n


def _inproj(x2, sc, sh, g, w_bf, sections, widths, out_dtypes, seq):
    t, d = x2.shape
    tm = min(TOKEN_TILE, seq)
    per_batch = seq // tm
    assert all(n % LANES == 0 for n in widths)
    mod_spec = pl.BlockSpec((1, 1, d), lambda i: (i // per_batch, 0, 0))
    kern = functools.partial(_inproj_kernel, sections=sections)
    return pl.pallas_call(
        kern,
        out_shape=[jax.ShapeDtypeStruct((t, n), dt) for n, dt in zip(widths, out_dtypes)],
        grid=(t // tm,),
        in_specs=[pl.BlockSpec((tm, d), lambda i: (i, 0)), mod_spec, mod_spec, _const_spec((1, d)),
                  pl.BlockSpec(w_bf.shape, lambda i: (0, 0), pipeline_mode=pl.Buffered(1))],
        out_specs=[pl.BlockSpec((tm, n), lambda i: (i, 0)) for n in widths],
        scratch_shapes=[pltpu.VMEM((d, sum(widths)), BF16)],
        compiler_params=_params("arbitrary"),
        name="inproj",
    )(x2, sc, sh, g, w_bf)


def _ssd_kernel(tail_ref, xbc_ref, z_ref, sm_ref, cw_ref, cb_ref, dtb_r_ref, dtb_c_ref, alog_r_ref, alog_c_ref,
                dsk_ref, g_ref, y_ref, st_ref, *, n_heads, d_ssm):
    L = SSM_CHUNK
    P = SSM_HEAD_DIM
    N = SSM_STATE
    G = SSM_GROUPS
    gw = d_ssm // G
    R = xbc_ref.shape[0]
    c = pl.program_id(1)

    @pl.when(c == 0)
    def _():
        st_ref[...] = jnp.zeros(st_ref.shape, F32)

    tail = tail_ref[...]
    tail = jnp.where(c == 0, jnp.zeros_like(tail), tail)
    conv_in = jnp.concatenate([tail, xbc_ref[...]], axis=0)

    ri = lax.broadcasted_iota(I32, (L, L), 0)
    ci = lax.broadcasted_iota(I32, (L, L), 1)
    causal = ri >= ci
    causal3 = jnp.concatenate([causal.astype(BF16)] * 3, axis=1)
    upper3 = jnp.concatenate([(ri <= ci).astype(BF16)] * 3, axis=0)
    wr = lax.broadcasted_iota(I32, ((SSM_CONV - 1) * L, CONV_CARRY + L), 0)
    wc = lax.broadcasted_iota(I32, ((SSM_CONV - 1) * L, CONV_CARRY + L), 1)
    shifts = (wc == (wr % L) + CONV_CARRY - (SSM_CONV - 1) + wr // L).astype(BF16)
    hh = lax.broadcasted_iota(I32, (n_heads, d_ssm), 0)
    jj = lax.broadcasted_iota(I32, (n_heads, d_ssm), 1)
    expand = ((jj // P) == hh).astype(BF16)
    expand2 = jnp.concatenate([expand, expand], axis=0)
    lane = lax.broadcasted_iota(I32, (L, LANES), 1)
    first_half = lane < P
    heads_per_group = n_heads // G
    a_row = -jnp.exp(alog_r_ref[...])
    a_col = -jnp.exp(alog_c_ref[...])

    for s in range(R // L):
        rows = slice(s * L, (s + 1) * L)
        window = conv_in[s * L:s * L + CONV_CARRY + L, :]
        taps = jnp.dot(shifts, window, preferred_element_type=F32)
        acc = cb_ref[...] + cw_ref[SSM_CONV - 1:SSM_CONV, :] * window[CONV_CARRY:, :].astype(F32)
        for j in range(SSM_CONV - 1):
            acc = acc + cw_ref[j:j + 1, :] * taps[j * L:(j + 1) * L, :]
        act = _silu(acc)
        xs = act[:, :d_ssm]
        bm = act[:, d_ssm:d_ssm + G * N].astype(BF16)
        cm = act[:, d_ssm + G * N:].astype(BF16)

        sm = sm_ref[rows, :]
        dt_col = _softplus(sm[:, :n_heads] + dtb_r_ref[...])
        dt_row = _softplus(sm.T[:n_heads, :] + dtb_c_ref[...])
        cs_col = jnp.dot(causal3, jnp.concatenate(_pieces(dt_col * a_row, 3), axis=0),
                         preferred_element_type=F32)
        cs_row = jnp.dot(jnp.concatenate(_pieces(dt_row * a_col, 3), axis=1), upper3,
                         preferred_element_type=F32)
        cs_last = cs_col[L - 1:L, :]

        per_head = jnp.concatenate([dt_col, jnp.exp(cs_col), jnp.exp(cs_last - cs_col)], axis=0)
        per_ch = jnp.dot(jnp.concatenate(_pieces(per_head, 2), axis=1), expand2,
                         preferred_element_type=F32)
        dt_e, ecs_e, dte_e = per_ch[0:L], per_ch[L:2 * L], per_ch[2 * L:3 * L]
        chunk_decay = ecs_e[L - 1:L, :]

        xdt = xs * dt_e
        xdt_b = xdt.astype(BF16)
        xdec_b = (xdt * dte_e).astype(BF16)

        y_parts = []
        y_off_parts = []
        for g in range(G):
            bm_g = bm[:, g * N:(g + 1) * N]
            cm_g = cm[:, g * N:(g + 1) * N]
            cb = lax.dot_general(cm_g, bm_g, _NT, preferred_element_type=F32)
            prev = st_ref[g]
            y_off_parts.append(jnp.dot(cm_g, prev.astype(BF16), preferred_element_type=F32))
            s_new = lax.dot_general(bm_g, xdec_b[:, g * gw:(g + 1) * gw], _TN, preferred_element_type=F32)
            st_ref[g] = prev * chunk_decay[:, g * gw:(g + 1) * gw] + s_new
            for p in range(heads_per_group // 2):
                h0 = g * heads_per_group + 2 * p
                ms = []
                for h in (h0, h0 + 1):
                    diff = cs_col[:, h:h + 1] - cs_row[h:h + 1, :]
                    ms.append((cb * jnp.exp(jnp.where(causal, diff, -jnp.inf))).astype(BF16))
                lhs = jnp.concatenate(ms, axis=1)
                xp = xdt_b[:, h0 * P:(h0 + 2) * P]
                zero = jnp.zeros_like(xp)
                rhs = jnp.concatenate([jnp.where(first_half, xp, zero), jnp.where(first_half, zero, xp)], axis=0)
                y_parts.append(jnp.dot(lhs, rhs, preferred_element_type=F32))
        y = jnp.concatenate(y_parts, axis=1) + jnp.concatenate(y_off_parts, axis=1) * ecs_e + dsk_ref[...] * xs
        y = y * _silu(z_ref[rows, :].astype(F32))
        y = jnp.concatenate([_rms(y[:, g * gw:(g + 1) * gw]) for g in range(G)], axis=1) * g_ref[...]
        y_ref[rows, :] = y.astype(y_ref.dtype)


def _ssd(xbc, z, small, conv_w, conv_b, dt_bias, a_log, d_skip_e, norm_g, batch, seq):
    t, cd = xbc.shape
    d_ssm = z.shape[1]
    n_heads = dt_bias.shape[0]
    L = min(SSD_STEP_ROWS, seq)
    nc = seq // L
    row = lambda b, c: (b * nc + c, 0)
    tail = lambda b, c: (jnp.maximum((b * nc + c) * (L // CONV_CARRY) - 1, 0), 0)
    kern = functools.partial(_ssd_kernel, n_heads=n_heads, d_ssm=d_ssm)
    return pl.pallas_call(
        kern,
        out_shape=jax.ShapeDtypeStruct((t, d_ssm), BF16),
        grid=(batch, nc),
        in_specs=[pl.BlockSpec((CONV_CARRY, cd), tail),
                  pl.BlockSpec((L, cd), row), pl.BlockSpec((L, d_ssm), row), pl.BlockSpec((L, LANES), row),
                  _const_spec(conv_w.shape), _const_spec((1, cd)),
                  _const_spec((1, n_heads)), _const_spec((n_heads, 1)),
                  _const_spec((1, n_heads)), _const_spec((n_heads, 1)),
                  _const_spec((1, d_ssm)), _const_spec((1, d_ssm))],
        out_specs=pl.BlockSpec((L, d_ssm), row),
        scratch_shapes=[pltpu.VMEM((SSM_GROUPS, SSM_STATE, d_ssm // SSM_GROUPS), F32)],
        compiler_params=_params("arbitrary", "arbitrary"),
        name="ssd",
    )(xbc, xbc, z, small, conv_w, conv_b.reshape(1, cd), dt_bias.reshape(1, n_heads), dt_bias.reshape(n_heads, 1),
      a_log.reshape(1, n_heads), a_log.reshape(n_heads, 1), d_skip_e, norm_g.reshape(1, d_ssm))


def _gla_kernel(q_ref, k_ref, v_ref, r_ref, sm_ref, wg2_ref, bg_ref, gn_ref, o_ref, st_ref, *, gate_col):
    L = GLA_CHUNK
    H = GLA_HEADS
    dk = q_ref.shape[1] // H
    dv = v_ref.shape[1] // H
    c = pl.program_id(1)

    @pl.when(c == 0)
    def _():
        st_ref[...] = jnp.zeros(st_ref.shape, F32)

    R = min(GLA_GROUP_ROWS, q_ref.shape[0])
    n_chunks = R // L
    ri = lax.broadcasted_iota(I32, (R, R), 0)
    ci = lax.broadcasted_iota(I32, (R, R), 1)
    same_chunk = (ri // L) == (ci // L)
    causal = jnp.logical_and(same_chunk, ri >= ci)
    tril = causal.astype(BF16)
    later = jnp.logical_and(same_chunk, ri < ci).astype(BF16)
    sr = lax.broadcasted_iota(I32, (R, n_chunks * LANES), 0)
    sc = lax.broadcasted_iota(I32, (R, n_chunks * LANES), 1)
    last_rows = (sr == (sc // LANES) * L + (L - 1)).astype(BF16)
    wg2_hi, wg2_lo = _pieces(wg2_ref[...], 2)

    tril3 = jnp.concatenate([jnp.concatenate([tril] * 3, axis=1), jnp.concatenate([later] * 3, axis=1)], axis=0)
    last3 = jnp.concatenate([last_rows] * 3, axis=0)
    wg3 = jnp.concatenate([wg2_hi, wg2_hi, wg2_lo], axis=0)

    for gi in range(q_ref.shape[0] // R):
        rs = slice(gi * R, (gi + 1) * R)
        q = q_ref[rs, :].astype(F32) * (dk ** -0.5)
        k = k_ref[rs, :].astype(F32)
        v = v_ref[rs, :]
        r = r_ref[rs, :].astype(F32)
        g_hi, g_lo = _pieces(sm_ref[rs, gate_col:gate_col + GLA_GATE_RANK], 2)
        pre = jnp.dot(jnp.concatenate([g_hi, g_lo, g_hi], axis=1), wg3, preferred_element_type=F32) + bg_ref[...]
        gk3 = jnp.concatenate(_pieces(_log_sigmoid(pre) / GLA_GATE_NORM, 3), axis=0)
        sums = jnp.dot(tril3, gk3, preferred_element_type=F32)
        bcum = sums[:R]
        to_end = sums[R:]
        q_t = (q * jnp.exp(bcum)).astype(BF16)
        k_t = (k * jnp.exp(-bcum)).astype(BF16)
        k_dec = (k * jnp.exp(to_end)).astype(BF16)
        dcol = jnp.exp(lax.dot_general(jnp.concatenate(_pieces(bcum, 3), axis=0), last3, _TN,
                                       preferred_element_type=F32))
        outs = []
        for h in range(H):
            ks = slice(h * dk, (h + 1) * dk)
            vs = slice(h * dv, (h + 1) * dv)
            att = lax.dot_general(q_t[:, ks], k_t[:, ks], _NT, preferred_element_type=F32)
            att = jnp.where(causal, att, 0.0).astype(BF16)
            o = jnp.dot(att, v[:, vs], preferred_element_type=F32)
            state = st_ref[h]
            inter = []
            for c in range(n_chunks):
                rows = slice(c * L, (c + 1) * L)
                inter.append(jnp.dot(q_t[rows, ks], state.astype(BF16), preferred_element_type=F32))
                s_new = lax.dot_general(k_dec[rows, ks], v[rows, vs], _TN, preferred_element_type=F32)
                dec = dcol[ks, c * LANES:(c + 1) * LANES]
                state = state * jnp.concatenate([dec] * (dv // LANES), axis=1) + s_new
            st_ref[h] = state
            o = o + jnp.concatenate(inter, axis=0)
            outs.append(_rms(o) * gn_ref[...] * _silu(r[:, vs]))
        o_ref[rs, :] = jnp.concatenate(outs, axis=1).astype(o_ref.dtype)


def _gla(q, k, v, r, small, wg2, bg, norm_g, batch, seq, gate_col):
    t, dkt = q.shape
    dvt = v.shape[1]
    rows = min(GLA_STEP_ROWS, seq)
    nc = seq // rows
    row = lambda b, c: (b * nc + c, 0)
    kern = functools.partial(_gla_kernel, gate_col=gate_col)
    return pl.pallas_call(
        kern,
        out_shape=jax.ShapeDtypeStruct((t, dvt), BF16),
        grid=(batch, nc),
        in_specs=[pl.BlockSpec((rows, dkt), row), pl.BlockSpec((rows, dkt), row), pl.BlockSpec((rows, dvt), row),
                  pl.BlockSpec((rows, dvt), row), pl.BlockSpec((rows, LANES), row),
                  _const_spec(wg2.shape), _const_spec((1, dkt)), _const_spec((1, dvt // GLA_HEADS))],
        out_specs=pl.BlockSpec((rows, dvt), row),
        scratch_shapes=[pltpu.VMEM((GLA_HEADS, dkt // GLA_HEADS, dvt // GLA_HEADS), F32)],
        compiler_params=_params("arbitrary", "arbitrary"),
        name="gla",
    )(q, k, v, r, small, wg2, bg.reshape(1, dkt), norm_g.reshape(1, dvt // GLA_HEADS))


def _outproj_kernel(y_ref, o_ref, x_ref, gt_ref, sc_ref, sh_ref, g_ref, wy_ref, wo_ref, wr_ref, br_ref,
                    x1_ref, h_ref, ti_ref, tw_ref, cnt_ref):
    mix = (jnp.dot(y_ref[...], wy_ref[...], preferred_element_type=F32)
           + jnp.dot(o_ref[...], wo_ref[...], preferred_element_type=F32))
    x1 = x_ref[...] + gt_ref[0] * mix
    x1_ref[...] = x1
    h = (_rms(x1) * g_ref[...]) * (1.0 + sc_ref[0]) + sh_ref[0]
    _token_rows_store(h_ref, h)
    n_e = br_ref.shape[0]
    h_hi, h_lo = _pieces(h, 2)
    wr = wr_ref[...]
    hw = lax.dot_general(wr, h_hi, _NT, preferred_element_type=F32)
    logits = (hw[:n_e] + hw[n_e:] + lax.dot_general(wr[:n_e], h_lo, _NT, preferred_element_type=F32)) + br_ref[...]
    expert = lax.broadcasted_iota(I32, logits.shape, 0)
    vals, idxs = [], []
    counts = jnp.zeros(logits.shape, F32)
    for _ in range(TOP_K):
        m = jnp.max(logits, axis=0, keepdims=True)
        idx = jnp.min(jnp.where(logits == m, expert, n_e), axis=0, keepdims=True)
        vals.append(m)
        idxs.append(idx)
        chosen = expert == idx
        counts = counts + chosen.astype(F32)
        logits = jnp.where(chosen, -jnp.inf, logits)
    exps = [jnp.exp(v - vals[0]) for v in vals]
    denom = functools.reduce(lambda a, b: a + b, exps)
    ti_ref[...] = jnp.concatenate(idxs, axis=0)
    tw_ref[...] = jnp.concatenate([e / denom for e in exps], axis=0)

    @pl.when(pl.program_id(0) == 0)
    def _():
        cnt_ref[...] = jnp.zeros(cnt_ref.shape, F32)

    cnt_ref[...] = cnt_ref[...] + jnp.sum(counts, axis=1, keepdims=True)


def _outproj(y, o, x2, gt, sc, sh, g, wy, wo, w_router, b_router, seq):
    t, d = x2.shape
    n_e = w_router.shape[1]
    tm = min(TOKEN_TILE, seq)
    per_batch = seq // tm
    row = lambda i: (i, 0)
    mod_spec = pl.BlockSpec((1, 1, d), lambda i: (i // per_batch, 0, 0))
    wr_t = w_router.T
    wr_hi = wr_t.astype(BF16)
    wr_cat = jnp.concatenate([wr_hi, (wr_t - wr_hi.astype(F32)).astype(BF16)], axis=0)
    col = lambda i: (0, i)
    return pl.pallas_call(
        _outproj_kernel,
        out_shape=[jax.ShapeDtypeStruct((t, d), F32), jax.ShapeDtypeStruct((t * SUBLANES, LANES), F32),
                   jax.ShapeDtypeStruct((TOP_K, t), I32), jax.ShapeDtypeStruct((TOP_K, t), F32),
                   jax.ShapeDtypeStruct((n_e, LANES), F32)],
        grid=(t // tm,),
        in_specs=[pl.BlockSpec((tm, y.shape[1]), row), pl.BlockSpec((tm, o.shape[1]), row),
                  pl.BlockSpec((tm, d), row), mod_spec, mod_spec, mod_spec, _const_spec((1, d)),
                  _const_spec(wy.shape), _const_spec(wo.shape), _const_spec(wr_cat.shape),
                  _const_spec((n_e, 1))],
        out_specs=[pl.BlockSpec((tm, d), row), pl.BlockSpec((tm * SUBLANES, LANES), row),
                   pl.BlockSpec((TOP_K, tm), col), pl.BlockSpec((TOP_K, tm), col),
                   _const_spec((n_e, LANES))],
        compiler_params=_params("arbitrary"),
        name="outproj",
    )(y, o, x2, gt, sc, sh, g, wy, wo, wr_cat, b_router.reshape(n_e, 1))


def _route_kernel(ti_ref, cnt_ref, dest_ref, be_ref, pend_ref, run_ref, *, n_blocks_pad):
    i = pl.program_id(0)
    n_e = cnt_ref.shape[0]
    tr = ti_ref.shape[1]

    @pl.when(i == 0)
    def _():
        counts = cnt_ref[...]
        padded = jnp.ceil(counts / EXPERT_BLOCK) * EXPERT_BLOCK
        ri = lax.broadcasted_iota(I32, (n_e, n_e), 0)
        ci = lax.broadcasted_iota(I32, (n_e, n_e), 1)
        pend = jnp.dot((ri >= ci).astype(F32), padded, precision=HIGHEST, preferred_element_type=F32)
        pend_ref[...] = pend
        run_ref[...] = pend - padded
        start = (lax.broadcasted_iota(I32, (n_e, n_blocks_pad), 1) * EXPERT_BLOCK).astype(F32)
        be = jnp.sum((pend[:, 0:1] <= start).astype(F32), axis=0, keepdims=True)
        be_ref[...] = jnp.minimum(be, n_e - 1).astype(I32)

    ti = ti_ref[...]
    expert = lax.broadcasted_iota(I32, (n_e, tr), 0)
    onehots = [expert == ti[k:k + 1, :] for k in range(TOP_K)]
    cnt = functools.reduce(lambda a, b: a + b, [oh.astype(F32) for oh in onehots])
    ri = lax.broadcasted_iota(I32, (tr, tr), 0)
    ci = lax.broadcasted_iota(I32, (tr, tr), 1)
    before = jnp.dot(cnt.astype(BF16), (ri < ci).astype(BF16), preferred_element_type=F32)
    base = run_ref[:, 0:1] + before
    dest = [jnp.sum(jnp.where(oh, base, 0.0), axis=0, keepdims=True) for oh in onehots]
    dest_ref[...] = jnp.concatenate(dest, axis=0).astype(I32)
    run_ref[...] = run_ref[...] + jnp.sum(cnt, axis=1, keepdims=True)


def _route(topi_t, counts, n_blocks):
    t = topi_t.shape[1]
    n_e = counts.shape[0]
    tr = min(ROUTE_TILE, t)
    n_blocks_pad = -(-n_blocks // LANES) * LANES
    kern = functools.partial(_route_kernel, n_blocks_pad=n_blocks_pad)
    return pl.pallas_call(
        kern,
        out_shape=[jax.ShapeDtypeStruct((TOP_K, t), I32), jax.ShapeDtypeStruct((1, n_blocks_pad), I32),
                   jax.ShapeDtypeStruct((n_e, LANES), F32)],
        grid=(t // tr,),
        in_specs=[pl.BlockSpec((TOP_K, tr), lambda i: (0, i)), _const_spec((n_e, LANES))],
        out_specs=[pl.BlockSpec((TOP_K, tr), lambda i: (0, i)), _const_spec((1, n_blocks_pad)),
                   _const_spec((n_e, LANES))],
        scratch_shapes=[pltpu.VMEM((n_e, LANES), F32)],
        compiler_params=_params("arbitrary"),
        name="route",
    )(topi_t, counts)


def _dispatch_kernel(pend_ref, dest_hbm, h_ref, xs_hbm, idx_ref, zero_ref, idx_sem, row_sem, *, n_experts):
    i = pl.program_id(0)
    tg = idx_ref.shape[1]

    @pl.when(i == 0)
    def _():
        zero_ref[...] = jnp.zeros(zero_ref.shape, zero_ref.dtype)
        for e in range(n_experts):
            end = pend_ref[e]
            prev = pend_ref[e - 1] if e > 0 else 0

            @pl.when(end > prev)
            def _():
                start = pl.multiple_of((end - EXPERT_BLOCK) * SUBLANES, EXPERT_BLOCK * SUBLANES)
                cp = pltpu.make_async_copy(zero_ref, xs_hbm.at[pl.ds(start, EXPERT_BLOCK * SUBLANES)], row_sem)
                cp.start()
                cp.wait()

        n_blocks = xs_hbm.shape[0] // (EXPERT_BLOCK * SUBLANES)
        total = pend_ref[n_experts - 1]
        for b in range(n_blocks - n_experts, n_blocks):
            @pl.when(b * EXPERT_BLOCK >= total)
            def _():
                cp = pltpu.make_async_copy(
                    zero_ref, xs_hbm.at[pl.ds(b * EXPERT_BLOCK * SUBLANES, EXPERT_BLOCK * SUBLANES)], row_sem)
                cp.start()
                cp.wait()

    n = pl.num_programs(0)

    def idx_copy(tile, s):
        return pltpu.make_async_copy(dest_hbm.at[:, pl.ds(pl.multiple_of(tile * tg, tg), tg)],
                                     idx_ref.at[pl.ds(s * TOP_K, TOP_K)], idx_sem.at[s])

    @pl.when(i == 0)
    def _():
        idx_copy(0, 0).start()

    def step(slot):
        idx_copy(i, slot).wait()

        @pl.when(i + 1 < n)
        def _():
            idx_copy(i + 1, 1 - slot).start()

        def issue(tl, carry):
            src = h_ref.at[pl.ds(pl.multiple_of(tl * SUBLANES, SUBLANES), SUBLANES)]
            for k in range(TOP_K):
                d = pl.multiple_of(idx_ref[slot * TOP_K + k, tl] * SUBLANES, SUBLANES)
                pltpu.make_async_copy(src, xs_hbm.at[pl.ds(d, SUBLANES)], row_sem).start(priority=k % 2)
            return carry

        lax.fori_loop(0, tg, issue, 0)

    for parity in range(2):
        pl.when(lax.rem(i, 2) == parity)(functools.partial(step, parity))
    for _ in range(TOP_K):
        pltpu.make_async_copy(h_ref, xs_hbm.at[pl.ds(0, tg * SUBLANES)], row_sem).wait()


def _dispatch(pend_i, dest_t, h, n_rows, n_experts):
    t = h.shape[0] // SUBLANES
    tg = min(DISPATCH_TILE, t)
    kern = functools.partial(_dispatch_kernel, n_experts=n_experts)
    return pl.pallas_call(
        kern,
        out_shape=jax.ShapeDtypeStruct((n_rows * SUBLANES, LANES), h.dtype),
        grid_spec=pltpu.PrefetchScalarGridSpec(
            num_scalar_prefetch=1,
            grid=(t // tg,),
            in_specs=[pl.BlockSpec(memory_space=pl.ANY),
                      pl.BlockSpec((tg * SUBLANES, LANES), lambda i, pend: (i, 0))],
            out_specs=pl.BlockSpec(memory_space=pl.ANY),
            scratch_shapes=[pltpu.SMEM((2 * TOP_K, tg), I32), pltpu.VMEM((EXPERT_BLOCK * SUBLANES, LANES), h.dtype),
                            pltpu.SemaphoreType.DMA((2,)), pltpu.SemaphoreType.DMA]),
        compiler_params=pltpu.CompilerParams(dimension_semantics=("arbitrary",), has_side_effects=True,
                                             vmem_limit_bytes=VMEM_LIMIT_BYTES),
        name="dispatch",
    )(pend_i, dest_t, h)


def _expert_kernel(be_ref, nu_ref, pend_ref, xs_ref, wg_hbm, bg_ref, wu_hbm, bu_ref, wd_hbm, bd_ref, y_ref,
                   wg_f, wu_f, wd_f, wg_b, wu_b, wd_b, slot_ref, sems):
    i = pl.program_id(0)
    used = i < nu_ref[0]
    e = be_ref[i]

    def fetch(expert, slot):
        return [pltpu.make_async_copy(src.at[expert], dst.at[slot], sems.at[slot])
                for src, dst in ((wg_hbm, wg_f), (wu_hbm, wu_f), (wd_hbm, wd_f))]

    @pl.when(i == 0)
    def _():
        slot_ref[0] = 0
        for cp in fetch(e, 0):
            cp.start()

    first_of_expert = jnp.logical_or(i == 0, e != be_ref[jnp.maximum(i - 1, 0)])

    @pl.when(jnp.logical_and(used, first_of_expert))
    def _():
        slot = slot_ref[0]
        for cp in fetch(e, slot):
            cp.wait()
        wg_b[...] = wg_f[slot].astype(BF16)
        wu_b[...] = wu_f[slot].astype(BF16)
        wd_b[...] = wd_f[slot].astype(BF16)
        nxt = lax.div(pend_ref[e], EXPERT_BLOCK)

        @pl.when(nxt < nu_ref[0])
        def _():
            for cp in fetch(be_ref[nxt], 1 - slot):
                cp.start()

        slot_ref[0] = 1 - slot

    def ffn(rows):
        x = _token_rows_load(xs_ref, rows).astype(BF16)
        gate = jnp.minimum(jnp.dot(x, wg_b[...], preferred_element_type=F32) + bg_ref[...], SWIGLU_LIMIT)
        up = jnp.clip(jnp.dot(x, wu_b[...], preferred_element_type=F32) + bu_ref[...],
                      -SWIGLU_LIMIT, SWIGLU_LIMIT)
        glu = gate * _sigmoid(SWIGLU_ALPHA * gate)
        mid = ((up + 1.0) * glu).astype(BF16)
        y = jnp.dot(mid, wd_b[...], preferred_element_type=F32) + bd_ref[...]
        _token_rows_store(y_ref, y)
        if rows < EXPERT_BLOCK:
            y_ref[rows * SUBLANES:, :] = jnp.zeros(((EXPERT_BLOCK - rows) * SUBLANES, LANES), y_ref.dtype)

    valid = pend_ref[pend_ref.shape[0] // 2 + e] - i * EXPERT_BLOCK
    quarter = EXPERT_BLOCK // EXPERT_PATHS
    for p in range(1, EXPERT_PATHS + 1):
        covers = valid <= p * quarter if p < EXPERT_PATHS else True
        needs = valid > (p - 1) * quarter if p > 1 else True
        pl.when(jnp.logical_and(used, jnp.logical_and(covers, needs)))(functools.partial(ffn, p * quarter))

    @pl.when(jnp.logical_not(used))
    def _():
        y_ref[...] = jnp.zeros(y_ref.shape, y_ref.dtype)


def _experts(block_e, n_used, pend_i, xs, w_gate, b_gate, w_up, b_up, w_down, b_down):
    n_rows = xs.shape[0] // SUBLANES
    n_e, d, f = w_gate.shape
    nb = n_rows // EXPERT_BLOCK
    blk = (EXPERT_BLOCK * SUBLANES, LANES)
    last = lambda i, be, nu, pend: jnp.maximum(jnp.minimum(i, nu[0] - 1), 0)
    bspec = lambda n: pl.BlockSpec((None, 1, n), lambda i, be, nu, pend: (be[last(i, be, nu, pend)], 0, 0))
    hbm = pl.BlockSpec(memory_space=pl.ANY)
    return pl.pallas_call(
        _expert_kernel,
        out_shape=jax.ShapeDtypeStruct((n_rows * SUBLANES, LANES), F32),
        grid_spec=pltpu.PrefetchScalarGridSpec(
            num_scalar_prefetch=3,
            grid=(nb,),
            in_specs=[pl.BlockSpec(blk, lambda i, be, nu, pend: (last(i, be, nu, pend), 0)),
                      hbm, bspec(f), hbm, bspec(f), hbm, bspec(d)],
            out_specs=pl.BlockSpec(blk, lambda i, be, nu, pend: (i, 0)),
            scratch_shapes=[pltpu.VMEM((2, d, f), F32), pltpu.VMEM((2, d, f), F32), pltpu.VMEM((2, f, d), F32),
                            pltpu.VMEM((d, f), BF16), pltpu.VMEM((d, f), BF16), pltpu.VMEM((f, d), BF16),
                            pltpu.SMEM((1,), I32), pltpu.SemaphoreType.DMA((2,))]),
        compiler_params=_params("arbitrary"),
        name="experts",
    )(block_e, n_used, pend_i, xs, w_gate, b_gate.reshape(n_e, 1, f), w_up, b_up.reshape(n_e, 1, f),
      w_down, b_down.reshape(n_e, 1, d))


def _combine_kernel(dest_hbm, ys_hbm, tw_ref, x1_ref, gt_ref, g_ref, o_ref, idx_ref, buf_ref, idx_sem, row_sem):
    i = pl.program_id(0)
    n = pl.num_programs(0)
    tc = idx_ref.shape[1]

    def idx_copy(tile, s):
        return pltpu.make_async_copy(dest_hbm.at[:, pl.ds(pl.multiple_of(tile * tc, tc), tc)],
                                     idx_ref.at[pl.ds(s * TOP_K, TOP_K)], idx_sem.at[s])

    def issue_rows(s):
        def issue(tl, carry):
            dst_row = pl.multiple_of(tl * SUBLANES, SUBLANES)
            for k in range(TOP_K):
                d = pl.multiple_of(idx_ref[s * TOP_K + k, tl] * SUBLANES, SUBLANES)
                pltpu.make_async_copy(ys_hbm.at[pl.ds(d, SUBLANES)], buf_ref.at[s, k, pl.ds(dst_row, SUBLANES)],
                                      row_sem.at[s]).start(priority=k % 2)
            return carry

        lax.fori_loop(0, tc, issue, 0)

    @pl.when(i == 0)
    def _():
        idx_copy(0, 0).start()
        idx_copy(0, 0).wait()
        issue_rows(0)

        @pl.when(n > 1)
        def _():
            idx_copy(1, 1).start()

    def step(slot):
        @pl.when(i + 1 < n)
        def _():
            idx_copy(i + 1, 1 - slot).wait()
            issue_rows(1 - slot)

            @pl.when(i + 2 < n)
            def _():
                idx_copy(i + 2, slot).start()

        for k in range(TOP_K):
            pltpu.make_async_copy(ys_hbm.at[pl.ds(0, tc * SUBLANES)], buf_ref.at[slot, k], row_sem.at[slot]).wait()

        tw = tw_ref[...]
        ffn = tw[:, 0:1] * _token_rows_load(buf_ref.at[slot, 0], tc)
        for k in range(1, TOP_K):
            ffn = ffn + tw[:, k:k + 1] * _token_rows_load(buf_ref.at[slot, k], tc)
        x2 = x1_ref[...] + gt_ref[0] * ffn
        o_ref[...] = _rms(x2) * g_ref[...]

    for parity in range(2):
        pl.when(lax.rem(i, 2) == parity)(functools.partial(step, parity))


def _combine(dest_t, ys, topw, x1, gt, g, seq):
    t, d = x1.shape
    tc = min(COMBINE_TILE, seq)
    per_batch = seq // tc
    row = lambda i: (i, 0)
    return pl.pallas_call(
        _combine_kernel,
        out_shape=jax.ShapeDtypeStruct((t, d), F32),
        grid=(t // tc,),
        in_specs=[pl.BlockSpec(memory_space=pl.ANY), pl.BlockSpec(memory_space=pl.ANY),
                  pl.BlockSpec((tc, TOP_K), row), pl.BlockSpec((tc, d), row),
                  pl.BlockSpec((1, 1, d), lambda i: (i // per_batch, 0, 0)), _const_spec((1, d))],
        out_specs=pl.BlockSpec((tc, d), row),
        scratch_shapes=[pltpu.SMEM((2 * TOP_K, tc), I32), pltpu.VMEM((2, TOP_K, tc * SUBLANES, LANES), F32),
                        pltpu.SemaphoreType.DMA((2,)), pltpu.SemaphoreType.DMA((2,))],
        compiler_params=_params("arbitrary"),
        name="combine",
    )(dest_t, ys, topw, x1, gt, g)


def _layer(x2, mod, batch, seq, norm1_g, w_in, conv_w, conv_b, dt_bias, a_log, d_skip, ssm_norm_g,
           gla_wg2, gla_bg, gla_norm_g, w_out, norm2_g, w_router, b_router,
           w_gate, b_gate, w_up, b_up, w_down, b_down):
    t, d = x2.shape
    n_heads = dt_bias.shape[0]
    d_ssm = n_heads * SSM_HEAD_DIM
    cd = conv_w.shape[1]
    dkt = gla_wg2.shape[1]
    dvt = w_out.shape[0] - d_ssm
    n_experts = w_router.shape[1]

    sh1, sc1, gt1, sh2, sc2, gt2 = [m.reshape(batch, 1, d) for m in jnp.split(mod[:batch], 6, axis=1)]

    sizes = (d_ssm, cd, n_heads, dkt, dkt, dvt, GLA_GATE_RANK, dvt)
    offs = [0]
    for s in sizes:
        offs.append(offs[-1] + s)
    piece = lambda j: (offs[j], sizes[j])
    sections = ((piece(0),), (piece(1),), (piece(3),), (piece(4),), (piece(5),), (piece(7),), (piece(2), piece(6)))
    z, xbc, q, k, v, r, small = _inproj(x2, sc1, sh1, norm1_g.reshape(1, d), w_in.astype(BF16), sections,
                                        (d_ssm, cd, dkt, dkt, dvt, dvt, LANES), [BF16] * 6 + [F32], seq)

    d_skip_e = jnp.repeat(d_skip, SSM_HEAD_DIM).reshape(1, d_ssm)
    y = _ssd(xbc, z, small, conv_w, conv_b, dt_bias, a_log, d_skip_e, ssm_norm_g, batch, seq)
    o = _gla(q, k, v, r, small, gla_wg2, gla_bg, gla_norm_g, batch, seq, gate_col=n_heads)

    x1, h2, topi_t, topw_t, counts = _outproj(y, o, x2, gt1, sc2, sh2, norm2_g.reshape(1, d),
                                        w_out[:d_ssm].astype(BF16), w_out[d_ssm:].astype(BF16),
                                        w_router, b_router, seq)

    n_blocks = (t * TOP_K) // EXPERT_BLOCK + n_experts
    dest_t, block_e, pend = _route(topi_t, counts, n_blocks)
    pend_i = pend[:, 0].astype(I32)
    n_used = (pend_i[n_experts - 1:] // EXPERT_BLOCK).astype(I32)
    topw = topw_t.T
    xs = _dispatch(pend_i, dest_t, h2, n_blocks * EXPERT_BLOCK, n_experts)
    real_end = jnp.concatenate([jnp.zeros((1,), I32), pend_i[:-1]]) + counts[:, 0].astype(I32)
    seg_ends = jnp.concatenate([pend_i, real_end])
    ys = _experts(block_e[0, :n_blocks], n_used, seg_ends, xs, w_gate, b_gate, w_up, b_up, w_down, b_down)
    return dest_t, ys, topw, x1, gt2


def kernel(x, c, w_ada, b_ada, norm1_g, w_in, conv_w, conv_b, dt_bias, a_log, d_skip, ssm_norm_g, gla_wg2,
           gla_bg, gla_norm_g, w_out, norm2_g, w_router, b_router, w_gate, b_gate, w_up, b_up, w_down, b_down,
           final_norm_g):
    batch, seq, d = x.shape
    assert w_ada.shape[0] == 1, "single-layer trunk"
    assert d == SUBLANES * LANES, "token rows are moved as one (8, 128) f32 tile each"
    assert seq % min(seq, max(TOKEN_TILE, SSM_CHUNK, GLA_STEP_ROWS, COMBINE_TILE, DISPATCH_TILE)) == 0
    assert seq % max(SSM_CHUNK, GLA_STEP_ROWS) == 0
    x2 = x.reshape(batch * seq, d)
    c_pad = jnp.zeros((SUBLANES, d), F32).at[:batch].set(c)
    mod = _ada(c_pad, w_ada[0], b_ada)
    dest_t, ys, topw, x1, gt2 = _layer(
        x2, mod, batch, seq, norm1_g[0], w_in[0], conv_w[0], conv_b[0], dt_bias[0], a_log[0], d_skip[0],
        ssm_norm_g[0], gla_wg2[0], gla_bg[0], gla_norm_g[0], w_out[0], norm2_g[0], w_router[0], b_router[0],
        w_gate[0], b_gate[0], w_up[0], b_up[0], w_down[0], b_down[0])
    out = _combine(dest_t, ys, topw, x1, gt2, final_norm_g.reshape(1, d), seq)
    return out.reshape(batch, seq, d)
```

```python
import functools

import jax
import jax.numpy as jnp
from jax import lax
from jax.experimental import pallas as pl
from jax.experimental.pallas import tpu as pltpu

F32 = jnp.float32
BF16 = jnp.bfloat16
I32 = jnp.int32
HIGHEST = lax.Precision.HIGHEST

EPS = 1e-6
SSM_HEAD_DIM = 64
SSM_GROUPS = 2
SSM_STATE = 128
SSM_CONV = 4
SSM_CHUNK = 128
GLA_HEADS = 4
GLA_GATE_RANK = 16
GLA_GATE_NORM = 16.0
GLA_CHUNK = 64
TOP_K = 4
SWIGLU_LIMIT = 7.0
SWIGLU_ALPHA = 1.702

LANES = 128
SUBLANES = 8
V7X_VMEM_BYTES = 64 * 1024 * 1024
VMEM_LIMIT_BYTES = V7X_VMEM_BYTES - 8 * 1024 * 1024

TOKEN_TILE = 512
SSD_STEP_ROWS = 512
CONV_CARRY = 16
GLA_STEP_ROWS = 512
GLA_GROUP_ROWS = 128
ROUTE_TILE = 512
EXPERT_BLOCK = 512
EXPERT_PATHS = 4
DISPATCH_TILE = 1024
COMBINE_TILE = 512

_NT = (((1,), (1,)), ((), ()))
_TN = (((0,), (0,)), ((), ()))


def _sigmoid(v):
    return 0.5 * jnp.tanh(0.5 * v) + 0.5


def _silu(v):
    return v * _sigmoid(v)


def _softplus(v):
    return jnp.maximum(v, 0.0) + jnp.log1p(jnp.exp(-jnp.abs(v)))


def _log_sigmoid(v):
    return jnp.minimum(v, 0.0) - jnp.log(1.0 + jnp.exp(-jnp.abs(v)))


def _rms(v):
    return v * lax.rsqrt(jnp.mean(v * v, axis=-1, keepdims=True) + EPS)


def _pieces(a, n):
    out = []
    for _ in range(n - 1):
        p = a.astype(BF16)
        out.append(p)
        a = a - p.astype(F32)
    out.append(a.astype(BF16))
    return out


def _token_rows_load(ref, rows):
    return jnp.concatenate([ref[pl.ds(s, rows, stride=SUBLANES), :] for s in range(SUBLANES)], axis=1)


def _token_rows_store(ref, v):
    rows = v.shape[0]
    for s in range(SUBLANES):
        ref[pl.ds(s, rows, stride=SUBLANES), :] = v[:, s * LANES:(s + 1) * LANES]


def _params(*semantics):
    return pltpu.CompilerParams(dimension_semantics=semantics, vmem_limit_bytes=VMEM_LIMIT_BYTES)


def _const_spec(shape):
    nd = len(shape)
    return pl.BlockSpec(shape, lambda *_: (0,) * nd)


def _ada_kernel(c_ref, w_ref, b_ref, o_ref):
    cond = _silu(c_ref[...])
    o_ref[...] = jnp.dot(cond, w_ref[...], precision=HIGHEST, preferred_element_type=F32) + b_ref[...]


def _ada(c_pad, w_ada, b_ada):
    rows, d = c_pad.shape
    n = w_ada.shape[1]
    tn = d
    return pl.pallas_call(
        _ada_kernel,
        out_shape=jax.ShapeDtypeStruct((rows, n), F32),
        grid=(n // tn,),
        in_specs=[pl.BlockSpec((rows, d), lambda j: (0, 0)),
                  pl.BlockSpec((d, tn), lambda j: (0, j)),
                  pl.BlockSpec((1, tn), lambda j: (0, j))],
        out_specs=pl.BlockSpec((rows, tn), lambda j: (0, j)),
        compiler_params=_params("arbitrary"),
        name="ada",
    )(c_pad, w_ada, b_ada)


def _inproj_kernel(x_ref, sc_ref, sh_ref, g_ref, w_ref, *refs, sections):
    o_refs, ws_ref = refs[:-1], refs[-1]

    @pl.when(pl.program_id(0) == 0)
    def _():
        off = 0
        for o_ref, pieces in zip(o_refs, sections):
            n = o_ref.shape[1]
            used = 0
            for src, width in pieces:
                ws_ref[:, off + used:off + used + width] = w_ref[:, src:src + width]
                used += width
            if used < n:
                ws_ref[:, off + used:off + n] = jnp.zeros((ws_ref.shape[0], n - used), ws_ref.dtype)
            off += n

    h = (_rms(x_ref[...]) * g_ref[...]) * (1.0 + sc_ref[0]) + sh_ref[0]
    hb = h.astype(BF16)
    off = 0
    for o_ref in o_refs:
        n = o_ref.shape[1]
        o_ref[...] = jnp.dot(hb, ws_ref[:, off:off + n], preferred_element_type=F32).astype(o_ref.dtype)
        off += n


def _inproj(x2, sc, sh, g, w_bf, sections, widths, out_dtypes, seq):
    t, d = x2.shape
    tm = min(TOKEN_TILE, seq)
    per_batch = seq // tm
    assert all(n % LANES == 0 for n in widths)
    mod_spec = pl.BlockSpec((1, 1, d), lambda i: (i // per_batch, 0, 0))
    kern = functools.partial(_inproj_kernel, sections=sections)
    return pl.pallas_call(
        kern,
        out_shape=[jax.ShapeDtypeStruct((t, n), dt) for n, dt in zip(widths, out_dtypes)],
        grid=(t // tm,),
        in_specs=[pl.BlockSpec((tm, d), lambda i: (i, 0)), mod_spec, mod_spec, _const_spec((1, d)),
                  pl.BlockSpec(w_bf.shape, lambda i: (0, 0), pipeline_mode=pl.Buffered(1))],
        out_specs=[pl.BlockSpec((tm, n), lambda i: (i, 0)) for n in widths],
        scratch_shapes=[pltpu.VMEM((d, sum(widths)), BF16)],
        compiler_params=_params("arbitrary"),
        name="inproj",
    )(x2, sc, sh, g, w_bf)


def _ssd_kernel(tail_ref, xbc_ref, z_ref, sm_ref, cw_ref, cb_ref, dtb_r_ref, dtb_c_ref, alog_r_ref, alog_c_ref,
                dsk_ref, g_ref, y_ref, st_ref, *, n_heads, d_ssm):
    L = SSM_CHUNK
    P = SSM_HEAD_DIM
    N = SSM_STATE
    G = SSM_GROUPS
    gw = d_ssm // G
    R = xbc_ref.shape[0]
    c = pl.program_id(1)

    @pl.when(c == 0)
    def _():
        st_ref[...] = jnp.zeros(st_ref.shape, F32)

    tail = tail_ref[...]
    tail = jnp.where(c == 0, jnp.zeros_like(tail), tail)
    conv_in = jnp.concatenate([tail, xbc_ref[...]], axis=0)

    ri = lax.broadcasted_iota(I32, (L, L), 0)
    ci = lax.broadcasted_iota(I32, (L, L), 1)
    causal = ri >= ci
    causal3 = jnp.concatenate([causal.astype(BF16)] * 3, axis=1)
    upper3 = jnp.concatenate([(ri <= ci).astype(BF16)] * 3, axis=0)
    wr = lax.broadcasted_iota(I32, ((SSM_CONV - 1) * L, CONV_CARRY + L), 0)
    wc = lax.broadcasted_iota(I32, ((SSM_CONV - 1) * L, CONV_CARRY + L), 1)
    shifts = (wc == (wr % L) + CONV_CARRY - (SSM_CONV - 1) + wr // L).astype(BF16)
    hh = lax.broadcasted_iota(I32, (n_heads, d_ssm), 0)
    jj = lax.broadcasted_iota(I32, (n_heads, d_ssm), 1)
    expand = ((jj // P) == hh).astype(BF16)
    expand2 = jnp.concatenate([expand, expand], axis=0)
    lane = lax.broadcasted_iota(I32, (L, LANES), 1)
    first_half = lane < P
    heads_per_group = n_heads // G
    a_row = -jnp.exp(alog_r_ref[...])
    a_col = -jnp.exp(alog_c_ref[...])

    for s in range(R // L):
        rows = slice(s * L, (s + 1) * L)
        window = conv_in[s * L:s * L + CONV_CARRY + L, :]
        taps = jnp.dot(shifts, window, preferred_element_type=F32)
        acc = cb_ref[...] + cw_ref[SSM_CONV - 1:SSM_CONV, :] * window[CONV_CARRY:, :].astype(F32)
        for j in range(SSM_CONV - 1):
            acc = acc + cw_ref[j:j + 1, :] * taps[j * L:(j + 1) * L, :]
        act = _silu(acc)
        xs = act[:, :d_ssm]
        bm = act[:, d_ssm:d_ssm + G * N].astype(BF16)
        cm = act[:, d_ssm + G * N:].astype(BF16)

        sm = sm_ref[rows, :]
        dt_col = _softplus(sm[:, :n_heads] + dtb_r_ref[...])
        dt_row = _softplus(sm.T[:n_heads, :] + dtb_c_ref[...])
        cs_col = jnp.dot(causal3, jnp.concatenate(_pieces(dt_col * a_row, 3), axis=0),
                         preferred_element_type=F32)
        cs_row = jnp.dot(jnp.concatenate(_pieces(dt_row * a_col, 3), axis=1), upper3,
                         preferred_element_type=F32)
        cs_last = cs_col[L - 1:L, :]

        per_head = jnp.concatenate([dt_col, jnp.exp(cs_col), jnp.exp(cs_last - cs_col)], axis=0)
        per_ch = jnp.dot(jnp.concatenate(_pieces(per_head, 2), axis=1), expand2,
                         preferred_element_type=F32)
        dt_e, ecs_e, dte_e = per_ch[0:L], per_ch[L:2 * L], per_ch[2 * L:3 * L]
        chunk_decay = ecs_e[L - 1:L, :]

        xdt = xs * dt_e
        xdt_b = xdt.astype(BF16)
        xdec_b = (xdt * dte_e).astype(BF16)

        y_parts = []
        y_off_parts = []
        for g in range(G):
            bm_g = bm[:, g * N:(g + 1) * N]
            cm_g = cm[:, g * N:(g + 1) * N]
            cb = lax.dot_general(cm_g, bm_g, _NT, preferred_element_type=F32)
            prev = st_ref[g]
            y_off_parts.append(jnp.dot(cm_g, prev.astype(BF16), preferred_element_type=F32))
            s_new = lax.dot_general(bm_g, xdec_b[:, g * gw:(g + 1) * gw], _TN, preferred_element_type=F32)
            st_ref[g] = prev * chunk_decay[:, g * gw:(g + 1) * gw] + s_new
            for p in range(heads_per_group // 2):
                h0 = g * heads_per_group + 2 * p
                ms = []
                for h in (h0, h0 + 1):
                    diff = cs_col[:, h:h + 1] - cs_row[h:h + 1, :]
                    ms.append((cb * jnp.exp(jnp.where(causal, diff, -jnp.inf))).astype(BF16))
                lhs = jnp.concatenate(ms, axis=1)
                xp = xdt_b[:, h0 * P:(h0 + 2) * P]
                zero = jnp.zeros_like(xp)
                rhs = jnp.concatenate([jnp.where(first_half, xp, zero), jnp.where(first_half, zero, xp)], axis=0)
                y_parts.append(jnp.dot(lhs, rhs, preferred_element_type=F32))
        y = jnp.concatenate(y_parts, axis=1) + jnp.concatenate(y_off_parts, axis=1) * ecs_e + dsk_ref[...] * xs
        y = y * _silu(z_ref[rows, :].astype(F32))
        y = jnp.concatenate([_rms(y[:, g * gw:(g + 1) * gw]) for g in range(G)], axis=1) * g_ref[...]
        y_ref[rows, :] = y.astype(y_ref.dtype)


def _ssd(xbc, z, small, conv_w, conv_b, dt_bias, a_log, d_skip_e, norm_g, batch, seq):
    t, cd = xbc.shape
    d_ssm = z.shape[1]
    n_heads = dt_bias.shape[0]
    L = min(SSD_STEP_ROWS, seq)
    nc = seq // L
    row = lambda b, c: (b * nc + c, 0)
    tail = lambda b, c: (jnp.maximum((b * nc + c) * (L // CONV_CARRY) - 1, 0), 0)
    kern = functools.partial(_ssd_kernel, n_heads=n_heads, d_ssm=d_ssm)
    return pl.pallas_call(
        kern,
        out_shape=jax.ShapeDtypeStruct((t, d_ssm), BF16),
        grid=(batch, nc),
        in_specs=[pl.BlockSpec((CONV_CARRY, cd), tail),
                  pl.BlockSpec((L, cd), row), pl.BlockSpec((L, d_ssm), row), pl.BlockSpec((L, LANES), row),
                  _const_spec(conv_w.shape), _const_spec((1, cd)),
                  _const_spec((1, n_heads)), _const_spec((n_heads, 1)),
                  _const_spec((1, n_heads)), _const_spec((n_heads, 1)),
                  _const_spec((1, d_ssm)), _const_spec((1, d_ssm))],
        out_specs=pl.BlockSpec((L, d_ssm), row),
        scratch_shapes=[pltpu.VMEM((SSM_GROUPS, SSM_STATE, d_ssm // SSM_GROUPS), F32)],
        compiler_params=_params("arbitrary", "arbitrary"),
        name="ssd",
    )(xbc, xbc, z, small, conv_w, conv_b.reshape(1, cd), dt_bias.reshape(1, n_heads), dt_bias.reshape(n_heads, 1),
      a_log.reshape(1, n_heads), a_log.reshape(n_heads, 1), d_skip_e, norm_g.reshape(1, d_ssm))


def _gla_kernel(q_ref, k_ref, v_ref, r_ref, sm_ref, wg2_ref, bg_ref, gn_ref, o_ref, st_ref, *, gate_col):
    L = GLA_CHUNK
    H = GLA_HEADS
    dk = q_ref.shape[1] // H
    dv = v_ref.shape[1] // H
    c = pl.program_id(1)

    @pl.when(c == 0)
    def _():
        st_ref[...] = jnp.zeros(st_ref.shape, F32)

    R = min(GLA_GROUP_ROWS, q_ref.shape[0])
    n_chunks = R // L
    ri = lax.broadcasted_iota(I32, (R, R), 0)
    ci = lax.broadcasted_iota(I32, (R, R), 1)
    same_chunk = (ri // L) == (ci // L)
    causal = jnp.logical_and(same_chunk, ri >= ci)
    tril = causal.astype(BF16)
    later = jnp.logical_and(same_chunk, ri < ci).astype(BF16)
    sr = lax.broadcasted_iota(I32, (R, n_chunks * LANES), 0)
    sc = lax.broadcasted_iota(I32, (R, n_chunks * LANES), 1)
    last_rows = (sr == (sc // LANES) * L + (L - 1)).astype(BF16)
    wg2_hi, wg2_lo = _pieces(wg2_ref[...], 2)

    tril3 = jnp.concatenate([jnp.concatenate([tril] * 3, axis=1), jnp.concatenate([later] * 3, axis=1)], axis=0)
    last3 = jnp.concatenate([last_rows] * 3, axis=0)
    wg3 = jnp.concatenate([wg2_hi, wg2_hi, wg2_lo], axis=0)

    for gi in range(q_ref.shape[0] // R):
        rs = slice(gi * R, (gi + 1) * R)
        q = q_ref[rs, :].astype(F32) * (dk ** -0.5)
        k = k_ref[rs, :].astype(F32)
        v = v_ref[rs, :]
        r = r_ref[rs, :].astype(F32)
        g_hi, g_lo = _pieces(sm_ref[rs, gate_col:gate_col + GLA_GATE_RANK], 2)
        pre = jnp.dot(jnp.concatenate([g_hi, g_lo, g_hi], axis=1), wg3, preferred_element_type=F32) + bg_ref[...]
        gk3 = jnp.concatenate(_pieces(_log_sigmoid(pre) / GLA_GATE_NORM, 3), axis=0)
        sums = jnp.dot(tril3, gk3, preferred_element_type=F32)
        bcum = sums[:R]
        to_end = sums[R:]
        q_t = (q * jnp.exp(bcum)).astype(BF16)
        k_t = (k * jnp.exp(-bcum)).astype(BF16)
        k_dec = (k * jnp.exp(to_end)).astype(BF16)
        dcol = jnp.exp(lax.dot_general(jnp.concatenate(_pieces(bcum, 3), axis=0), last3, _TN,
                                       preferred_element_type=F32))
        outs = []
        for h in range(H):
            ks = slice(h * dk, (h + 1) * dk)
            vs = slice(h * dv, (h + 1) * dv)
            att = lax.dot_general(q_t[:, ks], k_t[:, ks], _NT, preferred_element_type=F32)
            att = jnp.where(causal, att, 0.0).astype(BF16)
            o = jnp.dot(att, v[:, vs], preferred_element_type=F32)
            state = st_ref[h]
            inter = []
            for c in range(n_chunks):
                rows = slice(c * L, (c + 1) * L)
                inter.append(jnp.dot(q_t[rows, ks], state.astype(BF16), preferred_element_type=F32))
                s_new = lax.dot_general(k_dec[rows, ks], v[rows, vs], _TN, preferred_element_type=F32)
                dec = dcol[ks, c * LANES:(c + 1) * LANES]
                state = state * jnp.concatenate([dec] * (dv // LANES), axis=1) + s_new
            st_ref[h] = state
            o = o + jnp.concatenate(inter, axis=0)
            outs.append(_rms(o) * gn_ref[...] * _silu(r[:, vs]))
        o_ref[rs, :] = jnp.concatenate(outs, axis=1).astype(o_ref.dtype)


def _gla(q, k, v, r, small, wg2, bg, norm_g, batch, seq, gate_col):
    t, dkt = q.shape
    dvt = v.shape[1]
    rows = min(GLA_STEP_ROWS, seq)
    nc = seq // rows
    row = lambda b, c: (b * nc + c, 0)
    kern = functools.partial(_gla_kernel, gate_col=gate_col)
    return pl.pallas_call(
        kern,
        out_shape=jax.ShapeDtypeStruct((t, dvt), BF16),
        grid=(batch, nc),
        in_specs=[pl.BlockSpec((rows, dkt), row), pl.BlockSpec((rows, dkt), row), pl.BlockSpec((rows, dvt), row),
                  pl.BlockSpec((rows, dvt), row), pl.BlockSpec((rows, LANES), row),
                  _const_spec(wg2.shape), _const_spec((1, dkt)), _const_spec((1, dvt // GLA_HEADS))],
        out_specs=pl.BlockSpec((rows, dvt), row),
        scratch_shapes=[pltpu.VMEM((GLA_HEADS, dkt // GLA_HEADS, dvt // GLA_HEADS), F32)],
        compiler_params=_params("arbitrary", "arbitrary"),
        name="gla",
    )(q, k, v, r, small, wg2, bg.reshape(1, dkt), norm_g.reshape(1, dvt // GLA_HEADS))


def _outproj_kernel(y_ref, o_ref, x_ref, gt_ref, sc_ref, sh_ref, g_ref, wy_ref, wo_ref, wr_ref, br_ref,
                    x1_ref, h_ref, ti_ref, tw_ref, cnt_ref):
    mix = (jnp.dot(y_ref[...], wy_ref[...], preferred_element_type=F32)
           + jnp.dot(o_ref[...], wo_ref[...], preferred_element_type=F32))
    x1 = x_ref[...] + gt_ref[0] * mix
    x1_ref[...] = x1
    h = (_rms(x1) * g_ref[...]) * (1.0 + sc_ref[0]) + sh_ref[0]
    _token_rows_store(h_ref, h)
    n_e = br_ref.shape[0]
    h_hi, h_lo = _pieces(h, 2)
    wr = wr_ref[...]
    hw = lax.dot_general(wr, h_hi, _NT, preferred_element_type=F32)
    logits = (hw[:n_e] + hw[n_e:] + lax.dot_general(wr[:n_e], h_lo, _NT, preferred_element_type=F32)) + br_ref[...]
    expert = lax.broadcasted_iota(I32, logits.shape, 0)
    vals, idxs = [], []
    counts = jnp.zeros(logits.shape, F32)
    for _ in range(TOP_K):
        m = jnp.max(logits, axis=0, keepdims=True)
        idx = jnp.min(jnp.where(logits == m, expert, n_e), axis=0, keepdims=True)
        vals.append(m)
        idxs.append(idx)
        chosen = expert == idx
        counts = counts + chosen.astype(F32)
        logits = jnp.where(chosen, -jnp.inf, logits)
    exps = [jnp.exp(v - vals[0]) for v in vals]
    denom = functools.reduce(lambda a, b: a + b, exps)
    ti_ref[...] = jnp.concatenate(idxs, axis=0)
    tw_ref[...] = jnp.concatenate([e / denom for e in exps], axis=0)

    @pl.when(pl.program_id(0) == 0)
    def _():
        cnt_ref[...] = jnp.zeros(cnt_ref.shape, F32)

    cnt_ref[...] = cnt_ref[...] + jnp.sum(counts, axis=1, keepdims=True)


def _outproj(y, o, x2, gt, sc, sh, g, wy, wo, w_router, b_router, seq):
    t, d = x2.shape
    n_e = w_router.shape[1]
    tm = min(TOKEN_TILE, seq)
    per_batch = seq // tm
    row = lambda i: (i, 0)
    mod_spec = pl.BlockSpec((1, 1, d), lambda i: (i // per_batch, 0, 0))
    wr_t = w_router.T
    wr_hi = wr_t.astype(BF16)
    wr_cat = jnp.concatenate([wr_hi, (wr_t - wr_hi.astype(F32)).astype(BF16)], axis=0)
    col = lambda i: (0, i)
    return pl.pallas_call(
        _outproj_kernel,
        out_shape=[jax.ShapeDtypeStruct((t, d), F32), jax.ShapeDtypeStruct((t * SUBLANES, LANES), F32),
                   jax.ShapeDtypeStruct((TOP_K, t), I32), jax.ShapeDtypeStruct((TOP_K, t), F32),
                   jax.ShapeDtypeStruct((n_e, LANES), F32)],
        grid=(t // tm,),
        in_specs=[pl.BlockSpec((tm, y.shape[1]), row), pl.BlockSpec((tm, o.shape[1]), row),
                  pl.BlockSpec((tm, d), row), mod_spec, mod_spec, mod_spec, _const_spec((1, d)),
                  _const_spec(wy.shape), _const_spec(wo.shape), _const_spec(wr_cat.shape),
                  _const_spec((n_e, 1))],
        out_specs=[pl.BlockSpec((tm, d), row), pl.BlockSpec((tm * SUBLANES, LANES), row),
                   pl.BlockSpec((TOP_K, tm), col), pl.BlockSpec((TOP_K, tm), col),
                   _const_spec((n_e, LANES))],
        compiler_params=_params("arbitrary"),
        name="outproj",
    )(y, o, x2, gt, sc, sh, g, wy, wo, wr_cat, b_router.reshape(n_e, 1))


def _route_kernel(ti_ref, cnt_ref, dest_ref, be_ref, pend_ref, run_ref, *, n_blocks_pad):
    i = pl.program_id(0)
    n_e = cnt_ref.shape[0]
    tr = ti_ref.shape[1]

    @pl.when(i == 0)
    def _():
        counts = cnt_ref[...]
        padded = jnp.ceil(counts / EXPERT_BLOCK) * EXPERT_BLOCK
        ri = lax.broadcasted_iota(I32, (n_e, n_e), 0)
        ci = lax.broadcasted_iota(I32, (n_e, n_e), 1)
        pend = jnp.dot((ri >= ci).astype(F32), padded, precision=HIGHEST, preferred_element_type=F32)
        pend_ref[...] = pend
        run_ref[...] = pend - padded
        start = (lax.broadcasted_iota(I32, (n_e, n_blocks_pad), 1) * EXPERT_BLOCK).astype(F32)
        be = jnp.sum((pend[:, 0:1] <= start).astype(F32), axis=0, keepdims=True)
        be_ref[...] = jnp.minimum(be, n_e - 1).astype(I32)

    ti = ti_ref[...]
    expert = lax.broadcasted_iota(I32, (n_e, tr), 0)
    onehots = [expert == ti[k:k + 1, :] for k in range(TOP_K)]
    cnt = functools.reduce(lambda a, b: a + b, [oh.astype(F32) for oh in onehots])
    ri = lax.broadcasted_iota(I32, (tr, tr), 0)
    ci = lax.broadcasted_iota(I32, (tr, tr), 1)
    before = jnp.dot(cnt.astype(BF16), (ri < ci).astype(BF16), preferred_element_type=F32)
    base = run_ref[:, 0:1] + before
    dest = [jnp.sum(jnp.where(oh, base, 0.0), axis=0, keepdims=True) for oh in onehots]
    dest_ref[...] = jnp.concatenate(dest, axis=0).astype(I32)
    run_ref[...] = run_ref[...] + jnp.sum(cnt, axis=1, keepdims=True)


def _route(topi_t, counts, n_blocks):
    t = topi_t.shape[1]
    n_e = counts.shape[0]
    tr = min(ROUTE_TILE, t)
    n_blocks_pad = -(-n_blocks // LANES) * LANES
    kern = functools.partial(_route_kernel, n_blocks_pad=n_blocks_pad)
    return pl.pallas_call(
        kern,
        out_shape=[jax.ShapeDtypeStruct((TOP_K, t), I32), jax.ShapeDtypeStruct((1, n_blocks_pad), I32),
                   jax.ShapeDtypeStruct((n_e, LANES), F32)],
        grid=(t // tr,),
        in_specs=[pl.BlockSpec((TOP_K, tr), lambda i: (0, i)), _const_spec((n_e, LANES))],
        out_specs=[pl.BlockSpec((TOP_K, tr), lambda i: (0, i)), _const_spec((1, n_blocks_pad)),
                   _const_spec((n_e, LANES))],
        scratch_shapes=[pltpu.VMEM((n_e, LANES), F32)],
        compiler_params=_params("arbitrary"),
        name="route",
    )(topi_t, counts)


def _dispatch_kernel(pend_ref, dest_hbm, h_ref, xs_hbm, idx_ref, zero_ref, idx_sem, row_sem, *, n_experts):
    i = pl.program_id(0)
    tg = idx_ref.shape[1]

    @pl.when(i == 0)
    def _():
        zero_ref[...] = jnp.zeros(zero_ref.shape, zero_ref.dtype)
        n_blocks = xs_hbm.shape[0] // (EXPERT_BLOCK * SUBLANES)
        total = pend_ref[n_experts - 1]

        def fill(block_row):
            return pltpu.make_async_copy(zero_ref, xs_hbm.at[pl.ds(block_row, EXPERT_BLOCK * SUBLANES)], row_sem)

        for act in ("start", "wait"):
            for e in range(n_experts):
                end = pend_ref[e]
                prev = pend_ref[e - 1] if e > 0 else 0

                @pl.when(end > prev)
                def _():
                    cp = fill(pl.multiple_of((end - EXPERT_BLOCK) * SUBLANES, EXPERT_BLOCK * SUBLANES))
                    cp.start() if act == "start" else cp.wait()

            for b in range(n_blocks - n_experts, n_blocks):
                @pl.when(b * EXPERT_BLOCK >= total)
                def _():
                    cp = fill(b * EXPERT_BLOCK * SUBLANES)
                    cp.start() if act == "start" else cp.wait()

    n = pl.num_programs(0)

    def idx_copy(tile, s):
        return pltpu.make_async_copy(dest_hbm.at[:, pl.ds(pl.multiple_of(tile * tg, tg), tg)],
                                     idx_ref.at[pl.ds(s * TOP_K, TOP_K)], idx_sem.at[s])

    @pl.when(i == 0)
    def _():
        idx_copy(0, 0).start()

    def step(slot):
        idx_copy(i, slot).wait()

        @pl.when(i + 1 < n)
        def _():
            idx_copy(i + 1, 1 - slot).start()

        def issue(tl, carry):
            src = h_ref.at[pl.ds(pl.multiple_of(tl * SUBLANES, SUBLANES), SUBLANES)]
            for k in range(TOP_K):
                d = pl.multiple_of(idx_ref[slot * TOP_K + k, tl] * SUBLANES, SUBLANES)
                pltpu.make_async_copy(src, xs_hbm.at[pl.ds(d, SUBLANES)], row_sem).start(priority=k % 2)
            return carry

        lax.fori_loop(0, tg, issue, 0)

    for parity in range(2):
        pl.when(lax.rem(i, 2) == parity)(functools.partial(step, parity))
    for _ in range(TOP_K):
        pltpu.make_async_copy(h_ref, xs_hbm.at[pl.ds(0, tg * SUBLANES)], row_sem).wait()


def _dispatch(pend_i, dest_t, h, n_rows, n_experts):
    t = h.shape[0] // SUBLANES
    tg = min(DISPATCH_TILE, t)
    kern = functools.partial(_dispatch_kernel, n_experts=n_experts)
    return pl.pallas_call(
        kern,
        out_shape=jax.ShapeDtypeStruct((n_rows * SUBLANES, LANES), h.dtype),
        grid_spec=pltpu.PrefetchScalarGridSpec(
            num_scalar_prefetch=1,
            grid=(t // tg,),
            in_specs=[pl.BlockSpec(memory_space=pl.ANY),
                      pl.BlockSpec((tg * SUBLANES, LANES), lambda i, pend: (i, 0))],
            out_specs=pl.BlockSpec(memory_space=pl.ANY),
            scratch_shapes=[pltpu.SMEM((2 * TOP_K, tg), I32), pltpu.VMEM((EXPERT_BLOCK * SUBLANES, LANES), h.dtype),
                            pltpu.SemaphoreType.DMA((2,)), pltpu.SemaphoreType.DMA]),
        compiler_params=pltpu.CompilerParams(dimension_semantics=("arbitrary",), has_side_effects=True,
                                             vmem_limit_bytes=VMEM_LIMIT_BYTES),
        name="dispatch",
    )(pend_i, dest_t, h)


def _expert_kernel(be_ref, nu_ref, pend_ref, xs_ref, wg_hbm, bg_ref, wu_hbm, bu_ref, wd_hbm, bd_ref, y_ref,
                   wg_f, wu_f, wd_f, wg_b, wu_b, wd_b, slot_ref, sems):
    i = pl.program_id(0)
    used = i < nu_ref[0]
    e = be_ref[i]

    def fetch(expert, slot):
        return [pltpu.make_async_copy(src.at[expert], dst.at[slot], sems.at[slot])
                for src, dst in ((wg_hbm, wg_f), (wu_hbm, wu_f), (wd_hbm, wd_f))]

    @pl.when(i == 0)
    def _():
        slot_ref[0] = 0
        for cp in fetch(e, 0):
            cp.start()

    first_of_expert = jnp.logical_or(i == 0, e != be_ref[jnp.maximum(i - 1, 0)])

    @pl.when(jnp.logical_and(used, first_of_expert))
    def _():
        slot = slot_ref[0]
        for cp in fetch(e, slot):
            cp.wait()
        wg_b[...] = wg_f[slot].astype(BF16)
        wu_b[...] = wu_f[slot].astype(BF16)
        wd_b[...] = wd_f[slot].astype(BF16)
        nxt = lax.div(pend_ref[e], EXPERT_BLOCK)

        @pl.when(nxt < nu_ref[0])
        def _():
            for cp in fetch(be_ref[nxt], 1 - slot):
                cp.start()

        slot_ref[0] = 1 - slot

    def ffn(rows):
        x = _token_rows_load(xs_ref, rows).astype(BF16)
        gate = jnp.minimum(jnp.dot(x, wg_b[...], preferred_element_type=F32) + bg_ref[...], SWIGLU_LIMIT)
        up = jnp.clip(jnp.dot(x, wu_b[...], preferred_element_type=F32) + bu_ref[...],
                      -SWIGLU_LIMIT, SWIGLU_LIMIT)
        glu = gate * _sigmoid(SWIGLU_ALPHA * gate)
        mid = ((up + 1.0) * glu).astype(BF16)
        y = jnp.dot(mid, wd_b[...], preferred_element_type=F32) + bd_ref[...]
        _token_rows_store(y_ref, y)
        if rows < EXPERT_BLOCK:
            y_ref[rows * SUBLANES:, :] = jnp.zeros(((EXPERT_BLOCK - rows) * SUBLANES, LANES), y_ref.dtype)

    valid = pend_ref[pend_ref.shape[0] // 2 + e] - i * EXPERT_BLOCK
    quarter = EXPERT_BLOCK // EXPERT_PATHS
    for p in range(1, EXPERT_PATHS + 1):
        covers = valid <= p * quarter if p < EXPERT_PATHS else True
        needs = valid > (p - 1) * quarter if p > 1 else True
        pl.when(jnp.logical_and(used, jnp.logical_and(covers, needs)))(functools.partial(ffn, p * quarter))

    @pl.when(jnp.logical_not(used))
    def _():
        y_ref[...] = jnp.zeros(y_ref.shape, y_ref.dtype)


def _experts(block_e, n_used, pend_i, xs, w_gate, b_gate, w_up, b_up, w_down, b_down):
    n_rows = xs.shape[0] // SUBLANES
    n_e, d, f = w_gate.shape
    nb = n_rows // EXPERT_BLOCK
    blk = (EXPERT_BLOCK * SUBLANES, LANES)
    last = lambda i, be, nu, pend: jnp.maximum(jnp.minimum(i, nu[0] - 1), 0)
    bspec = lambda n: pl.BlockSpec((None, 1, n), lambda i, be, nu, pend: (be[last(i, be, nu, pend)], 0, 0))
    hbm = pl.BlockSpec(memory_space=pl.ANY)
    return pl.pallas_call(
        _expert_kernel,
        out_shape=jax.ShapeDtypeStruct((n_rows * SUBLANES, LANES), F32),
        grid_spec=pltpu.PrefetchScalarGridSpec(
            num_scalar_prefetch=3,
            grid=(nb,),
            in_specs=[pl.BlockSpec(blk, lambda i, be, nu, pend: (last(i, be, nu, pend), 0)),
                      hbm, bspec(f), hbm, bspec(f), hbm, bspec(d)],
            out_specs=pl.BlockSpec(blk, lambda i, be, nu, pend: (i, 0)),
            scratch_shapes=[pltpu.VMEM((2, d, f), F32), pltpu.VMEM((2, d, f), F32), pltpu.VMEM((2, f, d), F32),
                            pltpu.VMEM((d, f), BF16), pltpu.VMEM((d, f), BF16), pltpu.VMEM((f, d), BF16),
                            pltpu.SMEM((1,), I32), pltpu.SemaphoreType.DMA((2,))]),
        compiler_params=_params("arbitrary"),
        name="experts",
    )(block_e, n_used, pend_i, xs, w_gate, b_gate.reshape(n_e, 1, f), w_up, b_up.reshape(n_e, 1, f),
      w_down, b_down.reshape(n_e, 1, d))


def _combine_kernel(dest_hbm, ys_hbm, tw_ref, x1_ref, gt_ref, g_ref, o_ref, idx_ref, buf_ref, idx_sem, row_sem):
    i = pl.program_id(0)
    n = pl.num_programs(0)
    tc = idx_ref.shape[1]

    def idx_copy(tile, s):
        return pltpu.make_async_copy(dest_hbm.at[:, pl.ds(pl.multiple_of(tile * tc, tc), tc)],
                                     idx_ref.at[pl.ds(s * TOP_K, TOP_K)], idx_sem.at[s])

    def issue_rows(s):
        def issue(tl, carry):
            dst_row = pl.multiple_of(tl * SUBLANES, SUBLANES)
            for k in range(TOP_K):
                d = pl.multiple_of(idx_ref[s * TOP_K + k, tl] * SUBLANES, SUBLANES)
                pltpu.make_async_copy(ys_hbm.at[pl.ds(d, SUBLANES)], buf_ref.at[s, k, pl.ds(dst_row, SUBLANES)],
                                      row_sem.at[s]).start(priority=k % 2)
            return carry

        lax.fori_loop(0, tc, issue, 0)

    @pl.when(i == 0)
    def _():
        idx_copy(0, 0).start()
        idx_copy(0, 0).wait()
        issue_rows(0)

        @pl.when(n > 1)
        def _():
            idx_copy(1, 1).start()

    def step(slot):
        @pl.when(i + 1 < n)
        def _():
            idx_copy(i + 1, 1 - slot).wait()
            issue_rows(1 - slot)

            @pl.when(i + 2 < n)
            def _():
                idx_copy(i + 2, slot).start()

        for k in range(TOP_K):
            pltpu.make_async_copy(ys_hbm.at[pl.ds(0, tc * SUBLANES)], buf_ref.at[slot, k], row_sem.at[slot]).wait()

        tw = tw_ref[...]
        ffn = tw[:, 0:1] * _token_rows_load(buf_ref.at[slot, 0], tc)
        for k in range(1, TOP_K):
            ffn = ffn + tw[:, k:k + 1] * _token_rows_load(buf_ref.at[slot, k], tc)
        x2 = x1_ref[...] + gt_ref[0] * ffn
        o_ref[...] = _rms(x2) * g_ref[...]

    for parity in range(2):
        pl.when(lax.rem(i, 2) == parity)(functools.partial(step, parity))


def _combine(dest_t, ys, topw, x1, gt, g, seq):
    t, d = x1.shape
    tc = min(COMBINE_TILE, seq)
    per_batch = seq // tc
    row = lambda i: (i, 0)
    return pl.pallas_call(
        _combine_kernel,
        out_shape=jax.ShapeDtypeStruct((t, d), F32),
        grid=(t // tc,),
        in_specs=[pl.BlockSpec(memory_space=pl.ANY), pl.BlockSpec(memory_space=pl.ANY),
                  pl.BlockSpec((tc, TOP_K), row), pl.BlockSpec((tc, d), row),
                  pl.BlockSpec((1, 1, d), lambda i: (i // per_batch, 0, 0)), _const_spec((1, d))],
        out_specs=pl.BlockSpec((tc, d), row),
        scratch_shapes=[pltpu.SMEM((2 * TOP_K, tc), I32), pltpu.VMEM((2, TOP_K, tc * SUBLANES, LANES), F32),
                        pltpu.SemaphoreType.DMA((2,)), pltpu.SemaphoreType.DMA((2,))],
        compiler_params=_params("arbitrary"),
        name="combine",
    )(dest_t, ys, topw, x1, gt, g)


def _layer(x2, mod, batch, seq, norm1_g, w_in, conv_w, conv_b, dt_bias, a_log, d_skip, ssm_norm_g,
           gla_wg2, gla_bg, gla_norm_g, w_out, norm2_g, w_router, b_router,
           w_gate, b_gate, w_up, b_up, w_down, b_down):
    t, d = x2.shape
    n_heads = dt_bias.shape[0]
    d_ssm = n_heads * SSM_HEAD_DIM
    cd = conv_w.shape[1]
    dkt = gla_wg2.shape[1]
    dvt = w_out.shape[0] - d_ssm
    n_experts = w_router.shape[1]

    sh1, sc1, gt1, sh2, sc2, gt2 = [m.reshape(batch, 1, d) for m in jnp.split(mod[:batch], 6, axis=1)]

    sizes = (d_ssm, cd, n_heads, dkt, dkt, dvt, GLA_GATE_RANK, dvt)
    offs = [0]
    for s in sizes:
        offs.append(offs[-1] + s)
    piece = lambda j: (offs[j], sizes[j])
    sections = ((piece(0),), (piece(1),), (piece(3),), (piece(4),), (piece(5),), (piece(7),), (piece(2), piece(6)))
    z, xbc, q, k, v, r, small = _inproj(x2, sc1, sh1, norm1_g.reshape(1, d), w_in.astype(BF16), sections,
                                        (d_ssm, cd, dkt, dkt, dvt, dvt, LANES), [BF16] * 6 + [F32], seq)

    d_skip_e = jnp.repeat(d_skip, SSM_HEAD_DIM).reshape(1, d_ssm)
    y = _ssd(xbc, z, small, conv_w, conv_b, dt_bias, a_log, d_skip_e, ssm_norm_g, batch, seq)
    o = _gla(q, k, v, r, small, gla_wg2, gla_bg, gla_norm_g, batch, seq, gate_col=n_heads)

    x1, h2, topi_t, topw_t, counts = _outproj(y, o, x2, gt1, sc2, sh2, norm2_g.reshape(1, d),
                                        w_out[:d_ssm].astype(BF16), w_out[d_ssm:].astype(BF16),
                                        w_router, b_router, seq)

    n_blocks = (t * TOP_K) // EXPERT_BLOCK + n_experts
    dest_t, block_e, pend = _route(topi_t, counts, n_blocks)
    pend_i = pend[:, 0].astype(I32)
    n_used = (pend_i[n_experts - 1:] // EXPERT_BLOCK).astype(I32)
    topw = topw_t.T
    xs = _dispatch(pend_i, dest_t, h2, n_blocks * EXPERT_BLOCK, n_experts)
    real_end = jnp.concatenate([jnp.zeros((1,), I32), pend_i[:-1]]) + counts[:, 0].astype(I32)
    seg_ends = jnp.concatenate([pend_i, real_end])
    ys = _experts(block_e[0, :n_blocks], n_used, seg_ends, xs, w_gate, b_gate, w_up, b_up, w_down, b_down)
    return dest_t, ys, topw, x1, gt2


def kernel(x, c, w_ada, b_ada, norm1_g, w_in, conv_w, conv_b, dt_bias, a_log, d_skip, ssm_norm_g, gla_wg2,
           gla_bg, gla_norm_g, w_out, norm2_g, w_router, b_router, w_gate, b_gate, w_up, b_up, w_down, b_down,
           final_norm_g):
    batch, seq, d = x.shape
    assert w_ada.shape[0] == 1, "single-layer trunk"
    assert d == SUBLANES * LANES, "token rows are moved as one (8, 128) f32 tile each"
    assert seq % min(seq, max(TOKEN_TILE, SSM_CHUNK, GLA_STEP_ROWS, COMBINE_TILE, DISPATCH_TILE)) == 0
    assert seq % max(SSM_CHUNK, GLA_STEP_ROWS) == 0
    x2 = x.reshape(batch * seq, d)
    c_pad = jnp.zeros((SUBLANES, d), F32).at[:batch].set(c)
    mod = _ada(c_pad, w_ada[0], b_ada)
    dest_t, ys, topw, x1, gt2 = _layer(
        x2, mod, batch, seq, norm1_g[0], w_in[0], conv_w[0], conv_b[0], dt_bias[0], a_log[0], d_skip[0],
        ssm_norm_g[0], gla_wg2[0], gla_bg[0], gla_norm_g[0], w_out[0], norm2_g[0], w_router[0], b_router[0],
        w_gate[0], b_gate[0], w_up[0], b_up[0], w_down[0], b_down[0])
    out = _combine(dest_t, ys, topw, x1, gt2, final_norm_g.reshape(1, d), seq)
    return out.reshape(batch, seq, d)
```

```python
import functools

import jax
import jax.numpy as jnp
from jax import lax
from jax.experimental import pallas as pl
from jax.experimental.pallas import tpu as pltpu

F32 = jnp.float32
BF16 = jnp.bfloat16
I32 = jnp.int32
HIGHEST = lax.Precision.HIGHEST

EPS = 1e-6
SSM_HEAD_DIM = 64
SSM_GROUPS = 2
SSM_STATE = 128
SSM_CONV = 4
SSM_CHUNK = 128
GLA_HEADS = 4
GLA_GATE_RANK = 16
GLA_GATE_NORM = 16.0
GLA_CHUNK = 64
TOP_K = 4
SWIGLU_LIMIT = 7.0
SWIGLU_ALPHA = 1.702

LANES = 128
SUBLANES = 8
V7X_VMEM_BYTES = 64 * 1024 * 1024
VMEM_LIMIT_BYTES = V7X_VMEM_BYTES - 8 * 1024 * 1024

TOKEN_TILE = 512
SSD_STEP_ROWS = 512
CONV_CARRY = 16
GLA_STEP_ROWS = 512
GLA_GROUP_ROWS = 128
ROUTE_TILE = 512
EXPERT_BLOCK = 512
EXPERT_PATHS = 4
DISPATCH_TILE = 2048
COMBINE_TILE = 512

_NT = (((1,), (1,)), ((), ()))
_TN = (((0,), (0,)), ((), ()))


def _sigmoid(v):
    return 0.5 * jnp.tanh(0.5 * v) + 0.5


def _silu(v):
    return v * _sigmoid(v)


def _softplus(v):
    return jnp.maximum(v, 0.0) + jnp.log1p(jnp.exp(-jnp.abs(v)))


def _log_sigmoid(v):
    return jnp.minimum(v, 0.0) - jnp.log(1.0 + jnp.exp(-jnp.abs(v)))


def _rms(v):
    return v * lax.rsqrt(jnp.mean(v * v, axis=-1, keepdims=True) + EPS)


def _pieces(a, n):
    out = []
    for _ in range(n - 1):
        p = a.astype(BF16)
        out.append(p)
        a = a - p.astype(F32)
    out.append(a.astype(BF16))
    return out


def _token_rows_load(ref, rows):
    return jnp.concatenate([ref[pl.ds(s, rows, stride=SUBLANES), :] for s in range(SUBLANES)], axis=1)


def _token_rows_store(ref, v):
    rows = v.shape[0]
    for s in range(SUBLANES):
        ref[pl.ds(s, rows, stride=SUBLANES), :] = v[:, s * LANES:(s + 1) * LANES]


def _params(*semantics):
    return pltpu.CompilerParams(dimension_semantics=semantics, vmem_limit_bytes=VMEM_LIMIT_BYTES)


def _const_spec(shape):
    nd = len(shape)
    return pl.BlockSpec(shape, lambda *_: (0,) * nd)


def _ada_kernel(c_ref, w_ref, b_ref, o_ref):
    cond = _silu(c_ref[...])
    o_ref[...] = jnp.dot(cond, w_ref[...], precision=HIGHEST, preferred_element_type=F32) + b_ref[...]


def _ada(c_pad, w_ada, b_ada):
    rows, d = c_pad.shape
    n = w_ada.shape[1]
    tn = d
    return pl.pallas_call(
        _ada_kernel,
        out_shape=jax.ShapeDtypeStruct((rows, n), F32),
        grid=(n // tn,),
        in_specs=[pl.BlockSpec((rows, d), lambda j: (0, 0)),
                  pl.BlockSpec((d, tn), lambda j: (0, j)),
                  pl.BlockSpec((1, tn), lambda j: (0, j))],
        out_specs=pl.BlockSpec((rows, tn), lambda j: (0, j)),
        compiler_params=_params("arbitrary"),
        name="ada",
    )(c_pad, w_ada, b_ada)


def _inproj_kernel(x_ref, sc_ref, sh_ref, g_ref, w_ref, *refs, sections):
    o_refs, ws_ref = refs[:-1], refs[-1]

    @pl.when(pl.program_id(0) == 0)
    def _():
        off = 0
        for o_ref, pieces in zip(o_refs, sections):
            n = o_ref.shape[1]
            used = 0
            for src, width in pieces:
                ws_ref[:, off + used:off + used + width] = w_ref[:, src:src + width]
                used += width
            if used < n:
                ws_ref[:, off + used:off + n] = jnp.zeros((ws_ref.shape[0], n - used), ws_ref.dtype)
            off += n

    h = (_rms(x_ref[...]) * g_ref[...]) * (1.0 + sc_ref[0]) + sh_ref[0]
    hb = h.astype(BF16)
    off = 0
    for o_ref in o_refs:
        n = o_ref.shape[1]
        o_ref[...] = jnp.dot(hb, ws_ref[:, off:off + n], preferred_element_type=F32).astype(o_ref.dtype)
        off += n


def _inproj(x2, sc, sh, g, w_bf, sections, widths, out_dtypes, seq):
    t, d = x2.shape
    tm = min(TOKEN_TILE, seq)
    per_batch = seq // tm
    assert all(n % LANES == 0 for n in widths)
    mod_spec = pl.BlockSpec((1, 1, d), lambda i: (i // per_batch, 0, 0))
    kern = functools.partial(_inproj_kernel, sections=sections)
    return pl.pallas_call(
        kern,
        out_shape=[jax.ShapeDtypeStruct((t, n), dt) for n, dt in zip(widths, out_dtypes)],
        grid=(t // tm,),
        in_specs=[pl.BlockSpec((tm, d), lambda i: (i, 0)), mod_spec, mod_spec, _const_spec((1, d)),
                  pl.BlockSpec(w_bf.shape, lambda i: (0, 0), pipeline_mode=pl.Buffered(1))],
        out_specs=[pl.BlockSpec((tm, n), lambda i: (i, 0)) for n in widths],
        scratch_shapes=[pltpu.VMEM((d, sum(widths)), BF16)],
        compiler_params=_params("arbitrary"),
        name="inproj",
    )(x2, sc, sh, g, w_bf)


def _ssd_kernel(tail_ref, xbc_ref, z_ref, sm_ref, cw_ref, cb_ref, dtb_r_ref, dtb_c_ref, alog_r_ref, alog_c_ref,
                dsk_ref, g_ref, y_ref, st_ref, *, n_heads, d_ssm):
    L = SSM_CHUNK
    P = SSM_HEAD_DIM
    N = SSM_STATE
    G = SSM_GROUPS
    gw = d_ssm // G
    R = xbc_ref.shape[0]
    c = pl.program_id(1)

    @pl.when(c == 0)
    def _():
        st_ref[...] = jnp.zeros(st_ref.shape, F32)

    tail = tail_ref[...]
    tail = jnp.where(c == 0, jnp.zeros_like(tail), tail)
    conv_in = jnp.concatenate([tail, xbc_ref[...]], axis=0)

    ri = lax.broadcasted_iota(I32, (L, L), 0)
    ci = lax.broadcasted_iota(I32, (L, L), 1)
    causal = ri >= ci
    causal3 = jnp.concatenate([causal.astype(BF16)] * 3, axis=1)
    upper3 = jnp.concatenate([(ri <= ci).astype(BF16)] * 3, axis=0)
    wr = lax.broadcasted_iota(I32, ((SSM_CONV - 1) * L, CONV_CARRY + L), 0)
    wc = lax.broadcasted_iota(I32, ((SSM_CONV - 1) * L, CONV_CARRY + L), 1)
    shifts = (wc == (wr % L) + CONV_CARRY - (SSM_CONV - 1) + wr // L).astype(BF16)
    hh = lax.broadcasted_iota(I32, (n_heads, d_ssm), 0)
    jj = lax.broadcasted_iota(I32, (n_heads, d_ssm), 1)
    expand = ((jj // P) == hh).astype(BF16)
    expand2 = jnp.concatenate([expand, expand], axis=0)
    lane = lax.broadcasted_iota(I32, (L, LANES), 1)
    first_half = lane < P
    heads_per_group = n_heads // G
    a_row = -jnp.exp(alog_r_ref[...])
    a_col = -jnp.exp(alog_c_ref[...])

    for s in range(R // L):
        rows = slice(s * L, (s + 1) * L)
        window = conv_in[s * L:s * L + CONV_CARRY + L, :]
        taps = jnp.dot(shifts, window, preferred_element_type=F32)
        acc = cb_ref[...] + cw_ref[SSM_CONV - 1:SSM_CONV, :] * window[CONV_CARRY:, :].astype(F32)
        for j in range(SSM_CONV - 1):
            acc = acc + cw_ref[j:j + 1, :] * taps[j * L:(j + 1) * L, :]
        act = _silu(acc)
        xs = act[:, :d_ssm]
        bm = act[:, d_ssm:d_ssm + G * N].astype(BF16)
        cm = act[:, d_ssm + G * N:].astype(BF16)

        sm = sm_ref[rows, :]
        dt_col = _softplus(sm[:, :n_heads] + dtb_r_ref[...])
        dt_row = _softplus(sm.T[:n_heads, :] + dtb_c_ref[...])
        cs_col = jnp.dot(causal3, jnp.concatenate(_pieces(dt_col * a_row, 3), axis=0),
                         preferred_element_type=F32)
        cs_row = jnp.dot(jnp.concatenate(_pieces(dt_row * a_col, 3), axis=1), upper3,
                         preferred_element_type=F32)
        cs_last = cs_col[L - 1:L, :]

        per_head = jnp.concatenate([dt_col, jnp.exp(cs_col), jnp.exp(cs_last - cs_col)], axis=0)
        per_ch = jnp.dot(jnp.concatenate(_pieces(per_head, 2), axis=1), expand2,
                         preferred_element_type=F32)
        dt_e, ecs_e, dte_e = per_ch[0:L], per_ch[L:2 * L], per_ch[2 * L:3 * L]
        chunk_decay = ecs_e[L - 1:L, :]

        xdt = xs * dt_e
        xdt_b = xdt.astype(BF16)
        xdec_b = (xdt * dte_e).astype(BF16)

        y_parts = []
        y_off_parts = []
        for g in range(G):
            bm_g = bm[:, g * N:(g + 1) * N]
            cm_g = cm[:, g * N:(g + 1) * N]
            cb = lax.dot_general(cm_g, bm_g, _NT, preferred_element_type=F32)
            prev = st_ref[g]
            y_off_parts.append(jnp.dot(cm_g, prev.astype(BF16), preferred_element_type=F32))
            s_new = lax.dot_general(bm_g, xdec_b[:, g * gw:(g + 1) * gw], _TN, preferred_element_type=F32)
            st_ref[g] = prev * chunk_decay[:, g * gw:(g + 1) * gw] + s_new
            for p in range(heads_per_group // 2):
                h0 = g * heads_per_group + 2 * p
                ms = []
                for h in (h0, h0 + 1):
                    diff = cs_col[:, h:h + 1] - cs_row[h:h + 1, :]
                    ms.append((cb * jnp.exp(jnp.where(causal, diff, -jnp.inf))).astype(BF16))
                lhs = jnp.concatenate(ms, axis=1)
                xp = xdt_b[:, h0 * P:(h0 + 2) * P]
                zero = jnp.zeros_like(xp)
                rhs = jnp.concatenate([jnp.where(first_half, xp, zero), jnp.where(first_half, zero, xp)], axis=0)
                y_parts.append(jnp.dot(lhs, rhs, preferred_element_type=F32))
        y = jnp.concatenate(y_parts, axis=1) + jnp.concatenate(y_off_parts, axis=1) * ecs_e + dsk_ref[...] * xs
        y = y * _silu(z_ref[rows, :].astype(F32))
        y = jnp.concatenate([_rms(y[:, g * gw:(g + 1) * gw]) for g in range(G)], axis=1) * g_ref[...]
        y_ref[rows, :] = y.astype(y_ref.dtype)


def _ssd(xbc, z, small, conv_w, conv_b, dt_bias, a_log, d_skip_e, norm_g, batch, seq):
    t, cd = xbc.shape
    d_ssm = z.shape[1]
    n_heads = dt_bias.shape[0]
    L = min(SSD_STEP_ROWS, seq)
    nc = seq // L
    row = lambda b, c: (b * nc + c, 0)
    tail = lambda b, c: (jnp.maximum((b * nc + c) * (L // CONV_CARRY) - 1, 0), 0)
    kern = functools.partial(_ssd_kernel, n_heads=n_heads, d_ssm=d_ssm)
    return pl.pallas_call(
        kern,
        out_shape=jax.ShapeDtypeStruct((t, d_ssm), BF16),
        grid=(batch, nc),
        in_specs=[pl.BlockSpec((CONV_CARRY, cd), tail),
                  pl.BlockSpec((L, cd), row), pl.BlockSpec((L, d_ssm), row), pl.BlockSpec((L, LANES), row),
                  _const_spec(conv_w.shape), _const_spec((1, cd)),
                  _const_spec((1, n_heads)), _const_spec((n_heads, 1)),
                  _const_spec((1, n_heads)), _const_spec((n_heads, 1)),
                  _const_spec((1, d_ssm)), _const_spec((1, d_ssm))],
        out_specs=pl.BlockSpec((L, d_ssm), row),
        scratch_shapes=[pltpu.VMEM((SSM_GROUPS, SSM_STATE, d_ssm // SSM_GROUPS), F32)],
        compiler_params=_params("arbitrary", "arbitrary"),
        name="ssd",
    )(xbc, xbc, z, small, conv_w, conv_b.reshape(1, cd), dt_bias.reshape(1, n_heads), dt_bias.reshape(n_heads, 1),
      a_log.reshape(1, n_heads), a_log.reshape(n_heads, 1), d_skip_e, norm_g.reshape(1, d_ssm))


def _gla_kernel(q_ref, k_ref, v_ref, r_ref, sm_ref, wg2_ref, bg_ref, gn_ref, o_ref, st_ref, *, gate_col):
    L = GLA_CHUNK
    H = GLA_HEADS
    dk = q_ref.shape[1] // H
    dv = v_ref.shape[1] // H
    c = pl.program_id(1)

    @pl.when(c == 0)
    def _():
        st_ref[...] = jnp.zeros(st_ref.shape, F32)

    R = min(GLA_GROUP_ROWS, q_ref.shape[0])
    n_chunks = R // L
    ri = lax.broadcasted_iota(I32, (R, R), 0)
    ci = lax.broadcasted_iota(I32, (R, R), 1)
    same_chunk = (ri // L) == (ci // L)
    causal = jnp.logical_and(same_chunk, ri >= ci)
    tril = causal.astype(BF16)
    later = jnp.logical_and(same_chunk, ri < ci).astype(BF16)
    sr = lax.broadcasted_iota(I32, (R, n_chunks * LANES), 0)
    sc = lax.broadcasted_iota(I32, (R, n_chunks * LANES), 1)
    last_rows = (sr == (sc // LANES) * L + (L - 1)).astype(BF16)
    wg2_hi, wg2_lo = _pieces(wg2_ref[...], 2)

    tril3 = jnp.concatenate([jnp.concatenate([tril] * 3, axis=1), jnp.concatenate([later] * 3, axis=1)], axis=0)
    last3 = jnp.concatenate([last_rows] * 3, axis=0)
    wg3 = jnp.concatenate([wg2_hi, wg2_hi, wg2_lo], axis=0)

    for gi in range(q_ref.shape[0] // R):
        rs = slice(gi * R, (gi + 1) * R)
        q = q_ref[rs, :].astype(F32) * (dk ** -0.5)
        k = k_ref[rs, :].astype(F32)
        v = v_ref[rs, :]
        r = r_ref[rs, :].astype(F32)
        g_hi, g_lo = _pieces(sm_ref[rs, gate_col:gate_col + GLA_GATE_RANK], 2)
        pre = jnp.dot(jnp.concatenate([g_hi, g_lo, g_hi], axis=1), wg3, preferred_element_type=F32) + bg_ref[...]
        gk3 = jnp.concatenate(_pieces(_log_sigmoid(pre) / GLA_GATE_NORM, 3), axis=0)
        sums = jnp.dot(tril3, gk3, preferred_element_type=F32)
        bcum = sums[:R]
        to_end = sums[R:]
        q_t = (q * jnp.exp(bcum)).astype(BF16)
        k_t = (k * jnp.exp(-bcum)).astype(BF16)
        k_dec = (k * jnp.exp(to_end)).astype(BF16)
        dcol = jnp.exp(lax.dot_general(jnp.concatenate(_pieces(bcum, 3), axis=0), last3, _TN,
                                       preferred_element_type=F32))
        outs = []
        for h in range(H):
            ks = slice(h * dk, (h + 1) * dk)
            vs = slice(h * dv, (h + 1) * dv)
            att = lax.dot_general(q_t[:, ks], k_t[:, ks], _NT, preferred_element_type=F32)
            att = jnp.where(causal, att, 0.0).astype(BF16)
            o = jnp.dot(att, v[:, vs], preferred_element_type=F32)
            state = st_ref[h]
            inter = []
            for c in range(n_chunks):
                rows = slice(c * L, (c + 1) * L)
                inter.append(jnp.dot(q_t[rows, ks], state.astype(BF16), preferred_element_type=F32))
                s_new = lax.dot_general(k_dec[rows, ks], v[rows, vs], _TN, preferred_element_type=F32)
                dec = dcol[ks, c * LANES:(c + 1) * LANES]
                state = state * jnp.concatenate([dec] * (dv // LANES), axis=1) + s_new
            st_ref[h] = state
            o = o + jnp.concatenate(inter, axis=0)
            outs.append(_rms(o) * gn_ref[...] * _silu(r[:, vs]))
        o_ref[rs, :] = jnp.concatenate(outs, axis=1).astype(o_ref.dtype)


def _gla(q, k, v, r, small, wg2, bg, norm_g, batch, seq, gate_col):
    t, dkt = q.shape
    dvt = v.shape[1]
    rows = min(GLA_STEP_ROWS, seq)
    nc = seq // rows
    row = lambda b, c: (b * nc + c, 0)
    kern = functools.partial(_gla_kernel, gate_col=gate_col)
    return pl.pallas_call(
        kern,
        out_shape=jax.ShapeDtypeStruct((t, dvt), BF16),
        grid=(batch, nc),
        in_specs=[pl.BlockSpec((rows, dkt), row), pl.BlockSpec((rows, dkt), row), pl.BlockSpec((rows, dvt), row),
                  pl.BlockSpec((rows, dvt), row), pl.BlockSpec((rows, LANES), row),
                  _const_spec(wg2.shape), _const_spec((1, dkt)), _const_spec((1, dvt // GLA_HEADS))],
        out_specs=pl.BlockSpec((rows, dvt), row),
        scratch_shapes=[pltpu.VMEM((GLA_HEADS, dkt // GLA_HEADS, dvt // GLA_HEADS), F32)],
        compiler_params=_params("arbitrary", "arbitrary"),
        name="gla",
    )(q, k, v, r, small, wg2, bg.reshape(1, dkt), norm_g.reshape(1, dvt // GLA_HEADS))


def _outproj_kernel(y_ref, o_ref, x_ref, gt_ref, sc_ref, sh_ref, g_ref, wy_ref, wo_ref, wr_ref, br_ref,
                    x1_ref, h_ref, ti_ref, tw_ref, cnt_ref):
    mix = (jnp.dot(y_ref[...], wy_ref[...], preferred_element_type=F32)
           + jnp.dot(o_ref[...], wo_ref[...], preferred_element_type=F32))
    x1 = x_ref[...] + gt_ref[0] * mix
    x1_ref[...] = x1
    h = (_rms(x1) * g_ref[...]) * (1.0 + sc_ref[0]) + sh_ref[0]
    _token_rows_store(h_ref, h)
    n_e = br_ref.shape[0]
    h_hi, h_lo = _pieces(h, 2)
    wr = wr_ref[...]
    hw = lax.dot_general(wr, h_hi, _NT, preferred_element_type=F32)
    logits = (hw[:n_e] + hw[n_e:] + lax.dot_general(wr[:n_e], h_lo, _NT, preferred_element_type=F32)) + br_ref[...]
    expert = lax.broadcasted_iota(I32, logits.shape, 0)
    vals, idxs = [], []
    counts = jnp.zeros(logits.shape, F32)
    for _ in range(TOP_K):
        m = jnp.max(logits, axis=0, keepdims=True)
        idx = jnp.min(jnp.where(logits == m, expert, n_e), axis=0, keepdims=True)
        vals.append(m)
        idxs.append(idx)
        chosen = expert == idx
        counts = counts + chosen.astype(F32)
        logits = jnp.where(chosen, -jnp.inf, logits)
    exps = [jnp.exp(v - vals[0]) for v in vals]
    denom = functools.reduce(lambda a, b: a + b, exps)
    ti_ref[...] = jnp.concatenate(idxs, axis=0)
    tw_ref[...] = jnp.concatenate([e / denom for e in exps], axis=0)

    @pl.when(pl.program_id(0) == 0)
    def _():
        cnt_ref[...] = jnp.zeros(cnt_ref.shape, F32)

    cnt_ref[...] = cnt_ref[...] + jnp.sum(counts, axis=1, keepdims=True)


def _outproj(y, o, x2, gt, sc, sh, g, wy, wo, w_router, b_router, seq):
    t, d = x2.shape
    n_e = w_router.shape[1]
    tm = min(TOKEN_TILE, seq)
    per_batch = seq // tm
    row = lambda i: (i, 0)
    mod_spec = pl.BlockSpec((1, 1, d), lambda i: (i // per_batch, 0, 0))
    wr_t = w_router.T
    wr_hi = wr_t.astype(BF16)
    wr_cat = jnp.concatenate([wr_hi, (wr_t - wr_hi.astype(F32)).astype(BF16)], axis=0)
    col = lambda i: (0, i)
    return pl.pallas_call(
        _outproj_kernel,
        out_shape=[jax.ShapeDtypeStruct((t, d), F32), jax.ShapeDtypeStruct((t * SUBLANES, LANES), F32),
                   jax.ShapeDtypeStruct((TOP_K, t), I32), jax.ShapeDtypeStruct((TOP_K, t), F32),
                   jax.ShapeDtypeStruct((n_e, LANES), F32)],
        grid=(t // tm,),
        in_specs=[pl.BlockSpec((tm, y.shape[1]), row), pl.BlockSpec((tm, o.shape[1]), row),
                  pl.BlockSpec((tm, d), row), mod_spec, mod_spec, mod_spec, _const_spec((1, d)),
                  _const_spec(wy.shape), _const_spec(wo.shape), _const_spec(wr_cat.shape),
                  _const_spec((n_e, 1))],
        out_specs=[pl.BlockSpec((tm, d), row), pl.BlockSpec((tm * SUBLANES, LANES), row),
                   pl.BlockSpec((TOP_K, tm), col), pl.BlockSpec((TOP_K, tm), col),
                   _const_spec((n_e, LANES))],
        compiler_params=_params("arbitrary"),
        name="outproj",
    )(y, o, x2, gt, sc, sh, g, wy, wo, wr_cat, b_router.reshape(n_e, 1))


def _route_kernel(ti_ref, cnt_ref, dest_ref, be_ref, pend_ref, run_ref, *, n_blocks_pad):
    i = pl.program_id(0)
    n_e = cnt_ref.shape[0]
    tr = ti_ref.shape[1]

    @pl.when(i == 0)
    def _():
        counts = cnt_ref[...]
        padded = jnp.ceil(counts / EXPERT_BLOCK) * EXPERT_BLOCK
        ri = lax.broadcasted_iota(I32, (n_e, n_e), 0)
        ci = lax.broadcasted_iota(I32, (n_e, n_e), 1)
        pend = jnp.dot((ri >= ci).astype(F32), padded, precision=HIGHEST, preferred_element_type=F32)
        pend_ref[...] = pend
        run_ref[...] = pend - padded
        start = (lax.broadcasted_iota(I32, (n_e, n_blocks_pad), 1) * EXPERT_BLOCK).astype(F32)
        be = jnp.sum((pend[:, 0:1] <= start).astype(F32), axis=0, keepdims=True)
        be_ref[...] = jnp.minimum(be, n_e - 1).astype(I32)

    ti = ti_ref[...]
    expert = lax.broadcasted_iota(I32, (n_e, tr), 0)
    onehots = [expert == ti[k:k + 1, :] for k in range(TOP_K)]
    cnt = functools.reduce(lambda a, b: a + b, [oh.astype(F32) for oh in onehots])
    ri = lax.broadcasted_iota(I32, (tr, tr), 0)
    ci = lax.broadcasted_iota(I32, (tr, tr), 1)
    before = jnp.dot(cnt.astype(BF16), (ri < ci).astype(BF16), preferred_element_type=F32)
    base = run_ref[:, 0:1] + before
    dest = [jnp.sum(jnp.where(oh, base, 0.0), axis=0, keepdims=True) for oh in onehots]
    dest_ref[...] = jnp.concatenate(dest, axis=0).astype(I32)
    run_ref[...] = run_ref[...] + jnp.sum(cnt, axis=1, keepdims=True)


def _route(topi_t, counts, n_blocks):
    t = topi_t.shape[1]
    n_e = counts.shape[0]
    tr = min(ROUTE_TILE, t)
    n_blocks_pad = -(-n_blocks // LANES) * LANES
    kern = functools.partial(_route_kernel, n_blocks_pad=n_blocks_pad)
    return pl.pallas_call(
        kern,
        out_shape=[jax.ShapeDtypeStruct((TOP_K, t), I32), jax.ShapeDtypeStruct((1, n_blocks_pad), I32),
                   jax.ShapeDtypeStruct((n_e, LANES), F32)],
        grid=(t // tr,),
        in_specs=[pl.BlockSpec((TOP_K, tr), lambda i: (0, i)), _const_spec((n_e, LANES))],
        out_specs=[pl.BlockSpec((TOP_K, tr), lambda i: (0, i)), _const_spec((1, n_blocks_pad)),
                   _const_spec((n_e, LANES))],
        scratch_shapes=[pltpu.VMEM((n_e, LANES), F32)],
        compiler_params=_params("arbitrary"),
        name="route",
    )(topi_t, counts)


def _dispatch_kernel(pend_ref, dest_hbm, h_ref, xs_hbm, idx_ref, zero_ref, idx_sem, row_sem, *, n_experts):
    i = pl.program_id(0)
    tg = idx_ref.shape[1]

    @pl.when(i == 0)
    def _():
        zero_ref[...] = jnp.zeros(zero_ref.shape, zero_ref.dtype)
        n_blocks = xs_hbm.shape[0] // (EXPERT_BLOCK * SUBLANES)
        total = pend_ref[n_experts - 1]

        def fill(block_row):
            return pltpu.make_async_copy(zero_ref, xs_hbm.at[pl.ds(block_row, EXPERT_BLOCK * SUBLANES)], row_sem)

        for act in ("start", "wait"):
            for e in range(n_experts):
                end = pend_ref[e]
                prev = pend_ref[e - 1] if e > 0 else 0

                @pl.when(end > prev)
                def _():
                    cp = fill(pl.multiple_of((end - EXPERT_BLOCK) * SUBLANES, EXPERT_BLOCK * SUBLANES))
                    cp.start() if act == "start" else cp.wait()

            for b in range(n_blocks - n_experts, n_blocks):
                @pl.when(b * EXPERT_BLOCK >= total)
                def _():
                    cp = fill(b * EXPERT_BLOCK * SUBLANES)
                    cp.start() if act == "start" else cp.wait()

    n = pl.num_programs(0)

    def idx_copy(tile, s):
        return pltpu.make_async_copy(dest_hbm.at[:, pl.ds(pl.multiple_of(tile * tg, tg), tg)],
                                     idx_ref.at[pl.ds(s * TOP_K, TOP_K)], idx_sem.at[s])

    @pl.when(i == 0)
    def _():
        idx_copy(0, 0).start()

    def step(slot):
        idx_copy(i, slot).wait()

        @pl.when(i + 1 < n)
        def _():
            idx_copy(i + 1, 1 - slot).start()

        def issue(tl, carry):
            src = h_ref.at[pl.ds(pl.multiple_of(tl * SUBLANES, SUBLANES), SUBLANES)]
            for k in range(TOP_K):
                d = pl.multiple_of(idx_ref[slot * TOP_K + k, tl] * SUBLANES, SUBLANES)
                pltpu.make_async_copy(src, xs_hbm.at[pl.ds(d, SUBLANES)], row_sem).start(priority=k % 2)
            return carry

        lax.fori_loop(0, tg, issue, 0)

    for parity in range(2):
        pl.when(lax.rem(i, 2) == parity)(functools.partial(step, parity))
    for _ in range(TOP_K):
        pltpu.make_async_copy(h_ref, xs_hbm.at[pl.ds(0, tg * SUBLANES)], row_sem).wait()


def _dispatch(pend_i, dest_t, h, n_rows, n_experts):
    t = h.shape[0] // SUBLANES
    tg = min(DISPATCH_TILE, t)
    kern = functools.partial(_dispatch_kernel, n_experts=n_experts)
    return pl.pallas_call(
        kern,
        out_shape=jax.ShapeDtypeStruct((n_rows * SUBLANES, LANES), h.dtype),
        grid_spec=pltpu.PrefetchScalarGridSpec(
            num_scalar_prefetch=1,
            grid=(t // tg,),
            in_specs=[pl.BlockSpec(memory_space=pl.ANY),
                      pl.BlockSpec((tg * SUBLANES, LANES), lambda i, pend: (i, 0))],
            out_specs=pl.BlockSpec(memory_space=pl.ANY),
            scratch_shapes=[pltpu.SMEM((2 * TOP_K, tg), I32), pltpu.VMEM((EXPERT_BLOCK * SUBLANES, LANES), h.dtype),
                            pltpu.SemaphoreType.DMA((2,)), pltpu.SemaphoreType.DMA]),
        compiler_params=pltpu.CompilerParams(dimension_semantics=("arbitrary",), has_side_effects=True,
                                             vmem_limit_bytes=VMEM_LIMIT_BYTES),
        name="dispatch",
    )(pend_i, dest_t, h)


def _expert_kernel(be_ref, nu_ref, pend_ref, xs_ref, wg_hbm, bg_ref, wu_hbm, bu_ref, wd_hbm, bd_ref, y_ref,
                   wg_f, wu_f, wd_f, wg_b, wu_b, wd_b, slot_ref, sems):
    i = pl.program_id(0)
    used = i < nu_ref[0]
    e = be_ref[i]

    def fetch(expert, slot):
        return [pltpu.make_async_copy(src.at[expert], dst.at[slot], sems.at[slot])
                for src, dst in ((wg_hbm, wg_f), (wu_hbm, wu_f), (wd_hbm, wd_f))]

    @pl.when(i == 0)
    def _():
        slot_ref[0] = 0
        for cp in fetch(e, 0):
            cp.start()

    first_of_expert = jnp.logical_or(i == 0, e != be_ref[jnp.maximum(i - 1, 0)])

    @pl.when(jnp.logical_and(used, first_of_expert))
    def _():
        slot = slot_ref[0]
        for cp in fetch(e, slot):
            cp.wait()
        wg_b[...] = wg_f[slot].astype(BF16)
        wu_b[...] = wu_f[slot].astype(BF16)
        wd_b[...] = wd_f[slot].astype(BF16)
        nxt = lax.div(pend_ref[e], EXPERT_BLOCK)

        @pl.when(nxt < nu_ref[0])
        def _():
            for cp in fetch(be_ref[nxt], 1 - slot):
                cp.start()

        slot_ref[0] = 1 - slot

    def ffn(rows):
        x = _token_rows_load(xs_ref, rows).astype(BF16)
        gate = jnp.minimum(jnp.dot(x, wg_b[...], preferred_element_type=F32) + bg_ref[...], SWIGLU_LIMIT)
        up = jnp.clip(jnp.dot(x, wu_b[...], preferred_element_type=F32) + bu_ref[...],
                      -SWIGLU_LIMIT, SWIGLU_LIMIT)
        glu = gate * _sigmoid(SWIGLU_ALPHA * gate)
        mid = ((up + 1.0) * glu).astype(BF16)
        y = jnp.dot(mid, wd_b[...], preferred_element_type=F32) + bd_ref[...]
        _token_rows_store(y_ref, y)
        if rows < EXPERT_BLOCK:
            y_ref[rows * SUBLANES:, :] = jnp.zeros(((EXPERT_BLOCK - rows) * SUBLANES, LANES), y_ref.dtype)

    valid = pend_ref[pend_ref.shape[0] // 2 + e] - i * EXPERT_BLOCK
    quarter = EXPERT_BLOCK // EXPERT_PATHS
    for p in range(1, EXPERT_PATHS + 1):
        covers = valid <= p * quarter if p < EXPERT_PATHS else True
        needs = valid > (p - 1) * quarter if p > 1 else True
        pl.when(jnp.logical_and(used, jnp.logical_and(covers, needs)))(functools.partial(ffn, p * quarter))

    @pl.when(jnp.logical_not(used))
    def _():
        y_ref[...] = jnp.zeros(y_ref.shape, y_ref.dtype)


def _experts(block_e, n_used, pend_i, xs, w_gate, b_gate, w_up, b_up, w_down, b_down):
    n_rows = xs.shape[0] // SUBLANES
    n_e, d, f = w_gate.shape
    nb = n_rows // EXPERT_BLOCK
    blk = (EXPERT_BLOCK * SUBLANES, LANES)
    last = lambda i, be, nu, pend: jnp.maximum(jnp.minimum(i, nu[0] - 1), 0)
    bspec = lambda n: pl.BlockSpec((None, 1, n), lambda i, be, nu, pend: (be[last(i, be, nu, pend)], 0, 0))
    hbm = pl.BlockSpec(memory_space=pl.ANY)
    return pl.pallas_call(
        _expert_kernel,
        out_shape=jax.ShapeDtypeStruct((n_rows * SUBLANES, LANES), F32),
        grid_spec=pltpu.PrefetchScalarGridSpec(
            num_scalar_prefetch=3,
            grid=(nb,),
            in_specs=[pl.BlockSpec(blk, lambda i, be, nu, pend: (last(i, be, nu, pend), 0)),
                      hbm, bspec(f), hbm, bspec(f), hbm, bspec(d)],
            out_specs=pl.BlockSpec(blk, lambda i, be, nu, pend: (i, 0)),
            scratch_shapes=[pltpu.VMEM((2, d, f), F32), pltpu.VMEM((2, d, f), F32), pltpu.VMEM((2, f, d), F32),
                            pltpu.VMEM((d, f), BF16), pltpu.VMEM((d, f), BF16), pltpu.VMEM((f, d), BF16),
                            pltpu.SMEM((1,), I32), pltpu.SemaphoreType.DMA((2,))]),
        compiler_params=_params("arbitrary"),
        name="experts",
    )(block_e, n_used, pend_i, xs, w_gate, b_gate.reshape(n_e, 1, f), w_up, b_up.reshape(n_e, 1, f),
      w_down, b_down.reshape(n_e, 1, d))


def _combine_kernel(dest_hbm, ys_hbm, tw_ref, x1_ref, gt_ref, g_ref, o_ref, idx_ref, buf_ref, idx_sem, row_sem):
    i = pl.program_id(0)
    n = pl.num_programs(0)
    tc = idx_ref.shape[1]

    def idx_copy(tile, s):
        return pltpu.make_async_copy(dest_hbm.at[:, pl.ds(pl.multiple_of(tile * tc, tc), tc)],
                                     idx_ref.at[pl.ds(s * TOP_K, TOP_K)], idx_sem.at[s])

    def issue_rows(s):
        def issue(tl, carry):
            dst_row = pl.multiple_of(tl * SUBLANES, SUBLANES)
            for k in range(TOP_K):
                d = pl.multiple_of(idx_ref[s * TOP_K + k, tl] * SUBLANES, SUBLANES)
                pltpu.make_async_copy(ys_hbm.at[pl.ds(d, SUBLANES)], buf_ref.at[s, k, pl.ds(dst_row, SUBLANES)],
                                      row_sem.at[s]).start(priority=k % 2)
            return carry

        lax.fori_loop(0, tc, issue, 0)

    @pl.when(i == 0)
    def _():
        idx_copy(0, 0).start()
        idx_copy(0, 0).wait()
        issue_rows(0)

        @pl.when(n > 1)
        def _():
            idx_copy(1, 1).start()

    def step(slot):
        @pl.when(i + 1 < n)
        def _():
            idx_copy(i + 1, 1 - slot).wait()
            issue_rows(1 - slot)

            @pl.when(i + 2 < n)
            def _():
                idx_copy(i + 2, slot).start()

        for k in range(TOP_K):
            pltpu.make_async_copy(ys_hbm.at[pl.ds(0, tc * SUBLANES)], buf_ref.at[slot, k], row_sem.at[slot]).wait()

        tw = tw_ref[...]
        ffn = tw[:, 0:1] * _token_rows_load(buf_ref.at[slot, 0], tc)
        for k in range(1, TOP_K):
            ffn = ffn + tw[:, k:k + 1] * _token_rows_load(buf_ref.at[slot, k], tc)
        x2 = x1_ref[...] + gt_ref[0] * ffn
        o_ref[...] = _rms(x2) * g_ref[...]

    for parity in range(2):
        pl.when(lax.rem(i, 2) == parity)(functools.partial(step, parity))


def _combine(dest_t, ys, topw, x1, gt, g, seq):
    t, d = x1.shape
    tc = min(COMBINE_TILE, seq)
    per_batch = seq // tc
    row = lambda i: (i, 0)
    return pl.pallas_call(
        _combine_kernel,
        out_shape=jax.ShapeDtypeStruct((t, d), F32),
        grid=(t // tc,),
        in_specs=[pl.BlockSpec(memory_space=pl.ANY), pl.BlockSpec(memory_space=pl.ANY),
                  pl.BlockSpec((tc, TOP_K), row), pl.BlockSpec((tc, d), row),
                  pl.BlockSpec((1, 1, d), lambda i: (i // per_batch, 0, 0)), _const_spec((1, d))],
        out_specs=pl.BlockSpec((tc, d), row),
        scratch_shapes=[pltpu.SMEM((2 * TOP_K, tc), I32), pltpu.VMEM((2, TOP_K, tc * SUBLANES, LANES), F32),
                        pltpu.SemaphoreType.DMA((2,)), pltpu.SemaphoreType.DMA((2,))],
        compiler_params=_params("arbitrary"),
        name="combine",
    )(dest_t, ys, topw, x1, gt, g)


def _layer(x2, mod, batch, seq, norm1_g, w_in, conv_w, conv_b, dt_bias, a_log, d_skip, ssm_norm_g,
           gla_wg2, gla_bg, gla_norm_g, w_out, norm2_g, w_router, b_router,
           w_gate, b_gate, w_up, b_up, w_down, b_down):
    t, d = x2.shape
    n_heads = dt_bias.shape[0]
    d_ssm = n_heads * SSM_HEAD_DIM
    cd = conv_w.shape[1]
    dkt = gla_wg2.shape[1]
    dvt = w_out.shape[0] - d_ssm
    n_experts = w_router.shape[1]

    sh1, sc1, gt1, sh2, sc2, gt2 = [m.reshape(batch, 1, d) for m in jnp.split(mod[:batch], 6, axis=1)]

    sizes = (d_ssm, cd, n_heads, dkt, dkt, dvt, GLA_GATE_RANK, dvt)
    offs = [0]
    for s in sizes:
        offs.append(offs[-1] + s)
    piece = lambda j: (offs[j], sizes[j])
    sections = ((piece(0),), (piece(1),), (piece(3),), (piece(4),), (piece(5),), (piece(7),), (piece(2), piece(6)))
    z, xbc, q, k, v, r, small = _inproj(x2, sc1, sh1, norm1_g.reshape(1, d), w_in.astype(BF16), sections,
                                        (d_ssm, cd, dkt, dkt, dvt, dvt, LANES), [BF16] * 6 + [F32], seq)

    d_skip_e = jnp.repeat(d_skip, SSM_HEAD_DIM).reshape(1, d_ssm)
    y = _ssd(xbc, z, small, conv_w, conv_b, dt_bias, a_log, d_skip_e, ssm_norm_g, batch, seq)
    o = _gla(q, k, v, r, small, gla_wg2, gla_bg, gla_norm_g, batch, seq, gate_col=n_heads)

    x1, h2, topi_t, topw_t, counts = _outproj(y, o, x2, gt1, sc2, sh2, norm2_g.reshape(1, d),
                                        w_out[:d_ssm].astype(BF16), w_out[d_ssm:].astype(BF16),
                                        w_router, b_router, seq)

    n_blocks = (t * TOP_K) // EXPERT_BLOCK + n_experts
    dest_t, block_e, pend = _route(topi_t, counts, n_blocks)
    pend_i = pend[:, 0].astype(I32)
    n_used = (pend_i[n_experts - 1:] // EXPERT_BLOCK).astype(I32)
    topw = topw_t.T
    xs = _dispatch(pend_i, dest_t, h2, n_blocks * EXPERT_BLOCK, n_experts)
    real_end = jnp.concatenate([jnp.zeros((1,), I32), pend_i[:-1]]) + counts[:, 0].astype(I32)
    seg_ends = jnp.concatenate([pend_i, real_end])
    ys = _experts(block_e[0, :n_blocks], n_used, seg_ends, xs, w_gate, b_gate, w_up, b_up, w_down, b_down)
    return dest_t, ys, topw, x1, gt2


def kernel(x, c, w_ada, b_ada, norm1_g, w_in, conv_w, conv_b, dt_bias, a_log, d_skip, ssm_norm_g, gla_wg2,
           gla_bg, gla_norm_g, w_out, norm2_g, w_router, b_router, w_gate, b_gate, w_up, b_up, w_down, b_down,
           final_norm_g):
    batch, seq, d = x.shape
    assert w_ada.shape[0] == 1, "single-layer trunk"
    assert d == SUBLANES * LANES, "token rows are moved as one (8, 128) f32 tile each"
    assert seq % min(seq, max(TOKEN_TILE, SSM_CHUNK, GLA_STEP_ROWS, COMBINE_TILE, DISPATCH_TILE)) == 0
    assert seq % max(SSM_CHUNK, GLA_STEP_ROWS) == 0
    x2 = x.reshape(batch * seq, d)
    c_pad = jnp.zeros((SUBLANES, d), F32).at[:batch].set(c)
    mod = _ada(c_pad, w_ada[0], b_ada)
    dest_t, ys, topw, x1, gt2 = _layer(
        x2, mod, batch, seq, norm1_g[0], w_in[0], conv_w[0], conv_b[0], dt_bias[0], a_log[0], d_skip[0],
        ssm_norm_g[0], gla_wg2[0], gla_bg[0], gla_norm_g[0], w_out[0], norm2_g[0], w_router[0], b_router[0],
        w_gate[0], b_gate[0], w_up[0], b_up[0], w_down[0], b_down[0])
    out = _combine(dest_t, ys, topw, x1, gt2, final_norm_g.reshape(1, d), seq)
    return out.reshape(batch, seq, d)
```

```python
import functools

import jax
import jax.numpy as jnp
from jax import lax
from jax.experimental import pallas as pl
from jax.experimental.pallas import tpu as pltpu

F32 = jnp.float32
BF16 = jnp.bfloat16
I32 = jnp.int32
HIGHEST = lax.Precision.HIGHEST

EPS = 1e-6
SSM_HEAD_DIM = 64
SSM_GROUPS = 2
SSM_STATE = 128
SSM_CONV = 4
SSM_CHUNK = 128
GLA_HEADS = 4
GLA_GATE_RANK = 16
GLA_GATE_NORM = 16.0
GLA_CHUNK = 64
TOP_K = 4
SWIGLU_LIMIT = 7.0
SWIGLU_ALPHA = 1.702

LANES = 128
SUBLANES = 8
V7X_VMEM_BYTES = 64 * 1024 * 1024
VMEM_LIMIT_BYTES = V7X_VMEM_BYTES - 8 * 1024 * 1024

TOKEN_TILE = 512
MIXER_STEP_ROWS = 512
CONV_CARRY = 16
GLA_GROUP_ROWS = 128
ROUTE_TILE = 512
EXPERT_BLOCK = 512
EXPERT_PATHS = 4
DISPATCH_TILE = 2048
COMBINE_TILE = 512

_NT = (((1,), (1,)), ((), ()))
_TN = (((0,), (0,)), ((), ()))


def _sigmoid(v):
    return 0.5 * jnp.tanh(0.5 * v) + 0.5


def _silu(v):
    return v * _sigmoid(v)


def _softplus(v):
    return jnp.maximum(v, 0.0) + jnp.log1p(jnp.exp(-jnp.abs(v)))


def _log_sigmoid(v):
    return jnp.minimum(v, 0.0) - jnp.log(1.0 + jnp.exp(-jnp.abs(v)))


def _rms(v):
    return v * lax.rsqrt(jnp.mean(v * v, axis=-1, keepdims=True) + EPS)


def _pieces(a, n):
    out = []
    for _ in range(n - 1):
        p = a.astype(BF16)
        out.append(p)
        a = a - p.astype(F32)
    out.append(a.astype(BF16))
    return out


def _token_rows_load(ref, rows):
    return jnp.concatenate([ref[pl.ds(s, rows, stride=SUBLANES), :] for s in range(SUBLANES)], axis=1)


def _token_rows_store(ref, v):
    rows = v.shape[0]
    for s in range(SUBLANES):
        ref[pl.ds(s, rows, stride=SUBLANES), :] = v[:, s * LANES:(s + 1) * LANES]


def _params(*semantics):
    return pltpu.CompilerParams(dimension_semantics=semantics, vmem_limit_bytes=VMEM_LIMIT_BYTES)


def _const_spec(shape):
    nd = len(shape)
    return pl.BlockSpec(shape, lambda *_: (0,) * nd)


def _ada_kernel(c_ref, w_ref, b_ref, o_ref):
    cond = _silu(c_ref[...])
    o_ref[...] = jnp.dot(cond, w_ref[...], precision=HIGHEST, preferred_element_type=F32) + b_ref[...]


def _ada(c_pad, w_ada, b_ada):
    rows, d = c_pad.shape
    n = w_ada.shape[1]
    tn = d
    return pl.pallas_call(
        _ada_kernel,
        out_shape=jax.ShapeDtypeStruct((rows, n), F32),
        grid=(n // tn,),
        in_specs=[pl.BlockSpec((rows, d), lambda j: (0, 0)),
                  pl.BlockSpec((d, tn), lambda j: (0, j)),
                  pl.BlockSpec((1, tn), lambda j: (0, j))],
        out_specs=pl.BlockSpec((rows, tn), lambda j: (0, j)),
        compiler_params=_params("arbitrary"),
        name="ada",
    )(c_pad, w_ada, b_ada)


def _inproj_kernel(x_ref, sc_ref, sh_ref, g_ref, w_ref, *refs, sections):
    o_refs, ws_ref = refs[:-1], refs[-1]

    @pl.when(pl.program_id(0) == 0)
    def _():
        off = 0
        for o_ref, pieces in zip(o_refs, sections):
            n = o_ref.shape[1]
            used = 0
            for src, width in pieces:
                ws_ref[:, off + used:off + used + width] = w_ref[:, src:src + width]
                used += width
            if used < n:
                ws_ref[:, off + used:off + n] = jnp.zeros((ws_ref.shape[0], n - used), ws_ref.dtype)
            off += n

    h = (_rms(x_ref[...]) * g_ref[...]) * (1.0 + sc_ref[0]) + sh_ref[0]
    hb = h.astype(BF16)
    off = 0
    for o_ref in o_refs:
        n = o_ref.shape[1]
        o_ref[...] = jnp.dot(hb, ws_ref[:, off:off + n], preferred_element_type=F32).astype(o_ref.dtype)
        off += n


def _inproj(x2, sc, sh, g, w_bf, sections, widths, out_dtypes, seq):
    t, d = x2.shape
    tm = min(TOKEN_TILE, seq)
    per_batch = seq // tm
    assert all(n % LANES == 0 for n in widths)
    mod_spec = pl.BlockSpec((1, 1, d), lambda i: (i // per_batch, 0, 0))
    kern = functools.partial(_inproj_kernel, sections=sections)
    return pl.pallas_call(
        kern,
        out_shape=[jax.ShapeDtypeStruct((t, n), dt) for n, dt in zip(widths, out_dtypes)],
        grid=(t // tm,),
        in_specs=[pl.BlockSpec((tm, d), lambda i: (i, 0)), mod_spec, mod_spec, _const_spec((1, d)),
                  pl.BlockSpec(w_bf.shape, lambda i: (0, 0), pipeline_mode=pl.Buffered(1))],
        out_specs=[pl.BlockSpec((tm, n), lambda i: (i, 0)) for n in widths],
        scratch_shapes=[pltpu.VMEM((d, sum(widths)), BF16)],
        compiler_params=_params("arbitrary"),
        name="inproj",
    )(x2, sc, sh, g, w_bf)


def _ssd_kernel(tail_ref, xbc_ref, z_ref, sm_ref, cw_ref, cb_ref, dtb_r_ref, dtb_c_ref, alog_r_ref, alog_c_ref,
                dsk_ref, g_ref, y_ref, st_ref, *, n_heads, d_ssm):
    L = SSM_CHUNK
    P = SSM_HEAD_DIM
    N = SSM_STATE
    G = SSM_GROUPS
    gw = d_ssm // G
    R = xbc_ref.shape[0]
    c = pl.program_id(1)

    tail = tail_ref[...]
    tail = jnp.where(c == 0, jnp.zeros_like(tail), tail)
    conv_in = jnp.concatenate([tail, xbc_ref[...]], axis=0)

    ri = lax.broadcasted_iota(I32, (L, L), 0)
    ci = lax.broadcasted_iota(I32, (L, L), 1)
    causal = ri >= ci
    causal3 = jnp.concatenate([causal.astype(BF16)] * 3, axis=1)
    upper3 = jnp.concatenate([(ri <= ci).astype(BF16)] * 3, axis=0)
    wr = lax.broadcasted_iota(I32, ((SSM_CONV - 1) * L, CONV_CARRY + L), 0)
    wc = lax.broadcasted_iota(I32, ((SSM_CONV - 1) * L, CONV_CARRY + L), 1)
    shifts = (wc == (wr % L) + CONV_CARRY - (SSM_CONV - 1) + wr // L).astype(BF16)
    hh = lax.broadcasted_iota(I32, (n_heads, d_ssm), 0)
    jj = lax.broadcasted_iota(I32, (n_heads, d_ssm), 1)
    expand = ((jj // P) == hh).astype(BF16)
    expand2 = jnp.concatenate([expand, expand], axis=0)
    lane = lax.broadcasted_iota(I32, (L, LANES), 1)
    first_half = lane < P
    heads_per_group = n_heads // G
    a_row = -jnp.exp(alog_r_ref[...])
    a_col = -jnp.exp(alog_c_ref[...])

    for s in range(R // L):
        rows = slice(s * L, (s + 1) * L)
        window = conv_in[s * L:s * L + CONV_CARRY + L, :]
        taps = jnp.dot(shifts, window, preferred_element_type=F32)
        acc = cb_ref[...] + cw_ref[SSM_CONV - 1:SSM_CONV, :] * window[CONV_CARRY:, :].astype(F32)
        for j in range(SSM_CONV - 1):
            acc = acc + cw_ref[j:j + 1, :] * taps[j * L:(j + 1) * L, :]
        act = _silu(acc)
        xs = act[:, :d_ssm]
        bm = act[:, d_ssm:d_ssm + G * N].astype(BF16)
        cm = act[:, d_ssm + G * N:].astype(BF16)

        sm = sm_ref[rows, :]
        dt_col = _softplus(sm[:, :n_heads] + dtb_r_ref[...])
        dt_row = _softplus(sm.T[:n_heads, :] + dtb_c_ref[...])
        cs_col = jnp.dot(causal3, jnp.concatenate(_pieces(dt_col * a_row, 3), axis=0),
                         preferred_element_type=F32)
        cs_row = jnp.dot(jnp.concatenate(_pieces(dt_row * a_col, 3), axis=1), upper3,
                         preferred_element_type=F32)
        cs_last = cs_col[L - 1:L, :]

        per_head = jnp.concatenate([dt_col, jnp.exp(cs_col), jnp.exp(cs_last - cs_col)], axis=0)
        per_ch = jnp.dot(jnp.concatenate(_pieces(per_head, 2), axis=1), expand2,
                         preferred_element_type=F32)
        dt_e, ecs_e, dte_e = per_ch[0:L], per_ch[L:2 * L], per_ch[2 * L:3 * L]
        chunk_decay = ecs_e[L - 1:L, :]

        xdt = xs * dt_e
        xdt_b = xdt.astype(BF16)
        xdec_b = (xdt * dte_e).astype(BF16)

        y_parts = []
        y_off_parts = []
        for g in range(G):
            bm_g = bm[:, g * N:(g + 1) * N]
            cm_g = cm[:, g * N:(g + 1) * N]
            cb = lax.dot_general(cm_g, bm_g, _NT, preferred_element_type=F32)
            prev = st_ref[g]
            y_off_parts.append(jnp.dot(cm_g, prev.astype(BF16), preferred_element_type=F32))
            s_new = lax.dot_general(bm_g, xdec_b[:, g * gw:(g + 1) * gw], _TN, preferred_element_type=F32)
            st_ref[g] = prev * chunk_decay[:, g * gw:(g + 1) * gw] + s_new
            for p in range(heads_per_group // 2):
                h0 = g * heads_per_group + 2 * p
                ms = []
                for h in (h0, h0 + 1):
                    diff = cs_col[:, h:h + 1] - cs_row[h:h + 1, :]
                    ms.append((cb * jnp.exp(jnp.where(causal, diff, -jnp.inf))).astype(BF16))
                lhs = jnp.concatenate(ms, axis=1)
                xp = xdt_b[:, h0 * P:(h0 + 2) * P]
                zero = jnp.zeros_like(xp)
                rhs = jnp.concatenate([jnp.where(first_half, xp, zero), jnp.where(first_half, zero, xp)], axis=0)
                y_parts.append(jnp.dot(lhs, rhs, preferred_element_type=F32))
        y = jnp.concatenate(y_parts, axis=1) + jnp.concatenate(y_off_parts, axis=1) * ecs_e + dsk_ref[...] * xs
        y = y * _silu(z_ref[rows, :].astype(F32))
        y = jnp.concatenate([_rms(y[:, g * gw:(g + 1) * gw]) for g in range(G)], axis=1) * g_ref[...]
        y_ref[rows, :] = y.astype(y_ref.dtype)


def _gla_kernel(q_ref, k_ref, v_ref, r_ref, sm_ref, wg2_ref, bg_ref, gn_ref, o_ref, st_ref, *, gate_col):
    L = GLA_CHUNK
    H = GLA_HEADS
    dk = q_ref.shape[1] // H
    dv = v_ref.shape[1] // H

    R = min(GLA_GROUP_ROWS, q_ref.shape[0])
    n_chunks = R // L
    ri = lax.broadcasted_iota(I32, (R, R), 0)
    ci = lax.broadcasted_iota(I32, (R, R), 1)
    same_chunk = (ri // L) == (ci // L)
    causal = jnp.logical_and(same_chunk, ri >= ci)
    tril = causal.astype(BF16)
    later = jnp.logical_and(same_chunk, ri < ci).astype(BF16)
    sr = lax.broadcasted_iota(I32, (R, n_chunks * LANES), 0)
    sc = lax.broadcasted_iota(I32, (R, n_chunks * LANES), 1)
    last_rows = (sr == (sc // LANES) * L + (L - 1)).astype(BF16)
    wg2_hi, wg2_lo = _pieces(wg2_ref[...], 2)

    tril3 = jnp.concatenate([jnp.concatenate([tril] * 3, axis=1), jnp.concatenate([later] * 3, axis=1)], axis=0)
    last3 = jnp.concatenate([last_rows] * 3, axis=0)
    wg3 = jnp.concatenate([wg2_hi, wg2_hi, wg2_lo], axis=0)

    for gi in range(q_ref.shape[0] // R):
        rs = slice(gi * R, (gi + 1) * R)
        q = q_ref[rs, :].astype(F32) * (dk ** -0.5)
        k = k_ref[rs, :].astype(F32)
        v = v_ref[rs, :]
        r = r_ref[rs, :].astype(F32)
        g_hi, g_lo = _pieces(sm_ref[rs, gate_col:gate_col + GLA_GATE_RANK], 2)
        pre = jnp.dot(jnp.concatenate([g_hi, g_lo, g_hi], axis=1), wg3, preferred_element_type=F32) + bg_ref[...]
        gk3 = jnp.concatenate(_pieces(_log_sigmoid(pre) / GLA_GATE_NORM, 3), axis=0)
        sums = jnp.dot(tril3, gk3, preferred_element_type=F32)
        bcum = sums[:R]
        to_end = sums[R:]
        q_t = (q * jnp.exp(bcum)).astype(BF16)
        k_t = (k * jnp.exp(-bcum)).astype(BF16)
        k_dec = (k * jnp.exp(to_end)).astype(BF16)
        dcol = jnp.exp(lax.dot_general(jnp.concatenate(_pieces(bcum, 3), axis=0), last3, _TN,
                                       preferred_element_type=F32))
        outs = []
        for h in range(H):
            ks = slice(h * dk, (h + 1) * dk)
            vs = slice(h * dv, (h + 1) * dv)
            att = lax.dot_general(q_t[:, ks], k_t[:, ks], _NT, preferred_element_type=F32)
            att = jnp.where(causal, att, 0.0).astype(BF16)
            o = jnp.dot(att, v[:, vs], preferred_element_type=F32)
            state = st_ref[h]
            inter = []
            for c in range(n_chunks):
                rows = slice(c * L, (c + 1) * L)
                inter.append(jnp.dot(q_t[rows, ks], state.astype(BF16), preferred_element_type=F32))
                s_new = lax.dot_general(k_dec[rows, ks], v[rows, vs], _TN, preferred_element_type=F32)
                dec = dcol[ks, c * LANES:(c + 1) * LANES]
                state = state * jnp.concatenate([dec] * (dv // LANES), axis=1) + s_new
            st_ref[h] = state
            o = o + jnp.concatenate(inter, axis=0)
            outs.append(_rms(o) * gn_ref[...] * _silu(r[:, vs]))
        o_ref[rs, :] = jnp.concatenate(outs, axis=1).astype(o_ref.dtype)


N_SSD_INPUTS = 12
N_GLA_INPUTS = 8


def _mixer_kernel(*refs, n_heads, d_ssm, gate_col):
    ssd_in = refs[:N_SSD_INPUTS]
    gla_in = refs[N_SSD_INPUTS:N_SSD_INPUTS + N_GLA_INPUTS]
    y_ref, o_ref, ssd_st, gla_st = refs[N_SSD_INPUTS + N_GLA_INPUTS:]

    @pl.when(pl.program_id(1) == 0)
    def _():
        ssd_st[...] = jnp.zeros(ssd_st.shape, F32)
        gla_st[...] = jnp.zeros(gla_st.shape, F32)

    _ssd_kernel(*ssd_in, y_ref, ssd_st, n_heads=n_heads, d_ssm=d_ssm)
    _gla_kernel(*gla_in, o_ref, gla_st, gate_col=gate_col)


def _mixers(xbc, z, q, k, v, r, small, conv_w, conv_b, dt_bias, a_log, d_skip_e, ssm_norm_g, wg2, bg, gla_norm_g,
            batch, seq, gate_col):
    t, cd = xbc.shape
    d_ssm = z.shape[1]
    n_heads = dt_bias.shape[0]
    dkt = q.shape[1]
    dvt = v.shape[1]
    L = min(MIXER_STEP_ROWS, seq)
    nc = seq // L
    row = lambda b, c: (b * nc + c, 0)
    tail = lambda b, c: (jnp.maximum((b * nc + c) * (L // CONV_CARRY) - 1, 0), 0)
    kern = functools.partial(_mixer_kernel, n_heads=n_heads, d_ssm=d_ssm, gate_col=gate_col)
    ssd_specs = [pl.BlockSpec((CONV_CARRY, cd), tail),
                 pl.BlockSpec((L, cd), row), pl.BlockSpec((L, d_ssm), row), pl.BlockSpec((L, LANES), row),
                 _const_spec(conv_w.shape), _const_spec((1, cd)),
                 _const_spec((1, n_heads)), _const_spec((n_heads, 1)),
                 _const_spec((1, n_heads)), _const_spec((n_heads, 1)),
                 _const_spec((1, d_ssm)), _const_spec((1, d_ssm))]
    gla_specs = [pl.BlockSpec((L, dkt), row), pl.BlockSpec((L, dkt), row), pl.BlockSpec((L, dvt), row),
                 pl.BlockSpec((L, dvt), row), pl.BlockSpec((L, LANES), row),
                 _const_spec(wg2.shape), _const_spec((1, dkt)), _const_spec((1, dvt // GLA_HEADS))]
    assert len(ssd_specs) == N_SSD_INPUTS and len(gla_specs) == N_GLA_INPUTS
    return pl.pallas_call(
        kern,
        out_shape=[jax.ShapeDtypeStruct((t, d_ssm), BF16), jax.ShapeDtypeStruct((t, dvt), BF16)],
        grid=(batch, nc),
        in_specs=ssd_specs + gla_specs,
        out_specs=[pl.BlockSpec((L, d_ssm), row), pl.BlockSpec((L, dvt), row)],
        scratch_shapes=[pltpu.VMEM((SSM_GROUPS, SSM_STATE, d_ssm // SSM_GROUPS), F32),
                        pltpu.VMEM((GLA_HEADS, dkt // GLA_HEADS, dvt // GLA_HEADS), F32)],
        compiler_params=_params("arbitrary", "arbitrary"),
        name="mixers",
    )(xbc, xbc, z, small, conv_w, conv_b.reshape(1, cd), dt_bias.reshape(1, n_heads), dt_bias.reshape(n_heads, 1),
      a_log.reshape(1, n_heads), a_log.reshape(n_heads, 1), d_skip_e, ssm_norm_g.reshape(1, d_ssm),
      q, k, v, r, small, wg2, bg.reshape(1, dkt), gla_norm_g.reshape(1, dvt // GLA_HEADS))


def _outproj_kernel(y_ref, o_ref, x_ref, gt_ref, sc_ref, sh_ref, g_ref, wy_ref, wo_ref, wr_ref, br_ref,
                    x1_ref, h_ref, ti_ref, tw_ref, cnt_ref):
    mix = (jnp.dot(y_ref[...], wy_ref[...], preferred_element_type=F32)
           + jnp.dot(o_ref[...], wo_ref[...], preferred_element_type=F32))
    x1 = x_ref[...] + gt_ref[0] * mix
    x1_ref[...] = x1
    h = (_rms(x1) * g_ref[...]) * (1.0 + sc_ref[0]) + sh_ref[0]
    _token_rows_store(h_ref, h)
    n_e = br_ref.shape[0]
    h_hi, h_lo = _pieces(h, 2)
    wr = wr_ref[...]
    hw = lax.dot_general(wr, h_hi, _NT, preferred_element_type=F32)
    logits = (hw[:n_e] + hw[n_e:] + lax.dot_general(wr[:n_e], h_lo, _NT, preferred_element_type=F32)) + br_ref[...]
    expert = lax.broadcasted_iota(I32, logits.shape, 0)
    vals, idxs = [], []
    counts = jnp.zeros(logits.shape, F32)
    for _ in range(TOP_K):
        m = jnp.max(logits, axis=0, keepdims=True)
        idx = jnp.min(jnp.where(logits == m, expert, n_e), axis=0, keepdims=True)
        vals.append(m)
        idxs.append(idx)
        chosen = expert == idx
        counts = counts + chosen.astype(F32)
        logits = jnp.where(chosen, -jnp.inf, logits)
    exps = [jnp.exp(v - vals[0]) for v in vals]
    denom = functools.reduce(lambda a, b: a + b, exps)
    ti_ref[...] = jnp.concatenate(idxs, axis=0)
    tw_ref[...] = jnp.concatenate([e / denom for e in exps], axis=0)

    @pl.when(pl.program_id(0) == 0)
    def _():
        cnt_ref[...] = jnp.zeros(cnt_ref.shape, F32)

    cnt_ref[...] = cnt_ref[...] + jnp.sum(counts, axis=1, keepdims=True)


def _outproj(y, o, x2, gt, sc, sh, g, wy, wo, w_router, b_router, seq):
    t, d = x2.shape
    n_e = w_router.shape[1]
    tm = min(TOKEN_TILE, seq)
    per_batch = seq // tm
    row = lambda i: (i, 0)
    mod_spec = pl.BlockSpec((1, 1, d), lambda i: (i // per_batch, 0, 0))
    wr_t = w_router.T
    wr_hi = wr_t.astype(BF16)
    wr_cat = jnp.concatenate([wr_hi, (wr_t - wr_hi.astype(F32)).astype(BF16)], axis=0)
    col = lambda i: (0, i)
    return pl.pallas_call(
        _outproj_kernel,
        out_shape=[jax.ShapeDtypeStruct((t, d), F32), jax.ShapeDtypeStruct((t * SUBLANES, LANES), F32),
                   jax.ShapeDtypeStruct((TOP_K, t), I32), jax.ShapeDtypeStruct((TOP_K, t), F32),
                   jax.ShapeDtypeStruct((n_e, LANES), F32)],
        grid=(t // tm,),
        in_specs=[pl.BlockSpec((tm, y.shape[1]), row), pl.BlockSpec((tm, o.shape[1]), row),
                  pl.BlockSpec((tm, d), row), mod_spec, mod_spec, mod_spec, _const_spec((1, d)),
                  _const_spec(wy.shape), _const_spec(wo.shape), _const_spec(wr_cat.shape),
                  _const_spec((n_e, 1))],
        out_specs=[pl.BlockSpec((tm, d), row), pl.BlockSpec((tm * SUBLANES, LANES), row),
                   pl.BlockSpec((TOP_K, tm), col), pl.BlockSpec((TOP_K, tm), col),
                   _const_spec((n_e, LANES))],
        compiler_params=_params("arbitrary"),
        name="outproj",
    )(y, o, x2, gt, sc, sh, g, wy, wo, wr_cat, b_router.reshape(n_e, 1))


def _route_kernel(ti_ref, cnt_ref, dest_ref, be_ref, pend_ref, run_ref, *, n_blocks_pad):
    i = pl.program_id(0)
    n_e = cnt_ref.shape[0]
    tr = ti_ref.shape[1]

    @pl.when(i == 0)
    def _():
        counts = cnt_ref[...]
        padded = jnp.ceil(counts / EXPERT_BLOCK) * EXPERT_BLOCK
        ri = lax.broadcasted_iota(I32, (n_e, n_e), 0)
        ci = lax.broadcasted_iota(I32, (n_e, n_e), 1)
        pend = jnp.dot((ri >= ci).astype(F32), padded, precision=HIGHEST, preferred_element_type=F32)
        pend_ref[...] = pend
        run_ref[...] = pend - padded
        start = (lax.broadcasted_iota(I32, (n_e, n_blocks_pad), 1) * EXPERT_BLOCK).astype(F32)
        be = jnp.sum((pend[:, 0:1] <= start).astype(F32), axis=0, keepdims=True)
        be_ref[...] = jnp.minimum(be, n_e - 1).astype(I32)

    ti = ti_ref[...]
    expert = lax.broadcasted_iota(I32, (n_e, tr), 0)
    onehots = [expert == ti[k:k + 1, :] for k in range(TOP_K)]
    cnt = functools.reduce(lambda a, b: a + b, [oh.astype(F32) for oh in onehots])
    ri = lax.broadcasted_iota(I32, (tr, tr), 0)
    ci = lax.broadcasted_iota(I32, (tr, tr), 1)
    before = jnp.dot(cnt.astype(BF16), (ri < ci).astype(BF16), preferred_element_type=F32)
    base = run_ref[:, 0:1] + before
    dest = [jnp.sum(jnp.where(oh, base, 0.0), axis=0, keepdims=True) for oh in onehots]
    dest_ref[...] = jnp.concatenate(dest, axis=0).astype(I32)
    run_ref[...] = run_ref[...] + jnp.sum(cnt, axis=1, keepdims=True)


def _route(topi_t, counts, n_blocks):
    t = topi_t.shape[1]
    n_e = counts.shape[0]
    tr = min(ROUTE_TILE, t)
    n_blocks_pad = -(-n_blocks // LANES) * LANES
    kern = functools.partial(_route_kernel, n_blocks_pad=n_blocks_pad)
    return pl.pallas_call(
        kern,
        out_shape=[jax.ShapeDtypeStruct((TOP_K, t), I32), jax.ShapeDtypeStruct((1, n_blocks_pad), I32),
                   jax.ShapeDtypeStruct((n_e, LANES), F32)],
        grid=(t // tr,),
        in_specs=[pl.BlockSpec((TOP_K, tr), lambda i: (0, i)), _const_spec((n_e, LANES))],
        out_specs=[pl.BlockSpec((TOP_K, tr), lambda i: (0, i)), _const_spec((1, n_blocks_pad)),
                   _const_spec((n_e, LANES))],
        scratch_shapes=[pltpu.VMEM((n_e, LANES), F32)],
        compiler_params=_params("arbitrary"),
        name="route",
    )(topi_t, counts)


def _dispatch_kernel(pend_ref, dest_hbm, h_ref, xs_hbm, idx_ref, zero_ref, idx_sem, row_sem, *, n_experts):
    i = pl.program_id(0)
    tg = idx_ref.shape[1]

    @pl.when(i == 0)
    def _():
        zero_ref[...] = jnp.zeros(zero_ref.shape, zero_ref.dtype)
        n_blocks = xs_hbm.shape[0] // (EXPERT_BLOCK * SUBLANES)
        total = pend_ref[n_experts - 1]

        def fill(block_row):
            return pltpu.make_async_copy(zero_ref, xs_hbm.at[pl.ds(block_row, EXPERT_BLOCK * SUBLANES)], row_sem)

        for act in ("start", "wait"):
            for e in range(n_experts):
                end = pend_ref[e]
                prev = pend_ref[e - 1] if e > 0 else 0

                @pl.when(end > prev)
                def _():
                    cp = fill(pl.multiple_of((end - EXPERT_BLOCK) * SUBLANES, EXPERT_BLOCK * SUBLANES))
                    cp.start() if act == "start" else cp.wait()

            for b in range(n_blocks - n_experts, n_blocks):
                @pl.when(b * EXPERT_BLOCK >= total)
                def _():
                    cp = fill(b * EXPERT_BLOCK * SUBLANES)
                    cp.start() if act == "start" else cp.wait()

    n = pl.num_programs(0)

    def idx_copy(tile, s):
        return pltpu.make_async_copy(dest_hbm.at[:, pl.ds(pl.multiple_of(tile * tg, tg), tg)],
                                     idx_ref.at[pl.ds(s * TOP_K, TOP_K)], idx_sem.at[s])

    @pl.when(i == 0)
    def _():
        idx_copy(0, 0).start()

    def step(slot):
        idx_copy(i, slot).wait()

        @pl.when(i + 1 < n)
        def _():
            idx_copy(i + 1, 1 - slot).start()

        def issue(tl, carry):
            src = h_ref.at[pl.ds(pl.multiple_of(tl * SUBLANES, SUBLANES), SUBLANES)]
            for k in range(TOP_K):
                d = pl.multiple_of(idx_ref[slot * TOP_K + k, tl] * SUBLANES, SUBLANES)
                pltpu.make_async_copy(src, xs_hbm.at[pl.ds(d, SUBLANES)], row_sem).start(priority=k % 2)
            return carry

        lax.fori_loop(0, tg, issue, 0)

    for parity in range(2):
        pl.when(lax.rem(i, 2) == parity)(functools.partial(step, parity))
    for _ in range(TOP_K):
        pltpu.make_async_copy(h_ref, xs_hbm.at[pl.ds(0, tg * SUBLANES)], row_sem).wait()


def _dispatch(pend_i, dest_t, h, n_rows, n_experts):
    t = h.shape[0] // SUBLANES
    tg = min(DISPATCH_TILE, t)
    kern = functools.partial(_dispatch_kernel, n_experts=n_experts)
    return pl.pallas_call(
        kern,
        out_shape=jax.ShapeDtypeStruct((n_rows * SUBLANES, LANES), h.dtype),
        grid_spec=pltpu.PrefetchScalarGridSpec(
            num_scalar_prefetch=1,
            grid=(t // tg,),
            in_specs=[pl.BlockSpec(memory_space=pl.ANY),
                      pl.BlockSpec((tg * SUBLANES, LANES), lambda i, pend: (i, 0))],
            out_specs=pl.BlockSpec(memory_space=pl.ANY),
            scratch_shapes=[pltpu.SMEM((2 * TOP_K, tg), I32), pltpu.VMEM((EXPERT_BLOCK * SUBLANES, LANES), h.dtype),
                            pltpu.SemaphoreType.DMA((2,)), pltpu.SemaphoreType.DMA]),
        compiler_params=pltpu.CompilerParams(dimension_semantics=("arbitrary",), has_side_effects=True,
                                             vmem_limit_bytes=VMEM_LIMIT_BYTES),
        name="dispatch",
    )(pend_i, dest_t, h)


def _expert_kernel(be_ref, nu_ref, pend_ref, xs_ref, wg_hbm, bg_ref, wu_hbm, bu_ref, wd_hbm, bd_ref, y_ref,
                   wg_f, wu_f, wd_f, wg_b, wu_b, wd_b, slot_ref, sems):
    i = pl.program_id(0)
    used = i < nu_ref[0]
    e = be_ref[i]

    def fetch(expert, slot):
        return [pltpu.make_async_copy(src.at[expert], dst.at[slot], sems.at[slot])
                for src, dst in ((wg_hbm, wg_f), (wu_hbm, wu_f), (wd_hbm, wd_f))]

    @pl.when(i == 0)
    def _():
        slot_ref[0] = 0
        for cp in fetch(e, 0):
            cp.start()

    first_of_expert = jnp.logical_or(i == 0, e != be_ref[jnp.maximum(i - 1, 0)])

    @pl.when(jnp.logical_and(used, first_of_expert))
    def _():
        slot = slot_ref[0]
        for cp in fetch(e, slot):
            cp.wait()
        wg_b[...] = wg_f[slot].astype(BF16)
        wu_b[...] = wu_f[slot].astype(BF16)
        wd_b[...] = wd_f[slot].astype(BF16)
        nxt = lax.div(pend_ref[e], EXPERT_BLOCK)

        @pl.when(nxt < nu_ref[0])
        def _():
            for cp in fetch(be_ref[nxt], 1 - slot):
                cp.start()

        slot_ref[0] = 1 - slot

    def ffn(rows):
        x = _token_rows_load(xs_ref, rows).astype(BF16)
        gate = jnp.minimum(jnp.dot(x, wg_b[...], preferred_element_type=F32) + bg_ref[...], SWIGLU_LIMIT)
        up = jnp.clip(jnp.dot(x, wu_b[...], preferred_element_type=F32) + bu_ref[...],
                      -SWIGLU_LIMIT, SWIGLU_LIMIT)
        glu = gate * _sigmoid(SWIGLU_ALPHA * gate)
        mid = ((up + 1.0) * glu).astype(BF16)
        y = jnp.dot(mid, wd_b[...], preferred_element_type=F32) + bd_ref[...]
        _token_rows_store(y_ref, y)
        if rows < EXPERT_BLOCK:
            y_ref[rows * SUBLANES:, :] = jnp.zeros(((EXPERT_BLOCK - rows) * SUBLANES, LANES), y_ref.dtype)

    valid = pend_ref[pend_ref.shape[0] // 2 + e] - i * EXPERT_BLOCK
    quarter = EXPERT_BLOCK // EXPERT_PATHS
    for p in range(1, EXPERT_PATHS + 1):
        covers = valid <= p * quarter if p < EXPERT_PATHS else True
        needs = valid > (p - 1) * quarter if p > 1 else True
        pl.when(jnp.logical_and(used, jnp.logical_and(covers, needs)))(functools.partial(ffn, p * quarter))

    @pl.when(jnp.logical_not(used))
    def _():
        y_ref[...] = jnp.zeros(y_ref.shape, y_ref.dtype)


def _experts(block_e, n_used, pend_i, xs, w_gate, b_gate, w_up, b_up, w_down, b_down):
    n_rows = xs.shape[0] // SUBLANES
    n_e, d, f = w_gate.shape
    nb = n_rows // EXPERT_BLOCK
    blk = (EXPERT_BLOCK * SUBLANES, LANES)
    last = lambda i, be, nu, pend: jnp.maximum(jnp.minimum(i, nu[0] - 1), 0)
    bspec = lambda n: pl.BlockSpec((None, 1, n), lambda i, be, nu, pend: (be[last(i, be, nu, pend)], 0, 0))
    hbm = pl.BlockSpec(memory_space=pl.ANY)
    return pl.pallas_call(
        _expert_kernel,
        out_shape=jax.ShapeDtypeStruct((n_rows * SUBLANES, LANES), F32),
        grid_spec=pltpu.PrefetchScalarGridSpec(
            num_scalar_prefetch=3,
            grid=(nb,),
            in_specs=[pl.BlockSpec(blk, lambda i, be, nu, pend: (last(i, be, nu, pend), 0)),
                      hbm, bspec(f), hbm, bspec(f), hbm, bspec(d)],
            out_specs=pl.BlockSpec(blk, lambda i, be, nu, pend: (i, 0)),
            scratch_shapes=[pltpu.VMEM((2, d, f), F32), pltpu.VMEM((2, d, f), F32), pltpu.VMEM((2, f, d), F32),
                            pltpu.VMEM((d, f), BF16), pltpu.VMEM((d, f), BF16), pltpu.VMEM((f, d), BF16),
                            pltpu.SMEM((1,), I32), pltpu.SemaphoreType.DMA((2,))]),
        compiler_params=_params("arbitrary"),
        name="experts",
    )(block_e, n_used, pend_i, xs, w_gate, b_gate.reshape(n_e, 1, f), w_up, b_up.reshape(n_e, 1, f),
      w_down, b_down.reshape(n_e, 1, d))


def _combine_kernel(dest_hbm, ys_hbm, tw_ref, x1_ref, gt_ref, g_ref, o_ref, idx_ref, buf_ref, idx_sem, row_sem):
    i = pl.program_id(0)
    n = pl.num_programs(0)
    tc = idx_ref.shape[1]

    def idx_copy(tile, s):
        return pltpu.make_async_copy(dest_hbm.at[:, pl.ds(pl.multiple_of(tile * tc, tc), tc)],
                                     idx_ref.at[pl.ds(s * TOP_K, TOP_K)], idx_sem.at[s])

    def issue_rows(s):
        def issue(tl, carry):
            dst_row = pl.multiple_of(tl * SUBLANES, SUBLANES)
            for k in range(TOP_K):
                d = pl.multiple_of(idx_ref[s * TOP_K + k, tl] * SUBLANES, SUBLANES)
                pltpu.make_async_copy(ys_hbm.at[pl.ds(d, SUBLANES)], buf_ref.at[s, k, pl.ds(dst_row, SUBLANES)],
                                      row_sem.at[s]).start(priority=k % 2)
            return carry

        lax.fori_loop(0, tc, issue, 0)

    @pl.when(i == 0)
    def _():
        idx_copy(0, 0).start()
        idx_copy(0, 0).wait()
        issue_rows(0)

        @pl.when(n > 1)
        def _():
            idx_copy(1, 1).start()

    def step(slot):
        @pl.when(i + 1 < n)
        def _():
            idx_copy(i + 1, 1 - slot).wait()
            issue_rows(1 - slot)

            @pl.when(i + 2 < n)
            def _():
                idx_copy(i + 2, slot).start()

        for k in range(TOP_K):
            pltpu.make_async_copy(ys_hbm.at[pl.ds(0, tc * SUBLANES)], buf_ref.at[slot, k], row_sem.at[slot]).wait()

        tw = tw_ref[...]
        ffn = tw[:, 0:1] * _token_rows_load(buf_ref.at[slot, 0], tc)
        for k in range(1, TOP_K):
            ffn = ffn + tw[:, k:k + 1] * _token_rows_load(buf_ref.at[slot, k], tc)
        x2 = x1_ref[...] + gt_ref[0] * ffn
        o_ref[...] = _rms(x2) * g_ref[...]

    for parity in range(2):
        pl.when(lax.rem(i, 2) == parity)(functools.partial(step, parity))


def _combine(dest_t, ys, topw, x1, gt, g, seq):
    t, d = x1.shape
    tc = min(COMBINE_TILE, seq)
    per_batch = seq // tc
    row = lambda i: (i, 0)
    return pl.pallas_call(
        _combine_kernel,
        out_shape=jax.ShapeDtypeStruct((t, d), F32),
        grid=(t // tc,),
        in_specs=[pl.BlockSpec(memory_space=pl.ANY), pl.BlockSpec(memory_space=pl.ANY),
                  pl.BlockSpec((tc, TOP_K), row), pl.BlockSpec((tc, d), row),
                  pl.BlockSpec((1, 1, d), lambda i: (i // per_batch, 0, 0)), _const_spec((1, d))],
        out_specs=pl.BlockSpec((tc, d), row),
        scratch_shapes=[pltpu.SMEM((2 * TOP_K, tc), I32), pltpu.VMEM((2, TOP_K, tc * SUBLANES, LANES), F32),
                        pltpu.SemaphoreType.DMA((2,)), pltpu.SemaphoreType.DMA((2,))],
        compiler_params=_params("arbitrary"),
        name="combine",
    )(dest_t, ys, topw, x1, gt, g)


def _layer(x2, mod, batch, seq, norm1_g, w_in, conv_w, conv_b, dt_bias, a_log, d_skip, ssm_norm_g,
           gla_wg2, gla_bg, gla_norm_g, w_out, norm2_g, w_router, b_router,
           w_gate, b_gate, w_up, b_up, w_down, b_down):
    t, d = x2.shape
    n_heads = dt_bias.shape[0]
    d_ssm = n_heads * SSM_HEAD_DIM
    cd = conv_w.shape[1]
    dkt = gla_wg2.shape[1]
    dvt = w_out.shape[0] - d_ssm
    n_experts = w_router.shape[1]

    sh1, sc1, gt1, sh2, sc2, gt2 = [m.reshape(batch, 1, d) for m in jnp.split(mod[:batch], 6, axis=1)]

    sizes = (d_ssm, cd, n_heads, dkt, dkt, dvt, GLA_GATE_RANK, dvt)
    offs = [0]
    for s in sizes:
        offs.append(offs[-1] + s)
    piece = lambda j: (offs[j], sizes[j])
    sections = ((piece(0),), (piece(1),), (piece(3),), (piece(4),), (piece(5),), (piece(7),), (piece(2), piece(6)))
    z, xbc, q, k, v, r, small = _inproj(x2, sc1, sh1, norm1_g.reshape(1, d), w_in.astype(BF16), sections,
                                        (d_ssm, cd, dkt, dkt, dvt, dvt, LANES), [BF16] * 6 + [F32], seq)

    d_skip_e = jnp.repeat(d_skip, SSM_HEAD_DIM).reshape(1, d_ssm)
    y, o = _mixers(xbc, z, q, k, v, r, small, conv_w, conv_b, dt_bias, a_log, d_skip_e, ssm_norm_g,
                   gla_wg2, gla_bg, gla_norm_g, batch, seq, gate_col=n_heads)

    x1, h2, topi_t, topw_t, counts = _outproj(y, o, x2, gt1, sc2, sh2, norm2_g.reshape(1, d),
                                        w_out[:d_ssm].astype(BF16), w_out[d_ssm:].astype(BF16),
                                        w_router, b_router, seq)

    n_blocks = (t * TOP_K) // EXPERT_BLOCK + n_experts
    dest_t, block_e, pend = _route(topi_t, counts, n_blocks)
    pend_i = pend[:, 0].astype(I32)
    n_used = (pend_i[n_experts - 1:] // EXPERT_BLOCK).astype(I32)
    topw = topw_t.T
    xs = _dispatch(pend_i, dest_t, h2, n_blocks * EXPERT_BLOCK, n_experts)
    real_end = jnp.concatenate([jnp.zeros((1,), I32), pend_i[:-1]]) + counts[:, 0].astype(I32)
    seg_ends = jnp.concatenate([pend_i, real_end])
    ys = _experts(block_e[0, :n_blocks], n_used, seg_ends, xs, w_gate, b_gate, w_up, b_up, w_down, b_down)
    return dest_t, ys, topw, x1, gt2


def kernel(x, c, w_ada, b_ada, norm1_g, w_in, conv_w, conv_b, dt_bias, a_log, d_skip, ssm_norm_g, gla_wg2,
           gla_bg, gla_norm_g, w_out, norm2_g, w_router, b_router, w_gate, b_gate, w_up, b_up, w_down, b_down,
           final_norm_g):
    batch, seq, d = x.shape
    assert w_ada.shape[0] == 1, "single-layer trunk"
    assert d == SUBLANES * LANES, "token rows are moved as one (8, 128) f32 tile each"
    assert seq % min(seq, max(TOKEN_TILE, SSM_CHUNK, MIXER_STEP_ROWS, COMBINE_TILE)) == 0
    assert seq % max(SSM_CHUNK, GLA_GROUP_ROWS) == 0
    x2 = x.reshape(batch * seq, d)
    c_pad = jnp.zeros((SUBLANES, d), F32).at[:batch].set(c)
    mod = _ada(c_pad, w_ada[0], b_ada)
    dest_t, ys, topw, x1, gt2 = _layer(
        x2, mod, batch, seq, norm1_g[0], w_in[0], conv_w[0], conv_b[0], dt_bias[0], a_log[0], d_skip[0],
        ssm_norm_g[0], gla_wg2[0], gla_bg[0], gla_norm_g[0], w_out[0], norm2_g[0], w_router[0], b_router[0],
        w_gate[0], b_gate[0], w_up[0], b_up[0], w_down[0], b_down[0])
    out = _combine(dest_t, ys, topw, x1, gt2, final_norm_g.reshape(1, d), seq)
    return out.reshape(batch, seq, d)
```

```python
import functools

import jax
import jax.numpy as jnp
from jax import lax
from jax.experimental import pallas as pl
from jax.experimental.pallas import tpu as pltpu

F32 = jnp.float32
BF16 = jnp.bfloat16
I32 = jnp.int32
HIGHEST = lax.Precision.HIGHEST

EPS = 1e-6
SSM_HEAD_DIM = 64
SSM_GROUPS = 2
SSM_STATE = 128
SSM_CONV = 4
SSM_CHUNK = 128
GLA_HEADS = 4
GLA_GATE_RANK = 16
GLA_GATE_NORM = 16.0
GLA_CHUNK = 64
TOP_K = 4
SWIGLU_LIMIT = 7.0
SWIGLU_ALPHA = 1.702

LANES = 128
SUBLANES = 8
V7X_VMEM_BYTES = 64 * 1024 * 1024
VMEM_LIMIT_BYTES = V7X_VMEM_BYTES - 8 * 1024 * 1024

TOKEN_TILE = 512
MIXER_STEP_ROWS = 512
CONV_CARRY = 16
GLA_GROUP_ROWS = 128
ROUTE_TILE = 512
EXPERT_BLOCK = 512
EXPERT_PATHS = 4
DISPATCH_TILE = 2048
COMBINE_TILE = 512

_NT = (((1,), (1,)), ((), ()))
_TN = (((0,), (0,)), ((), ()))


def _sigmoid(v):
    return 0.5 * jnp.tanh(0.5 * v) + 0.5


def _silu(v):
    return v * _sigmoid(v)


def _softplus(v):
    return jnp.maximum(v, 0.0) + jnp.log1p(jnp.exp(-jnp.abs(v)))


def _log_sigmoid(v):
    return jnp.minimum(v, 0.0) - jnp.log(1.0 + jnp.exp(-jnp.abs(v)))


def _rms(v):
    return v * lax.rsqrt(jnp.mean(v * v, axis=-1, keepdims=True) + EPS)


def _pieces(a, n):
    out = []
    for _ in range(n - 1):
        p = a.astype(BF16)
        out.append(p)
        a = a - p.astype(F32)
    out.append(a.astype(BF16))
    return out


def _token_rows_load(ref, rows):
    return jnp.concatenate([ref[pl.ds(s, rows, stride=SUBLANES), :] for s in range(SUBLANES)], axis=1)


def _token_rows_store(ref, v):
    rows = v.shape[0]
    for s in range(SUBLANES):
        ref[pl.ds(s, rows, stride=SUBLANES), :] = v[:, s * LANES:(s + 1) * LANES]


def _params(*semantics):
    return pltpu.CompilerParams(dimension_semantics=semantics, vmem_limit_bytes=VMEM_LIMIT_BYTES)


def _const_spec(shape):
    nd = len(shape)
    return pl.BlockSpec(shape, lambda *_: (0,) * nd)


def _ada_kernel(c_ref, w_ref, b_ref, o_ref):
    cond = _silu(c_ref[...])
    o_ref[...] = jnp.dot(cond, w_ref[...], precision=HIGHEST, preferred_element_type=F32) + b_ref[...]


def _ada(c_pad, w_ada, b_ada):
    rows, d = c_pad.shape
    n = w_ada.shape[1]
    tn = d
    return pl.pallas_call(
        _ada_kernel,
        out_shape=jax.ShapeDtypeStruct((rows, n), F32),
        grid=(n // tn,),
        in_specs=[pl.BlockSpec((rows, d), lambda j: (0, 0)),
                  pl.BlockSpec((d, tn), lambda j: (0, j)),
                  pl.BlockSpec((1, tn), lambda j: (0, j))],
        out_specs=pl.BlockSpec((rows, tn), lambda j: (0, j)),
        compiler_params=_params("arbitrary"),
        name="ada",
    )(c_pad, w_ada, b_ada)


def _inproj_kernel(x_ref, sc_ref, sh_ref, g_ref, w_ref, *refs, sections):
    o_refs, ws_ref = refs[:-1], refs[-1]

    @pl.when(pl.program_id(0) == 0)
    def _():
        off = 0
        for o_ref, pieces in zip(o_refs, sections):
            n = o_ref.shape[1]
            used = 0
            for src, width in pieces:
                ws_ref[:, off + used:off + used + width] = w_ref[:, src:src + width]
                used += width
            if used < n:
                ws_ref[:, off + used:off + n] = jnp.zeros((ws_ref.shape[0], n - used), ws_ref.dtype)
            off += n

    h = (_rms(x_ref[...]) * g_ref[...]) * (1.0 + sc_ref[0]) + sh_ref[0]
    hb = h.astype(BF16)
    off = 0
    for o_ref in o_refs:
        n = o_ref.shape[1]
        o_ref[...] = jnp.dot(hb, ws_ref[:, off:off + n], preferred_element_type=F32).astype(o_ref.dtype)
        off += n


def _inproj(x2, sc, sh, g, w_bf, sections, widths, out_dtypes, seq):
    t, d = x2.shape
    tm = min(TOKEN_TILE, seq)
    per_batch = seq // tm
    assert all(n % LANES == 0 for n in widths)
    mod_spec = pl.BlockSpec((1, 1, d), lambda i: (i // per_batch, 0, 0))
    kern = functools.partial(_inproj_kernel, sections=sections)
    return pl.pallas_call(
        kern,
        out_shape=[jax.ShapeDtypeStruct((t, n), dt) for n, dt in zip(widths, out_dtypes)],
        grid=(t // tm,),
        in_specs=[pl.BlockSpec((tm, d), lambda i: (i, 0)), mod_spec, mod_spec, _const_spec((1, d)),
                  pl.BlockSpec(w_bf.shape, lambda i: (0, 0), pipeline_mode=pl.Buffered(1))],
        out_specs=[pl.BlockSpec((tm, n), lambda i: (i, 0)) for n in widths],
        scratch_shapes=[pltpu.VMEM((d, sum(widths)), BF16)],
        compiler_params=_params("arbitrary"),
        name="inproj",
    )(x2, sc, sh, g, w_bf)


def _ssd_kernel(tail_ref, xbc_ref, z_ref, sm_ref, cw_ref, cb_ref, dtb_r_ref, dtb_c_ref, alog_r_ref, alog_c_ref,
                dsk_ref, g_ref, y_ref, st_ref, *, n_heads, d_ssm):
    L = SSM_CHUNK
    P = SSM_HEAD_DIM
    N = SSM_STATE
    G = SSM_GROUPS
    gw = d_ssm // G
    R = xbc_ref.shape[0]
    c = pl.program_id(1)

    tail = tail_ref[...]
    tail = jnp.where(c == 0, jnp.zeros_like(tail), tail)
    conv_in = jnp.concatenate([tail, xbc_ref[...]], axis=0)

    ri = lax.broadcasted_iota(I32, (L, L), 0)
    ci = lax.broadcasted_iota(I32, (L, L), 1)
    causal = ri >= ci
    causal3 = jnp.concatenate([causal.astype(BF16)] * 3, axis=1)
    upper3 = jnp.concatenate([(ri <= ci).astype(BF16)] * 3, axis=0)
    wr = lax.broadcasted_iota(I32, ((SSM_CONV - 1) * L, CONV_CARRY + L), 0)
    wc = lax.broadcasted_iota(I32, ((SSM_CONV - 1) * L, CONV_CARRY + L), 1)
    shifts = (wc == (wr % L) + CONV_CARRY - (SSM_CONV - 1) + wr // L).astype(BF16)
    hh = lax.broadcasted_iota(I32, (n_heads, d_ssm), 0)
    jj = lax.broadcasted_iota(I32, (n_heads, d_ssm), 1)
    expand = ((jj // P) == hh).astype(BF16)
    expand2 = jnp.concatenate([expand, expand], axis=0)
    lane = lax.broadcasted_iota(I32, (L, LANES), 1)
    first_half = lane < P
    heads_per_group = n_heads // G
    a_row = -jnp.exp(alog_r_ref[...])
    a_col = -jnp.exp(alog_c_ref[...])

    for s in range(R // L):
        rows = slice(s * L, (s + 1) * L)
        window = conv_in[s * L:s * L + CONV_CARRY + L, :]
        taps = jnp.dot(shifts, window, preferred_element_type=F32)
        acc = cb_ref[...] + cw_ref[SSM_CONV - 1:SSM_CONV, :] * window[CONV_CARRY:, :].astype(F32)
        for j in range(SSM_CONV - 1):
            acc = acc + cw_ref[j:j + 1, :] * taps[j * L:(j + 1) * L, :]
        act = _silu(acc)
        xs = act[:, :d_ssm]
        bm = act[:, d_ssm:d_ssm + G * N].astype(BF16)
        cm = act[:, d_ssm + G * N:].astype(BF16)

        sm = sm_ref[rows, :]
        dt_col = _softplus(sm[:, :n_heads] + dtb_r_ref[...])
        dt_row = _softplus(sm.T[:n_heads, :] + dtb_c_ref[...])
        cs_col = jnp.dot(causal3, jnp.concatenate(_pieces(dt_col * a_row, 3), axis=0),
                         preferred_element_type=F32)
        cs_row = jnp.dot(jnp.concatenate(_pieces(dt_row * a_col, 3), axis=1), upper3,
                         preferred_element_type=F32)
        cs_last = cs_col[L - 1:L, :]

        per_head = jnp.concatenate([dt_col, jnp.exp(cs_col), jnp.exp(cs_last - cs_col)], axis=0)
        per_ch = jnp.dot(jnp.concatenate(_pieces(per_head, 2), axis=1), expand2,
                         preferred_element_type=F32)
        dt_e, ecs_e, dte_e = per_ch[0:L], per_ch[L:2 * L], per_ch[2 * L:3 * L]
        chunk_decay = ecs_e[L - 1:L, :]

        xdt = xs * dt_e
        xdt_b = xdt.astype(BF16)
        xdec_b = (xdt * dte_e).astype(BF16)

        y_parts = []
        y_off_parts = []
        for g in range(G):
            bm_g = bm[:, g * N:(g + 1) * N]
            cm_g = cm[:, g * N:(g + 1) * N]
            cb = lax.dot_general(cm_g, bm_g, _NT, preferred_element_type=F32)
            prev = st_ref[g]
            y_off_parts.append(jnp.dot(cm_g, prev.astype(BF16), preferred_element_type=F32))
            s_new = lax.dot_general(bm_g, xdec_b[:, g * gw:(g + 1) * gw], _TN, preferred_element_type=F32)
            st_ref[g] = prev * chunk_decay[:, g * gw:(g + 1) * gw] + s_new
            for p in range(heads_per_group // 2):
                h0 = g * heads_per_group + 2 * p
                ms = []
                for h in (h0, h0 + 1):
                    diff = cs_col[:, h:h + 1] - cs_row[h:h + 1, :]
                    ms.append((cb * jnp.exp(jnp.where(causal, diff, -jnp.inf))).astype(BF16))
                lhs = jnp.concatenate(ms, axis=1)
                xp = xdt_b[:, h0 * P:(h0 + 2) * P]
                zero = jnp.zeros_like(xp)
                rhs = jnp.concatenate([jnp.where(first_half, xp, zero), jnp.where(first_half, zero, xp)], axis=0)
                y_parts.append(jnp.dot(lhs, rhs, preferred_element_type=F32))
        y = jnp.concatenate(y_parts, axis=1) + jnp.concatenate(y_off_parts, axis=1) * ecs_e + dsk_ref[...] * xs
        y = y * _silu(z_ref[rows, :].astype(F32))
        y = jnp.concatenate([_rms(y[:, g * gw:(g + 1) * gw]) for g in range(G)], axis=1) * g_ref[...]
        y_ref[rows, :] = y.astype(y_ref.dtype)


def _gla_kernel(q_ref, k_ref, v_ref, r_ref, sm_ref, wg2_ref, bg_ref, gn_ref, o_ref, st_ref, *, gate_col):
    L = GLA_CHUNK
    H = GLA_HEADS
    dk = q_ref.shape[1] // H
    dv = v_ref.shape[1] // H

    R = min(GLA_GROUP_ROWS, q_ref.shape[0])
    n_chunks = R // L
    ri = lax.broadcasted_iota(I32, (R, R), 0)
    ci = lax.broadcasted_iota(I32, (R, R), 1)
    same_chunk = (ri // L) == (ci // L)
    causal = jnp.logical_and(same_chunk, ri >= ci)
    tril = causal.astype(BF16)
    later = jnp.logical_and(same_chunk, ri < ci).astype(BF16)
    sr = lax.broadcasted_iota(I32, (R, n_chunks * LANES), 0)
    sc = lax.broadcasted_iota(I32, (R, n_chunks * LANES), 1)
    last_rows = (sr == (sc // LANES) * L + (L - 1)).astype(BF16)
    wg2_hi, wg2_lo = _pieces(wg2_ref[...], 2)

    tril3 = jnp.concatenate([jnp.concatenate([tril] * 3, axis=1), jnp.concatenate([later] * 3, axis=1)], axis=0)
    last3 = jnp.concatenate([last_rows] * 3, axis=0)
    wg3 = jnp.concatenate([wg2_hi, wg2_hi, wg2_lo], axis=0)

    for gi in range(q_ref.shape[0] // R):
        rs = slice(gi * R, (gi + 1) * R)
        q = q_ref[rs, :].astype(F32) * (dk ** -0.5)
        k = k_ref[rs, :].astype(F32)
        v = v_ref[rs, :]
        r = r_ref[rs, :].astype(F32)
        g_hi, g_lo = _pieces(sm_ref[rs, gate_col:gate_col + GLA_GATE_RANK], 2)
        pre = jnp.dot(jnp.concatenate([g_hi, g_lo, g_hi], axis=1), wg3, preferred_element_type=F32) + bg_ref[...]
        gk3 = jnp.concatenate(_pieces(_log_sigmoid(pre) / GLA_GATE_NORM, 3), axis=0)
        sums = jnp.dot(tril3, gk3, preferred_element_type=F32)
        bcum = sums[:R]
        to_end = sums[R:]
        q_t = (q * jnp.exp(bcum)).astype(BF16)
        k_t = (k * jnp.exp(-bcum)).astype(BF16)
        k_dec = (k * jnp.exp(to_end)).astype(BF16)
        dcol = jnp.exp(lax.dot_general(jnp.concatenate(_pieces(bcum, 3), axis=0), last3, _TN,
                                       preferred_element_type=F32))
        outs = []
        for h in range(H):
            ks = slice(h * dk, (h + 1) * dk)
            vs = slice(h * dv, (h + 1) * dv)
            att = lax.dot_general(q_t[:, ks], k_t[:, ks], _NT, preferred_element_type=F32)
            att = jnp.where(causal, att, 0.0).astype(BF16)
            o = jnp.dot(att, v[:, vs], preferred_element_type=F32)
            state = st_ref[h]
            inter = []
            for c in range(n_chunks):
                rows = slice(c * L, (c + 1) * L)
                inter.append(jnp.dot(q_t[rows, ks], state.astype(BF16), preferred_element_type=F32))
                s_new = lax.dot_general(k_dec[rows, ks], v[rows, vs], _TN, preferred_element_type=F32)
                dec = dcol[ks, c * LANES:(c + 1) * LANES]
                state = state * jnp.concatenate([dec] * (dv // LANES), axis=1) + s_new
            st_ref[h] = state
            o = o + jnp.concatenate(inter, axis=0)
            outs.append(_rms(o) * gn_ref[...] * _silu(r[:, vs]))
        o_ref[rs, :] = jnp.concatenate(outs, axis=1).astype(o_ref.dtype)


N_SSD_INPUTS = 12
N_GLA_INPUTS = 8
N_OUTPROJ_INPUTS = 9


def _mixer_kernel(*refs, n_heads, d_ssm, gate_col):
    n_in = N_SSD_INPUTS + N_GLA_INPUTS + N_OUTPROJ_INPUTS
    ssd_in = refs[:N_SSD_INPUTS]
    gla_in = refs[N_SSD_INPUTS:N_SSD_INPUTS + N_GLA_INPUTS]
    proj_in = refs[N_SSD_INPUTS + N_GLA_INPUTS:n_in]
    y_ref, o_ref, x1_ref, h_ref, ti_ref, tw_ref, cnt_ref, ssd_st, gla_st = refs[n_in:]

    @pl.when(pl.program_id(1) == 0)
    def _():
        ssd_st[...] = jnp.zeros(ssd_st.shape, F32)
        gla_st[...] = jnp.zeros(gla_st.shape, F32)

    _ssd_kernel(*ssd_in, y_ref, ssd_st, n_heads=n_heads, d_ssm=d_ssm)
    _gla_kernel(*gla_in, o_ref, gla_st, gate_col=gate_col)
    first_step = jnp.logical_and(pl.program_id(0) == 0, pl.program_id(1) == 0)
    _outproj_kernel(y_ref, o_ref, *proj_in, x1_ref, h_ref, ti_ref, tw_ref, cnt_ref, first_step=first_step)


def _mixers(xbc, z, q, k, v, r, small, conv_w, conv_b, dt_bias, a_log, d_skip_e, ssm_norm_g, wg2, bg, gla_norm_g,
            x2, gt, sc, sh, g2, wy, wo, w_router, b_router, batch, seq, gate_col):
    d = x2.shape[1]
    n_e = w_router.shape[1]
    wr_t = w_router.T
    wr_hi = wr_t.astype(BF16)
    wr_cat = jnp.concatenate([wr_hi, (wr_t - wr_hi.astype(F32)).astype(BF16)], axis=0)
    t, cd = xbc.shape
    d_ssm = z.shape[1]
    n_heads = dt_bias.shape[0]
    dkt = q.shape[1]
    dvt = v.shape[1]
    L = min(MIXER_STEP_ROWS, seq)
    nc = seq // L
    row = lambda b, c: (b * nc + c, 0)
    tail = lambda b, c: (jnp.maximum((b * nc + c) * (L // CONV_CARRY) - 1, 0), 0)
    kern = functools.partial(_mixer_kernel, n_heads=n_heads, d_ssm=d_ssm, gate_col=gate_col)
    ssd_specs = [pl.BlockSpec((CONV_CARRY, cd), tail),
                 pl.BlockSpec((L, cd), row), pl.BlockSpec((L, d_ssm), row), pl.BlockSpec((L, LANES), row),
                 _const_spec(conv_w.shape), _const_spec((1, cd)),
                 _const_spec((1, n_heads)), _const_spec((n_heads, 1)),
                 _const_spec((1, n_heads)), _const_spec((n_heads, 1)),
                 _const_spec((1, d_ssm)), _const_spec((1, d_ssm))]
    gla_specs = [pl.BlockSpec((L, dkt), row), pl.BlockSpec((L, dkt), row), pl.BlockSpec((L, dvt), row),
                 pl.BlockSpec((L, dvt), row), pl.BlockSpec((L, LANES), row),
                 _const_spec(wg2.shape), _const_spec((1, dkt)), _const_spec((1, dvt // GLA_HEADS))]
    mod_spec = pl.BlockSpec((1, 1, d), lambda b, c: (b, 0, 0))
    col = lambda b, c: (0, b * nc + c)
    proj_specs = [pl.BlockSpec((L, d), row), mod_spec, mod_spec, mod_spec, _const_spec((1, d)),
                  _const_spec(wy.shape), _const_spec(wo.shape), _const_spec(wr_cat.shape), _const_spec((n_e, 1))]
    assert (len(ssd_specs), len(gla_specs), len(proj_specs)) == (N_SSD_INPUTS, N_GLA_INPUTS, N_OUTPROJ_INPUTS)
    return pl.pallas_call(
        kern,
        out_shape=[jax.ShapeDtypeStruct((t, d_ssm), BF16), jax.ShapeDtypeStruct((t, dvt), BF16),
                   jax.ShapeDtypeStruct((t, d), F32), jax.ShapeDtypeStruct((t * SUBLANES, LANES), F32),
                   jax.ShapeDtypeStruct((TOP_K, t), I32), jax.ShapeDtypeStruct((TOP_K, t), F32),
                   jax.ShapeDtypeStruct((n_e, LANES), F32)],
        grid=(batch, nc),
        in_specs=ssd_specs + gla_specs + proj_specs,
        out_specs=[pl.BlockSpec((L, d_ssm), row), pl.BlockSpec((L, dvt), row),
                   pl.BlockSpec((L, d), row), pl.BlockSpec((L * SUBLANES, LANES), row),
                   pl.BlockSpec((TOP_K, L), col), pl.BlockSpec((TOP_K, L), col), _const_spec((n_e, LANES))],
        scratch_shapes=[pltpu.VMEM((SSM_GROUPS, SSM_STATE, d_ssm // SSM_GROUPS), F32),
                        pltpu.VMEM((GLA_HEADS, dkt // GLA_HEADS, dvt // GLA_HEADS), F32)],
        compiler_params=_params("arbitrary", "arbitrary"),
        name="mixers",
    )(xbc, xbc, z, small, conv_w, conv_b.reshape(1, cd), dt_bias.reshape(1, n_heads), dt_bias.reshape(n_heads, 1),
      a_log.reshape(1, n_heads), a_log.reshape(n_heads, 1), d_skip_e, ssm_norm_g.reshape(1, d_ssm),
      q, k, v, r, small, wg2, bg.reshape(1, dkt), gla_norm_g.reshape(1, dvt // GLA_HEADS),
      x2, gt, sc, sh, g2, wy, wo, wr_cat, b_router.reshape(n_e, 1))


def _outproj_kernel(y_ref, o_ref, x_ref, gt_ref, sc_ref, sh_ref, g_ref, wy_ref, wo_ref, wr_ref, br_ref,
                    x1_ref, h_ref, ti_ref, tw_ref, cnt_ref, *, first_step):
    mix = (jnp.dot(y_ref[...], wy_ref[...], preferred_element_type=F32)
           + jnp.dot(o_ref[...], wo_ref[...], preferred_element_type=F32))
    x1 = x_ref[...] + gt_ref[0] * mix
    x1_ref[...] = x1
    h = (_rms(x1) * g_ref[...]) * (1.0 + sc_ref[0]) + sh_ref[0]
    _token_rows_store(h_ref, h)
    n_e = br_ref.shape[0]
    h_hi, h_lo = _pieces(h, 2)
    wr = wr_ref[...]
    hw = lax.dot_general(wr, h_hi, _NT, preferred_element_type=F32)
    logits = (hw[:n_e] + hw[n_e:] + lax.dot_general(wr[:n_e], h_lo, _NT, preferred_element_type=F32)) + br_ref[...]
    expert = lax.broadcasted_iota(I32, logits.shape, 0)
    vals, idxs = [], []
    counts = jnp.zeros(logits.shape, F32)
    for _ in range(TOP_K):
        m = jnp.max(logits, axis=0, keepdims=True)
        idx = jnp.min(jnp.where(logits == m, expert, n_e), axis=0, keepdims=True)
        vals.append(m)
        idxs.append(idx)
        chosen = expert == idx
        counts = counts + chosen.astype(F32)
        logits = jnp.where(chosen, -jnp.inf, logits)
    exps = [jnp.exp(v - vals[0]) for v in vals]
    denom = functools.reduce(lambda a, b: a + b, exps)
    ti_ref[...] = jnp.concatenate(idxs, axis=0)
    tw_ref[...] = jnp.concatenate([e / denom for e in exps], axis=0)

    @pl.when(first_step)
    def _():
        cnt_ref[...] = jnp.zeros(cnt_ref.shape, F32)

    cnt_ref[...] = cnt_ref[...] + jnp.sum(counts, axis=1, keepdims=True)


def _route_kernel(ti_ref, cnt_ref, dest_ref, be_ref, pend_ref, run_ref, *, n_blocks_pad):
    i = pl.program_id(0)
    n_e = cnt_ref.shape[0]
    tr = ti_ref.shape[1]

    @pl.when(i == 0)
    def _():
        counts = cnt_ref[...]
        padded = jnp.ceil(counts / EXPERT_BLOCK) * EXPERT_BLOCK
        ri = lax.broadcasted_iota(I32, (n_e, n_e), 0)
        ci = lax.broadcasted_iota(I32, (n_e, n_e), 1)
        pend = jnp.dot((ri >= ci).astype(F32), padded, precision=HIGHEST, preferred_element_type=F32)
        pend_ref[...] = pend
        run_ref[...] = pend - padded
        start = (lax.broadcasted_iota(I32, (n_e, n_blocks_pad), 1) * EXPERT_BLOCK).astype(F32)
        be = jnp.sum((pend[:, 0:1] <= start).astype(F32), axis=0, keepdims=True)
        be_ref[...] = jnp.minimum(be, n_e - 1).astype(I32)

    ti = ti_ref[...]
    expert = lax.broadcasted_iota(I32, (n_e, tr), 0)
    onehots = [expert == ti[k:k + 1, :] for k in range(TOP_K)]
    cnt = functools.reduce(lambda a, b: a + b, [oh.astype(F32) for oh in onehots])
    ri = lax.broadcasted_iota(I32, (tr, tr), 0)
    ci = lax.broadcasted_iota(I32, (tr, tr), 1)
    before = jnp.dot(cnt.astype(BF16), (ri < ci).astype(BF16), preferred_element_type=F32)
    base = run_ref[:, 0:1] + before
    dest = [jnp.sum(jnp.where(oh, base, 0.0), axis=0, keepdims=True) for oh in onehots]
    dest_ref[...] = jnp.concatenate(dest, axis=0).astype(I32)
    run_ref[...] = run_ref[...] + jnp.sum(cnt, axis=1, keepdims=True)


def _route(topi_t, counts, n_blocks):
    t = topi_t.shape[1]
    n_e = counts.shape[0]
    tr = min(ROUTE_TILE, t)
    n_blocks_pad = -(-n_blocks // LANES) * LANES
    kern = functools.partial(_route_kernel, n_blocks_pad=n_blocks_pad)
    return pl.pallas_call(
        kern,
        out_shape=[jax.ShapeDtypeStruct((TOP_K, t), I32), jax.ShapeDtypeStruct((1, n_blocks_pad), I32),
                   jax.ShapeDtypeStruct((n_e, LANES), F32)],
        grid=(t // tr,),
        in_specs=[pl.BlockSpec((TOP_K, tr), lambda i: (0, i)), _const_spec((n_e, LANES))],
        out_specs=[pl.BlockSpec((TOP_K, tr), lambda i: (0, i)), _const_spec((1, n_blocks_pad)),
                   _const_spec((n_e, LANES))],
        scratch_shapes=[pltpu.VMEM((n_e, LANES), F32)],
        compiler_params=_params("arbitrary"),
        name="route",
    )(topi_t, counts)


def _dispatch_kernel(pend_ref, dest_hbm, h_ref, xs_hbm, idx_ref, zero_ref, idx_sem, row_sem, *, n_experts):
    i = pl.program_id(0)
    tg = idx_ref.shape[1]

    @pl.when(i == 0)
    def _():
        zero_ref[...] = jnp.zeros(zero_ref.shape, zero_ref.dtype)
        n_blocks = xs_hbm.shape[0] // (EXPERT_BLOCK * SUBLANES)
        total = pend_ref[n_experts - 1]

        def fill(block_row):
            return pltpu.make_async_copy(zero_ref, xs_hbm.at[pl.ds(block_row, EXPERT_BLOCK * SUBLANES)], row_sem)

        for act in ("start", "wait"):
            for e in range(n_experts):
                end = pend_ref[e]
                prev = pend_ref[e - 1] if e > 0 else 0

                @pl.when(end > prev)
                def _():
                    cp = fill(pl.multiple_of((end - EXPERT_BLOCK) * SUBLANES, EXPERT_BLOCK * SUBLANES))
                    cp.start() if act == "start" else cp.wait()

            for b in range(n_blocks - n_experts, n_blocks):
                @pl.when(b * EXPERT_BLOCK >= total)
                def _():
                    cp = fill(b * EXPERT_BLOCK * SUBLANES)
                    cp.start() if act == "start" else cp.wait()

    n = pl.num_programs(0)

    def idx_copy(tile, s):
        return pltpu.make_async_copy(dest_hbm.at[:, pl.ds(pl.multiple_of(tile * tg, tg), tg)],
                                     idx_ref.at[pl.ds(s * TOP_K, TOP_K)], idx_sem.at[s])

    @pl.when(i == 0)
    def _():
        idx_copy(0, 0).start()

    def step(slot):
        idx_copy(i, slot).wait()

        @pl.when(i + 1 < n)
        def _():
            idx_copy(i + 1, 1 - slot).start()

        def issue(tl, carry):
            src = h_ref.at[pl.ds(pl.multiple_of(tl * SUBLANES, SUBLANES), SUBLANES)]
            for k in range(TOP_K):
                d = pl.multiple_of(idx_ref[slot * TOP_K + k, tl] * SUBLANES, SUBLANES)
                pltpu.make_async_copy(src, xs_hbm.at[pl.ds(d, SUBLANES)], row_sem).start(priority=k % 2)
            return carry

        lax.fori_loop(0, tg, issue, 0)

    for parity in range(2):
        pl.when(lax.rem(i, 2) == parity)(functools.partial(step, parity))
    for _ in range(TOP_K):
        pltpu.make_async_copy(h_ref, xs_hbm.at[pl.ds(0, tg * SUBLANES)], row_sem).wait()


def _dispatch(pend_i, dest_t, h, n_rows, n_experts):
    t = h.shape[0] // SUBLANES
    tg = min(DISPATCH_TILE, t)
    kern = functools.partial(_dispatch_kernel, n_experts=n_experts)
    return pl.pallas_call(
        kern,
        out_shape=jax.ShapeDtypeStruct((n_rows * SUBLANES, LANES), h.dtype),
        grid_spec=pltpu.PrefetchScalarGridSpec(
            num_scalar_prefetch=1,
            grid=(t // tg,),
            in_specs=[pl.BlockSpec(memory_space=pl.ANY),
                      pl.BlockSpec((tg * SUBLANES, LANES), lambda i, pend: (i, 0))],
            out_specs=pl.BlockSpec(memory_space=pl.ANY),
            scratch_shapes=[pltpu.SMEM((2 * TOP_K, tg), I32), pltpu.VMEM((EXPERT_BLOCK * SUBLANES, LANES), h.dtype),
                            pltpu.SemaphoreType.DMA((2,)), pltpu.SemaphoreType.DMA]),
        compiler_params=pltpu.CompilerParams(dimension_semantics=("arbitrary",), has_side_effects=True,
                                             vmem_limit_bytes=VMEM_LIMIT_BYTES),
        name="dispatch",
    )(pend_i, dest_t, h)


def _expert_kernel(be_ref, nu_ref, pend_ref, xs_ref, wg_hbm, bg_ref, wu_hbm, bu_ref, wd_hbm, bd_ref, y_ref,
                   wg_f, wu_f, wd_f, wg_b, wu_b, wd_b, slot_ref, sems):
    i = pl.program_id(0)
    used = i < nu_ref[0]
    e = be_ref[i]

    def fetch(expert, slot):
        return [pltpu.make_async_copy(src.at[expert], dst.at[slot], sems.at[slot])
                for src, dst in ((wg_hbm, wg_f), (wu_hbm, wu_f), (wd_hbm, wd_f))]

    @pl.when(i == 0)
    def _():
        slot_ref[0] = 0
        for cp in fetch(e, 0):
            cp.start()

    first_of_expert = jnp.logical_or(i == 0, e != be_ref[jnp.maximum(i - 1, 0)])

    @pl.when(jnp.logical_and(used, first_of_expert))
    def _():
        slot = slot_ref[0]
        for cp in fetch(e, slot):
            cp.wait()
        wg_b[...] = wg_f[slot].astype(BF16)
        wu_b[...] = wu_f[slot].astype(BF16)
        wd_b[...] = wd_f[slot].astype(BF16)
        nxt = lax.div(pend_ref[e], EXPERT_BLOCK)

        @pl.when(nxt < nu_ref[0])
        def _():
            for cp in fetch(be_ref[nxt], 1 - slot):
                cp.start()

        slot_ref[0] = 1 - slot

    def ffn(rows):
        x = _token_rows_load(xs_ref, rows).astype(BF16)
        gate = jnp.minimum(jnp.dot(x, wg_b[...], preferred_element_type=F32) + bg_ref[...], SWIGLU_LIMIT)
        up = jnp.clip(jnp.dot(x, wu_b[...], preferred_element_type=F32) + bu_ref[...],
                      -SWIGLU_LIMIT, SWIGLU_LIMIT)
        glu = gate * _sigmoid(SWIGLU_ALPHA * gate)
        mid = ((up + 1.0) * glu).astype(BF16)
        y = jnp.dot(mid, wd_b[...], preferred_element_type=F32) + bd_ref[...]
        _token_rows_store(y_ref, y)
        if rows < EXPERT_BLOCK:
            y_ref[rows * SUBLANES:, :] = jnp.zeros(((EXPERT_BLOCK - rows) * SUBLANES, LANES), y_ref.dtype)

    valid = pend_ref[pend_ref.shape[0] // 2 + e] - i * EXPERT_BLOCK
    quarter = EXPERT_BLOCK // EXPERT_PATHS
    for p in range(1, EXPERT_PATHS + 1):
        covers = valid <= p * quarter if p < EXPERT_PATHS else True
        needs = valid > (p - 1) * quarter if p > 1 else True
        pl.when(jnp.logical_and(used, jnp.logical_and(covers, needs)))(functools.partial(ffn, p * quarter))

    @pl.when(jnp.logical_not(used))
    def _():
        y_ref[...] = jnp.zeros(y_ref.shape, y_ref.dtype)


def _experts(block_e, n_used, pend_i, xs, w_gate, b_gate, w_up, b_up, w_down, b_down):
    n_rows = xs.shape[0] // SUBLANES
    n_e, d, f = w_gate.shape
    nb = n_rows // EXPERT_BLOCK
    blk = (EXPERT_BLOCK * SUBLANES, LANES)
    last = lambda i, be, nu, pend: jnp.maximum(jnp.minimum(i, nu[0] - 1), 0)
    bspec = lambda n: pl.BlockSpec((None, 1, n), lambda i, be, nu, pend: (be[last(i, be, nu, pend)], 0, 0))
    hbm = pl.BlockSpec(memory_space=pl.ANY)
    return pl.pallas_call(
        _expert_kernel,
        out_shape=jax.ShapeDtypeStruct((n_rows * SUBLANES, LANES), F32),
        grid_spec=pltpu.PrefetchScalarGridSpec(
            num_scalar_prefetch=3,
            grid=(nb,),
            in_specs=[pl.BlockSpec(blk, lambda i, be, nu, pend: (last(i, be, nu, pend), 0)),
                      hbm, bspec(f), hbm, bspec(f), hbm, bspec(d)],
            out_specs=pl.BlockSpec(blk, lambda i, be, nu, pend: (i, 0)),
            scratch_shapes=[pltpu.VMEM((2, d, f), F32), pltpu.VMEM((2, d, f), F32), pltpu.VMEM((2, f, d), F32),
                            pltpu.VMEM((d, f), BF16), pltpu.VMEM((d, f), BF16), pltpu.VMEM((f, d), BF16),
                            pltpu.SMEM((1,), I32), pltpu.SemaphoreType.DMA((2,))]),
        compiler_params=_params("arbitrary"),
        name="experts",
    )(block_e, n_used, pend_i, xs, w_gate, b_gate.reshape(n_e, 1, f), w_up, b_up.reshape(n_e, 1, f),
      w_down, b_down.reshape(n_e, 1, d))


def _combine_kernel(dest_hbm, ys_hbm, tw_ref, x1_ref, gt_ref, g_ref, o_ref, idx_ref, buf_ref, idx_sem, row_sem):
    i = pl.program_id(0)
    n = pl.num_programs(0)
    tc = idx_ref.shape[1]

    def idx_copy(tile, s):
        return pltpu.make_async_copy(dest_hbm.at[:, pl.ds(pl.multiple_of(tile * tc, tc), tc)],
                                     idx_ref.at[pl.ds(s * TOP_K, TOP_K)], idx_sem.at[s])

    def issue_rows(s):
        def issue(tl, carry):
            dst_row = pl.multiple_of(tl * SUBLANES, SUBLANES)
            for k in range(TOP_K):
                d = pl.multiple_of(idx_ref[s * TOP_K + k, tl] * SUBLANES, SUBLANES)
                pltpu.make_async_copy(ys_hbm.at[pl.ds(d, SUBLANES)], buf_ref.at[s, k, pl.ds(dst_row, SUBLANES)],
                                      row_sem.at[s]).start(priority=k % 2)
            return carry

        lax.fori_loop(0, tc, issue, 0)

    @pl.when(i == 0)
    def _():
        idx_copy(0, 0).start()
        idx_copy(0, 0).wait()
        issue_rows(0)

        @pl.when(n > 1)
        def _():
            idx_copy(1, 1).start()

    def step(slot):
        @pl.when(i + 1 < n)
        def _():
            idx_copy(i + 1, 1 - slot).wait()
            issue_rows(1 - slot)

            @pl.when(i + 2 < n)
            def _():
                idx_copy(i + 2, slot).start()

        for k in range(TOP_K):
            pltpu.make_async_copy(ys_hbm.at[pl.ds(0, tc * SUBLANES)], buf_ref.at[slot, k], row_sem.at[slot]).wait()

        tw = tw_ref[...]
        ffn = tw[:, 0:1] * _token_rows_load(buf_ref.at[slot, 0], tc)
        for k in range(1, TOP_K):
            ffn = ffn + tw[:, k:k + 1] * _token_rows_load(buf_ref.at[slot, k], tc)
        x2 = x1_ref[...] + gt_ref[0] * ffn
        o_ref[...] = _rms(x2) * g_ref[...]

    for parity in range(2):
        pl.when(lax.rem(i, 2) == parity)(functools.partial(step, parity))


def _combine(dest_t, ys, topw, x1, gt, g, seq):
    t, d = x1.shape
    tc = min(COMBINE_TILE, seq)
    per_batch = seq // tc
    row = lambda i: (i, 0)
    return pl.pallas_call(
        _combine_kernel,
        out_shape=jax.ShapeDtypeStruct((t, d), F32),
        grid=(t // tc,),
        in_specs=[pl.BlockSpec(memory_space=pl.ANY), pl.BlockSpec(memory_space=pl.ANY),
                  pl.BlockSpec((tc, TOP_K), row), pl.BlockSpec((tc, d), row),
                  pl.BlockSpec((1, 1, d), lambda i: (i // per_batch, 0, 0)), _const_spec((1, d))],
        out_specs=pl.BlockSpec((tc, d), row),
        scratch_shapes=[pltpu.SMEM((2 * TOP_K, tc), I32), pltpu.VMEM((2, TOP_K, tc * SUBLANES, LANES), F32),
                        pltpu.SemaphoreType.DMA((2,)), pltpu.SemaphoreType.DMA((2,))],
        compiler_params=_params("arbitrary"),
        name="combine",
    )(dest_t, ys, topw, x1, gt, g)


def _layer(x2, mod, batch, seq, norm1_g, w_in, conv_w, conv_b, dt_bias, a_log, d_skip, ssm_norm_g,
           gla_wg2, gla_bg, gla_norm_g, w_out, norm2_g, w_router, b_router,
           w_gate, b_gate, w_up, b_up, w_down, b_down):
    t, d = x2.shape
    n_heads = dt_bias.shape[0]
    d_ssm = n_heads * SSM_HEAD_DIM
    cd = conv_w.shape[1]
    dkt = gla_wg2.shape[1]
    dvt = w_out.shape[0] - d_ssm
    n_experts = w_router.shape[1]

    sh1, sc1, gt1, sh2, sc2, gt2 = [m.reshape(batch, 1, d) for m in jnp.split(mod[:batch], 6, axis=1)]

    sizes = (d_ssm, cd, n_heads, dkt, dkt, dvt, GLA_GATE_RANK, dvt)
    offs = [0]
    for s in sizes:
        offs.append(offs[-1] + s)
    piece = lambda j: (offs[j], sizes[j])
    sections = ((piece(0),), (piece(1),), (piece(3),), (piece(4),), (piece(5),), (piece(7),), (piece(2), piece(6)))
    z, xbc, q, k, v, r, small = _inproj(x2, sc1, sh1, norm1_g.reshape(1, d), w_in.astype(BF16), sections,
                                        (d_ssm, cd, dkt, dkt, dvt, dvt, LANES), [BF16] * 6 + [F32], seq)

    d_skip_e = jnp.repeat(d_skip, SSM_HEAD_DIM).reshape(1, d_ssm)
    _, _, x1, h2, topi_t, topw_t, counts = _mixers(
        xbc, z, q, k, v, r, small, conv_w, conv_b, dt_bias, a_log, d_skip_e, ssm_norm_g, gla_wg2, gla_bg, gla_norm_g,
        x2, gt1, sc2, sh2, norm2_g.reshape(1, d), w_out[:d_ssm].astype(BF16), w_out[d_ssm:].astype(BF16),
        w_router, b_router, batch, seq, gate_col=n_heads)

    n_blocks = (t * TOP_K) // EXPERT_BLOCK + n_experts
    dest_t, block_e, pend = _route(topi_t, counts, n_blocks)
    pend_i = pend[:, 0].astype(I32)
    n_used = (pend_i[n_experts - 1:] // EXPERT_BLOCK).astype(I32)
    topw = topw_t.T
    xs = _dispatch(pend_i, dest_t, h2, n_blocks * EXPERT_BLOCK, n_experts)
    real_end = jnp.concatenate([jnp.zeros((1,), I32), pend_i[:-1]]) + counts[:, 0].astype(I32)
    seg_ends = jnp.concatenate([pend_i, real_end])
    ys = _experts(block_e[0, :n_blocks], n_used, seg_ends, xs, w_gate, b_gate, w_up, b_up, w_down, b_down)
    return dest_t, ys, topw, x1, gt2


def kernel(x, c, w_ada, b_ada, norm1_g, w_in, conv_w, conv_b, dt_bias, a_log, d_skip, ssm_norm_g, gla_wg2,
           gla_bg, gla_norm_g, w_out, norm2_g, w_router, b_router, w_gate, b_gate, w_up, b_up, w_down, b_down,
           final_norm_g):
    batch, seq, d = x.shape
    assert w_ada.shape[0] == 1, "single-layer trunk"
    assert d == SUBLANES * LANES, "token rows are moved as one (8, 128) f32 tile each"
    assert seq % min(seq, max(TOKEN_TILE, SSM_CHUNK, MIXER_STEP_ROWS, COMBINE_TILE)) == 0
    assert seq % max(SSM_CHUNK, GLA_GROUP_ROWS) == 0
    x2 = x.reshape(batch * seq, d)
    c_pad = jnp.zeros((SUBLANES, d), F32).at[:batch].set(c)
    mod = _ada(c_pad, w_ada[0], b_ada)
    dest_t, ys, topw, x1, gt2 = _layer(
        x2, mod, batch, seq, norm1_g[0], w_in[0], conv_w[0], conv_b[0], dt_bias[0], a_log[0], d_skip[0],
        ssm_norm_g[0], gla_wg2[0], gla_bg[0], gla_norm_g[0], w_out[0], norm2_g[0], w_router[0], b_router[0],
        w_gate[0], b_gate[0], w_up[0], b_up[0], w_down[0], b_down[0])
    out = _combine(dest_t, ys, topw, x1, gt2, final_norm_g.reshape(1, d), seq)
    return out.reshape(batch, seq, d)
```

```python
import functools

import jax
import jax.numpy as jnp
from jax import lax
from jax.experimental import pallas as pl
from jax.experimental.pallas import tpu as pltpu

F32 = jnp.float32
BF16 = jnp.bfloat16
I32 = jnp.int32
HIGHEST = lax.Precision.HIGHEST

EPS = 1e-6
SSM_HEAD_DIM = 64
SSM_GROUPS = 2
SSM_STATE = 128
SSM_CONV = 4
SSM_CHUNK = 128
GLA_HEADS = 4
GLA_GATE_RANK = 16
GLA_GATE_NORM = 16.0
GLA_CHUNK = 64
TOP_K = 4
SWIGLU_LIMIT = 7.0
SWIGLU_ALPHA = 1.702

LANES = 128
SUBLANES = 8
V7X_VMEM_BYTES = 64 * 1024 * 1024
VMEM_LIMIT_BYTES = V7X_VMEM_BYTES - 8 * 1024 * 1024

TOKEN_TILE = 512
MIXER_STEP_ROWS = 512
CONV_CARRY = 16
GLA_GROUP_ROWS = 128
ROUTE_TILE = 512
EXPERT_BLOCK = 512
EXPERT_PATHS = 4
DISPATCH_TILE = 2048
COMBINE_TILE = 512

_NT = (((1,), (1,)), ((), ()))
_TN = (((0,), (0,)), ((), ()))


def _sigmoid(v):
    return 0.5 * jnp.tanh(0.5 * v) + 0.5


def _silu(v):
    return v * _sigmoid(v)


def _softplus(v):
    return jnp.maximum(v, 0.0) + jnp.log1p(jnp.exp(-jnp.abs(v)))


def _log_sigmoid(v):
    return jnp.minimum(v, 0.0) - jnp.log(1.0 + jnp.exp(-jnp.abs(v)))


def _rms(v):
    return v * lax.rsqrt(jnp.mean(v * v, axis=-1, keepdims=True) + EPS)


def _pieces(a, n):
    out = []
    for _ in range(n - 1):
        p = a.astype(BF16)
        out.append(p)
        a = a - p.astype(F32)
    out.append(a.astype(BF16))
    return out


def _token_rows_load(ref, rows):
    return jnp.concatenate([ref[pl.ds(s, rows, stride=SUBLANES), :] for s in range(SUBLANES)], axis=1)


def _token_rows_store(ref, v):
    rows = v.shape[0]
    for s in range(SUBLANES):
        ref[pl.ds(s, rows, stride=SUBLANES), :] = v[:, s * LANES:(s + 1) * LANES]


def _params(*semantics):
    return pltpu.CompilerParams(dimension_semantics=semantics, vmem_limit_bytes=VMEM_LIMIT_BYTES)


def _const_spec(shape):
    nd = len(shape)
    return pl.BlockSpec(shape, lambda *_: (0,) * nd)


def _ada_kernel(c_ref, w_ref, b_ref, o_ref):
    cond = _silu(c_ref[...])
    o_ref[...] = jnp.dot(cond, w_ref[...], precision=HIGHEST, preferred_element_type=F32) + b_ref[...]


def _ada(c_pad, w_ada, b_ada):
    rows, d = c_pad.shape
    n = w_ada.shape[1]
    tn = d
    return pl.pallas_call(
        _ada_kernel,
        out_shape=jax.ShapeDtypeStruct((rows, n), F32),
        grid=(n // tn,),
        in_specs=[pl.BlockSpec((rows, d), lambda j: (0, 0)),
                  pl.BlockSpec((d, tn), lambda j: (0, j)),
                  pl.BlockSpec((1, tn), lambda j: (0, j))],
        out_specs=pl.BlockSpec((rows, tn), lambda j: (0, j)),
        compiler_params=_params("arbitrary"),
        name="ada",
    )(c_pad, w_ada, b_ada)


def _inproj_kernel(x_ref, sc_ref, sh_ref, g_ref, w_ref, *refs, sections):
    o_refs, ws_ref = refs[:-1], refs[-1]

    @pl.when(pl.program_id(0) == 0)
    def _():
        off = 0
        for o_ref, pieces in zip(o_refs, sections):
            n = o_ref.shape[1]
            used = 0
            for src, width in pieces:
                ws_ref[:, off + used:off + used + width] = w_ref[:, src:src + width].astype(ws_ref.dtype)
                used += width
            if used < n:
                ws_ref[:, off + used:off + n] = jnp.zeros((ws_ref.shape[0], n - used), ws_ref.dtype)
            off += n

    h = (_rms(x_ref[...]) * g_ref[...]) * (1.0 + sc_ref[0]) + sh_ref[0]
    hb = h.astype(BF16)
    off = 0
    for o_ref in o_refs:
        n = o_ref.shape[1]
        o_ref[...] = jnp.dot(hb, ws_ref[:, off:off + n], preferred_element_type=F32).astype(o_ref.dtype)
        off += n


def _inproj(x2, sc, sh, g, w_bf, sections, widths, out_dtypes, seq):
    t, d = x2.shape
    tm = min(TOKEN_TILE, seq)
    per_batch = seq // tm
    assert all(n % LANES == 0 for n in widths)
    mod_spec = pl.BlockSpec((1, 1, d), lambda i: (i // per_batch, 0, 0))
    kern = functools.partial(_inproj_kernel, sections=sections)
    return pl.pallas_call(
        kern,
        out_shape=[jax.ShapeDtypeStruct((t, n), dt) for n, dt in zip(widths, out_dtypes)],
        grid=(t // tm,),
        in_specs=[pl.BlockSpec((tm, d), lambda i: (i, 0)), mod_spec, mod_spec, _const_spec((1, d)),
                  pl.BlockSpec(w_bf.shape, lambda i: (0, 0), pipeline_mode=pl.Buffered(1))],
        out_specs=[pl.BlockSpec((tm, n), lambda i: (i, 0)) for n in widths],
        scratch_shapes=[pltpu.VMEM((d, sum(widths)), BF16)],
        compiler_params=_params("arbitrary"),
        name="inproj",
    )(x2, sc, sh, g, w_bf)


def _ssd_kernel(tail_ref, xbc_ref, z_ref, sm_ref, cw_ref, cb_ref, dtb_r_ref, dtb_c_ref, alog_r_ref, alog_c_ref,
                dsk_ref, g_ref, y_ref, st_ref, *, n_heads, d_ssm):
    L = SSM_CHUNK
    P = SSM_HEAD_DIM
    N = SSM_STATE
    G = SSM_GROUPS
    gw = d_ssm // G
    R = xbc_ref.shape[0]
    c = pl.program_id(1)

    tail = tail_ref[...]
    tail = jnp.where(c == 0, jnp.zeros_like(tail), tail)
    conv_in = jnp.concatenate([tail, xbc_ref[...]], axis=0)

    ri = lax.broadcasted_iota(I32, (L, L), 0)
    ci = lax.broadcasted_iota(I32, (L, L), 1)
    causal = ri >= ci
    causal3 = jnp.concatenate([causal.astype(BF16)] * 3, axis=1)
    upper3 = jnp.concatenate([(ri <= ci).astype(BF16)] * 3, axis=0)
    wr = lax.broadcasted_iota(I32, ((SSM_CONV - 1) * L, CONV_CARRY + L), 0)
    wc = lax.broadcasted_iota(I32, ((SSM_CONV - 1) * L, CONV_CARRY + L), 1)
    shifts = (wc == (wr % L) + CONV_CARRY - (SSM_CONV - 1) + wr // L).astype(BF16)
    hh = lax.broadcasted_iota(I32, (n_heads, d_ssm), 0)
    jj = lax.broadcasted_iota(I32, (n_heads, d_ssm), 1)
    expand = ((jj // P) == hh).astype(BF16)
    expand2 = jnp.concatenate([expand, expand], axis=0)
    lane = lax.broadcasted_iota(I32, (L, LANES), 1)
    first_half = lane < P
    heads_per_group = n_heads // G
    a_row = -jnp.exp(alog_r_ref[...])
    a_col = -jnp.exp(alog_c_ref[...])

    for s in range(R // L):
        rows = slice(s * L, (s + 1) * L)
        window = conv_in[s * L:s * L + CONV_CARRY + L, :]
        taps = jnp.dot(shifts, window, preferred_element_type=F32)
        acc = cb_ref[...] + cw_ref[SSM_CONV - 1:SSM_CONV, :] * window[CONV_CARRY:, :].astype(F32)
        for j in range(SSM_CONV - 1):
            acc = acc + cw_ref[j:j + 1, :] * taps[j * L:(j + 1) * L, :]
        act = _silu(acc)
        xs = act[:, :d_ssm]
        bm = act[:, d_ssm:d_ssm + G * N].astype(BF16)
        cm = act[:, d_ssm + G * N:].astype(BF16)

        sm = sm_ref[rows, :]
        dt_col = _softplus(sm[:, :n_heads] + dtb_r_ref[...])
        dt_row = _softplus(sm.T[:n_heads, :] + dtb_c_ref[...])
        cs_col = jnp.dot(causal3, jnp.concatenate(_pieces(dt_col * a_row, 3), axis=0),
                         preferred_element_type=F32)
        cs_row = jnp.dot(jnp.concatenate(_pieces(dt_row * a_col, 3), axis=1), upper3,
                         preferred_element_type=F32)
        cs_last = cs_col[L - 1:L, :]

        per_head = jnp.concatenate([dt_col, jnp.exp(cs_col), jnp.exp(cs_last - cs_col)], axis=0)
        per_ch = jnp.dot(jnp.concatenate(_pieces(per_head, 2), axis=1), expand2,
                         preferred_element_type=F32)
        dt_e, ecs_e, dte_e = per_ch[0:L], per_ch[L:2 * L], per_ch[2 * L:3 * L]
        chunk_decay = ecs_e[L - 1:L, :]

        xdt = xs * dt_e
        xdt_b = xdt.astype(BF16)
        xdec_b = (xdt * dte_e).astype(BF16)

        y_parts = []
        y_off_parts = []
        for g in range(G):
            bm_g = bm[:, g * N:(g + 1) * N]
            cm_g = cm[:, g * N:(g + 1) * N]
            cb = lax.dot_general(cm_g, bm_g, _NT, preferred_element_type=F32)
            prev = st_ref[g]
            y_off_parts.append(jnp.dot(cm_g, prev.astype(BF16), preferred_element_type=F32))
            s_new = lax.dot_general(bm_g, xdec_b[:, g * gw:(g + 1) * gw], _TN, preferred_element_type=F32)
            st_ref[g] = prev * chunk_decay[:, g * gw:(g + 1) * gw] + s_new
            for p in range(heads_per_group // 2):
                h0 = g * heads_per_group + 2 * p
                ms = []
                for h in (h0, h0 + 1):
                    diff = cs_col[:, h:h + 1] - cs_row[h:h + 1, :]
                    ms.append((cb * jnp.exp(jnp.where(causal, diff, -jnp.inf))).astype(BF16))
                lhs = jnp.concatenate(ms, axis=1)
                xp = xdt_b[:, h0 * P:(h0 + 2) * P]
                zero = jnp.zeros_like(xp)
                rhs = jnp.concatenate([jnp.where(first_half, xp, zero), jnp.where(first_half, zero, xp)], axis=0)
                y_parts.append(jnp.dot(lhs, rhs, preferred_element_type=F32))
        y = jnp.concatenate(y_parts, axis=1) + jnp.concatenate(y_off_parts, axis=1) * ecs_e + dsk_ref[...] * xs
        y = y * _silu(z_ref[rows, :].astype(F32))
        y = jnp.concatenate([_rms(y[:, g * gw:(g + 1) * gw]) for g in range(G)], axis=1) * g_ref[...]
        y_ref[rows, :] = y.astype(y_ref.dtype)


def _gla_kernel(q_ref, k_ref, v_ref, r_ref, sm_ref, wg2_ref, bg_ref, gn_ref, o_ref, st_ref, *, gate_col):
    L = GLA_CHUNK
    H = GLA_HEADS
    dk = q_ref.shape[1] // H
    dv = v_ref.shape[1] // H

    R = min(GLA_GROUP_ROWS, q_ref.shape[0])
    n_chunks = R // L
    ri = lax.broadcasted_iota(I32, (R, R), 0)
    ci = lax.broadcasted_iota(I32, (R, R), 1)
    same_chunk = (ri // L) == (ci // L)
    causal = jnp.logical_and(same_chunk, ri >= ci)
    tril = causal.astype(BF16)
    later = jnp.logical_and(same_chunk, ri < ci).astype(BF16)
    sr = lax.broadcasted_iota(I32, (R, n_chunks * LANES), 0)
    sc = lax.broadcasted_iota(I32, (R, n_chunks * LANES), 1)
    last_rows = (sr == (sc // LANES) * L + (L - 1)).astype(BF16)
    wg2_hi, wg2_lo = _pieces(wg2_ref[...], 2)

    tril3 = jnp.concatenate([jnp.concatenate([tril] * 3, axis=1), jnp.concatenate([later] * 3, axis=1)], axis=0)
    last3 = jnp.concatenate([last_rows] * 3, axis=0)
    wg3 = jnp.concatenate([wg2_hi, wg2_hi, wg2_lo], axis=0)

    for gi in range(q_ref.shape[0] // R):
        rs = slice(gi * R, (gi + 1) * R)
        q = q_ref[rs, :].astype(F32) * (dk ** -0.5)
        k = k_ref[rs, :].astype(F32)
        v = v_ref[rs, :]
        r = r_ref[rs, :].astype(F32)
        g_hi, g_lo = _pieces(sm_ref[rs, gate_col:gate_col + GLA_GATE_RANK], 2)
        pre = jnp.dot(jnp.concatenate([g_hi, g_lo, g_hi], axis=1), wg3, preferred_element_type=F32) + bg_ref[...]
        gk3 = jnp.concatenate(_pieces(_log_sigmoid(pre) / GLA_GATE_NORM, 3), axis=0)
        sums = jnp.dot(tril3, gk3, preferred_element_type=F32)
        bcum = sums[:R]
        to_end = sums[R:]
        q_t = (q * jnp.exp(bcum)).astype(BF16)
        k_t = (k * jnp.exp(-bcum)).astype(BF16)
        k_dec = (k * jnp.exp(to_end)).astype(BF16)
        dcol = jnp.exp(lax.dot_general(jnp.concatenate(_pieces(bcum, 3), axis=0), last3, _TN,
                                       preferred_element_type=F32))
        outs = []
        for h in range(H):
            ks = slice(h * dk, (h + 1) * dk)
            vs = slice(h * dv, (h + 1) * dv)
            att = lax.dot_general(q_t[:, ks], k_t[:, ks], _NT, preferred_element_type=F32)
            att = jnp.where(causal, att, 0.0).astype(BF16)
            o = jnp.dot(att, v[:, vs], preferred_element_type=F32)
            state = st_ref[h]
            inter = []
            for c in range(n_chunks):
                rows = slice(c * L, (c + 1) * L)
                inter.append(jnp.dot(q_t[rows, ks], state.astype(BF16), preferred_element_type=F32))
                s_new = lax.dot_general(k_dec[rows, ks], v[rows, vs], _TN, preferred_element_type=F32)
                dec = dcol[ks, c * LANES:(c + 1) * LANES]
                state = state * jnp.concatenate([dec] * (dv // LANES), axis=1) + s_new
            st_ref[h] = state
            o = o + jnp.concatenate(inter, axis=0)
            outs.append(_rms(o) * gn_ref[...] * _silu(r[:, vs]))
        o_ref[rs, :] = jnp.concatenate(outs, axis=1).astype(o_ref.dtype)


N_SSD_INPUTS = 12
N_GLA_INPUTS = 8
N_OUTPROJ_INPUTS = 9


def _mixer_kernel(*refs, n_heads, d_ssm, gate_col):
    n_in = N_SSD_INPUTS + N_GLA_INPUTS + N_OUTPROJ_INPUTS
    ssd_in = refs[:N_SSD_INPUTS]
    gla_in = refs[N_SSD_INPUTS:N_SSD_INPUTS + N_GLA_INPUTS]
    proj_in = refs[N_SSD_INPUTS + N_GLA_INPUTS:n_in]
    y_ref, o_ref, x1_ref, h_ref, ti_ref, tw_ref, cnt_ref, ssd_st, gla_st = refs[n_in:]

    @pl.when(pl.program_id(1) == 0)
    def _():
        ssd_st[...] = jnp.zeros(ssd_st.shape, F32)
        gla_st[...] = jnp.zeros(gla_st.shape, F32)

    _ssd_kernel(*ssd_in, y_ref, ssd_st, n_heads=n_heads, d_ssm=d_ssm)
    _gla_kernel(*gla_in, o_ref, gla_st, gate_col=gate_col)
    first_step = jnp.logical_and(pl.program_id(0) == 0, pl.program_id(1) == 0)
    _outproj_kernel(y_ref, o_ref, *proj_in, x1_ref, h_ref, ti_ref, tw_ref, cnt_ref, first_step=first_step)


def _mixers(xbc, z, q, k, v, r, small, conv_w, conv_b, dt_bias, a_log, d_skip_e, ssm_norm_g, wg2, bg, gla_norm_g,
            x2, gt, sc, sh, g2, wy, wo, w_router, b_router, batch, seq, gate_col):
    d = x2.shape[1]
    n_e = w_router.shape[1]
    wr_t = w_router.T
    wr_hi = wr_t.astype(BF16)
    wr_cat = jnp.concatenate([wr_hi, (wr_t - wr_hi.astype(F32)).astype(BF16)], axis=0)
    t, cd = xbc.shape
    d_ssm = z.shape[1]
    n_heads = dt_bias.shape[0]
    dkt = q.shape[1]
    dvt = v.shape[1]
    L = min(MIXER_STEP_ROWS, seq)
    nc = seq // L
    row = lambda b, c: (b * nc + c, 0)
    tail = lambda b, c: (jnp.maximum((b * nc + c) * (L // CONV_CARRY) - 1, 0), 0)
    kern = functools.partial(_mixer_kernel, n_heads=n_heads, d_ssm=d_ssm, gate_col=gate_col)
    ssd_specs = [pl.BlockSpec((CONV_CARRY, cd), tail),
                 pl.BlockSpec((L, cd), row), pl.BlockSpec((L, d_ssm), row), pl.BlockSpec((L, LANES), row),
                 _const_spec(conv_w.shape), _const_spec((1, cd)),
                 _const_spec((1, n_heads)), _const_spec((n_heads, 1)),
                 _const_spec((1, n_heads)), _const_spec((n_heads, 1)),
                 _const_spec((1, d_ssm)), _const_spec((1, d_ssm))]
    gla_specs = [pl.BlockSpec((L, dkt), row), pl.BlockSpec((L, dkt), row), pl.BlockSpec((L, dvt), row),
                 pl.BlockSpec((L, dvt), row), pl.BlockSpec((L, LANES), row),
                 _const_spec(wg2.shape), _const_spec((1, dkt)), _const_spec((1, dvt // GLA_HEADS))]
    mod_spec = pl.BlockSpec((1, 1, d), lambda b, c: (b, 0, 0))
    col = lambda b, c: (0, b * nc + c)
    proj_specs = [pl.BlockSpec((L, d), row), mod_spec, mod_spec, mod_spec, _const_spec((1, d)),
                  _const_spec(wy.shape), _const_spec(wo.shape), _const_spec(wr_cat.shape), _const_spec((n_e, 1))]
    assert (len(ssd_specs), len(gla_specs), len(proj_specs)) == (N_SSD_INPUTS, N_GLA_INPUTS, N_OUTPROJ_INPUTS)
    return pl.pallas_call(
        kern,
        out_shape=[jax.ShapeDtypeStruct((t, d_ssm), BF16), jax.ShapeDtypeStruct((t, dvt), BF16),
                   jax.ShapeDtypeStruct((t, d), F32), jax.ShapeDtypeStruct((t * SUBLANES, LANES), F32),
                   jax.ShapeDtypeStruct((TOP_K, t), I32), jax.ShapeDtypeStruct((TOP_K, t), F32),
                   jax.ShapeDtypeStruct((n_e, LANES), F32)],
        grid=(batch, nc),
        in_specs=ssd_specs + gla_specs + proj_specs,
        out_specs=[pl.BlockSpec((L, d_ssm), row), pl.BlockSpec((L, dvt), row),
                   pl.BlockSpec((L, d), row), pl.BlockSpec((L * SUBLANES, LANES), row),
                   pl.BlockSpec((TOP_K, L), col), pl.BlockSpec((TOP_K, L), col), _const_spec((n_e, LANES))],
        scratch_shapes=[pltpu.VMEM((SSM_GROUPS, SSM_STATE, d_ssm // SSM_GROUPS), F32),
                        pltpu.VMEM((GLA_HEADS, dkt // GLA_HEADS, dvt // GLA_HEADS), F32)],
        compiler_params=_params("arbitrary", "arbitrary"),
        name="mixers",
    )(xbc, xbc, z, small, conv_w, conv_b.reshape(1, cd), dt_bias.reshape(1, n_heads), dt_bias.reshape(n_heads, 1),
      a_log.reshape(1, n_heads), a_log.reshape(n_heads, 1), d_skip_e, ssm_norm_g.reshape(1, d_ssm),
      q, k, v, r, small, wg2, bg.reshape(1, dkt), gla_norm_g.reshape(1, dvt // GLA_HEADS),
      x2, gt, sc, sh, g2, wy, wo, wr_cat, b_router.reshape(n_e, 1))


def _outproj_kernel(y_ref, o_ref, x_ref, gt_ref, sc_ref, sh_ref, g_ref, wy_ref, wo_ref, wr_ref, br_ref,
                    x1_ref, h_ref, ti_ref, tw_ref, cnt_ref, *, first_step):
    mix = (jnp.dot(y_ref[...], wy_ref[...], preferred_element_type=F32)
           + jnp.dot(o_ref[...], wo_ref[...], preferred_element_type=F32))
    x1 = x_ref[...] + gt_ref[0] * mix
    x1_ref[...] = x1
    h = (_rms(x1) * g_ref[...]) * (1.0 + sc_ref[0]) + sh_ref[0]
    _token_rows_store(h_ref, h)
    n_e = br_ref.shape[0]
    h_hi, h_lo = _pieces(h, 2)
    wr = wr_ref[...]
    hw = lax.dot_general(wr, h_hi, _NT, preferred_element_type=F32)
    logits = (hw[:n_e] + hw[n_e:] + lax.dot_general(wr[:n_e], h_lo, _NT, preferred_element_type=F32)) + br_ref[...]
    expert = lax.broadcasted_iota(I32, logits.shape, 0)
    vals, idxs = [], []
    counts = jnp.zeros(logits.shape, F32)
    for _ in range(TOP_K):
        m = jnp.max(logits, axis=0, keepdims=True)
        idx = jnp.min(jnp.where(logits == m, expert, n_e), axis=0, keepdims=True)
        vals.append(m)
        idxs.append(idx)
        chosen = expert == idx
        counts = counts + chosen.astype(F32)
        logits = jnp.where(chosen, -jnp.inf, logits)
    exps = [jnp.exp(v - vals[0]) for v in vals]
    denom = functools.reduce(lambda a, b: a + b, exps)
    ti_ref[...] = jnp.concatenate(idxs, axis=0)
    tw_ref[...] = jnp.concatenate([e / denom for e in exps], axis=0)

    @pl.when(first_step)
    def _():
        cnt_ref[...] = jnp.zeros(cnt_ref.shape, F32)

    cnt_ref[...] = cnt_ref[...] + jnp.sum(counts, axis=1, keepdims=True)


def _route_kernel(ti_ref, cnt_ref, dest_ref, be_ref, pend_ref, run_ref, *, n_blocks_pad):
    i = pl.program_id(0)
    n_e = cnt_ref.shape[0]
    tr = ti_ref.shape[1]

    @pl.when(i == 0)
    def _():
        counts = cnt_ref[...]
        padded = jnp.ceil(counts / EXPERT_BLOCK) * EXPERT_BLOCK
        ri = lax.broadcasted_iota(I32, (n_e, n_e), 0)
        ci = lax.broadcasted_iota(I32, (n_e, n_e), 1)
        pend = jnp.dot((ri >= ci).astype(F32), padded, precision=HIGHEST, preferred_element_type=F32)
        pend_ref[...] = pend
        run_ref[...] = pend - padded
        start = (lax.broadcasted_iota(I32, (n_e, n_blocks_pad), 1) * EXPERT_BLOCK).astype(F32)
        be = jnp.sum((pend[:, 0:1] <= start).astype(F32), axis=0, keepdims=True)
        be_ref[...] = jnp.minimum(be, n_e - 1).astype(I32)

    ti = ti_ref[...]
    expert = lax.broadcasted_iota(I32, (n_e, tr), 0)
    onehots = [expert == ti[k:k + 1, :] for k in range(TOP_K)]
    cnt = functools.reduce(lambda a, b: a + b, [oh.astype(F32) for oh in onehots])
    ri = lax.broadcasted_iota(I32, (tr, tr), 0)
    ci = lax.broadcasted_iota(I32, (tr, tr), 1)
    before = jnp.dot(cnt.astype(BF16), (ri < ci).astype(BF16), preferred_element_type=F32)
    base = run_ref[:, 0:1] + before
    dest = [jnp.sum(jnp.where(oh, base, 0.0), axis=0, keepdims=True) for oh in onehots]
    dest_ref[...] = jnp.concatenate(dest, axis=0).astype(I32)
    run_ref[...] = run_ref[...] + jnp.sum(cnt, axis=1, keepdims=True)


def _route(topi_t, counts, n_blocks):
    t = topi_t.shape[1]
    n_e = counts.shape[0]
    tr = min(ROUTE_TILE, t)
    n_blocks_pad = -(-n_blocks // LANES) * LANES
    kern = functools.partial(_route_kernel, n_blocks_pad=n_blocks_pad)
    return pl.pallas_call(
        kern,
        out_shape=[jax.ShapeDtypeStruct((TOP_K, t), I32), jax.ShapeDtypeStruct((1, n_blocks_pad), I32),
                   jax.ShapeDtypeStruct((n_e, LANES), F32)],
        grid=(t // tr,),
        in_specs=[pl.BlockSpec((TOP_K, tr), lambda i: (0, i)), _const_spec((n_e, LANES))],
        out_specs=[pl.BlockSpec((TOP_K, tr), lambda i: (0, i)), _const_spec((1, n_blocks_pad)),
                   _const_spec((n_e, LANES))],
        scratch_shapes=[pltpu.VMEM((n_e, LANES), F32)],
        compiler_params=_params("arbitrary"),
        name="route",
    )(topi_t, counts)


def _dispatch_kernel(pend_ref, dest_hbm, h_ref, xs_hbm, idx_ref, zero_ref, idx_sem, row_sem, *, n_experts):
    i = pl.program_id(0)
    tg = idx_ref.shape[1]

    @pl.when(i == 0)
    def _():
        zero_ref[...] = jnp.zeros(zero_ref.shape, zero_ref.dtype)
        n_blocks = xs_hbm.shape[0] // (EXPERT_BLOCK * SUBLANES)
        total = pend_ref[n_experts - 1]

        def fill(block_row):
            return pltpu.make_async_copy(zero_ref, xs_hbm.at[pl.ds(block_row, EXPERT_BLOCK * SUBLANES)], row_sem)

        for act in ("start", "wait"):
            for e in range(n_experts):
                end = pend_ref[e]
                prev = pend_ref[e - 1] if e > 0 else 0

                @pl.when(end > prev)
                def _():
                    cp = fill(pl.multiple_of((end - EXPERT_BLOCK) * SUBLANES, EXPERT_BLOCK * SUBLANES))
                    cp.start() if act == "start" else cp.wait()

            for b in range(n_blocks - n_experts, n_blocks):
                @pl.when(b * EXPERT_BLOCK >= total)
                def _():
                    cp = fill(b * EXPERT_BLOCK * SUBLANES)
                    cp.start() if act == "start" else cp.wait()

    n = pl.num_programs(0)

    def idx_copy(tile, s):
        return pltpu.make_async_copy(dest_hbm.at[:, pl.ds(pl.multiple_of(tile * tg, tg), tg)],
                                     idx_ref.at[pl.ds(s * TOP_K, TOP_K)], idx_sem.at[s])

    @pl.when(i == 0)
    def _():
        idx_copy(0, 0).start()

    def step(slot):
        idx_copy(i, slot).wait()

        @pl.when(i + 1 < n)
        def _():
            idx_copy(i + 1, 1 - slot).start()

        def issue(tl, carry):
            src = h_ref.at[pl.ds(pl.multiple_of(tl * SUBLANES, SUBLANES), SUBLANES)]
            for k in range(TOP_K):
                d = pl.multiple_of(idx_ref[slot * TOP_K + k, tl] * SUBLANES, SUBLANES)
                pltpu.make_async_copy(src, xs_hbm.at[pl.ds(d, SUBLANES)], row_sem).start(priority=k % 2)
            return carry

        lax.fori_loop(0, tg, issue, 0)

    for parity in range(2):
        pl.when(lax.rem(i, 2) == parity)(functools.partial(step, parity))
    for _ in range(TOP_K):
        pltpu.make_async_copy(h_ref, xs_hbm.at[pl.ds(0, tg * SUBLANES)], row_sem).wait()


def _dispatch(pend_i, dest_t, h, n_rows, n_experts):
    t = h.shape[0] // SUBLANES
    tg = min(DISPATCH_TILE, t)
    kern = functools.partial(_dispatch_kernel, n_experts=n_experts)
    return pl.pallas_call(
        kern,
        out_shape=jax.ShapeDtypeStruct((n_rows * SUBLANES, LANES), h.dtype),
        grid_spec=pltpu.PrefetchScalarGridSpec(
            num_scalar_prefetch=1,
            grid=(t // tg,),
            in_specs=[pl.BlockSpec(memory_space=pl.ANY),
                      pl.BlockSpec((tg * SUBLANES, LANES), lambda i, pend: (i, 0))],
            out_specs=pl.BlockSpec(memory_space=pl.ANY),
            scratch_shapes=[pltpu.SMEM((2 * TOP_K, tg), I32), pltpu.VMEM((EXPERT_BLOCK * SUBLANES, LANES), h.dtype),
                            pltpu.SemaphoreType.DMA((2,)), pltpu.SemaphoreType.DMA]),
        compiler_params=pltpu.CompilerParams(dimension_semantics=("arbitrary",), has_side_effects=True,
                                             vmem_limit_bytes=VMEM_LIMIT_BYTES),
        name="dispatch",
    )(pend_i, dest_t, h)


def _expert_kernel(be_ref, nu_ref, pend_ref, xs_ref, wg_hbm, bg_ref, wu_hbm, bu_ref, wd_hbm, bd_ref, y_ref,
                   wg_f, wu_f, wd_f, wg_b, wu_b, wd_b, slot_ref, sems):
    i = pl.program_id(0)
    used = i < nu_ref[0]
    e = be_ref[i]

    def fetch(expert, slot):
        return [pltpu.make_async_copy(src.at[expert], dst.at[slot], sems.at[slot])
                for src, dst in ((wg_hbm, wg_f), (wu_hbm, wu_f), (wd_hbm, wd_f))]

    @pl.when(i == 0)
    def _():
        slot_ref[0] = 0
        for cp in fetch(e, 0):
            cp.start()

    first_of_expert = jnp.logical_or(i == 0, e != be_ref[jnp.maximum(i - 1, 0)])

    @pl.when(jnp.logical_and(used, first_of_expert))
    def _():
        slot = slot_ref[0]
        for cp in fetch(e, slot):
            cp.wait()
        wg_b[...] = wg_f[slot].astype(BF16)
        wu_b[...] = wu_f[slot].astype(BF16)
        wd_b[...] = wd_f[slot].astype(BF16)
        nxt = lax.div(pend_ref[e], EXPERT_BLOCK)

        @pl.when(nxt < nu_ref[0])
        def _():
            for cp in fetch(be_ref[nxt], 1 - slot):
                cp.start()

        slot_ref[0] = 1 - slot

    def ffn(rows):
        x = _token_rows_load(xs_ref, rows).astype(BF16)
        gate = jnp.minimum(jnp.dot(x, wg_b[...], preferred_element_type=F32) + bg_ref[...], SWIGLU_LIMIT)
        up = jnp.clip(jnp.dot(x, wu_b[...], preferred_element_type=F32) + bu_ref[...],
                      -SWIGLU_LIMIT, SWIGLU_LIMIT)
        glu = gate * _sigmoid(SWIGLU_ALPHA * gate)
        mid = ((up + 1.0) * glu).astype(BF16)
        y = jnp.dot(mid, wd_b[...], preferred_element_type=F32) + bd_ref[...]
        _token_rows_store(y_ref, y)
        if rows < EXPERT_BLOCK:
            y_ref[rows * SUBLANES:, :] = jnp.zeros(((EXPERT_BLOCK - rows) * SUBLANES, LANES), y_ref.dtype)

    valid = pend_ref[pend_ref.shape[0] // 2 + e] - i * EXPERT_BLOCK
    quarter = EXPERT_BLOCK // EXPERT_PATHS
    for p in range(1, EXPERT_PATHS + 1):
        covers = valid <= p * quarter if p < EXPERT_PATHS else True
        needs = valid > (p - 1) * quarter if p > 1 else True
        pl.when(jnp.logical_and(used, jnp.logical_and(covers, needs)))(functools.partial(ffn, p * quarter))

    @pl.when(jnp.logical_not(used))
    def _():
        y_ref[...] = jnp.zeros(y_ref.shape, y_ref.dtype)


def _experts(block_e, n_used, pend_i, xs, w_gate, b_gate, w_up, b_up, w_down, b_down):
    n_rows = xs.shape[0] // SUBLANES
    n_e, d, f = w_gate.shape
    nb = n_rows // EXPERT_BLOCK
    blk = (EXPERT_BLOCK * SUBLANES, LANES)
    last = lambda i, be, nu, pend: jnp.maximum(jnp.minimum(i, nu[0] - 1), 0)
    bspec = lambda n: pl.BlockSpec((None, 1, n), lambda i, be, nu, pend: (be[last(i, be, nu, pend)], 0, 0))
    hbm = pl.BlockSpec(memory_space=pl.ANY)
    return pl.pallas_call(
        _expert_kernel,
        out_shape=jax.ShapeDtypeStruct((n_rows * SUBLANES, LANES), F32),
        grid_spec=pltpu.PrefetchScalarGridSpec(
            num_scalar_prefetch=3,
            grid=(nb,),
            in_specs=[pl.BlockSpec(blk, lambda i, be, nu, pend: (last(i, be, nu, pend), 0)),
                      hbm, bspec(f), hbm, bspec(f), hbm, bspec(d)],
            out_specs=pl.BlockSpec(blk, lambda i, be, nu, pend: (i, 0)),
            scratch_shapes=[pltpu.VMEM((2, d, f), F32), pltpu.VMEM((2, d, f), F32), pltpu.VMEM((2, f, d), F32),
                            pltpu.VMEM((d, f), BF16), pltpu.VMEM((d, f), BF16), pltpu.VMEM((f, d), BF16),
                            pltpu.SMEM((1,), I32), pltpu.SemaphoreType.DMA((2,))]),
        compiler_params=_params("arbitrary"),
        name="experts",
    )(block_e, n_used, pend_i, xs, w_gate, b_gate.reshape(n_e, 1, f), w_up, b_up.reshape(n_e, 1, f),
      w_down, b_down.reshape(n_e, 1, d))


def _combine_kernel(dest_hbm, ys_hbm, tw_ref, x1_ref, gt_ref, g_ref, o_ref, idx_ref, buf_ref, idx_sem, row_sem):
    i = pl.program_id(0)
    n = pl.num_programs(0)
    tc = idx_ref.shape[1]

    def idx_copy(tile, s):
        return pltpu.make_async_copy(dest_hbm.at[:, pl.ds(pl.multiple_of(tile * tc, tc), tc)],
                                     idx_ref.at[pl.ds(s * TOP_K, TOP_K)], idx_sem.at[s])

    def issue_rows(s):
        def issue(tl, carry):
            dst_row = pl.multiple_of(tl * SUBLANES, SUBLANES)
            for k in range(TOP_K):
                d = pl.multiple_of(idx_ref[s * TOP_K + k, tl] * SUBLANES, SUBLANES)
                pltpu.make_async_copy(ys_hbm.at[pl.ds(d, SUBLANES)], buf_ref.at[s, k, pl.ds(dst_row, SUBLANES)],
                                      row_sem.at[s]).start(priority=k % 2)
            return carry

        lax.fori_loop(0, tc, issue, 0)

    @pl.when(i == 0)
    def _():
        idx_copy(0, 0).start()
        idx_copy(0, 0).wait()
        issue_rows(0)

        @pl.when(n > 1)
        def _():
            idx_copy(1, 1).start()

    def step(slot):
        @pl.when(i + 1 < n)
        def _():
            idx_copy(i + 1, 1 - slot).wait()
            issue_rows(1 - slot)

            @pl.when(i + 2 < n)
            def _():
                idx_copy(i + 2, slot).start()

        for k in range(TOP_K):
            pltpu.make_async_copy(ys_hbm.at[pl.ds(0, tc * SUBLANES)], buf_ref.at[slot, k], row_sem.at[slot]).wait()

        tw = tw_ref[...]
        ffn = tw[:, 0:1] * _token_rows_load(buf_ref.at[slot, 0], tc)
        for k in range(1, TOP_K):
            ffn = ffn + tw[:, k:k + 1] * _token_rows_load(buf_ref.at[slot, k], tc)
        x2 = x1_ref[...] + gt_ref[0] * ffn
        o_ref[...] = _rms(x2) * g_ref[...]

    for parity in range(2):
        pl.when(lax.rem(i, 2) == parity)(functools.partial(step, parity))


def _combine(dest_t, ys, topw, x1, gt, g, seq):
    t, d = x1.shape
    tc = min(COMBINE_TILE, seq)
    per_batch = seq // tc
    row = lambda i: (i, 0)
    return pl.pallas_call(
        _combine_kernel,
        out_shape=jax.ShapeDtypeStruct((t, d), F32),
        grid=(t // tc,),
        in_specs=[pl.BlockSpec(memory_space=pl.ANY), pl.BlockSpec(memory_space=pl.ANY),
                  pl.BlockSpec((tc, TOP_K), row), pl.BlockSpec((tc, d), row),
                  pl.BlockSpec((1, 1, d), lambda i: (i // per_batch, 0, 0)), _const_spec((1, d))],
        out_specs=pl.BlockSpec((tc, d), row),
        scratch_shapes=[pltpu.SMEM((2 * TOP_K, tc), I32), pltpu.VMEM((2, TOP_K, tc * SUBLANES, LANES), F32),
                        pltpu.SemaphoreType.DMA((2,)), pltpu.SemaphoreType.DMA((2,))],
        compiler_params=_params("arbitrary"),
        name="combine",
    )(dest_t, ys, topw, x1, gt, g)


def _layer(x2, mod, batch, seq, norm1_g, w_in, conv_w, conv_b, dt_bias, a_log, d_skip, ssm_norm_g,
           gla_wg2, gla_bg, gla_norm_g, w_out, norm2_g, w_router, b_router,
           w_gate, b_gate, w_up, b_up, w_down, b_down):
    t, d = x2.shape
    n_heads = dt_bias.shape[0]
    d_ssm = n_heads * SSM_HEAD_DIM
    cd = conv_w.shape[1]
    dkt = gla_wg2.shape[1]
    dvt = w_out.shape[0] - d_ssm
    n_experts = w_router.shape[1]

    sh1, sc1, gt1, sh2, sc2, gt2 = [m.reshape(batch, 1, d) for m in jnp.split(mod[:batch], 6, axis=1)]

    sizes = (d_ssm, cd, n_heads, dkt, dkt, dvt, GLA_GATE_RANK, dvt)
    offs = [0]
    for s in sizes:
        offs.append(offs[-1] + s)
    piece = lambda j: (offs[j], sizes[j])
    sections = ((piece(0),), (piece(1),), (piece(3),), (piece(4),), (piece(5),), (piece(7),), (piece(2), piece(6)))
    z, xbc, q, k, v, r, small = _inproj(x2, sc1, sh1, norm1_g.reshape(1, d), w_in, sections,
                                        (d_ssm, cd, dkt, dkt, dvt, dvt, LANES), [BF16] * 6 + [F32], seq)

    d_skip_e = jnp.repeat(d_skip, SSM_HEAD_DIM).reshape(1, d_ssm)
    _, _, x1, h2, topi_t, topw_t, counts = _mixers(
        xbc, z, q, k, v, r, small, conv_w, conv_b, dt_bias, a_log, d_skip_e, ssm_norm_g, gla_wg2, gla_bg, gla_norm_g,
        x2, gt1, sc2, sh2, norm2_g.reshape(1, d), w_out[:d_ssm].astype(BF16), w_out[d_ssm:].astype(BF16),
        w_router, b_router, batch, seq, gate_col=n_heads)

    n_blocks = (t * TOP_K) // EXPERT_BLOCK + n_experts
    dest_t, block_e, pend = _route(topi_t, counts, n_blocks)
    pend_i = pend[:, 0].astype(I32)
    n_used = (pend_i[n_experts - 1:] // EXPERT_BLOCK).astype(I32)
    topw = topw_t.T
    xs = _dispatch(pend_i, dest_t, h2, n_blocks * EXPERT_BLOCK, n_experts)
    real_end = jnp.concatenate([jnp.zeros((1,), I32), pend_i[:-1]]) + counts[:, 0].astype(I32)
    seg_ends = jnp.concatenate([pend_i, real_end])
    ys = _experts(block_e[0, :n_blocks], n_used, seg_ends, xs, w_gate, b_gate, w_up, b_up, w_down, b_down)
    return dest_t, ys, topw, x1, gt2


def kernel(x, c, w_ada, b_ada, norm1_g, w_in, conv_w, conv_b, dt_bias, a_log, d_skip, ssm_norm_g, gla_wg2,
           gla_bg, gla_norm_g, w_out, norm2_g, w_router, b_router, w_gate, b_gate, w_up, b_up, w_down, b_down,
           final_norm_g):
    batch, seq, d = x.shape
    assert w_ada.shape[0] == 1, "single-layer trunk"
    assert d == SUBLANES * LANES, "token rows are moved as one (8, 128) f32 tile each"
    assert seq % min(seq, max(TOKEN_TILE, SSM_CHUNK, MIXER_STEP_ROWS, COMBINE_TILE)) == 0
    assert seq % max(SSM_CHUNK, GLA_GROUP_ROWS) == 0
    x2 = x.reshape(batch * seq, d)
    c_pad = jnp.zeros((SUBLANES, d), F32).at[:batch].set(c)
    mod = _ada(c_pad, w_ada[0], b_ada)
    dest_t, ys, topw, x1, gt2 = _layer(
        x2, mod, batch, seq, norm1_g[0], w_in[0], conv_w[0], conv_b[0], dt_bias[0], a_log[0], d_skip[0],
        ssm_norm_g[0], gla_wg2[0], gla_bg[0], gla_norm_g[0], w_out[0], norm2_g[0], w_router[0], b_router[0],
        w_gate[0], b_gate[0], w_up[0], b_up[0], w_down[0], b_down[0])
    out = _combine(dest_t, ys, topw, x1, gt2, final_norm_g.reshape(1, d), seq)
    return out.reshape(batch, seq, d)
```
